```python
import math
import jax, jax.numpy as jnp
from jax import lax
import numpy as np

D_MODEL = 1024
BATCH = 4
SEQ = 4096
DEPTH = 4

N_MIXERS = 3
N_RET = (DEPTH + 2) // 3
N_MOBA = (DEPTH + 1) // 3
N_DIFF = DEPTH // 3

RET_HEADS = 4
RET_DK = D_MODEL // RET_HEADS
RET_DV = 2 * RET_DK
RET_CHUNK = 128
ROPE_BASE = 10000.0

ATTN_HEADS = 16
HEAD_DIM = D_MODEL // ATTN_HEADS
MOBA_BLOCK = 256
MOBA_TOPK = 3
MOBA_Q_CHUNK = 16
DIFF_HEADS = ATTN_HEADS // 2
DIFF_V_DIM = 2 * HEAD_DIM
ATTN_Q_BLOCK = 128

REL_BUCKETS = 32
REL_MAX_EXACT = REL_BUCKETS // 2
REL_MAX_DISTANCE = 1024

D_FF = 4 * D_MODEL
EPS = 1e-6
NEG = -1e30

kernel_name = "hybrid_retention_moba_diffattn_trunk"


def _rmsnorm(x, w):
    xf = x.astype(jnp.float32)
    y = xf * lax.rsqrt(jnp.mean(xf * xf, axis=-1, keepdims=True) + EPS)
    return y.astype(x.dtype) * w


def _rel_bucket(dist):
    n = jnp.maximum(dist, 0)
    nf = jnp.maximum(n, 1).astype(jnp.float32)
    large = REL_MAX_EXACT + (jnp.log(nf / REL_MAX_EXACT) / math.log(REL_MAX_DISTANCE / REL_MAX_EXACT)
                             * (REL_BUCKETS - REL_MAX_EXACT)).astype(jnp.int32)
    large = jnp.minimum(large, REL_BUCKETS - 1)
    return jnp.where(n < REL_MAX_EXACT, n, large)


def _rotary(x):
    S, d = x.shape[2], x.shape[3]
    inv_freq = ROPE_BASE ** (-jnp.arange(0, d, 2, dtype=jnp.float32) / d)
    ang = jnp.arange(S, dtype=jnp.float32)[:, None] * inv_freq[None, :]
    cos, sin = jnp.cos(ang), jnp.sin(ang)
    x1, x2 = x[..., : d // 2], x[..., d // 2:]
    return jnp.concatenate([x1 * cos - x2 * sin, x1 * sin + x2 * cos], axis=-1)


def _squared_relu_mlp(h, w_up, w_down):
    u = jax.nn.relu(h @ w_up)
    return (u * u) @ w_down


def _retention_mixer(h, w_in, w_out):
    B, S, _ = h.shape
    H, dk, dv, C = RET_HEADS, RET_DK, RET_DV, RET_CHUNK
    N = S // C
    proj = (h @ w_in).astype(jnp.float32)
    q, k, v, g = jnp.split(proj, [H * dk, 2 * H * dk, 2 * H * dk + H * dv], axis=-1)
    q = _rotary(q.reshape(B, S, H, dk).transpose(0, 2, 1, 3))
    k = _rotary(k.reshape(B, S, H, dk).transpose(0, 2, 1, 3)) * (dk ** -0.5)
    v = v.reshape(B, S, H, dv).transpose(0, 2, 1, 3)

    log_gamma = jnp.log(1.0 - 2.0 ** (-5.0 - jnp.arange(H, dtype=jnp.float32)))
    pos = jnp.arange(C, dtype=jnp.float32)
    rel = pos[:, None] - pos[None, :]
    inner_decay = jnp.where(rel >= 0, jnp.exp(jnp.maximum(rel, 0.0)[None] * log_gamma[:, None, None]), 0.0)

    qc = q.reshape(B, H, N, C, dk)
    kc = k.reshape(B, H, N, C, dk)
    vc = v.reshape(B, H, N, C, dv)
    s_in = jnp.einsum('bhnid,bhnjd->bhnij', qc, kc) * inner_decay[:, None]
    inner = jnp.einsum('bhnij,bhnjv->bhniv', s_in, vc)
    q_dec = qc * jnp.exp((pos + 1.0)[None, :] * log_gamma[:, None])[:, None, :, None]
    k_dec = kc * jnp.exp((C - 1.0 - pos)[None, :] * log_gamma[:, None])[:, None, :, None]
    chunk_decay = jnp.exp(C * log_gamma)[:, None, None]

    def step(state, xs):
        qn, kn, vn = xs
        out = jnp.einsum('bhid,bhdv->bhiv', qn, state)
        state = state * chunk_decay + jnp.einsum('bhjd,bhjv->bhdv', kn, vn)
        return state, out

    xs = (jnp.moveaxis(q_dec, 2, 0), jnp.moveaxis(k_dec, 2, 0), jnp.moveaxis(vc, 2, 0))
    _, cross = lax.scan(step, jnp.zeros((B, H, dk, dv), jnp.float32), xs)
    o = (inner + jnp.moveaxis(cross, 0, 2)).reshape(B, H, S, dv)
    o = o * lax.rsqrt(jnp.mean(o * o, axis=-1, keepdims=True) + EPS)
    o = o.transpose(0, 2, 1, 3).reshape(B, S, H * dv)
    return ((jax.nn.silu(g) * o) @ w_out).astype(h.dtype)


def _moba_mixer(h, w_in, q_norm_w, k_norm_w, w_out, rel_bias):
    B, S, _ = h.shape
    H, d, BS, QC = ATTN_HEADS, HEAD_DIM, MOBA_BLOCK, MOBA_Q_CHUNK
    proj = h @ w_in
    q, k, v = jnp.split(proj, 3, axis=-1)
    q = _rmsnorm(q.reshape(B, S, H, d), q_norm_w).transpose(0, 2, 1, 3).astype(jnp.float32) * (d ** -0.5)
    k = _rmsnorm(k.reshape(B, S, H, d), k_norm_w).transpose(0, 2, 1, 3).astype(jnp.float32)
    v = v.reshape(B, S, H, d).transpose(0, 2, 1, 3)
    NB = -(-S // BS)
    pad = NB * BS - S
    k_pad = jnp.pad(k, ((0, 0), (0, 0), (0, pad), (0, 0)))
    v_pad = jnp.pad(v, ((0, 0), (0, 0), (0, pad), (0, 0)))
    kb = k_pad.reshape(B, H, NB, BS, d)
    vb = v_pad.reshape(B, H, NB, BS, d)
    k_mean = kb.mean(axis=3)
    topk = min(MOBA_TOPK, NB)
    bias_t = rel_bias.T.astype(jnp.float32)
    head_idx = jnp.arange(H)[None, :, None, None, None]
    in_block = jnp.arange(BS)
    gather = jax.vmap(jax.vmap(lambda blocks, ix: blocks[ix]))

    def chunk(c):
        t0 = c * QC
        b = t0 // BS
        t = t0 + jnp.arange(QC)
        qc = lax.dynamic_slice_in_dim(q, t0, QC, axis=2)
        gate = jnp.einsum('bhqd,bhnd->bhqn', qc, k_mean)
        gate = jnp.where(jnp.arange(NB) < b, gate, NEG)
        _, idx = lax.top_k(gate, topk)
        valid = idx < b
        k_sel = gather(kb, idx)
        v_sel = gather(vb, idx)
        s_sel = jnp.einsum('bhqd,bhqkjd->bhqkj', qc, k_sel)
        kpos = idx[..., None] * BS + in_block
        s_sel = s_sel + bias_t[head_idx, _rel_bucket(t[:, None, None] - kpos)]
        s_sel = jnp.where(valid[..., None], s_sel, NEG)
        k_own = lax.dynamic_slice_in_dim(k_pad, b * BS, BS, axis=2)
        v_own = lax.dynamic_slice_in_dim(v_pad, b * BS, BS, axis=2)
        dist = t[:, None] - (b * BS + in_block)[None, :]
        s_own = jnp.einsum('bhqd,bhjd->bhqj', qc, k_own) + bias_t[:, _rel_bucket(dist)]
        s_own = jnp.where(dist >= 0, s_own, NEG)
        scores = jnp.concatenate([s_sel.reshape(B, H, QC, topk * BS), s_own], axis=-1)
        p = jax.nn.softmax(scores, axis=-1)
        p_sel = p[..., : topk * BS].reshape(B, H, QC, topk, BS)
        p_own = p[..., topk * BS:]
        return (jnp.einsum('bhqkj,bhqkjd->bhqd', p_sel, v_sel)
                + jnp.einsum('bhqj,bhjd->bhqd', p_own, v_own))

    outs = lax.map(chunk, jnp.arange(S // QC))
    o = outs.transpose(1, 2, 0, 3, 4).reshape(B, H, S, d).transpose(0, 2, 1, 3).reshape(B, S, H * d)
    return (o @ w_out).astype(h.dtype)


def _diff_mixer(h, w_in, q_norm_w, k_norm_w, lambdas, subln_w, w_out, rel_bias, lambda_init):
    B, S, _ = h.shape
    H2, d, QB = 2 * DIFF_HEADS, HEAD_DIM, ATTN_Q_BLOCK
    proj = h @ w_in
    q, k, v = jnp.split(proj, 3, axis=-1)
    q = _rmsnorm(q.reshape(B, S, H2, d), q_norm_w).transpose(0, 2, 1, 3).astype(jnp.float32) * (d ** -0.5)
    k = _rmsnorm(k.reshape(B, S, H2, d), k_norm_w).transpose(0, 2, 1, 3).astype(jnp.float32)
    v = v.reshape(B, S, DIFF_HEADS, DIFF_V_DIM).transpose(0, 2, 1, 3)
    lam = lambdas.astype(jnp.float32)
    lambda_full = jnp.exp(jnp.sum(lam[0] * lam[1])) - jnp.exp(jnp.sum(lam[2] * lam[3])) + lambda_init
    bias_t = rel_bias.T.astype(jnp.float32)
    kpos = jnp.arange(S)

    def block(c):
        t0 = c * QB
        t = t0 + jnp.arange(QB)
        qb = lax.dynamic_slice_in_dim(q, t0, QB, axis=2)
        dist = t[:, None] - kpos[None, :]
        s = jnp.einsum('bhqd,bhkd->bhqk', qb, k) + bias_t[:, _rel_bucket(dist)]
        s = jnp.where(dist >= 0, s, NEG)
        p = jax.nn.softmax(s, axis=-1).reshape(B, DIFF_HEADS, 2, QB, S)
        attn = p[:, :, 0] - lambda_full * p[:, :, 1]
        return jnp.einsum('bhqk,bhkv->bhqv', attn, v)

    outs = lax.map(block, jnp.arange(S // QB))
    o = outs.transpose(1, 2, 0, 3, 4).reshape(B, DIFF_HEADS, S, DIFF_V_DIM)
    o = _rmsnorm(o, subln_w) * (1.0 - lambda_init)
    o = o.transpose(0, 2, 1, 3).reshape(B, S, DIFF_HEADS * DIFF_V_DIM)
    return (o @ w_out).astype(h.dtype)


def setup_inputs(seed: int = 0) -> dict:
    key = jax.random.key(seed)
    ks = jax.random.split(key, 18)

    def nrm(k, shape, scale):
        return jax.random.normal(k, shape, jnp.float32) * scale

    D = D_MODEL
    ret_in = 2 * RET_HEADS * RET_DK + 2 * RET_HEADS * RET_DV
    return {
        "x": nrm(ks[0], (BATCH, SEQ, D), 1.0),
        "rel_bias": nrm(ks[1], (REL_BUCKETS, ATTN_HEADS), 0.5),
        "norm1": 1.0 + nrm(ks[2], (DEPTH, D), 0.02),
        "norm2": 1.0 + nrm(ks[3], (DEPTH, D), 0.02),
        "w_up": nrm(ks[4], (DEPTH, D, D_FF), D ** -0.5),
        "w_down": nrm(ks[5], (DEPTH, D_FF, D), D_FF ** -0.5),
        "ret_w_in": nrm(ks[6], (N_RET, D, ret_in), D ** -0.5),
        "ret_w_out": nrm(ks[7], (N_RET, RET_HEADS * RET_DV, D), (RET_HEADS * RET_DV) ** -0.5),
        "moba_w_in": nrm(ks[8], (N_MOBA, D, 3 * ATTN_HEADS * HEAD_DIM), D ** -0.5),
        "moba_q_norm": 1.0 + nrm(ks[9], (N_MOBA, HEAD_DIM), 0.02),
        "moba_k_norm": 1.0 + nrm(ks[10], (N_MOBA, HEAD_DIM), 0.02),
        "moba_w_out": nrm(ks[11], (N_MOBA, ATTN_HEADS * HEAD_DIM, D), (ATTN_HEADS * HEAD_DIM) ** -0.5),
        "diff_w_in": nrm(ks[12], (N_DIFF, D, 3 * ATTN_HEADS * HEAD_DIM), D ** -0.5),
        "diff_q_norm": 1.0 + nrm(ks[13], (N_DIFF, HEAD_DIM), 0.02),
        "diff_k_norm": 1.0 + nrm(ks[14], (N_DIFF, HEAD_DIM), 0.02),
        "diff_lambda": nrm(ks[15], (N_DIFF, 4, HEAD_DIM), 0.1),
        "diff_subln": 1.0 + nrm(ks[16], (N_DIFF, DIFF_V_DIM), 0.02),
        "diff_w_out": nrm(ks[17], (N_DIFF, DIFF_HEADS * DIFF_V_DIM, D), (DIFF_HEADS * DIFF_V_DIM) ** -0.5),
    }


def reference(x, rel_bias, norm1, norm2, w_up, w_down, ret_w_in, ret_w_out,
              moba_w_in, moba_q_norm, moba_k_norm, moba_w_out,
              diff_w_in, diff_q_norm, diff_k_norm, diff_lambda, diff_subln, diff_w_out):
    for i in range(DEPTH):
        kind, j = i % N_MIXERS, i // N_MIXERS
        h = _rmsnorm(x, norm1[i])
        if kind == 0:
            m = _retention_mixer(h, ret_w_in[j], ret_w_out[j])
        elif kind == 1:
            m = _moba_mixer(h, moba_w_in[j], moba_q_norm[j], moba_k_norm[j], moba_w_out[j], rel_bias)
        else:
            lambda_init = 0.8 - 0.6 * math.exp(-0.3 * i)
            m = _diff_mixer(h, diff_w_in[j], diff_q_norm[j], diff_k_norm[j], diff_lambda[j],
                            diff_subln[j], diff_w_out[j], rel_bias, lambda_init)
        x = x + m
        x = x + _squared_relu_mlp(_rmsnorm(x, norm2[i]), w_up[i], w_down[i])
    return x
```

```python
import functools
import math

import numpy as np
import jax
import jax.numpy as jnp
from jax import lax
from jax.experimental import pallas as pl
from jax.experimental.pallas import tpu as pltpu

F32 = jnp.float32
BF16 = jnp.bfloat16

D_MODEL = 1024
N_MIXERS = 3
RET_HEADS = 4
RET_DK = D_MODEL // RET_HEADS
RET_DV = 2 * RET_DK
ROPE_BASE = 10000.0
ATTN_HEADS = 16
HEAD_DIM = D_MODEL // ATTN_HEADS
MOBA_BLOCK = 256
MOBA_TOPK = 3
REL_BUCKETS = 32
REL_MAX_EXACT = REL_BUCKETS // 2
REL_MAX_DISTANCE = 1024
D_FF = 4 * D_MODEL
EPS = 1e-6
NEG = -1e30

LANES = 128
HEADS_PER_STEP = LANES // HEAD_DIM
VMEM_LIMIT_BYTES = 56 * 1024 * 1024

ROW_TILE = 512
COL_TILE = 256
FF_TILE = 1024
RET_CHUNK = 256
ATT_TILE = 256
BIAS_TILES = 6


def _params(*sem):
    return pltpu.CompilerParams(dimension_semantics=sem, vmem_limit_bytes=VMEM_LIMIT_BYTES)


def _rms(xf, w):
    ms = jnp.mean(xf * xf, axis=-1, keepdims=True)
    return xf * lax.rsqrt(ms + EPS) * w


def _dot(a, b):
    return jnp.dot(a, b, preferred_element_type=F32)


def _dot_nt(a, b):
    return lax.dot_general(a, b, (((1,), (1,)), ((), ())), preferred_element_type=F32)


def _dot_tn(a, b):
    return lax.dot_general(a, b, (((0,), (0,)), ((), ())), preferred_element_type=F32)


def _ret_proj_kernel(x_ref, nw_ref, w_ref, cos_ref, sin_ref, q_ref, k_ref, v_ref, g_ref):
    h = _rms(x_ref[...], nw_ref[...]).astype(BF16)
    cos = cos_ref[...]
    sin = sin_ref[...]
    half = RET_DK // 2
    for part, out_ref, scale in ((0, q_ref, 1.0), (1, k_ref, RET_DK ** -0.5)):
        for hd in range(RET_HEADS):
            base = part * RET_HEADS * RET_DK + hd * RET_DK
            acc = _dot(h, w_ref[:, base:base + RET_DK])
            x1 = acc[:, :half]
            x2 = acc[:, half:]
            r1 = x1 * cos - x2 * sin
            r2 = x1 * sin + x2 * cos
            if scale != 1.0:
                r1 = r1 * scale
                r2 = r2 * scale
            out_ref[:, hd * RET_DK:hd * RET_DK + half] = r1.astype(BF16)
            out_ref[:, hd * RET_DK + half:(hd + 1) * RET_DK] = r2.astype(BF16)
    vg_base = 2 * RET_HEADS * RET_DK
    n_v = RET_HEADS * RET_DV
    for part, out_ref in ((0, v_ref), (1, g_ref)):
        for c in range(n_v // COL_TILE):
            base = vg_base + part * n_v + c * COL_TILE
            acc = _dot(h, w_ref[:, base:base + COL_TILE])
            out_ref[:, c * COL_TILE:(c + 1) * COL_TILE] = acc.astype(BF16)


def _ret_proj(x, nw, w, cos, sin, seq):
    t = x.shape[0]
    n_in = w.shape[1]
    tiles_per_seq = seq // ROW_TILE
    row = lambda i: (i, 0)
    const = lambda i: (0, 0)
    pos = lambda i: (i % tiles_per_seq, 0)
    n_qk = RET_HEADS * RET_DK
    n_v = RET_HEADS * RET_DV
    return pl.pallas_call(
        _ret_proj_kernel,
        grid=(t // ROW_TILE,),
        in_specs=[
            pl.BlockSpec((ROW_TILE, D_MODEL), row),
            pl.BlockSpec((1, D_MODEL), const),
            pl.BlockSpec((D_MODEL, n_in), const),
            pl.BlockSpec((ROW_TILE, RET_DK // 2), pos),
            pl.BlockSpec((ROW_TILE, RET_DK // 2), pos),
        ],
        out_specs=[
            pl.BlockSpec((ROW_TILE, n_qk), row),
            pl.BlockSpec((ROW_TILE, n_qk), row),
            pl.BlockSpec((ROW_TILE, n_v), row),
            pl.BlockSpec((ROW_TILE, n_v), row),
        ],
        out_shape=[
            jax.ShapeDtypeStruct((t, n_qk), BF16),
            jax.ShapeDtypeStruct((t, n_qk), BF16),
            jax.ShapeDtypeStruct((t, n_v), BF16),
            jax.ShapeDtypeStruct((t, n_v), BF16),
        ],
        compiler_params=_params("parallel"),
        name="ret_proj",
    )(x, nw, w, cos, sin)


def _attn_proj_kernel(x_ref, nw_ref, w_ref, qn_ref, kn_ref, grp_ref, q_ref, k_ref, v_ref):
    h = _rms(x_ref[...], nw_ref[...]).astype(BF16)
    grp = grp_ref[...]
    for part, out_ref, hw_ref in ((0, q_ref, qn_ref), (1, k_ref, kn_ref), (2, v_ref, None)):
        for c in range(D_MODEL // COL_TILE):
            base = part * D_MODEL + c * COL_TILE
            acc = _dot(h, w_ref[:, base:base + COL_TILE])
            if hw_ref is not None:
                sq = acc * acc
                hi = sq.astype(BF16)
                lo = (sq - hi.astype(F32)).astype(BF16)
                ss = _dot(hi, grp) + _dot(lo, grp)
                acc = acc * lax.rsqrt(ss * (1.0 / HEAD_DIM) + EPS) * hw_ref[:, c * COL_TILE:(c + 1) * COL_TILE]
            out_ref[:, c * COL_TILE:(c + 1) * COL_TILE] = acc.astype(BF16)


def _attn_proj(x, nw, w, qn, kn, grp):
    t = x.shape[0]
    row = lambda i: (i, 0)
    const = lambda i: (0, 0)
    return pl.pallas_call(
        _attn_proj_kernel,
        grid=(t // ROW_TILE,),
        in_specs=[
            pl.BlockSpec((ROW_TILE, D_MODEL), row),
            pl.BlockSpec((1, D_MODEL), const),
            pl.BlockSpec((D_MODEL, 3 * D_MODEL), const),
            pl.BlockSpec((1, D_MODEL), const),
            pl.BlockSpec((1, D_MODEL), const),
            pl.BlockSpec((COL_TILE, COL_TILE), const),
        ],
        out_specs=[pl.BlockSpec((ROW_TILE, D_MODEL), row)] * 3,
        out_shape=[jax.ShapeDtypeStruct((t, D_MODEL), BF16)] * 3,
        compiler_params=_params("parallel"),
        name="attn_proj",
    )(x, nw, w, qn, kn, grp)


def _ret_core_kernel(q_ref, k_ref, v_ref, dm_ref, rs_ref, ks_ref, cd_ref, o_ref, state_ref):
    c_len = RET_CHUNK
    n_chunks = q_ref.shape[0] // c_len
    state_ref[...] = jnp.zeros_like(state_ref)

    def body(c, carry):
        sl = pl.ds(pl.multiple_of(c * c_len, c_len), c_len)
        q = q_ref[sl, :]
        k = k_ref[sl, :]
        v = v_ref[sl, :]
        s = _dot_nt(q, k) * dm_ref[0]
        inner = _dot(s.astype(BF16), v)
        state = state_ref[...]
        cross = _dot(q, state.astype(BF16)) * rs_ref[0]
        o = inner + cross
        o = o * lax.rsqrt(jnp.mean(o * o, axis=-1, keepdims=True) + EPS)
        o_ref[sl, :] = o.astype(BF16)
        kd = (k.astype(F32) * ks_ref[0]).astype(BF16)
        state_ref[...] = state * cd_ref[0] + _dot_tn(kd, v)
        return carry

    lax.fori_loop(0, n_chunks, body, 0)


def _ret_core(q, k, v, dm, rs, ks, cd, batch, seq):
    t = q.shape[0]
    c_len = RET_CHUNK
    tok = lambda b, h: (b, h)
    head3 = lambda b, h: (h, 0, 0)
    return pl.pallas_call(
        _ret_core_kernel,
        grid=(batch, RET_HEADS),
        in_specs=[
            pl.BlockSpec((seq, RET_DK), tok),
            pl.BlockSpec((seq, RET_DK), tok),
            pl.BlockSpec((seq, RET_DV), tok),
            pl.BlockSpec((1, c_len, c_len), head3),
            pl.BlockSpec((1, c_len, 1), head3),
            pl.BlockSpec((1, c_len, 1), head3),
            pl.BlockSpec((1, 1, 1), head3),
        ],
        out_specs=pl.BlockSpec((seq, RET_DV), tok),
        out_shape=jax.ShapeDtypeStruct((t, RET_HEADS * RET_DV), BF16),
        scratch_shapes=[pltpu.VMEM((RET_DK, RET_DV), F32)],
        compiler_params=_params("parallel", "parallel"),
        name="ret_core",
    )(q, k, v, dm, rs, ks, cd)


def _out_mlp_kernel(gated, *refs):
    if gated:
        a_ref, g_ref, wo_ref, x_ref, nw_ref, wu_ref, wd_ref, o_ref = refs
        g = g_ref[...].astype(F32)
        a = (g * (1.0 / (1.0 + jnp.exp(-g))) * a_ref[...].astype(F32)).astype(BF16)
    else:
        a_ref, wo_ref, x_ref, nw_ref, wu_ref, wd_ref, o_ref = refs
        a = a_ref[...]
    x = x_ref[...] + _dot(a, wo_ref[...])
    h = _rms(x, nw_ref[...]).astype(BF16)
    acc = x
    for c in range(D_FF // FF_TILE):
        u = _dot(h, wu_ref[:, c * FF_TILE:(c + 1) * FF_TILE])
        u = jnp.maximum(u, 0.0)
        acc = acc + _dot((u * u).astype(BF16), wd_ref[c * FF_TILE:(c + 1) * FF_TILE, :])
    o_ref[...] = acc


def _out_mlp(a, g, wo, x, nw, wu, wd):
    t = x.shape[0]
    ka = a.shape[1]
    row = lambda i: (i, 0)
    const = lambda i: (0, 0)
    gated = g is not None
    in_specs = [pl.BlockSpec((ROW_TILE, ka), row)]
    args = [a]
    if gated:
        in_specs.append(pl.BlockSpec((ROW_TILE, ka), row))
        args.append(g)
    in_specs += [
        pl.BlockSpec((ka, D_MODEL), const),
        pl.BlockSpec((ROW_TILE, D_MODEL), row),
        pl.BlockSpec((1, D_MODEL), const),
        pl.BlockSpec((D_MODEL, D_FF), const, pipeline_mode=pl.Buffered(1)),
        pl.BlockSpec((D_FF, D_MODEL), const, pipeline_mode=pl.Buffered(1)),
    ]
    args += [wo, x, nw, wu, wd]
    return pl.pallas_call(
        functools.partial(_out_mlp_kernel, gated),
        grid=(t // ROW_TILE,),
        in_specs=in_specs,
        out_specs=pl.BlockSpec((ROW_TILE, D_MODEL), row),
        out_shape=jax.ShapeDtypeStruct((t, D_MODEL), F32),
        compiler_params=_params("parallel"),
        name="out_mlp_gated" if gated else "out_mlp",
    )(*args)


def _bias_tiles_kernel(rb_ref, idx_ref, o_ref):
    head = pl.program_id(0)
    idx = idx_ref[0]
    acc = jnp.full(idx.shape, NEG, F32)
    for b in range(REL_BUCKETS):
        acc = jnp.where(idx == b, rb_ref[b, head], acc)
    o_ref[0, 0] = acc


def _bias_tiles(rel_bias, bucket_idx):
    nd, tq, tk = bucket_idx.shape
    return pl.pallas_call(
        _bias_tiles_kernel,
        grid=(ATTN_HEADS, nd),
        in_specs=[
            pl.BlockSpec(memory_space=pltpu.SMEM),
            pl.BlockSpec((1, tq, tk), lambda h, d: (d, 0, 0)),
        ],
        out_specs=pl.BlockSpec((1, 1, tq, tk), lambda h, d: (h, d, 0, 0)),
        out_shape=jax.ShapeDtypeStruct((ATTN_HEADS, nd, tq, tk), F32),
        compiler_params=_params("parallel", "parallel"),
        name="bias_tiles",
    )(rel_bias, bucket_idx)


def _rel_bucket(dist):
    n = jnp.maximum(dist, 0)
    nf = jnp.maximum(n, 1).astype(F32)
    large = REL_MAX_EXACT + (jnp.log(nf / REL_MAX_EXACT) / math.log(REL_MAX_DISTANCE / REL_MAX_EXACT)
                             * (REL_BUCKETS - REL_MAX_EXACT)).astype(jnp.int32)
    large = jnp.minimum(large, REL_BUCKETS - 1)
    return jnp.where(n < REL_MAX_EXACT, n, large)


def _bucket_index_tiles():
    r = np.arange(ATT_TILE)
    dist = (np.arange(BIAS_TILES)[:, None, None] * ATT_TILE + r[None, :, None] - r[None, None, :])
    dist = jnp.asarray(dist, jnp.int32)
    return jnp.where(dist >= 0, _rel_bucket(dist), REL_BUCKETS).astype(jnp.int32)


def _head_lane_masks():
    lane = lax.broadcasted_iota(jnp.int32, (1, LANES), 1)
    return [(lane >= e * HEAD_DIM) & (lane < (e + 1) * HEAD_DIM) for e in range(HEADS_PER_STEP)]


def _softmax_step(qs, kt, vt, bias_pair, carry, row_masks=None):
    new = []
    for e in range(HEADS_PER_STEP):
        m, l, acc = carry[e]
        s = _dot_nt(qs[e], kt) + bias_pair[e]
        if row_masks is not None:
            s = s + row_masks[e]
        m_new = jnp.maximum(m, jnp.max(s, axis=-1, keepdims=True))
        p = jnp.exp(s - m_new)
        alpha = jnp.exp(m - m_new)
        l_new = alpha * l + jnp.sum(p, axis=-1, keepdims=True)
        acc_new = alpha * acc + _dot(p.astype(BF16), vt)
        new.append((m_new, l_new, acc_new))
    return tuple(new)


def _init_carry(tq):
    one = (jnp.full((tq, 1), -jnp.inf, F32), jnp.zeros((tq, 1), F32), jnp.zeros((tq, LANES), F32))
    return tuple(one for _ in range(HEADS_PER_STEP))


def _bias_at(bias_ref, delta):
    d = jnp.minimum(delta, BIAS_TILES - 1)
    return [bias_ref[e, d] for e in range(HEADS_PER_STEP)]


def _diff_attn_kernel(lambda_init, q_ref, k_ref, v_ref, bias_ref, lam_ref, sw_ref, o_ref):
    i = pl.program_id(2)
    tq = q_ref.shape[0]
    masks = _head_lane_masks()
    q = q_ref[...]
    qs = [jnp.where(mk, q, jnp.zeros_like(q)) for mk in masks]

    def body(j, carry):
        sl = pl.ds(pl.multiple_of(j * ATT_TILE, ATT_TILE), ATT_TILE)
        return _softmax_step(qs, k_ref[sl, :], v_ref[sl, :], _bias_at(bias_ref, i - j), carry)

    carry = lax.fori_loop(0, i + 1, body, _init_carry(tq))
    lam = lam_ref[...]
    lam_full = (jnp.exp(jnp.sum(lam[0:1] * lam[1:2], axis=-1, keepdims=True))
                - jnp.exp(jnp.sum(lam[2:3] * lam[3:4], axis=-1, keepdims=True)) + lambda_init)
    (_, l0, a0), (_, l1, a1) = carry
    o = a0 / l0 - lam_full * (a1 / l1)
    o = _rms(o, sw_ref[...]) * (1.0 - lambda_init)
    o_ref[...] = o.astype(BF16)


def _diff_attn(q, k, v, bias, lam, sw, lambda_init, batch, seq):
    t = q.shape[0]
    nq = seq // ATT_TILE
    pairs = ATTN_HEADS // HEADS_PER_STEP
    return pl.pallas_call(
        functools.partial(_diff_attn_kernel, lambda_init),
        grid=(batch, pairs, nq),
        in_specs=[
            pl.BlockSpec((ATT_TILE, LANES), lambda b, p, i: (b * nq + i, p)),
            pl.BlockSpec((seq, LANES), lambda b, p, i: (b, p)),
            pl.BlockSpec((seq, LANES), lambda b, p, i: (b, p)),
            pl.BlockSpec((HEADS_PER_STEP, BIAS_TILES, ATT_TILE, ATT_TILE), lambda b, p, i: (p, 0, 0, 0)),
            pl.BlockSpec((4, HEAD_DIM), lambda b, p, i: (0, 0)),
            pl.BlockSpec((1, LANES), lambda b, p, i: (0, 0)),
        ],
        out_specs=pl.BlockSpec((ATT_TILE, LANES), lambda b, p, i: (b * nq + i, p)),
        out_shape=jax.ShapeDtypeStruct((t, D_MODEL), BF16),
        compiler_params=_params("parallel", "parallel", "arbitrary"),
        name="diff_attn",
    )(q, k, v, bias, lam, sw)


def _moba_attn_kernel(q_ref, k_ref, v_ref, bias_ref, blk_ref, o_ref, kmean_ref):
    i = pl.program_id(2)
    tq = q_ref.shape[0]
    nb = blk_ref.shape[0]
    masks = _head_lane_masks()

    @pl.when(i == 0)
    def _():
        kmean_ref[...] = _dot(blk_ref[...], k_ref[...])

    q = q_ref[...]
    qs = [jnp.where(mk, q, jnp.zeros_like(q)) for mk in masks]

    km = kmean_ref[...]
    km_hi = km.astype(BF16)
    km_lo = (km - km_hi.astype(F32)).astype(BF16)
    col = lax.broadcasted_iota(jnp.int32, (tq, nb), 1)
    row_neg = []
    for e in range(HEADS_PER_STEP):
        gate = _dot_nt(qs[e], km_hi) + _dot_nt(qs[e], km_lo)
        gate = jnp.where(col < i, gate, NEG)
        rank = jnp.zeros((tq, nb), jnp.int32)
        for c in range(nb):
            gc = gate[:, c:c + 1]
            ahead = (gc > gate) | ((gc == gate) & (c < col))
            rank = rank + ahead.astype(jnp.int32)
        sel = (rank < MOBA_TOPK) & (col < i)
        row_neg.append(jnp.where(sel, 0.0, NEG))

    sl_own = pl.ds(pl.multiple_of(i * ATT_TILE, ATT_TILE), ATT_TILE)
    carry = _softmax_step(qs, k_ref[sl_own, :], v_ref[sl_own, :], _bias_at(bias_ref, 0), _init_carry(tq))

    def body(j, carry):
        sl = pl.ds(pl.multiple_of(j * ATT_TILE, ATT_TILE), ATT_TILE)
        rm = [jnp.sum(jnp.where(col == j, rn, 0.0), axis=-1, keepdims=True) for rn in row_neg]
        return _softmax_step(qs, k_ref[sl, :], v_ref[sl, :], _bias_at(bias_ref, i - j), carry, rm)

    carry = lax.fori_loop(0, i, body, carry)
    (_, l0, a0), (_, l1, a1) = carry
    o = jnp.where(masks[0], a0 / l0, a1 / l1)
    o_ref[...] = o.astype(BF16)


def _moba_attn(q, k, v, bias, blk, batch, seq):
    t = q.shape[0]
    nq = seq // ATT_TILE
    nb = blk.shape[0]
    pairs = ATTN_HEADS // HEADS_PER_STEP
    return pl.pallas_call(
        _moba_attn_kernel,
        grid=(batch, pairs, nq),
        in_specs=[
            pl.BlockSpec((ATT_TILE, LANES), lambda b, p, i: (b * nq + i, p)),
            pl.BlockSpec((seq, LANES), lambda b, p, i: (b, p)),
            pl.BlockSpec((seq, LANES), lambda b, p, i: (b, p)),
            pl.BlockSpec((HEADS_PER_STEP, BIAS_TILES, ATT_TILE, ATT_TILE), lambda b, p, i: (p, 0, 0, 0)),
            pl.BlockSpec((nb, seq), lambda b, p, i: (0, 0)),
        ],
        out_specs=pl.BlockSpec((ATT_TILE, LANES), lambda b, p, i: (b * nq + i, p)),
        out_shape=jax.ShapeDtypeStruct((t, D_MODEL), BF16),
        scratch_shapes=[pltpu.VMEM((nb, LANES), F32)],
        compiler_params=_params("parallel", "parallel", "arbitrary"),
        name="moba_attn",
    )(q, k, v, bias, blk)


def _rotary_tables(seq):
    d = RET_DK
    inv_freq = ROPE_BASE ** (-np.arange(0, d, 2, dtype=np.float64) / d)
    ang = np.arange(seq, dtype=np.float64)[:, None] * inv_freq[None, :]
    return jnp.asarray(np.cos(ang), F32), jnp.asarray(np.sin(ang), F32)


def _retention_decay_tables():
    c_len = RET_CHUNK
    log_gamma = np.log(1.0 - 2.0 ** (-5.0 - np.arange(RET_HEADS, dtype=np.float64)))
    pos = np.arange(c_len, dtype=np.float64)
    rel = pos[:, None] - pos[None, :]
    dm = np.where(rel >= 0, np.exp(np.maximum(rel, 0.0)[None] * log_gamma[:, None, None]), 0.0)
    rs = np.exp((pos + 1.0)[None, :] * log_gamma[:, None])[:, :, None]
    ks = np.exp((c_len - 1.0 - pos)[None, :] * log_gamma[:, None])[:, :, None]
    cd = np.exp(c_len * log_gamma)[:, None, None]
    return tuple(jnp.asarray(a, F32) for a in (dm, rs, ks, cd))


def _block_mean_matrix(seq):
    nb = seq // MOBA_BLOCK
    m = (np.arange(seq)[None, :] // MOBA_BLOCK == np.arange(nb)[:, None]) / float(MOBA_BLOCK)
    return jnp.asarray(m, BF16)


def _head_group_matrix():
    g = np.arange(COL_TILE)[:, None] // HEAD_DIM == np.arange(COL_TILE)[None, :] // HEAD_DIM
    return jnp.asarray(g, BF16)


def kernel(x, rel_bias, norm1, norm2, w_up, w_down, ret_w_in, ret_w_out,
           moba_w_in, moba_q_norm, moba_k_norm, moba_w_out,
           diff_w_in, diff_q_norm, diff_k_norm, diff_lambda, diff_subln, diff_w_out):
    batch, seq, d = x.shape
    depth = norm1.shape[0]
    assert d == D_MODEL and seq % ROW_TILE == 0 and seq % ATT_TILE == 0 and seq % RET_CHUNK == 0
    assert seq % MOBA_BLOCK == 0 and MOBA_BLOCK == ATT_TILE
    t = batch * seq
    xf = x.reshape(t, d)

    bias = _bias_tiles(rel_bias.astype(F32), _bucket_index_tiles())
    grp = _head_group_matrix()
    q_scale = HEAD_DIM ** -0.5

    for i in range(depth):
        kind, j = i % N_MIXERS, i // N_MIXERS
        nw1 = norm1[i].reshape(1, d)
        nw2 = norm2[i].reshape(1, d)
        wu = w_up[i].astype(BF16)
        wd = w_down[i].astype(BF16)
        if kind == 0:
            cos, sin = _rotary_tables(seq)
            dm, rs, ks, cd = _retention_decay_tables()
            q, k, v, g = _ret_proj(xf, nw1, ret_w_in[j].astype(BF16), cos, sin, seq)
            o = _ret_core(q, k, v, dm, rs, ks, cd, batch, seq)
            xf = _out_mlp(o, g, ret_w_out[j].astype(BF16), xf, nw2, wu, wd)
        elif kind == 1:
            qn = (jnp.tile(moba_q_norm[j], ATTN_HEADS) * q_scale).reshape(1, d)
            kn = jnp.tile(moba_k_norm[j], ATTN_HEADS).reshape(1, d)
            q, k, v = _attn_proj(xf, nw1, moba_w_in[j].astype(BF16), qn, kn, grp)
            o = _moba_attn(q, k, v, bias, _block_mean_matrix(seq), batch, seq)
            xf = _out_mlp(o, None, moba_w_out[j].astype(BF16), xf, nw2, wu, wd)
        else:
            lambda_init = 0.8 - 0.6 * math.exp(-0.3 * i)
            qn = (jnp.tile(diff_q_norm[j], ATTN_HEADS) * q_scale).reshape(1, d)
            kn = jnp.tile(diff_k_norm[j], ATTN_HEADS).reshape(1, d)
            q, k, v = _attn_proj(xf, nw1, diff_w_in[j].astype(BF16), qn, kn, grp)
            o = _diff_attn(q, k, v, bias, diff_lambda[j].astype(F32), diff_subln[j].reshape(1, LANES),
                           lambda_init, batch, seq)
            xf = _out_mlp(o, None, diff_w_out[j].astype(BF16), xf, nw2, wu, wd)
    return xf.reshape(batch, seq, d)
```

```python
import functools
import math

import numpy as np
import jax
import jax.numpy as jnp
from jax import lax
from jax.experimental import pallas as pl
from jax.experimental.pallas import tpu as pltpu

F32 = jnp.float32
BF16 = jnp.bfloat16

D_MODEL = 1024
N_MIXERS = 3
RET_HEADS = 4
RET_DK = D_MODEL // RET_HEADS
RET_DV = 2 * RET_DK
ROPE_BASE = 10000.0
ATTN_HEADS = 16
HEAD_DIM = D_MODEL // ATTN_HEADS
MOBA_BLOCK = 256
MOBA_TOPK = 3
REL_BUCKETS = 32
REL_MAX_EXACT = REL_BUCKETS // 2
REL_MAX_DISTANCE = 1024
D_FF = 4 * D_MODEL
EPS = 1e-6
NEG = -1e30

LANES = 128
HEADS_PER_STEP = LANES // HEAD_DIM
VMEM_LIMIT_BYTES = 56 * 1024 * 1024

ROW_TILE = 512
COL_TILE = 256
FF_TILE = 1024
RET_CHUNK = 256
ATT_TILE = 256
BIAS_TILES = 6


def _params(*sem):
    return pltpu.CompilerParams(dimension_semantics=sem, vmem_limit_bytes=VMEM_LIMIT_BYTES)


def _rms(xf, w):
    ms = jnp.mean(xf * xf, axis=-1, keepdims=True)
    return xf * lax.rsqrt(ms + EPS) * w


def _dot(a, b):
    return jnp.dot(a, b, preferred_element_type=F32)


def _dot_nt(a, b):
    return lax.dot_general(a, b, (((1,), (1,)), ((), ())), preferred_element_type=F32)


def _dot_tn(a, b):
    return lax.dot_general(a, b, (((0,), (0,)), ((), ())), preferred_element_type=F32)


def _ret_proj_kernel(x_ref, nw_ref, w_ref, cos_ref, sin_ref, q_ref, k_ref, v_ref, g_ref):
    h = _rms(x_ref[...], nw_ref[...]).astype(BF16)
    cos = cos_ref[...]
    sin = sin_ref[...]
    half = RET_DK // 2
    for part, out_ref, scale in ((0, q_ref, 1.0), (1, k_ref, RET_DK ** -0.5)):
        for hd in range(RET_HEADS):
            base = part * RET_HEADS * RET_DK + hd * RET_DK
            acc = _dot(h, w_ref[:, base:base + RET_DK])
            x1 = acc[:, :half]
            x2 = acc[:, half:]
            r1 = x1 * cos - x2 * sin
            r2 = x1 * sin + x2 * cos
            if scale != 1.0:
                r1 = r1 * scale
                r2 = r2 * scale
            out_ref[:, hd * RET_DK:hd * RET_DK + half] = r1.astype(BF16)
            out_ref[:, hd * RET_DK + half:(hd + 1) * RET_DK] = r2.astype(BF16)
    vg_base = 2 * RET_HEADS * RET_DK
    n_v = RET_HEADS * RET_DV
    for part, out_ref in ((0, v_ref), (1, g_ref)):
        for c in range(n_v // COL_TILE):
            base = vg_base + part * n_v + c * COL_TILE
            acc = _dot(h, w_ref[:, base:base + COL_TILE])
            out_ref[:, c * COL_TILE:(c + 1) * COL_TILE] = acc.astype(BF16)


def _ret_proj(x, nw, w, cos, sin, seq):
    t = x.shape[0]
    n_in = w.shape[1]
    tiles_per_seq = seq // ROW_TILE
    row = lambda i: (i, 0)
    const = lambda i: (0, 0)
    pos = lambda i: (i % tiles_per_seq, 0)
    n_qk = RET_HEADS * RET_DK
    n_v = RET_HEADS * RET_DV
    return pl.pallas_call(
        _ret_proj_kernel,
        grid=(t // ROW_TILE,),
        in_specs=[
            pl.BlockSpec((ROW_TILE, D_MODEL), row),
            pl.BlockSpec((1, D_MODEL), const),
            pl.BlockSpec((D_MODEL, n_in), const),
            pl.BlockSpec((ROW_TILE, RET_DK // 2), pos),
            pl.BlockSpec((ROW_TILE, RET_DK // 2), pos),
        ],
        out_specs=[
            pl.BlockSpec((ROW_TILE, n_qk), row),
            pl.BlockSpec((ROW_TILE, n_qk), row),
            pl.BlockSpec((ROW_TILE, n_v), row),
            pl.BlockSpec((ROW_TILE, n_v), row),
        ],
        out_shape=[
            jax.ShapeDtypeStruct((t, n_qk), BF16),
            jax.ShapeDtypeStruct((t, n_qk), BF16),
            jax.ShapeDtypeStruct((t, n_v), BF16),
            jax.ShapeDtypeStruct((t, n_v), BF16),
        ],
        compiler_params=_params("parallel"),
        name="ret_proj",
    )(x, nw, w, cos, sin)


def _attn_proj_kernel(x_ref, nw_ref, w_ref, qn_ref, kn_ref, grp_ref, q_ref, k_ref, v_ref):
    h = _rms(x_ref[...], nw_ref[...]).astype(BF16)
    grp = grp_ref[...]
    for part, out_ref, hw_ref in ((0, q_ref, qn_ref), (1, k_ref, kn_ref), (2, v_ref, None)):
        for c in range(D_MODEL // COL_TILE):
            base = part * D_MODEL + c * COL_TILE
            acc = _dot(h, w_ref[:, base:base + COL_TILE])
            if hw_ref is not None:
                sq = acc * acc
                hi = sq.astype(BF16)
                lo = (sq - hi.astype(F32)).astype(BF16)
                ss = _dot(hi, grp) + _dot(lo, grp)
                acc = acc * lax.rsqrt(ss * (1.0 / HEAD_DIM) + EPS) * hw_ref[:, c * COL_TILE:(c + 1) * COL_TILE]
            out_ref[:, c * COL_TILE:(c + 1) * COL_TILE] = acc.astype(BF16)


def _attn_proj(x, nw, w, qn, kn, grp):
    t = x.shape[0]
    row = lambda i: (i, 0)
    const = lambda i: (0, 0)
    return pl.pallas_call(
        _attn_proj_kernel,
        grid=(t // ROW_TILE,),
        in_specs=[
            pl.BlockSpec((ROW_TILE, D_MODEL), row),
            pl.BlockSpec((1, D_MODEL), const),
            pl.BlockSpec((D_MODEL, 3 * D_MODEL), const),
            pl.BlockSpec((1, D_MODEL), const),
            pl.BlockSpec((1, D_MODEL), const),
            pl.BlockSpec((COL_TILE, COL_TILE), const),
        ],
        out_specs=[pl.BlockSpec((ROW_TILE, D_MODEL), row)] * 3,
        out_shape=[jax.ShapeDtypeStruct((t, D_MODEL), BF16)] * 3,
        compiler_params=_params("parallel"),
        name="attn_proj",
    )(x, nw, w, qn, kn, grp)


def _ret_core_kernel(q_ref, k_ref, v_ref, dm_ref, rs_ref, ks_ref, cd_ref, o_ref, state_ref):
    c_len = RET_CHUNK
    n_chunks = q_ref.shape[0] // c_len
    state_ref[...] = jnp.zeros_like(state_ref)

    def body(c, carry):
        sl = pl.ds(pl.multiple_of(c * c_len, c_len), c_len)
        q = q_ref[sl, :]
        k = k_ref[sl, :]
        v = v_ref[sl, :]
        s = _dot_nt(q, k) * dm_ref[0]
        inner = _dot(s.astype(BF16), v)
        state = state_ref[...]
        cross = _dot(q, state.astype(BF16)) * rs_ref[0]
        o = inner + cross
        o = o * lax.rsqrt(jnp.mean(o * o, axis=-1, keepdims=True) + EPS)
        o_ref[sl, :] = o.astype(BF16)
        kd = (k.astype(F32) * ks_ref[0]).astype(BF16)
        state_ref[...] = state * cd_ref[0] + _dot_tn(kd, v)
        return carry

    lax.fori_loop(0, n_chunks, body, 0)


def _ret_core(q, k, v, dm, rs, ks, cd, batch, seq):
    t = q.shape[0]
    c_len = RET_CHUNK
    tok = lambda b, h: (b, h)
    head3 = lambda b, h: (h, 0, 0)
    return pl.pallas_call(
        _ret_core_kernel,
        grid=(batch, RET_HEADS),
        in_specs=[
            pl.BlockSpec((seq, RET_DK), tok),
            pl.BlockSpec((seq, RET_DK), tok),
            pl.BlockSpec((seq, RET_DV), tok),
            pl.BlockSpec((1, c_len, c_len), head3),
            pl.BlockSpec((1, c_len, 1), head3),
            pl.BlockSpec((1, c_len, 1), head3),
            pl.BlockSpec((1, 1, 1), head3),
        ],
        out_specs=pl.BlockSpec((seq, RET_DV), tok),
        out_shape=jax.ShapeDtypeStruct((t, RET_HEADS * RET_DV), BF16),
        scratch_shapes=[pltpu.VMEM((RET_DK, RET_DV), F32)],
        compiler_params=_params("parallel", "parallel"),
        name="ret_core",
    )(q, k, v, dm, rs, ks, cd)


def _out_mlp_kernel(gated, *refs):
    if gated:
        a_ref, g_ref, wo_ref, x_ref, nw_ref, wu_ref, wd_ref, o_ref = refs
        g = g_ref[...].astype(F32)
        a = (g * (1.0 / (1.0 + jnp.exp(-g))) * a_ref[...].astype(F32)).astype(BF16)
    else:
        a_ref, wo_ref, x_ref, nw_ref, wu_ref, wd_ref, o_ref = refs
        a = a_ref[...]
    x = x_ref[...] + _dot(a, wo_ref[...])
    h = _rms(x, nw_ref[...]).astype(BF16)
    acc = x
    for c in range(D_FF // FF_TILE):
        u = _dot(h, wu_ref[:, c * FF_TILE:(c + 1) * FF_TILE])
        u = jnp.maximum(u, 0.0)
        acc = acc + _dot((u * u).astype(BF16), wd_ref[c * FF_TILE:(c + 1) * FF_TILE, :])
    o_ref[...] = acc


def _out_mlp(a, g, wo, x, nw, wu, wd):
    t = x.shape[0]
    ka = a.shape[1]
    row = lambda i: (i, 0)
    const = lambda i: (0, 0)
    gated = g is not None
    in_specs = [pl.BlockSpec((ROW_TILE, ka), row)]
    args = [a]
    if gated:
        in_specs.append(pl.BlockSpec((ROW_TILE, ka), row))
        args.append(g)
    in_specs += [
        pl.BlockSpec((ka, D_MODEL), const),
        pl.BlockSpec((ROW_TILE, D_MODEL), row),
        pl.BlockSpec((1, D_MODEL), const),
        pl.BlockSpec((D_MODEL, D_FF), const, pipeline_mode=pl.Buffered(1)),
        pl.BlockSpec((D_FF, D_MODEL), const, pipeline_mode=pl.Buffered(1)),
    ]
    args += [wo, x, nw, wu, wd]
    return pl.pallas_call(
        functools.partial(_out_mlp_kernel, gated),
        grid=(t // ROW_TILE,),
        in_specs=in_specs,
        out_specs=pl.BlockSpec((ROW_TILE, D_MODEL), row),
        out_shape=jax.ShapeDtypeStruct((t, D_MODEL), F32),
        compiler_params=_params("parallel"),
        name="out_mlp_gated" if gated else "out_mlp",
    )(*args)


def _bias_tiles_kernel(rb_ref, idx_ref, o_ref):
    head = pl.program_id(0)
    idx = idx_ref[0]
    acc = jnp.full(idx.shape, NEG, F32)
    for b in range(REL_BUCKETS):
        acc = jnp.where(idx == b, rb_ref[b, head], acc)
    o_ref[0, 0] = acc


def _bias_tiles(rel_bias, bucket_idx):
    nd, tq, tk = bucket_idx.shape
    return pl.pallas_call(
        _bias_tiles_kernel,
        grid=(ATTN_HEADS, nd),
        in_specs=[
            pl.BlockSpec(memory_space=pltpu.SMEM),
            pl.BlockSpec((1, tq, tk), lambda h, d: (d, 0, 0)),
        ],
        out_specs=pl.BlockSpec((1, 1, tq, tk), lambda h, d: (h, d, 0, 0)),
        out_shape=jax.ShapeDtypeStruct((ATTN_HEADS, nd, tq, tk), F32),
        compiler_params=_params("parallel", "parallel"),
        name="bias_tiles",
    )(rel_bias, bucket_idx)


def _rel_bucket(dist):
    n = jnp.maximum(dist, 0)
    nf = jnp.maximum(n, 1).astype(F32)
    large = REL_MAX_EXACT + (jnp.log(nf / REL_MAX_EXACT) / math.log(REL_MAX_DISTANCE / REL_MAX_EXACT)
                             * (REL_BUCKETS - REL_MAX_EXACT)).astype(jnp.int32)
    large = jnp.minimum(large, REL_BUCKETS - 1)
    return jnp.where(n < REL_MAX_EXACT, n, large)


def _bucket_index_tiles():
    r = np.arange(ATT_TILE)
    dist = (np.arange(BIAS_TILES)[:, None, None] * ATT_TILE + r[None, None, :] - r[None, :, None])
    dist = jnp.asarray(dist, jnp.int32)
    return jnp.where(dist >= 0, _rel_bucket(dist), REL_BUCKETS).astype(jnp.int32)


def _head_row_masks():
    row = lax.broadcasted_iota(jnp.int32, (LANES, 1), 0)
    return [(row >= e * HEAD_DIM) & (row < (e + 1) * HEAD_DIM) for e in range(HEADS_PER_STEP)]


def _split_heads(qt):
    qf = qt.astype(F32)
    return [jnp.where(mk, qf, 0.0).astype(BF16) for mk in _head_row_masks()]


def _softmax_step(qts, kt, vtt, bias_pair, carry, query_masks=None):
    heads = range(HEADS_PER_STEP)
    scores = [_dot(kt, qts[e]) for e in heads]
    stats = []
    for e in heads:
        m, l, _ = carry[e]
        s = scores[e] + bias_pair[e]
        if query_masks is not None:
            s = s + query_masks[e]
        m_new = jnp.maximum(m, jnp.max(s, axis=0, keepdims=True))
        p = jnp.exp(s - m_new)
        alpha = jnp.exp(m - m_new)
        l_new = alpha * l + jnp.sum(p, axis=0, keepdims=True)
        stats.append((m_new, l_new, alpha, p.astype(BF16)))
    pv = [_dot(vtt, stats[e][3]) for e in heads]
    return tuple((stats[e][0], stats[e][1], stats[e][2] * carry[e][2] + pv[e]) for e in heads)


def _init_carry(tq):
    one = (jnp.full((1, tq), -jnp.inf, F32), jnp.zeros((1, tq), F32), jnp.zeros((LANES, tq), F32))
    return tuple(one for _ in range(HEADS_PER_STEP))


def _bias_at(bias_ref, delta):
    d = jnp.minimum(delta, BIAS_TILES - 1)
    return [bias_ref[e, d] for e in range(HEADS_PER_STEP)]


def _key_tile(j):
    return pl.ds(pl.multiple_of(j * ATT_TILE, ATT_TILE), ATT_TILE)


def _attn_specs(seq, nq):
    return [
        pl.BlockSpec((LANES, ATT_TILE), lambda b, p, i: (p, b * nq + i)),
        pl.BlockSpec((seq, LANES), lambda b, p, i: (b, p)),
        pl.BlockSpec((LANES, seq), lambda b, p, i: (p, b)),
        pl.BlockSpec((HEADS_PER_STEP, BIAS_TILES, ATT_TILE, ATT_TILE), lambda b, p, i: (p, 0, 0, 0)),
    ]


def _diff_attn_kernel(lambda_init, qt_ref, k_ref, vt_ref, bias_ref, lam_ref, sw_ref, ot_ref):
    i = pl.program_id(2)
    tq = qt_ref.shape[1]
    qts = _split_heads(qt_ref[...])

    def body(j, carry):
        sl = _key_tile(j)
        return _softmax_step(qts, k_ref[sl, :], vt_ref[:, sl], _bias_at(bias_ref, i - j), carry)

    carry = lax.fori_loop(0, i + 1, body, _init_carry(tq))
    lam = lam_ref[...]
    lam_full = (jnp.exp(jnp.sum(lam[0:1] * lam[1:2], axis=-1, keepdims=True))
                - jnp.exp(jnp.sum(lam[2:3] * lam[3:4], axis=-1, keepdims=True)) + lambda_init)
    (_, l0, a0), (_, l1, a1) = carry
    o = a0 / l0 - lam_full * (a1 / l1)
    o = o * lax.rsqrt(jnp.mean(o * o, axis=0, keepdims=True) + EPS) * sw_ref[...] * (1.0 - lambda_init)
    ot_ref[...] = o.astype(BF16)


def _diff_attn(qt, k, vt, bias, lam, sw, lambda_init, batch, seq):
    t = k.shape[0]
    nq = seq // ATT_TILE
    pairs = ATTN_HEADS // HEADS_PER_STEP
    return pl.pallas_call(
        functools.partial(_diff_attn_kernel, lambda_init),
        grid=(batch, pairs, nq),
        in_specs=_attn_specs(seq, nq) + [
            pl.BlockSpec((4, HEAD_DIM), lambda b, p, i: (0, 0)),
            pl.BlockSpec((LANES, 1), lambda b, p, i: (0, 0)),
        ],
        out_specs=pl.BlockSpec((LANES, ATT_TILE), lambda b, p, i: (p, b * nq + i)),
        out_shape=jax.ShapeDtypeStruct((D_MODEL, t), BF16),
        compiler_params=_params("parallel", "parallel", "arbitrary"),
        name="diff_attn",
    )(qt, k, vt, bias, lam, sw)


def _moba_attn_kernel(qt_ref, k_ref, vt_ref, bias_ref, blk_ref, ot_ref, kmean_ref, neg_ref):
    i = pl.program_id(2)
    tq = qt_ref.shape[1]
    nb = blk_ref.shape[0]

    @pl.when(i == 0)
    def _():
        kmean_ref[...] = _dot(blk_ref[...], k_ref[...])

    qts = _split_heads(qt_ref[...])
    km = kmean_ref[...]
    km_hi = km.astype(BF16)
    km_lo = (km - km_hi.astype(F32)).astype(BF16)
    blk_id = lax.broadcasted_iota(jnp.int32, (nb, tq), 0)
    for e in range(HEADS_PER_STEP):
        gate = _dot(km_hi, qts[e]) + _dot(km_lo, qts[e])
        gate = jnp.where(blk_id < i, gate, NEG)
        rank = jnp.zeros((nb, tq), jnp.int32)
        for c in range(nb):
            gc = gate[c:c + 1, :]
            ahead = (gc > gate) | ((gc == gate) & (c < blk_id))
            rank = rank + ahead.astype(jnp.int32)
        sel = (rank < MOBA_TOPK) & (blk_id < i)
        neg_ref[e] = jnp.where(sel, 0.0, NEG)

    sl_own = _key_tile(i)
    carry = _softmax_step(qts, k_ref[sl_own, :], vt_ref[:, sl_own], _bias_at(bias_ref, 0), _init_carry(tq))

    def body(j, carry):
        sl = _key_tile(j)
        qm = [neg_ref[e, pl.ds(j, 1), :] for e in range(HEADS_PER_STEP)]
        return _softmax_step(qts, k_ref[sl, :], vt_ref[:, sl], _bias_at(bias_ref, i - j), carry, qm)

    carry = lax.fori_loop(0, i, body, carry)
    (_, l0, a0), (_, l1, a1) = carry
    o = jnp.where(_head_row_masks()[0], a0 / l0, a1 / l1)
    ot_ref[...] = o.astype(BF16)


def _moba_attn(qt, k, vt, bias, blk, batch, seq):
    t = k.shape[0]
    nq = seq // ATT_TILE
    nb = blk.shape[0]
    pairs = ATTN_HEADS // HEADS_PER_STEP
    return pl.pallas_call(
        _moba_attn_kernel,
        grid=(batch, pairs, nq),
        in_specs=_attn_specs(seq, nq) + [pl.BlockSpec((nb, seq), lambda b, p, i: (0, 0))],
        out_specs=pl.BlockSpec((LANES, ATT_TILE), lambda b, p, i: (p, b * nq + i)),
        out_shape=jax.ShapeDtypeStruct((D_MODEL, t), BF16),
        scratch_shapes=[pltpu.VMEM((nb, LANES), F32), pltpu.VMEM((HEADS_PER_STEP, nb, ATT_TILE), F32)],
        compiler_params=_params("parallel", "parallel", "arbitrary"),
        name="moba_attn",
    )(qt, k, vt, bias, blk)


def _rotary_tables(seq):
    d = RET_DK
    inv_freq = ROPE_BASE ** (-np.arange(0, d, 2, dtype=np.float64) / d)
    ang = np.arange(seq, dtype=np.float64)[:, None] * inv_freq[None, :]
    return jnp.asarray(np.cos(ang), F32), jnp.asarray(np.sin(ang), F32)


def _retention_decay_tables():
    c_len = RET_CHUNK
    log_gamma = np.log(1.0 - 2.0 ** (-5.0 - np.arange(RET_HEADS, dtype=np.float64)))
    pos = np.arange(c_len, dtype=np.float64)
    rel = pos[:, None] - pos[None, :]
    dm = np.where(rel >= 0, np.exp(np.maximum(rel, 0.0)[None] * log_gamma[:, None, None]), 0.0)
    rs = np.exp((pos + 1.0)[None, :] * log_gamma[:, None])[:, :, None]
    ks = np.exp((c_len - 1.0 - pos)[None, :] * log_gamma[:, None])[:, :, None]
    cd = np.exp(c_len * log_gamma)[:, None, None]
    return tuple(jnp.asarray(a, F32) for a in (dm, rs, ks, cd))


def _block_mean_matrix(seq):
    nb = seq // MOBA_BLOCK
    m = (np.arange(seq)[None, :] // MOBA_BLOCK == np.arange(nb)[:, None]) / float(MOBA_BLOCK)
    return jnp.asarray(m, BF16)


def _head_group_matrix():
    g = np.arange(COL_TILE)[:, None] // HEAD_DIM == np.arange(COL_TILE)[None, :] // HEAD_DIM
    return jnp.asarray(g, BF16)


def kernel(x, rel_bias, norm1, norm2, w_up, w_down, ret_w_in, ret_w_out,
           moba_w_in, moba_q_norm, moba_k_norm, moba_w_out,
           diff_w_in, diff_q_norm, diff_k_norm, diff_lambda, diff_subln, diff_w_out):
    batch, seq, d = x.shape
    depth = norm1.shape[0]
    assert d == D_MODEL and seq % ROW_TILE == 0 and seq % ATT_TILE == 0 and seq % RET_CHUNK == 0
    assert seq % MOBA_BLOCK == 0 and MOBA_BLOCK == ATT_TILE
    t = batch * seq
    xf = x.reshape(t, d)

    bias = _bias_tiles(rel_bias.astype(F32), _bucket_index_tiles())
    grp = _head_group_matrix()
    q_scale = HEAD_DIM ** -0.5

    for i in range(depth):
        kind, j = i % N_MIXERS, i // N_MIXERS
        nw1 = norm1[i].reshape(1, d)
        nw2 = norm2[i].reshape(1, d)
        wu = w_up[i].astype(BF16)
        wd = w_down[i].astype(BF16)
        if kind == 0:
            cos, sin = _rotary_tables(seq)
            dm, rs, ks, cd = _retention_decay_tables()
            q, k, v, g = _ret_proj(xf, nw1, ret_w_in[j].astype(BF16), cos, sin, seq)
            o = _ret_core(q, k, v, dm, rs, ks, cd, batch, seq)
            xf = _out_mlp(o, g, ret_w_out[j].astype(BF16), xf, nw2, wu, wd)
        elif kind == 1:
            qn = (jnp.tile(moba_q_norm[j], ATTN_HEADS) * q_scale).reshape(1, d)
            kn = jnp.tile(moba_k_norm[j], ATTN_HEADS).reshape(1, d)
            q, k, v = _attn_proj(xf, nw1, moba_w_in[j].astype(BF16), qn, kn, grp)
            ot = _moba_attn(q.T, k, v.T, bias, _block_mean_matrix(seq), batch, seq)
            xf = _out_mlp(ot.T, None, moba_w_out[j].astype(BF16), xf, nw2, wu, wd)
        else:
            lambda_init = 0.8 - 0.6 * math.exp(-0.3 * i)
            qn = (jnp.tile(diff_q_norm[j], ATTN_HEADS) * q_scale).reshape(1, d)
            kn = jnp.tile(diff_k_norm[j], ATTN_HEADS).reshape(1, d)
            q, k, v = _attn_proj(xf, nw1, diff_w_in[j].astype(BF16), qn, kn, grp)
            ot = _diff_attn(q.T, k, v.T, bias, diff_lambda[j].astype(F32), diff_subln[j].reshape(LANES, 1),
                            lambda_init, batch, seq)
            xf = _out_mlp(ot.T, None, diff_w_out[j].astype(BF16), xf, nw2, wu, wd)
    return xf.reshape(batch, seq, d)
```

```python
import functools
import math

import numpy as np
import jax
import jax.numpy as jnp
from jax import lax
from jax.experimental import pallas as pl
from jax.experimental.pallas import tpu as pltpu

F32 = jnp.float32
BF16 = jnp.bfloat16

D_MODEL = 1024
N_MIXERS = 3
RET_HEADS = 4
RET_DK = D_MODEL // RET_HEADS
RET_DV = 2 * RET_DK
ROPE_BASE = 10000.0
ATTN_HEADS = 16
HEAD_DIM = D_MODEL // ATTN_HEADS
MOBA_BLOCK = 256
MOBA_TOPK = 3
REL_BUCKETS = 32
REL_MAX_EXACT = REL_BUCKETS // 2
REL_MAX_DISTANCE = 1024
D_FF = 4 * D_MODEL
EPS = 1e-6
NEG = -1e30

LANES = 128
BF16_SUBLANES = 16
VMEM_LIMIT_BYTES = 56 * 1024 * 1024

ROW_TILE = 512
COL_TILE = 256
FF_TILE = 1024
RET_CHUNK = 256
ATT_TILE = 256
BIAS_TILES = 6
HEADS_PER_PAIR = LANES // HEAD_DIM
PAIRS_PER_STEP = 2
HEADS_PER_STEP = HEADS_PER_PAIR * PAIRS_PER_STEP
V_ROWS = LANES + BF16_SUBLANES


def _params(*sem):
    return pltpu.CompilerParams(dimension_semantics=sem, vmem_limit_bytes=VMEM_LIMIT_BYTES)


def _rms(xf, w):
    ms = jnp.mean(xf * xf, axis=-1, keepdims=True)
    return xf * lax.rsqrt(ms + EPS) * w


def _dot(a, b):
    return jnp.dot(a, b, preferred_element_type=F32)


def _dot_nt(a, b):
    return lax.dot_general(a, b, (((1,), (1,)), ((), ())), preferred_element_type=F32)


def _dot_tn(a, b):
    return lax.dot_general(a, b, (((0,), (0,)), ((), ())), preferred_element_type=F32)


def _ret_proj_kernel(x_ref, nw_ref, w_ref, cos_ref, sin_ref, q_ref, k_ref, v_ref, g_ref):
    h = _rms(x_ref[...], nw_ref[...]).astype(BF16)
    cos = cos_ref[...]
    sin = sin_ref[...]
    half = RET_DK // 2
    for part, out_ref, scale in ((0, q_ref, 1.0), (1, k_ref, RET_DK ** -0.5)):
        for hd in range(RET_HEADS):
            base = part * RET_HEADS * RET_DK + hd * RET_DK
            acc = _dot(h, w_ref[:, base:base + RET_DK])
            x1 = acc[:, :half]
            x2 = acc[:, half:]
            r1 = x1 * cos - x2 * sin
            r2 = x1 * sin + x2 * cos
            if scale != 1.0:
                r1 = r1 * scale
                r2 = r2 * scale
            out_ref[:, hd * RET_DK:hd * RET_DK + half] = r1.astype(BF16)
            out_ref[:, hd * RET_DK + half:(hd + 1) * RET_DK] = r2.astype(BF16)
    vg_base = 2 * RET_HEADS * RET_DK
    n_v = RET_HEADS * RET_DV
    for part, out_ref in ((0, v_ref), (1, g_ref)):
        for c in range(n_v // COL_TILE):
            base = vg_base + part * n_v + c * COL_TILE
            acc = _dot(h, w_ref[:, base:base + COL_TILE])
            out_ref[:, c * COL_TILE:(c + 1) * COL_TILE] = acc.astype(BF16)


def _ret_proj(x, nw, w, cos, sin, seq):
    t = x.shape[0]
    n_in = w.shape[1]
    tiles_per_seq = seq // ROW_TILE
    row = lambda i: (i, 0)
    const = lambda i: (0, 0)
    pos = lambda i: (i % tiles_per_seq, 0)
    n_qk = RET_HEADS * RET_DK
    n_v = RET_HEADS * RET_DV
    return pl.pallas_call(
        _ret_proj_kernel,
        grid=(t // ROW_TILE,),
        in_specs=[
            pl.BlockSpec((ROW_TILE, D_MODEL), row),
            pl.BlockSpec((1, D_MODEL), const),
            pl.BlockSpec((D_MODEL, n_in), const),
            pl.BlockSpec((ROW_TILE, RET_DK // 2), pos),
            pl.BlockSpec((ROW_TILE, RET_DK // 2), pos),
        ],
        out_specs=[
            pl.BlockSpec((ROW_TILE, n_qk), row),
            pl.BlockSpec((ROW_TILE, n_qk), row),
            pl.BlockSpec((ROW_TILE, n_v), row),
            pl.BlockSpec((ROW_TILE, n_v), row),
        ],
        out_shape=[
            jax.ShapeDtypeStruct((t, n_qk), BF16),
            jax.ShapeDtypeStruct((t, n_qk), BF16),
            jax.ShapeDtypeStruct((t, n_v), BF16),
            jax.ShapeDtypeStruct((t, n_v), BF16),
        ],
        compiler_params=_params("parallel"),
        name="ret_proj",
    )(x, nw, w, cos, sin)


def _attn_proj_kernel(x_ref, nw_ref, w_ref, qn_ref, kn_ref, grp_ref, q_ref, k_ref, v_ref):
    h = _rms(x_ref[...], nw_ref[...]).astype(BF16)
    grp = grp_ref[...]
    for part, out_ref, hw_ref in ((0, q_ref, qn_ref), (1, k_ref, kn_ref), (2, v_ref, None)):
        for c in range(D_MODEL // COL_TILE):
            base = part * D_MODEL + c * COL_TILE
            acc = _dot(h, w_ref[:, base:base + COL_TILE])
            if hw_ref is not None:
                sq = acc * acc
                hi = sq.astype(BF16)
                lo = (sq - hi.astype(F32)).astype(BF16)
                ss = _dot(hi, grp) + _dot(lo, grp)
                acc = acc * lax.rsqrt(ss * (1.0 / HEAD_DIM) + EPS) * hw_ref[:, c * COL_TILE:(c + 1) * COL_TILE]
            out_ref[:, c * COL_TILE:(c + 1) * COL_TILE] = acc.astype(BF16)


def _attn_proj(x, nw, w, qn, kn, grp):
    t = x.shape[0]
    row = lambda i: (i, 0)
    const = lambda i: (0, 0)
    return pl.pallas_call(
        _attn_proj_kernel,
        grid=(t // ROW_TILE,),
        in_specs=[
            pl.BlockSpec((ROW_TILE, D_MODEL), row),
            pl.BlockSpec((1, D_MODEL), const),
            pl.BlockSpec((D_MODEL, 3 * D_MODEL), const),
            pl.BlockSpec((1, D_MODEL), const),
            pl.BlockSpec((1, D_MODEL), const),
            pl.BlockSpec((COL_TILE, COL_TILE), const),
        ],
        out_specs=[pl.BlockSpec((ROW_TILE, D_MODEL), row)] * 3,
        out_shape=[jax.ShapeDtypeStruct((t, D_MODEL), BF16)] * 3,
        compiler_params=_params("parallel"),
        name="attn_proj",
    )(x, nw, w, qn, kn, grp)


def _ret_core_kernel(q_ref, k_ref, v_ref, dm_ref, rs_ref, ks_ref, cd_ref, o_ref, state_ref):
    c_len = RET_CHUNK
    n_chunks = q_ref.shape[0] // c_len
    state_ref[...] = jnp.zeros_like(state_ref)

    def body(c, carry):
        sl = pl.ds(pl.multiple_of(c * c_len, c_len), c_len)
        q = q_ref[sl, :]
        k = k_ref[sl, :]
        v = v_ref[sl, :]
        s = _dot_nt(q, k) * dm_ref[0]
        inner = _dot(s.astype(BF16), v)
        state = state_ref[...]
        cross = _dot(q, state.astype(BF16)) * rs_ref[0]
        o = inner + cross
        o = o * lax.rsqrt(jnp.mean(o * o, axis=-1, keepdims=True) + EPS)
        o_ref[sl, :] = o.astype(BF16)
        kd = (k.astype(F32) * ks_ref[0]).astype(BF16)
        state_ref[...] = state * cd_ref[0] + _dot_tn(kd, v)
        return carry

    lax.fori_loop(0, n_chunks, body, 0)


def _ret_core(q, k, v, dm, rs, ks, cd, batch, seq):
    t = q.shape[0]
    c_len = RET_CHUNK
    tok = lambda b, h: (b, h)
    head3 = lambda b, h: (h, 0, 0)
    return pl.pallas_call(
        _ret_core_kernel,
        grid=(batch, RET_HEADS),
        in_specs=[
            pl.BlockSpec((seq, RET_DK), tok),
            pl.BlockSpec((seq, RET_DK), tok),
            pl.BlockSpec((seq, RET_DV), tok),
            pl.BlockSpec((1, c_len, c_len), head3),
            pl.BlockSpec((1, c_len, 1), head3),
            pl.BlockSpec((1, c_len, 1), head3),
            pl.BlockSpec((1, 1, 1), head3),
        ],
        out_specs=pl.BlockSpec((seq, RET_DV), tok),
        out_shape=jax.ShapeDtypeStruct((t, RET_HEADS * RET_DV), BF16),
        scratch_shapes=[pltpu.VMEM((RET_DK, RET_DV), F32)],
        compiler_params=_params("parallel", "parallel"),
        name="ret_core",
    )(q, k, v, dm, rs, ks, cd)


def _out_mlp_kernel(gated, *refs):
    if gated:
        a_ref, g_ref, wo_ref, x_ref, nw_ref, wu_ref, wd_ref, o_ref = refs
        g = g_ref[...].astype(F32)
        a = (g * (1.0 / (1.0 + jnp.exp(-g))) * a_ref[...].astype(F32)).astype(BF16)
    else:
        a_ref, wo_ref, x_ref, nw_ref, wu_ref, wd_ref, o_ref = refs
        a = a_ref[...]
    x = x_ref[...] + _dot(a, wo_ref[...])
    h = _rms(x, nw_ref[...]).astype(BF16)
    acc = x
    for c in range(D_FF // FF_TILE):
        u = _dot(h, wu_ref[:, c * FF_TILE:(c + 1) * FF_TILE])
        u = jnp.maximum(u, 0.0)
        acc = acc + _dot((u * u).astype(BF16), wd_ref[c * FF_TILE:(c + 1) * FF_TILE, :])
    o_ref[...] = acc


def _out_mlp(a, g, wo, x, nw, wu, wd):
    t = x.shape[0]
    ka = a.shape[1]
    row = lambda i: (i, 0)
    const = lambda i: (0, 0)
    gated = g is not None
    in_specs = [pl.BlockSpec((ROW_TILE, ka), row)]
    args = [a]
    if gated:
        in_specs.append(pl.BlockSpec((ROW_TILE, ka), row))
        args.append(g)
    in_specs += [
        pl.BlockSpec((ka, D_MODEL), const),
        pl.BlockSpec((ROW_TILE, D_MODEL), row),
        pl.BlockSpec((1, D_MODEL), const),
        pl.BlockSpec((D_MODEL, D_FF), const, pipeline_mode=pl.Buffered(1)),
        pl.BlockSpec((D_FF, D_MODEL), const, pipeline_mode=pl.Buffered(1)),
    ]
    args += [wo, x, nw, wu, wd]
    return pl.pallas_call(
        functools.partial(_out_mlp_kernel, gated),
        grid=(t // ROW_TILE,),
        in_specs=in_specs,
        out_specs=pl.BlockSpec((ROW_TILE, D_MODEL), row),
        out_shape=jax.ShapeDtypeStruct((t, D_MODEL), F32),
        compiler_params=_params("parallel"),
        name="out_mlp_gated" if gated else "out_mlp",
    )(*args)


def _bias_tiles_kernel(rb_ref, idx_ref, o_ref):
    head = pl.program_id(0)
    idx = idx_ref[0]
    acc = jnp.full(idx.shape, NEG, F32)
    for b in range(REL_BUCKETS):
        acc = jnp.where(idx == b, rb_ref[b, head], acc)
    o_ref[0, 0] = acc


def _bias_tiles(rel_bias, bucket_idx):
    nd, tq, tk = bucket_idx.shape
    return pl.pallas_call(
        _bias_tiles_kernel,
        grid=(ATTN_HEADS, nd),
        in_specs=[
            pl.BlockSpec(memory_space=pltpu.SMEM),
            pl.BlockSpec((1, tq, tk), lambda h, d: (d, 0, 0)),
        ],
        out_specs=pl.BlockSpec((1, 1, tq, tk), lambda h, d: (h, d, 0, 0)),
        out_shape=jax.ShapeDtypeStruct((ATTN_HEADS, nd, tq, tk), F32),
        compiler_params=_params("parallel", "parallel"),
        name="bias_tiles",
    )(rel_bias, bucket_idx)


def _rel_bucket(dist):
    n = jnp.maximum(dist, 0)
    nf = jnp.maximum(n, 1).astype(F32)
    large = REL_MAX_EXACT + (jnp.log(nf / REL_MAX_EXACT) / math.log(REL_MAX_DISTANCE / REL_MAX_EXACT)
                             * (REL_BUCKETS - REL_MAX_EXACT)).astype(jnp.int32)
    large = jnp.minimum(large, REL_BUCKETS - 1)
    return jnp.where(n < REL_MAX_EXACT, n, large)


def _bucket_index_tiles():
    r = np.arange(ATT_TILE)
    dist = (np.arange(BIAS_TILES)[:, None, None] * ATT_TILE + r[None, None, :] - r[None, :, None])
    dist = jnp.asarray(dist, jnp.int32)
    return jnp.where(dist >= 0, _rel_bucket(dist), REL_BUCKETS).astype(jnp.int32)


TAB_Q, TAB_K, TAB_DELTA, TAB_FIRST, TAB_LAST = range(5)
PIPE_LAG = 2


def _tile_schedule(nq, own_first):
    rows = []
    for i in range(nq):
        keys = ([i] + list(range(i))) if own_first else list(range(i + 1))
        for n, j in enumerate(keys):
            rows.append((i, j, min(i - j, BIAS_TILES - 1), int(n == 0), int(n == len(keys) - 1)))
    n_tiles = len(rows)
    idle_front = [(0, 0, 0, 1, 0)] * PIPE_LAG
    li, lj, ld, _, _ = rows[-1]
    idle_back = [(li, lj, ld, 1, 0)] * PIPE_LAG
    tab = np.array(idle_front + rows + idle_back, np.int32).T
    return jnp.asarray(tab), n_tiles


def _tile_slice(idx):
    return pl.ds(pl.multiple_of(idx * ATT_TILE, ATT_TILE), ATT_TILE)


def _head_row_mask(h):
    row = lax.broadcasted_iota(jnp.int32, (LANES, 1), 0)
    return (row >= h * HEAD_DIM) & (row < (h + 1) * HEAD_DIM)


def _pair_rows(pair, n=LANES):
    return slice(pair * n, (pair + 1) * n)


def _attn_kernel(kind, lambda_init, n_tiles, tab_ref, qt_ref, k_ref, vt_ref, bias_ref, *rest):
    if kind == "moba":
        blk_ref, ot_ref, qts_ref, s_ref, p_ref, acc_ref, neg_ref = rest
    else:
        lam_ref, sw_ref, ot_ref, qts_ref, s_ref, p_ref, acc_ref = rest
    heads = range(HEADS_PER_STEP)
    seq = qt_ref.shape[1]

    for e in heads:
        pair, h = divmod(e, HEADS_PER_PAIR)
        qf = qt_ref[_pair_rows(pair), :].astype(F32)
        qts_ref[e] = jnp.where(_head_row_mask(h), qf, 0.0).astype(BF16)
    s_ref[...] = jnp.zeros_like(s_ref)
    p_ref[...] = jnp.zeros_like(p_ref)
    acc_ref[...] = jnp.zeros_like(acc_ref)

    if kind == "moba":
        nb = blk_ref.shape[0]
        kmean = _dot(blk_ref[...], k_ref[...])
        km_hi = kmean.astype(BF16)
        km_lo = (kmean - km_hi.astype(F32)).astype(BF16)
        blk_id = lax.broadcasted_iota(jnp.int32, (nb, seq), 0)
        own_blk = lax.broadcasted_iota(jnp.int32, (nb, seq), 1) // MOBA_BLOCK
        past = blk_id < own_blk
        for e in heads:
            pair = e // HEADS_PER_PAIR
            q_e = qts_ref[e]
            gate = _dot(km_hi[:, _pair_rows(pair)], q_e) + _dot(km_lo[:, _pair_rows(pair)], q_e)
            gate = jnp.where(past, gate, NEG)
            rank = jnp.zeros((nb, seq), jnp.int32)
            for c in range(nb):
                gc = gate[c:c + 1, :]
                tie = jnp.where(blk_id > c, 1, 0)
                rank = rank + jnp.where(gc > gate, 1, jnp.where(gc == gate, tie, 0))
            chosen = jnp.where(rank < MOBA_TOPK, jnp.where(past, 1, 0), 0)
            keep = jnp.maximum(chosen, jnp.where(blk_id == own_blk, 1, 0))
            neg_ref[e] = jnp.where(keep == 1, 0.0, NEG)

    def body(t, carry):
        m_prev, alpha_prev = carry

        jc = tab_ref[TAB_K, t]
        for e in heads:
            pair = e // HEADS_PER_PAIR
            vt = vt_ref[_pair_rows(pair, V_ROWS), _tile_slice(jc)]
            acc_ref[e] = alpha_prev[e] * acc_ref[e] + _dot(vt, p_ref[e])

        ib = tab_ref[TAB_Q, t + 1]
        jb = tab_ref[TAB_K, t + 1]
        db = tab_ref[TAB_DELTA, t + 1]
        first = tab_ref[TAB_FIRST, t + 1] != 0
        m_new, alpha_new = [], []
        for e in heads:
            s = s_ref[e] + bias_ref[e, db]
            if kind == "moba":
                s = s + neg_ref[e, pl.ds(jb, 1), _tile_slice(ib)]
            m_in = jnp.where(first, -jnp.inf, m_prev[e])
            m_e = jnp.maximum(m_in, jnp.max(s, axis=0, keepdims=True))
            p_ref[e] = jnp.exp(s - m_e).astype(BF16)
            alpha_new.append(jnp.exp(m_in - m_e))
            m_new.append(m_e)

        ia = tab_ref[TAB_Q, t + 2]
        ja = tab_ref[TAB_K, t + 2]
        kt = k_ref[_tile_slice(ja), :]
        for e in heads:
            pair = e // HEADS_PER_PAIR
            s_ref[e] = _dot(kt[:, _pair_rows(pair)], qts_ref[e, :, _tile_slice(ia)])

        @pl.when(tab_ref[TAB_LAST, t] != 0)
        def _():
            cols = _tile_slice(tab_ref[TAB_Q, t])
            for pair in range(PAIRS_PER_STEP):
                outs = []
                for h in range(HEADS_PER_PAIR):
                    e = pair * HEADS_PER_PAIR + h
                    outs.append(acc_ref[e, :LANES, :] * (1.0 / acc_ref[e, LANES:LANES + 1, :]))
                if kind == "moba":
                    o = jnp.where(_head_row_mask(0), outs[0], outs[1])
                else:
                    lam = lam_ref[...]
                    lam_full = (jnp.exp(jnp.sum(lam[0:1] * lam[1:2], axis=-1, keepdims=True))
                                - jnp.exp(jnp.sum(lam[2:3] * lam[3:4], axis=-1, keepdims=True)) + lambda_init)
                    o = outs[0] - lam_full * outs[1]
                    o = (o * lax.rsqrt(jnp.mean(o * o, axis=0, keepdims=True) + EPS)
                         * sw_ref[...] * (1.0 - lambda_init))
                ot_ref[_pair_rows(pair), cols] = o.astype(BF16)

        return tuple(m_new), tuple(alpha_new)

    zeros = tuple(jnp.zeros((1, ATT_TILE), F32) for _ in heads)
    lax.fori_loop(0, n_tiles + PIPE_LAG, body, (zeros, zeros))


def _attention(kind, qt, k, vt_ext, bias, extras, lambda_init, batch, seq):
    t = k.shape[0]
    nq = seq // ATT_TILE
    tab, n_tiles = _tile_schedule(nq, own_first=(kind == "moba"))
    groups = ATTN_HEADS // HEADS_PER_STEP
    rows = LANES * PAIRS_PER_STEP
    in_specs = [
        pl.BlockSpec(memory_space=pltpu.SMEM),
        pl.BlockSpec((rows, seq), lambda b, g: (g, b)),
        pl.BlockSpec((seq, rows), lambda b, g: (b, g)),
        pl.BlockSpec((V_ROWS * PAIRS_PER_STEP, seq), lambda b, g: (g, b)),
        pl.BlockSpec((HEADS_PER_STEP, BIAS_TILES, ATT_TILE, ATT_TILE), lambda b, g: (g, 0, 0, 0)),
    ]
    scratch = [
        pltpu.VMEM((HEADS_PER_STEP, LANES, seq), BF16),
        pltpu.VMEM((HEADS_PER_STEP, ATT_TILE, ATT_TILE), F32),
        pltpu.VMEM((HEADS_PER_STEP, ATT_TILE, ATT_TILE), BF16),
        pltpu.VMEM((HEADS_PER_STEP, V_ROWS, ATT_TILE), F32),
    ]
    if kind == "moba":
        (blk,) = extras
        nb = blk.shape[0]
        in_specs.append(pl.BlockSpec((nb, seq), lambda b, g: (0, 0)))
        scratch.append(pltpu.VMEM((HEADS_PER_STEP, nb, seq), F32))
    else:
        lam, sw = extras
        in_specs += [
            pl.BlockSpec((4, HEAD_DIM), lambda b, g: (0, 0)),
            pl.BlockSpec((LANES, 1), lambda b, g: (0, 0)),
        ]
    return pl.pallas_call(
        functools.partial(_attn_kernel, kind, lambda_init, n_tiles),
        grid=(batch, groups),
        in_specs=in_specs,
        out_specs=pl.BlockSpec((rows, seq), lambda b, g: (g, b)),
        out_shape=jax.ShapeDtypeStruct((D_MODEL, t), BF16),
        scratch_shapes=scratch,
        compiler_params=_params("parallel", "parallel"),
        name=kind + "_attn",
    )(tab, qt, k, vt_ext, bias, *extras)


def _values_t_with_ones(v):
    t = v.shape[0]
    pairs = D_MODEL // LANES
    vt = v.T.reshape(pairs, LANES, t)
    ones = jnp.ones((pairs, V_ROWS - LANES, t), v.dtype)
    return jnp.concatenate([vt, ones], axis=1).reshape(pairs * V_ROWS, t)


def _rotary_tables(seq):
    d = RET_DK
    inv_freq = ROPE_BASE ** (-np.arange(0, d, 2, dtype=np.float64) / d)
    ang = np.arange(seq, dtype=np.float64)[:, None] * inv_freq[None, :]
    return jnp.asarray(np.cos(ang), F32), jnp.asarray(np.sin(ang), F32)


def _retention_decay_tables():
    c_len = RET_CHUNK
    log_gamma = np.log(1.0 - 2.0 ** (-5.0 - np.arange(RET_HEADS, dtype=np.float64)))
    pos = np.arange(c_len, dtype=np.float64)
    rel = pos[:, None] - pos[None, :]
    dm = np.where(rel >= 0, np.exp(np.maximum(rel, 0.0)[None] * log_gamma[:, None, None]), 0.0)
    rs = np.exp((pos + 1.0)[None, :] * log_gamma[:, None])[:, :, None]
    ks = np.exp((c_len - 1.0 - pos)[None, :] * log_gamma[:, None])[:, :, None]
    cd = np.exp(c_len * log_gamma)[:, None, None]
    return tuple(jnp.asarray(a, F32) for a in (dm, rs, ks, cd))


def _block_mean_matrix(seq):
    nb = seq // MOBA_BLOCK
    m = (np.arange(seq)[None, :] // MOBA_BLOCK == np.arange(nb)[:, None]) / float(MOBA_BLOCK)
    return jnp.asarray(m, BF16)


def _head_group_matrix():
    g = np.arange(COL_TILE)[:, None] // HEAD_DIM == np.arange(COL_TILE)[None, :] // HEAD_DIM
    return jnp.asarray(g, BF16)


def kernel(x, rel_bias, norm1, norm2, w_up, w_down, ret_w_in, ret_w_out,
           moba_w_in, moba_q_norm, moba_k_norm, moba_w_out,
           diff_w_in, diff_q_norm, diff_k_norm, diff_lambda, diff_subln, diff_w_out):
    batch, seq, d = x.shape
    depth = norm1.shape[0]
    assert d == D_MODEL and seq % ROW_TILE == 0 and seq % ATT_TILE == 0 and seq % RET_CHUNK == 0
    assert seq % MOBA_BLOCK == 0 and MOBA_BLOCK == ATT_TILE
    t = batch * seq
    xf = x.reshape(t, d)

    bias = _bias_tiles(rel_bias.astype(F32), _bucket_index_tiles())
    grp = _head_group_matrix()
    q_scale = HEAD_DIM ** -0.5

    for i in range(depth):
        kind, j = i % N_MIXERS, i // N_MIXERS
        nw1 = norm1[i].reshape(1, d)
        nw2 = norm2[i].reshape(1, d)
        wu = w_up[i].astype(BF16)
        wd = w_down[i].astype(BF16)
        if kind == 0:
            cos, sin = _rotary_tables(seq)
            dm, rs, ks, cd = _retention_decay_tables()
            q, k, v, g = _ret_proj(xf, nw1, ret_w_in[j].astype(BF16), cos, sin, seq)
            o = _ret_core(q, k, v, dm, rs, ks, cd, batch, seq)
            xf = _out_mlp(o, g, ret_w_out[j].astype(BF16), xf, nw2, wu, wd)
        elif kind == 1:
            qn = (jnp.tile(moba_q_norm[j], ATTN_HEADS) * q_scale).reshape(1, d)
            kn = jnp.tile(moba_k_norm[j], ATTN_HEADS).reshape(1, d)
            q, k, v = _attn_proj(xf, nw1, moba_w_in[j].astype(BF16), qn, kn, grp)
            ot = _attention("moba", q.T, k, _values_t_with_ones(v), bias, (_block_mean_matrix(seq),),
                            0.0, batch, seq)
            xf = _out_mlp(ot.T, None, moba_w_out[j].astype(BF16), xf, nw2, wu, wd)
        else:
            lambda_init = 0.8 - 0.6 * math.exp(-0.3 * i)
            qn = (jnp.tile(diff_q_norm[j], ATTN_HEADS) * q_scale).reshape(1, d)
            kn = jnp.tile(diff_k_norm[j], ATTN_HEADS).reshape(1, d)
            q, k, v = _attn_proj(xf, nw1, diff_w_in[j].astype(BF16), qn, kn, grp)
            extras = (diff_lambda[j].astype(F32), diff_subln[j].reshape(LANES, 1))
            ot = _attention("diff", q.T, k, _values_t_with_ones(v), bias, extras, lambda_init, batch, seq)
            xf = _out_mlp(ot.T, None, diff_w_out[j].astype(BF16), xf, nw2, wu, wd)
    return xf.reshape(batch, seq, d)
```

```python
import functools
import math

import numpy as np
import jax
import jax.numpy as jnp
from jax import lax
from jax.experimental import pallas as pl
from jax.experimental.pallas import tpu as pltpu

F32 = jnp.float32
BF16 = jnp.bfloat16

D_MODEL = 1024
N_MIXERS = 3
RET_HEADS = 4
RET_DK = D_MODEL // RET_HEADS
RET_DV = 2 * RET_DK
ROPE_BASE = 10000.0
ATTN_HEADS = 16
HEAD_DIM = D_MODEL // ATTN_HEADS
MOBA_BLOCK = 256
MOBA_TOPK = 3
REL_BUCKETS = 32
REL_MAX_EXACT = REL_BUCKETS // 2
REL_MAX_DISTANCE = 1024
D_FF = 4 * D_MODEL
EPS = 1e-6
NEG = -1e30
LOG2E = math.log2(math.e)

LANES = 128
BF16_SUBLANES = 16
VMEM_LIMIT_BYTES = 56 * 1024 * 1024

ROW_TILE = 512
COL_TILE = 256
FF_TILE = 1024
RET_CHUNK = 256
ATT_TILE = 256
BIAS_TILES = 6
HEADS_PER_PAIR = LANES // HEAD_DIM
PAIRS_PER_STEP = 2
HEADS_PER_STEP = HEADS_PER_PAIR * PAIRS_PER_STEP
V_ROWS = LANES + BF16_SUBLANES


def _params(*sem):
    return pltpu.CompilerParams(dimension_semantics=sem, vmem_limit_bytes=VMEM_LIMIT_BYTES)


def _rms(xf, w):
    ms = jnp.mean(xf * xf, axis=-1, keepdims=True)
    return xf * lax.rsqrt(ms + EPS) * w


def _dot(a, b):
    return jnp.dot(a, b, preferred_element_type=F32)


def _dot_nt(a, b):
    return lax.dot_general(a, b, (((1,), (1,)), ((), ())), preferred_element_type=F32)


def _dot_tn(a, b):
    return lax.dot_general(a, b, (((0,), (0,)), ((), ())), preferred_element_type=F32)


def _ret_proj_kernel(x_ref, nw_ref, w_ref, cos_ref, sin_ref, q_ref, k_ref, v_ref, g_ref):
    h = _rms(x_ref[...], nw_ref[...]).astype(BF16)
    cos = cos_ref[...]
    sin = sin_ref[...]
    half = RET_DK // 2
    for part, out_ref, scale in ((0, q_ref, 1.0), (1, k_ref, RET_DK ** -0.5)):
        for hd in range(RET_HEADS):
            base = part * RET_HEADS * RET_DK + hd * RET_DK
            acc = _dot(h, w_ref[:, base:base + RET_DK])
            x1 = acc[:, :half]
            x2 = acc[:, half:]
            r1 = x1 * cos - x2 * sin
            r2 = x1 * sin + x2 * cos
            if scale != 1.0:
                r1 = r1 * scale
                r2 = r2 * scale
            out_ref[:, hd * RET_DK:hd * RET_DK + half] = r1.astype(BF16)
            out_ref[:, hd * RET_DK + half:(hd + 1) * RET_DK] = r2.astype(BF16)
    vg_base = 2 * RET_HEADS * RET_DK
    n_v = RET_HEADS * RET_DV
    for part, out_ref in ((0, v_ref), (1, g_ref)):
        for c in range(n_v // COL_TILE):
            base = vg_base + part * n_v + c * COL_TILE
            acc = _dot(h, w_ref[:, base:base + COL_TILE])
            out_ref[:, c * COL_TILE:(c + 1) * COL_TILE] = acc.astype(BF16)


def _ret_proj(x, nw, w, cos, sin, seq):
    t = x.shape[0]
    n_in = w.shape[1]
    tiles_per_seq = seq // ROW_TILE
    row = lambda i: (i, 0)
    const = lambda i: (0, 0)
    pos = lambda i: (i % tiles_per_seq, 0)
    n_qk = RET_HEADS * RET_DK
    n_v = RET_HEADS * RET_DV
    return pl.pallas_call(
        _ret_proj_kernel,
        grid=(t // ROW_TILE,),
        in_specs=[
            pl.BlockSpec((ROW_TILE, D_MODEL), row),
            pl.BlockSpec((1, D_MODEL), const),
            pl.BlockSpec((D_MODEL, n_in), const),
            pl.BlockSpec((ROW_TILE, RET_DK // 2), pos),
            pl.BlockSpec((ROW_TILE, RET_DK // 2), pos),
        ],
        out_specs=[
            pl.BlockSpec((ROW_TILE, n_qk), row),
            pl.BlockSpec((ROW_TILE, n_qk), row),
            pl.BlockSpec((ROW_TILE, n_v), row),
            pl.BlockSpec((ROW_TILE, n_v), row),
        ],
        out_shape=[
            jax.ShapeDtypeStruct((t, n_qk), BF16),
            jax.ShapeDtypeStruct((t, n_qk), BF16),
            jax.ShapeDtypeStruct((t, n_v), BF16),
            jax.ShapeDtypeStruct((t, n_v), BF16),
        ],
        compiler_params=_params("parallel"),
        name="ret_proj",
    )(x, nw, w, cos, sin)


def _attn_proj_kernel(x_ref, nw_ref, w_ref, qn_ref, kn_ref, grp_ref, q_ref, k_ref, v_ref):
    h = _rms(x_ref[...], nw_ref[...]).astype(BF16)
    grp = grp_ref[...]
    for part, out_ref, hw_ref in ((0, q_ref, qn_ref), (1, k_ref, kn_ref), (2, v_ref, None)):
        for c in range(D_MODEL // COL_TILE):
            base = part * D_MODEL + c * COL_TILE
            acc = _dot(h, w_ref[:, base:base + COL_TILE])
            if hw_ref is not None:
                ms = _dot((acc * acc).astype(BF16), grp)
                acc = acc * lax.rsqrt(ms + EPS) * hw_ref[:, c * COL_TILE:(c + 1) * COL_TILE]
            out_ref[:, c * COL_TILE:(c + 1) * COL_TILE] = acc.astype(BF16)


def _attn_proj(x, nw, w, qn, kn, grp):
    t = x.shape[0]
    row = lambda i: (i, 0)
    const = lambda i: (0, 0)
    return pl.pallas_call(
        _attn_proj_kernel,
        grid=(t // ROW_TILE,),
        in_specs=[
            pl.BlockSpec((ROW_TILE, D_MODEL), row),
            pl.BlockSpec((1, D_MODEL), const),
            pl.BlockSpec((D_MODEL, 3 * D_MODEL), const),
            pl.BlockSpec((1, D_MODEL), const),
            pl.BlockSpec((1, D_MODEL), const),
            pl.BlockSpec((COL_TILE, COL_TILE), const),
        ],
        out_specs=[pl.BlockSpec((ROW_TILE, D_MODEL), row)] * 3,
        out_shape=[jax.ShapeDtypeStruct((t, D_MODEL), BF16)] * 3,
        compiler_params=_params("parallel"),
        name="attn_proj",
    )(x, nw, w, qn, kn, grp)


def _ret_core_kernel(q_ref, k_ref, v_ref, dm_ref, rs_ref, ks_ref, cd_ref, o_ref, state_ref):
    c_len = RET_CHUNK
    n_chunks = q_ref.shape[0] // c_len
    state_ref[...] = jnp.zeros_like(state_ref)

    def body(c, carry):
        sl = pl.ds(pl.multiple_of(c * c_len, c_len), c_len)
        q = q_ref[sl, :]
        k = k_ref[sl, :]
        v = v_ref[sl, :]
        s = _dot_nt(q, k) * dm_ref[0]
        inner = _dot(s.astype(BF16), v)
        state = state_ref[...]
        cross = _dot(q, state.astype(BF16)) * rs_ref[0]
        o = inner + cross
        o = o * lax.rsqrt(jnp.mean(o * o, axis=-1, keepdims=True) + EPS)
        o_ref[sl, :] = o.astype(BF16)
        kd = (k.astype(F32) * ks_ref[0]).astype(BF16)
        state_ref[...] = state * cd_ref[0] + _dot_tn(kd, v)
        return carry

    lax.fori_loop(0, n_chunks, body, 0)


def _ret_core(q, k, v, dm, rs, ks, cd, batch, seq):
    t = q.shape[0]
    c_len = RET_CHUNK
    tok = lambda b, h: (b, h)
    head3 = lambda b, h: (h, 0, 0)
    return pl.pallas_call(
        _ret_core_kernel,
        grid=(batch, RET_HEADS),
        in_specs=[
            pl.BlockSpec((seq, RET_DK), tok),
            pl.BlockSpec((seq, RET_DK), tok),
            pl.BlockSpec((seq, RET_DV), tok),
            pl.BlockSpec((1, c_len, c_len), head3),
            pl.BlockSpec((1, c_len, 1), head3),
            pl.BlockSpec((1, c_len, 1), head3),
            pl.BlockSpec((1, 1, 1), head3),
        ],
        out_specs=pl.BlockSpec((seq, RET_DV), tok),
        out_shape=jax.ShapeDtypeStruct((t, RET_HEADS * RET_DV), BF16),
        scratch_shapes=[pltpu.VMEM((RET_DK, RET_DV), F32)],
        compiler_params=_params("parallel", "parallel"),
        name="ret_core",
    )(q, k, v, dm, rs, ks, cd)


def _out_mlp_kernel(gated, *refs):
    if gated:
        a_ref, g_ref, wo_ref, x_ref, nw_ref, wu_ref, wd_ref, o_ref = refs
        g = g_ref[...].astype(F32)
        a = (g * (1.0 / (1.0 + jnp.exp(-g))) * a_ref[...].astype(F32)).astype(BF16)
    else:
        a_ref, wo_ref, x_ref, nw_ref, wu_ref, wd_ref, o_ref = refs
        a = a_ref[...]
    x = x_ref[...] + _dot(a, wo_ref[...])
    h = _rms(x, nw_ref[...]).astype(BF16)
    acc = x
    for c in range(D_FF // FF_TILE):
        u = _dot(h, wu_ref[:, c * FF_TILE:(c + 1) * FF_TILE])
        u = jnp.maximum(u, 0.0)
        acc = acc + _dot((u * u).astype(BF16), wd_ref[c * FF_TILE:(c + 1) * FF_TILE, :])
    o_ref[...] = acc


def _out_mlp(a, g, wo, x, nw, wu, wd):
    t = x.shape[0]
    ka = a.shape[1]
    row = lambda i: (i, 0)
    const = lambda i: (0, 0)
    gated = g is not None
    in_specs = [pl.BlockSpec((ROW_TILE, ka), row)]
    args = [a]
    if gated:
        in_specs.append(pl.BlockSpec((ROW_TILE, ka), row))
        args.append(g)
    in_specs += [
        pl.BlockSpec((ka, D_MODEL), const),
        pl.BlockSpec((ROW_TILE, D_MODEL), row),
        pl.BlockSpec((1, D_MODEL), const),
        pl.BlockSpec((D_MODEL, D_FF), const, pipeline_mode=pl.Buffered(1)),
        pl.BlockSpec((D_FF, D_MODEL), const, pipeline_mode=pl.Buffered(1)),
    ]
    args += [wo, x, nw, wu, wd]
    return pl.pallas_call(
        functools.partial(_out_mlp_kernel, gated),
        grid=(t // ROW_TILE,),
        in_specs=in_specs,
        out_specs=pl.BlockSpec((ROW_TILE, D_MODEL), row),
        out_shape=jax.ShapeDtypeStruct((t, D_MODEL), F32),
        compiler_params=_params("parallel"),
        name="out_mlp_gated" if gated else "out_mlp",
    )(*args)


def _bias_tiles_kernel(rb_ref, idx_ref, o_ref):
    head = pl.program_id(0)
    idx = idx_ref[0]
    acc = jnp.full(idx.shape, NEG, F32)
    for b in range(REL_BUCKETS):
        acc = jnp.where(idx == b, rb_ref[b, head] * LOG2E, acc)
    o_ref[0, 0] = acc


def _bias_tiles(rel_bias, bucket_idx):
    nd, tq, tk = bucket_idx.shape
    return pl.pallas_call(
        _bias_tiles_kernel,
        grid=(ATTN_HEADS, nd),
        in_specs=[
            pl.BlockSpec(memory_space=pltpu.SMEM),
            pl.BlockSpec((1, tq, tk), lambda h, d: (d, 0, 0)),
        ],
        out_specs=pl.BlockSpec((1, 1, tq, tk), lambda h, d: (h, d, 0, 0)),
        out_shape=jax.ShapeDtypeStruct((ATTN_HEADS, nd, tq, tk), F32),
        compiler_params=_params("parallel", "parallel"),
        name="bias_tiles",
    )(rel_bias, bucket_idx)


def _rel_bucket(dist):
    n = jnp.maximum(dist, 0)
    nf = jnp.maximum(n, 1).astype(F32)
    large = REL_MAX_EXACT + (jnp.log(nf / REL_MAX_EXACT) / math.log(REL_MAX_DISTANCE / REL_MAX_EXACT)
                             * (REL_BUCKETS - REL_MAX_EXACT)).astype(jnp.int32)
    large = jnp.minimum(large, REL_BUCKETS - 1)
    return jnp.where(n < REL_MAX_EXACT, n, large)


def _bucket_index_tiles():
    r = np.arange(ATT_TILE)
    dist = (np.arange(BIAS_TILES)[:, None, None] * ATT_TILE + r[None, None, :] - r[None, :, None])
    dist = jnp.asarray(dist, jnp.int32)
    return jnp.where(dist >= 0, _rel_bucket(dist), REL_BUCKETS).astype(jnp.int32)


TAB_Q, TAB_K, TAB_DELTA, TAB_FIRST, TAB_ACC = range(5)
PIPE_LAG = 2
PIPE_UNROLL = 4
QUERY_HALVES = tuple(slice(h * LANES, (h + 1) * LANES) for h in range(ATT_TILE // LANES))


def _tile_schedule(nq, own_first):
    rows = []
    for i in range(nq):
        keys = ([i] + list(range(i))) if own_first else list(range(i + 1))
        for n, j in enumerate(keys):
            rows.append((i, j, min(i - j, BIAS_TILES - 1), int(n == 0), i))
    n_iters = -(-(len(rows) + PIPE_LAG) // PIPE_UNROLL) * PIPE_UNROLL
    idle = (0, 0, 0, 1, nq)
    cols = [idle] * PIPE_LAG + rows
    cols += [idle] * (n_iters + PIPE_LAG - len(cols))
    return jnp.asarray(np.array(cols, np.int32).T), n_iters


def _tile_slice(idx):
    return pl.ds(pl.multiple_of(idx * ATT_TILE, ATT_TILE), ATT_TILE)


def _head_row_mask(h):
    row = lax.broadcasted_iota(jnp.int32, (LANES, 1), 0)
    return (row >= h * HEAD_DIM) & (row < (h + 1) * HEAD_DIM)


def _pair_rows(pair, n=LANES):
    return slice(pair * n, (pair + 1) * n)


def _attn_kernel(kind, lambda_init, n_iters, tab_ref, qt_ref, k_ref, vt_ref, bias_ref, *rest):
    if kind == "moba":
        blk_ref, ot_ref, qts_ref, s_ref, p_ref, acc_ref, neg_ref = rest
    else:
        lam_ref, sw_ref, ot_ref, qts_ref, s_ref, p_ref, acc_ref = rest
    heads = range(HEADS_PER_STEP)
    seq = qt_ref.shape[1]

    for e in heads:
        pair, h = divmod(e, HEADS_PER_PAIR)
        qf = qt_ref[_pair_rows(pair), :].astype(F32)
        qts_ref[e] = jnp.where(_head_row_mask(h), qf, 0.0).astype(BF16)
    s_ref[...] = jnp.zeros_like(s_ref)
    p_ref[...] = jnp.zeros_like(p_ref)
    acc_ref[...] = jnp.zeros_like(acc_ref)

    if kind == "moba":
        nb = blk_ref.shape[0]
        kmean = _dot(blk_ref[...], k_ref[...])
        km_hi = kmean.astype(BF16)
        km_lo = (kmean - km_hi.astype(F32)).astype(BF16)
        blk_id = lax.broadcasted_iota(jnp.int32, (nb, seq), 0)
        own_blk = lax.broadcasted_iota(jnp.int32, (nb, seq), 1) // MOBA_BLOCK
        past = blk_id < own_blk
        for e in heads:
            pair = e // HEADS_PER_PAIR
            q_e = qts_ref[e]
            gate = _dot(km_hi[:, _pair_rows(pair)], q_e) + _dot(km_lo[:, _pair_rows(pair)], q_e)
            gate = jnp.where(past, gate, NEG)
            rank = jnp.zeros((nb, seq), jnp.int32)
            for c in range(nb):
                gc = gate[c:c + 1, :]
                tie = jnp.where(blk_id > c, 1, 0)
                rank = rank + jnp.where(gc > gate, 1, jnp.where(gc == gate, tie, 0))
            chosen = jnp.where(rank < MOBA_TOPK, jnp.where(past, 1, 0), 0)
            keep = jnp.maximum(chosen, jnp.where(blk_id == own_blk, 1, 0))
            neg_ref[e] = jnp.where(keep == 1, 0.0, NEG)

    def step(t, cur, carry):
        nxt = 1 - cur
        m_prev, alpha_prev, mtile_prev = carry
        ic = tab_ref[TAB_ACC, t]
        jc = tab_ref[TAB_K, t]
        ib = tab_ref[TAB_Q, t + 1]
        jb = tab_ref[TAB_K, t + 1]
        first = tab_ref[TAB_FIRST, t + 1] != 0
        ia = tab_ref[TAB_Q, t + 2]
        ja = tab_ref[TAB_K, t + 2]
        da = tab_ref[TAB_DELTA, t + 2]
        m_new, alpha_new, mtile_new = [], [], []
        for e in heads:
            pair = e // HEADS_PER_PAIR

            vt = vt_ref[_pair_rows(pair, V_ROWS), _tile_slice(jc)]
            pv = _dot(vt, p_ref[cur, e])
            for hf, lanes in enumerate(QUERY_HALVES):
                acc_ref[ic, e, :, lanes] = alpha_prev[e][hf] * acc_ref[ic, e, :, lanes] + pv[:, lanes]

            kt = k_ref[_tile_slice(ja), _pair_rows(pair)]
            sb = _dot(kt, qts_ref[e, :, _tile_slice(ia)]) + bias_ref[e, da]
            s_ref[nxt, e] = sb
            mtile_new.append(tuple(jnp.max(sb[:, lanes], axis=0, keepdims=True) for lanes in QUERY_HALVES))

            m_parts, alpha_parts = [], []
            if kind == "moba":
                neg_row = neg_ref[e, pl.ds(jb, 1), _tile_slice(ib)]
            for hf, lanes in enumerate(QUERY_HALVES):
                s = s_ref[cur, e, :, lanes]
                m_in = jnp.where(first, -jnp.inf, m_prev[e][hf])
                m_tile = mtile_prev[e][hf]
                if kind == "moba":
                    neg = neg_row[:, lanes]
                    m_e = jnp.maximum(m_in, m_tile + neg)
                    shift = m_e - neg
                else:
                    m_e = jnp.maximum(m_in, m_tile)
                    shift = m_e
                p_ref[nxt, e, :, lanes] = jnp.exp2(s - shift).astype(BF16)
                alpha_parts.append(jnp.exp2(m_in - m_e))
                m_parts.append(m_e)
            alpha_new.append(tuple(alpha_parts))
            m_new.append(tuple(m_parts))

        return tuple(m_new), tuple(alpha_new), tuple(mtile_new)

    zeros = tuple(tuple(jnp.zeros((1, LANES), F32) for _ in QUERY_HALVES) for _ in heads)

    def body(u, carry):
        for r in range(PIPE_UNROLL):
            carry = step(PIPE_UNROLL * u + r, r % 2, carry)
        return carry

    lax.fori_loop(0, n_iters // PIPE_UNROLL, body, (zeros, zeros, zeros))

    def emit(i, carry):
        cols = _tile_slice(i)
        for pair in range(PAIRS_PER_STEP):
            outs = []
            for h in range(HEADS_PER_PAIR):
                e = pair * HEADS_PER_PAIR + h
                outs.append(acc_ref[i, e, :LANES, :] * (1.0 / acc_ref[i, e, LANES:LANES + 1, :]))
            if kind == "moba":
                o = jnp.where(_head_row_mask(0), outs[0], outs[1])
            else:
                lam = lam_ref[...]
                lam_full = (jnp.exp(jnp.sum(lam[0:1] * lam[1:2], axis=-1, keepdims=True))
                            - jnp.exp(jnp.sum(lam[2:3] * lam[3:4], axis=-1, keepdims=True)) + lambda_init)
                o = outs[0] - lam_full * outs[1]
                o = (o * lax.rsqrt(jnp.mean(o * o, axis=0, keepdims=True) + EPS)
                     * sw_ref[...] * (1.0 - lambda_init))
            ot_ref[_pair_rows(pair), cols] = o.astype(BF16)
        return carry

    lax.fori_loop(0, seq // ATT_TILE, emit, 0)


def _attention(kind, qt, k, vt_ext, bias, extras, lambda_init, batch, seq):
    t = k.shape[0]
    nq = seq // ATT_TILE
    tab, n_iters = _tile_schedule(nq, own_first=(kind == "moba"))
    groups = ATTN_HEADS // HEADS_PER_STEP
    rows = LANES * PAIRS_PER_STEP
    in_specs = [
        pl.BlockSpec(memory_space=pltpu.SMEM),
        pl.BlockSpec((rows, seq), lambda b, g: (g, b)),
        pl.BlockSpec((seq, rows), lambda b, g: (b, g)),
        pl.BlockSpec((V_ROWS * PAIRS_PER_STEP, seq), lambda b, g: (g, b)),
        pl.BlockSpec((HEADS_PER_STEP, BIAS_TILES, ATT_TILE, ATT_TILE), lambda b, g: (g, 0, 0, 0)),
    ]
    scratch = [
        pltpu.VMEM((HEADS_PER_STEP, LANES, seq), BF16),
        pltpu.VMEM((2, HEADS_PER_STEP, ATT_TILE, ATT_TILE), F32),
        pltpu.VMEM((2, HEADS_PER_STEP, ATT_TILE, ATT_TILE), BF16),
        pltpu.VMEM((nq + 1, HEADS_PER_STEP, V_ROWS, ATT_TILE), F32),
    ]
    if kind == "moba":
        (blk,) = extras
        nb = blk.shape[0]
        in_specs.append(pl.BlockSpec((nb, seq), lambda b, g: (0, 0)))
        scratch.append(pltpu.VMEM((HEADS_PER_STEP, nb, seq), F32))
    else:
        lam, sw = extras
        in_specs += [
            pl.BlockSpec((4, HEAD_DIM), lambda b, g: (0, 0)),
            pl.BlockSpec((LANES, 1), lambda b, g: (0, 0)),
        ]
    return pl.pallas_call(
        functools.partial(_attn_kernel, kind, lambda_init, n_iters),
        grid=(batch, groups),
        in_specs=in_specs,
        out_specs=pl.BlockSpec((rows, seq), lambda b, g: (g, b)),
        out_shape=jax.ShapeDtypeStruct((D_MODEL, t), BF16),
        scratch_shapes=scratch,
        compiler_params=_params("parallel", "parallel"),
        name=kind + "_attn",
    )(tab, qt, k, vt_ext, bias, *extras)


def _values_t_with_ones(v):
    t = v.shape[0]
    pairs = D_MODEL // LANES
    vt = v.T.reshape(pairs, LANES, t)
    ones = jnp.ones((pairs, V_ROWS - LANES, t), v.dtype)
    return jnp.concatenate([vt, ones], axis=1).reshape(pairs * V_ROWS, t)


def _rotary_tables(seq):
    d = RET_DK
    inv_freq = ROPE_BASE ** (-np.arange(0, d, 2, dtype=np.float64) / d)
    ang = np.arange(seq, dtype=np.float64)[:, None] * inv_freq[None, :]
    return jnp.asarray(np.cos(ang), F32), jnp.asarray(np.sin(ang), F32)


def _retention_decay_tables():
    c_len = RET_CHUNK
    log_gamma = np.log(1.0 - 2.0 ** (-5.0 - np.arange(RET_HEADS, dtype=np.float64)))
    pos = np.arange(c_len, dtype=np.float64)
    rel = pos[:, None] - pos[None, :]
    dm = np.where(rel >= 0, np.exp(np.maximum(rel, 0.0)[None] * log_gamma[:, None, None]), 0.0)
    rs = np.exp((pos + 1.0)[None, :] * log_gamma[:, None])[:, :, None]
    ks = np.exp((c_len - 1.0 - pos)[None, :] * log_gamma[:, None])[:, :, None]
    cd = np.exp(c_len * log_gamma)[:, None, None]
    return tuple(jnp.asarray(a, F32) for a in (dm, rs, ks, cd))


def _block_mean_matrix(seq):
    nb = seq // MOBA_BLOCK
    m = (np.arange(seq)[None, :] // MOBA_BLOCK == np.arange(nb)[:, None]) / float(MOBA_BLOCK)
    return jnp.asarray(m, BF16)


def _head_group_matrix():
    g = np.arange(COL_TILE)[:, None] // HEAD_DIM == np.arange(COL_TILE)[None, :] // HEAD_DIM
    return jnp.asarray(g / float(HEAD_DIM), BF16)


def kernel(x, rel_bias, norm1, norm2, w_up, w_down, ret_w_in, ret_w_out,
           moba_w_in, moba_q_norm, moba_k_norm, moba_w_out,
           diff_w_in, diff_q_norm, diff_k_norm, diff_lambda, diff_subln, diff_w_out):
    batch, seq, d = x.shape
    depth = norm1.shape[0]
    assert d == D_MODEL and seq % ROW_TILE == 0 and seq % ATT_TILE == 0 and seq % RET_CHUNK == 0
    assert seq % MOBA_BLOCK == 0 and MOBA_BLOCK == ATT_TILE
    t = batch * seq
    xf = x.reshape(t, d)

    bias = _bias_tiles(rel_bias.astype(F32), _bucket_index_tiles())
    grp = _head_group_matrix()
    q_scale = HEAD_DIM ** -0.5 * LOG2E

    for i in range(depth):
        kind, j = i % N_MIXERS, i // N_MIXERS
        nw1 = norm1[i].reshape(1, d)
        nw2 = norm2[i].reshape(1, d)
        wu = w_up[i].astype(BF16)
        wd = w_down[i].astype(BF16)
        if kind == 0:
            cos, sin = _rotary_tables(seq)
            dm, rs, ks, cd = _retention_decay_tables()
            q, k, v, g = _ret_proj(xf, nw1, ret_w_in[j].astype(BF16), cos, sin, seq)
            o = _ret_core(q, k, v, dm, rs, ks, cd, batch, seq)
            xf = _out_mlp(o, g, ret_w_out[j].astype(BF16), xf, nw2, wu, wd)
        elif kind == 1:
            qn = (jnp.tile(moba_q_norm[j], ATTN_HEADS) * q_scale).reshape(1, d)
            kn = jnp.tile(moba_k_norm[j], ATTN_HEADS).reshape(1, d)
            q, k, v = _attn_proj(xf, nw1, moba_w_in[j].astype(BF16), qn, kn, grp)
            ot = _attention("moba", q.T, k, _values_t_with_ones(v), bias, (_block_mean_matrix(seq),),
                            0.0, batch, seq)
            xf = _out_mlp(ot.T, None, moba_w_out[j].astype(BF16), xf, nw2, wu, wd)
        else:
            lambda_init = 0.8 - 0.6 * math.exp(-0.3 * i)
            qn = (jnp.tile(diff_q_norm[j], ATTN_HEADS) * q_scale).reshape(1, d)
            kn = jnp.tile(diff_k_norm[j], ATTN_HEADS).reshape(1, d)
            q, k, v = _attn_proj(xf, nw1, diff_w_in[j].astype(BF16), qn, kn, grp)
            extras = (diff_lambda[j].astype(F32), diff_subln[j].reshape(LANES, 1))
            ot = _attention("diff", q.T, k, _values_t_with_ones(v), bias, extras, lambda_init, batch, seq)
            xf = _out_mlp(ot.T, None, diff_w_out[j].astype(BF16), xf, nw2, wu, wd)
    return xf.reshape(batch, seq, d)
```

```python
import functools
import math

import numpy as np
import jax
import jax.numpy as jnp
from jax import lax
from jax.experimental import pallas as pl
from jax.experimental.pallas import tpu as pltpu

F32 = jnp.float32
BF16 = jnp.bfloat16

D_MODEL = 1024
N_MIXERS = 3
RET_HEADS = 4
RET_DK = D_MODEL // RET_HEADS
RET_DV = 2 * RET_DK
ROPE_BASE = 10000.0
ATTN_HEADS = 16
HEAD_DIM = D_MODEL // ATTN_HEADS
MOBA_BLOCK = 256
MOBA_TOPK = 3
REL_BUCKETS = 32
REL_MAX_EXACT = REL_BUCKETS // 2
REL_MAX_DISTANCE = 1024
D_FF = 4 * D_MODEL
EPS = 1e-6
NEG = -1e30
LOG2E = math.log2(math.e)

LANES = 128
BF16_SUBLANES = 16
VMEM_LIMIT_BYTES = 56 * 1024 * 1024

ROW_TILE = 512
COL_TILE = 256
FF_TILE = 1024
RET_CHUNK = 256
ATT_TILE = 256
BIAS_TILES = 6
HEADS_PER_PAIR = LANES // HEAD_DIM
PAIRS_PER_STEP = 2
HEADS_PER_STEP = HEADS_PER_PAIR * PAIRS_PER_STEP
V_ROWS = LANES + BF16_SUBLANES


def _params(*sem):
    return pltpu.CompilerParams(dimension_semantics=sem, vmem_limit_bytes=VMEM_LIMIT_BYTES)


def _rms(xf, w):
    ms = jnp.mean(xf * xf, axis=-1, keepdims=True)
    return xf * lax.rsqrt(ms + EPS) * w


def _dot(a, b):
    return jnp.dot(a, b, preferred_element_type=F32)


def _dot_nt(a, b):
    return lax.dot_general(a, b, (((1,), (1,)), ((), ())), preferred_element_type=F32)


def _dot_tn(a, b):
    return lax.dot_general(a, b, (((0,), (0,)), ((), ())), preferred_element_type=F32)


def _ret_proj_kernel(x_ref, nw_ref, w_ref, cos_ref, sin_ref, q_ref, k_ref, v_ref, g_ref):
    h = _rms(x_ref[...], nw_ref[...]).astype(BF16)
    cos = cos_ref[...]
    sin = sin_ref[...]
    half = RET_DK // 2
    for part, out_ref, scale in ((0, q_ref, 1.0), (1, k_ref, RET_DK ** -0.5)):
        for hd in range(RET_HEADS):
            base = part * RET_HEADS * RET_DK + hd * RET_DK
            acc = _dot(h, w_ref[:, base:base + RET_DK])
            x1 = acc[:, :half]
            x2 = acc[:, half:]
            r1 = x1 * cos - x2 * sin
            r2 = x1 * sin + x2 * cos
            if scale != 1.0:
                r1 = r1 * scale
                r2 = r2 * scale
            out_ref[:, hd * RET_DK:hd * RET_DK + half] = r1.astype(BF16)
            out_ref[:, hd * RET_DK + half:(hd + 1) * RET_DK] = r2.astype(BF16)
    vg_base = 2 * RET_HEADS * RET_DK
    n_v = RET_HEADS * RET_DV
    for part, out_ref in ((0, v_ref), (1, g_ref)):
        for c in range(n_v // COL_TILE):
            base = vg_base + part * n_v + c * COL_TILE
            acc = _dot(h, w_ref[:, base:base + COL_TILE])
            out_ref[:, c * COL_TILE:(c + 1) * COL_TILE] = acc.astype(BF16)


def _ret_proj(x, nw, w, cos, sin, seq):
    t = x.shape[0]
    n_in = w.shape[1]
    tiles_per_seq = seq // ROW_TILE
    row = lambda i: (i, 0)
    const = lambda i: (0, 0)
    pos = lambda i: (i % tiles_per_seq, 0)
    n_qk = RET_HEADS * RET_DK
    n_v = RET_HEADS * RET_DV
    return pl.pallas_call(
        _ret_proj_kernel,
        grid=(t // ROW_TILE,),
        in_specs=[
            pl.BlockSpec((ROW_TILE, D_MODEL), row),
            pl.BlockSpec((1, D_MODEL), const),
            pl.BlockSpec((D_MODEL, n_in), const),
            pl.BlockSpec((ROW_TILE, RET_DK // 2), pos),
            pl.BlockSpec((ROW_TILE, RET_DK // 2), pos),
        ],
        out_specs=[
            pl.BlockSpec((ROW_TILE, n_qk), row),
            pl.BlockSpec((ROW_TILE, n_qk), row),
            pl.BlockSpec((ROW_TILE, n_v), row),
            pl.BlockSpec((ROW_TILE, n_v), row),
        ],
        out_shape=[
            jax.ShapeDtypeStruct((t, n_qk), BF16),
            jax.ShapeDtypeStruct((t, n_qk), BF16),
            jax.ShapeDtypeStruct((t, n_v), BF16),
            jax.ShapeDtypeStruct((t, n_v), BF16),
        ],
        compiler_params=_params("parallel"),
        name="ret_proj",
    )(x, nw, w, cos, sin)


def _attn_proj_kernel(x_ref, nw_ref, w_ref, qn_ref, kn_ref, grp_ref, qt_ref, k_ref, vt_ref):
    h = _rms(x_ref[...], nw_ref[...]).astype(BF16)
    grp = grp_ref[...]
    pairs_per_chunk = COL_TILE // LANES
    for part, hw_ref in ((0, qn_ref), (1, kn_ref), (2, None)):
        for c in range(D_MODEL // COL_TILE):
            base = part * D_MODEL + c * COL_TILE
            cols = slice(c * COL_TILE, (c + 1) * COL_TILE)
            acc = _dot(h, w_ref[:, base:base + COL_TILE])
            if hw_ref is not None:
                ms = _dot((acc * acc).astype(BF16), grp)
                acc = acc * lax.rsqrt(ms + EPS) * hw_ref[:, cols]
            if part == 0:
                qt_ref[cols, :] = acc.T.astype(BF16)
            elif part == 1:
                k_ref[:, cols] = acc.astype(BF16)
            else:
                acc_t = acc.T.astype(BF16)
                for p in range(pairs_per_chunk):
                    row0 = (c * pairs_per_chunk + p) * V_ROWS
                    vt_ref[row0:row0 + LANES, :] = acc_t[p * LANES:(p + 1) * LANES, :]
                    vt_ref[row0 + LANES:row0 + V_ROWS, :] = jnp.ones((V_ROWS - LANES, acc_t.shape[1]), BF16)


def _attn_proj(x, nw, w, qn, kn, grp):
    t = x.shape[0]
    row = lambda i: (i, 0)
    col = lambda i: (0, i)
    const = lambda i: (0, 0)
    vt_rows = (D_MODEL // LANES) * V_ROWS
    return pl.pallas_call(
        _attn_proj_kernel,
        grid=(t // ROW_TILE,),
        in_specs=[
            pl.BlockSpec((ROW_TILE, D_MODEL), row),
            pl.BlockSpec((1, D_MODEL), const),
            pl.BlockSpec((D_MODEL, 3 * D_MODEL), const),
            pl.BlockSpec((1, D_MODEL), const),
            pl.BlockSpec((1, D_MODEL), const),
            pl.BlockSpec((COL_TILE, COL_TILE), const),
        ],
        out_specs=[
            pl.BlockSpec((D_MODEL, ROW_TILE), col),
            pl.BlockSpec((ROW_TILE, D_MODEL), row),
            pl.BlockSpec((vt_rows, ROW_TILE), col),
        ],
        out_shape=[
            jax.ShapeDtypeStruct((D_MODEL, t), BF16),
            jax.ShapeDtypeStruct((t, D_MODEL), BF16),
            jax.ShapeDtypeStruct((vt_rows, t), BF16),
        ],
        compiler_params=_params("parallel"),
        name="attn_proj",
    )(x, nw, w, qn, kn, grp)


def _ret_core_kernel(q_ref, k_ref, v_ref, dm_ref, rs_ref, ks_ref, cd_ref, o_ref, state_ref):
    c_len = RET_CHUNK
    n_chunks = q_ref.shape[0] // c_len
    state_ref[...] = jnp.zeros_like(state_ref)

    def body(c, carry):
        sl = pl.ds(pl.multiple_of(c * c_len, c_len), c_len)
        q = q_ref[sl, :]
        k = k_ref[sl, :]
        v = v_ref[sl, :]
        s = _dot_nt(q, k) * dm_ref[0]
        inner = _dot(s.astype(BF16), v)
        state = state_ref[...]
        cross = _dot(q, state.astype(BF16)) * rs_ref[0]
        o = inner + cross
        o = o * lax.rsqrt(jnp.mean(o * o, axis=-1, keepdims=True) + EPS)
        o_ref[sl, :] = o.astype(BF16)
        kd = (k.astype(F32) * ks_ref[0]).astype(BF16)
        state_ref[...] = state * cd_ref[0] + _dot_tn(kd, v)
        return carry

    lax.fori_loop(0, n_chunks, body, 0)


def _ret_core(q, k, v, dm, rs, ks, cd, batch, seq):
    t = q.shape[0]
    c_len = RET_CHUNK
    tok = lambda b, h: (b, h)
    head3 = lambda b, h: (h, 0, 0)
    return pl.pallas_call(
        _ret_core_kernel,
        grid=(batch, RET_HEADS),
        in_specs=[
            pl.BlockSpec((seq, RET_DK), tok),
            pl.BlockSpec((seq, RET_DK), tok),
            pl.BlockSpec((seq, RET_DV), tok),
            pl.BlockSpec((1, c_len, c_len), head3),
            pl.BlockSpec((1, c_len, 1), head3),
            pl.BlockSpec((1, c_len, 1), head3),
            pl.BlockSpec((1, 1, 1), head3),
        ],
        out_specs=pl.BlockSpec((seq, RET_DV), tok),
        out_shape=jax.ShapeDtypeStruct((t, RET_HEADS * RET_DV), BF16),
        scratch_shapes=[pltpu.VMEM((RET_DK, RET_DV), F32)],
        compiler_params=_params("parallel", "parallel"),
        name="ret_core",
    )(q, k, v, dm, rs, ks, cd)


def _out_mlp_kernel(gated, *refs):
    if gated:
        a_ref, g_ref, wo_ref, x_ref, nw_ref, wu_ref, wd_ref, o_ref = refs
        g = g_ref[...].astype(F32)
        a = (g * (1.0 / (1.0 + jnp.exp(-g))) * a_ref[...].astype(F32)).astype(BF16)
        x = x_ref[...] + _dot(a, wo_ref[...])
    else:
        a_ref, wo_ref, x_ref, nw_ref, wu_ref, wd_ref, o_ref = refs
        x = x_ref[...] + _dot_tn(a_ref[...], wo_ref[...])
    h = _rms(x, nw_ref[...]).astype(BF16)
    acc = x
    for c in range(D_FF // FF_TILE):
        u = _dot(h, wu_ref[:, c * FF_TILE:(c + 1) * FF_TILE])
        u = jnp.maximum(u, 0.0)
        acc = acc + _dot((u * u).astype(BF16), wd_ref[c * FF_TILE:(c + 1) * FF_TILE, :])
    o_ref[...] = acc


def _out_mlp(a, g, wo, x, nw, wu, wd):
    t = x.shape[0]
    ka = wo.shape[0]
    row = lambda i: (i, 0)
    const = lambda i: (0, 0)
    gated = g is not None
    if gated:
        in_specs = [pl.BlockSpec((ROW_TILE, ka), row), pl.BlockSpec((ROW_TILE, ka), row)]
        args = [a, g]
    else:
        in_specs = [pl.BlockSpec((ka, ROW_TILE), lambda i: (0, i))]
        args = [a]
    in_specs += [
        pl.BlockSpec((ka, D_MODEL), const),
        pl.BlockSpec((ROW_TILE, D_MODEL), row),
        pl.BlockSpec((1, D_MODEL), const),
        pl.BlockSpec((D_MODEL, D_FF), const, pipeline_mode=pl.Buffered(1)),
        pl.BlockSpec((D_FF, D_MODEL), const, pipeline_mode=pl.Buffered(1)),
    ]
    args += [wo, x, nw, wu, wd]
    return pl.pallas_call(
        functools.partial(_out_mlp_kernel, gated),
        grid=(t // ROW_TILE,),
        in_specs=in_specs,
        out_specs=pl.BlockSpec((ROW_TILE, D_MODEL), row),
        out_shape=jax.ShapeDtypeStruct((t, D_MODEL), F32),
        compiler_params=_params("parallel"),
        name="out_mlp_gated" if gated else "out_mlp",
    )(*args)


def _bias_tiles_kernel(rb_ref, idx_ref, o_ref):
    head = pl.program_id(0)
    idx = idx_ref[0]
    acc = jnp.full(idx.shape, NEG, F32)
    for b in range(REL_BUCKETS):
        acc = jnp.where(idx == b, rb_ref[b, head] * LOG2E, acc)
    o_ref[0, 0] = acc


def _bias_tiles(rel_bias, bucket_idx):
    nd, tq, tk = bucket_idx.shape
    return pl.pallas_call(
        _bias_tiles_kernel,
        grid=(ATTN_HEADS, nd),
        in_specs=[
            pl.BlockSpec(memory_space=pltpu.SMEM),
            pl.BlockSpec((1, tq, tk), lambda h, d: (d, 0, 0)),
        ],
        out_specs=pl.BlockSpec((1, 1, tq, tk), lambda h, d: (h, d, 0, 0)),
        out_shape=jax.ShapeDtypeStruct((ATTN_HEADS, nd, tq, tk), F32),
        compiler_params=_params("parallel", "parallel"),
        name="bias_tiles",
    )(rel_bias, bucket_idx)


def _rel_bucket(dist):
    n = jnp.maximum(dist, 0)
    nf = jnp.maximum(n, 1).astype(F32)
    large = REL_MAX_EXACT + (jnp.log(nf / REL_MAX_EXACT) / math.log(REL_MAX_DISTANCE / REL_MAX_EXACT)
                             * (REL_BUCKETS - REL_MAX_EXACT)).astype(jnp.int32)
    large = jnp.minimum(large, REL_BUCKETS - 1)
    return jnp.where(n < REL_MAX_EXACT, n, large)


def _bucket_index_tiles():
    r = np.arange(ATT_TILE)
    dist = (np.arange(BIAS_TILES)[:, None, None] * ATT_TILE + r[None, None, :] - r[None, :, None])
    dist = jnp.asarray(dist, jnp.int32)
    return jnp.where(dist >= 0, _rel_bucket(dist), REL_BUCKETS).astype(jnp.int32)


TAB_Q, TAB_K, TAB_DELTA, TAB_FIRST, TAB_ACC = range(5)
PIPE_LAG = 2
PIPE_UNROLL = 4
QUERY_HALVES = tuple(slice(h * LANES, (h + 1) * LANES) for h in range(ATT_TILE // LANES))


def _tile_schedule(nq, own_first):
    rows = []
    for i in range(nq):
        keys = ([i] + list(range(i))) if own_first else list(range(i + 1))
        for n, j in enumerate(keys):
            rows.append((i, j, min(i - j, BIAS_TILES - 1), int(n == 0), i))
    n_iters = -(-(len(rows) + PIPE_LAG) // PIPE_UNROLL) * PIPE_UNROLL
    idle = (0, 0, 0, 1, nq)
    cols = [idle] * PIPE_LAG + rows
    cols += [idle] * (n_iters + PIPE_LAG - len(cols))
    return jnp.asarray(np.array(cols, np.int32).T), n_iters


def _tile_slice(idx):
    return pl.ds(pl.multiple_of(idx * ATT_TILE, ATT_TILE), ATT_TILE)


def _head_row_mask(h):
    row = lax.broadcasted_iota(jnp.int32, (LANES, 1), 0)
    return (row >= h * HEAD_DIM) & (row < (h + 1) * HEAD_DIM)


def _pair_rows(pair, n=LANES):
    return slice(pair * n, (pair + 1) * n)


def _attn_kernel(kind, lambda_init, n_iters, tab_ref, qt_ref, k_ref, vt_ref, bias_ref, *rest):
    if kind == "moba":
        blk_ref, ot_ref, qts_ref, s_ref, p_ref, acc_ref, neg_ref = rest
    else:
        lam_ref, sw_ref, ot_ref, qts_ref, s_ref, p_ref, acc_ref = rest
    heads = range(HEADS_PER_STEP)
    seq = qt_ref.shape[1]

    for e in heads:
        pair, h = divmod(e, HEADS_PER_PAIR)
        qf = qt_ref[_pair_rows(pair), :].astype(F32)
        qts_ref[e] = jnp.where(_head_row_mask(h), qf, 0.0).astype(BF16)
    s_ref[...] = jnp.zeros_like(s_ref)
    p_ref[...] = jnp.zeros_like(p_ref)
    acc_ref[...] = jnp.zeros_like(acc_ref)

    if kind == "moba":
        nb = blk_ref.shape[0]
        kmean = _dot(blk_ref[...], k_ref[...])
        km_hi = kmean.astype(BF16)
        km_lo = (kmean - km_hi.astype(F32)).astype(BF16)
        blk_id = lax.broadcasted_iota(jnp.int32, (nb, seq), 0)
        own_blk = lax.broadcasted_iota(jnp.int32, (nb, seq), 1) // MOBA_BLOCK
        past = blk_id < own_blk
        for e in heads:
            pair = e // HEADS_PER_PAIR
            q_e = qts_ref[e]
            gate = _dot(km_hi[:, _pair_rows(pair)], q_e) + _dot(km_lo[:, _pair_rows(pair)], q_e)
            gate = jnp.where(past, gate, NEG)
            rank = jnp.zeros((nb, seq), jnp.int32)
            for c in range(nb):
                gc = gate[c:c + 1, :]
                tie = jnp.where(blk_id > c, 1, 0)
                rank = rank + jnp.where(gc > gate, 1, jnp.where(gc == gate, tie, 0))
            chosen = jnp.where(rank < MOBA_TOPK, jnp.where(past, 1, 0), 0)
            keep = jnp.maximum(chosen, jnp.where(blk_id == own_blk, 1, 0))
            neg_ref[e] = jnp.where(keep == 1, 0.0, NEG)

    def step(t, cur, carry):
        nxt = 1 - cur
        m_prev, alpha_prev, mtile_prev = carry
        ic = tab_ref[TAB_ACC, t]
        jc = tab_ref[TAB_K, t]
        ib = tab_ref[TAB_Q, t + 1]
        jb = tab_ref[TAB_K, t + 1]
        first = tab_ref[TAB_FIRST, t + 1] != 0
        ia = tab_ref[TAB_Q, t + 2]
        ja = tab_ref[TAB_K, t + 2]
        da = tab_ref[TAB_DELTA, t + 2]
        m_new, alpha_new, mtile_new = [], [], []
        for e in heads:
            pair = e // HEADS_PER_PAIR

            vt = vt_ref[_pair_rows(pair, V_ROWS), _tile_slice(jc)]
            pv = _dot(vt, p_ref[cur, e])
            for hf, lanes in enumerate(QUERY_HALVES):
                acc_ref[ic, e, :, lanes] = alpha_prev[e][hf] * acc_ref[ic, e, :, lanes] + pv[:, lanes]

            kt = k_ref[_tile_slice(ja), _pair_rows(pair)]
            sb = _dot(kt, qts_ref[e, :, _tile_slice(ia)]) + bias_ref[e, da]
            s_ref[nxt, e] = sb
            mtile_new.append(tuple(jnp.max(sb[:, lanes], axis=0, keepdims=True) for lanes in QUERY_HALVES))

            m_parts, alpha_parts = [], []
            if kind == "moba":
                neg_row = neg_ref[e, pl.ds(jb, 1), _tile_slice(ib)]
            for hf, lanes in enumerate(QUERY_HALVES):
                s = s_ref[cur, e, :, lanes]
                m_in = jnp.where(first, -jnp.inf, m_prev[e][hf])
                m_tile = mtile_prev[e][hf]
                if kind == "moba":
                    neg = neg_row[:, lanes]
                    m_e = jnp.maximum(m_in, m_tile + neg)
                    shift = m_e - neg
                else:
                    m_e = jnp.maximum(m_in, m_tile)
                    shift = m_e
                p_ref[nxt, e, :, lanes] = jnp.exp2(s - shift).astype(BF16)
                alpha_parts.append(jnp.exp2(m_in - m_e))
                m_parts.append(m_e)
            alpha_new.append(tuple(alpha_parts))
            m_new.append(tuple(m_parts))

        return tuple(m_new), tuple(alpha_new), tuple(mtile_new)

    zeros = tuple(tuple(jnp.zeros((1, LANES), F32) for _ in QUERY_HALVES) for _ in heads)

    def body(u, carry):
        for r in range(PIPE_UNROLL):
            carry = step(PIPE_UNROLL * u + r, r % 2, carry)
        return carry

    lax.fori_loop(0, n_iters // PIPE_UNROLL, body, (zeros, zeros, zeros))

    def emit(i, carry):
        cols = _tile_slice(i)
        for pair in range(PAIRS_PER_STEP):
            outs = []
            for h in range(HEADS_PER_PAIR):
                e = pair * HEADS_PER_PAIR + h
                outs.append(acc_ref[i, e, :LANES, :] * (1.0 / acc_ref[i, e, LANES:LANES + 1, :]))
            if kind == "moba":
                o = jnp.where(_head_row_mask(0), outs[0], outs[1])
            else:
                lam = lam_ref[...]
                lam_full = (jnp.exp(jnp.sum(lam[0:1] * lam[1:2], axis=-1, keepdims=True))
                            - jnp.exp(jnp.sum(lam[2:3] * lam[3:4], axis=-1, keepdims=True)) + lambda_init)
                o = outs[0] - lam_full * outs[1]
                o = (o * lax.rsqrt(jnp.mean(o * o, axis=0, keepdims=True) + EPS)
                     * sw_ref[...] * (1.0 - lambda_init))
            ot_ref[_pair_rows(pair), cols] = o.astype(BF16)
        return carry

    lax.fori_loop(0, seq // ATT_TILE, emit, 0)


def _attention(kind, qt, k, vt_ext, bias, extras, lambda_init, batch, seq):
    t = k.shape[0]
    nq = seq // ATT_TILE
    tab, n_iters = _tile_schedule(nq, own_first=(kind == "moba"))
    groups = ATTN_HEADS // HEADS_PER_STEP
    rows = LANES * PAIRS_PER_STEP
    in_specs = [
        pl.BlockSpec(memory_space=pltpu.SMEM),
        pl.BlockSpec((rows, seq), lambda b, g: (g, b)),
        pl.BlockSpec((seq, rows), lambda b, g: (b, g)),
        pl.BlockSpec((V_ROWS * PAIRS_PER_STEP, seq), lambda b, g: (g, b)),
        pl.BlockSpec((HEADS_PER_STEP, BIAS_TILES, ATT_TILE, ATT_TILE), lambda b, g: (g, 0, 0, 0)),
    ]
    scratch = [
        pltpu.VMEM((HEADS_PER_STEP, LANES, seq), BF16),
        pltpu.VMEM((2, HEADS_PER_STEP, ATT_TILE, ATT_TILE), F32),
        pltpu.VMEM((2, HEADS_PER_STEP, ATT_TILE, ATT_TILE), BF16),
        pltpu.VMEM((nq + 1, HEADS_PER_STEP, V_ROWS, ATT_TILE), F32),
    ]
    if kind == "moba":
        (blk,) = extras
        nb = blk.shape[0]
        in_specs.append(pl.BlockSpec((nb, seq), lambda b, g: (0, 0)))
        scratch.append(pltpu.VMEM((HEADS_PER_STEP, nb, seq), F32))
    else:
        lam, sw = extras
        in_specs += [
            pl.BlockSpec((4, HEAD_DIM), lambda b, g: (0, 0)),
            pl.BlockSpec((LANES, 1), lambda b, g: (0, 0)),
        ]
    return pl.pallas_call(
        functools.partial(_attn_kernel, kind, lambda_init, n_iters),
        grid=(batch, groups),
        in_specs=in_specs,
        out_specs=pl.BlockSpec((rows, seq), lambda b, g: (g, b)),
        out_shape=jax.ShapeDtypeStruct((D_MODEL, t), BF16),
        scratch_shapes=scratch,
        compiler_params=_params("parallel", "parallel"),
        name=kind + "_attn",
    )(tab, qt, k, vt_ext, bias, *extras)


def _rotary_tables(seq):
    d = RET_DK
    inv_freq = ROPE_BASE ** (-np.arange(0, d, 2, dtype=np.float64) / d)
    ang = np.arange(seq, dtype=np.float64)[:, None] * inv_freq[None, :]
    return jnp.asarray(np.cos(ang), F32), jnp.asarray(np.sin(ang), F32)


def _retention_decay_tables():
    c_len = RET_CHUNK
    log_gamma = np.log(1.0 - 2.0 ** (-5.0 - np.arange(RET_HEADS, dtype=np.float64)))
    pos = np.arange(c_len, dtype=np.float64)
    rel = pos[:, None] - pos[None, :]
    dm = np.where(rel >= 0, np.exp(np.maximum(rel, 0.0)[None] * log_gamma[:, None, None]), 0.0)
    rs = np.exp((pos + 1.0)[None, :] * log_gamma[:, None])[:, :, None]
    ks = np.exp((c_len - 1.0 - pos)[None, :] * log_gamma[:, None])[:, :, None]
    cd = np.exp(c_len * log_gamma)[:, None, None]
    return tuple(jnp.asarray(a, F32) for a in (dm, rs, ks, cd))


def _block_mean_matrix(seq):
    nb = seq // MOBA_BLOCK
    m = (np.arange(seq)[None, :] // MOBA_BLOCK == np.arange(nb)[:, None]) / float(MOBA_BLOCK)
    return jnp.asarray(m, BF16)


def _head_group_matrix():
    g = np.arange(COL_TILE)[:, None] // HEAD_DIM == np.arange(COL_TILE)[None, :] // HEAD_DIM
    return jnp.asarray(g / float(HEAD_DIM), BF16)


def kernel(x, rel_bias, norm1, norm2, w_up, w_down, ret_w_in, ret_w_out,
           moba_w_in, moba_q_norm, moba_k_norm, moba_w_out,
           diff_w_in, diff_q_norm, diff_k_norm, diff_lambda, diff_subln, diff_w_out):
    batch, seq, d = x.shape
    depth = norm1.shape[0]
    assert d == D_MODEL and seq % ROW_TILE == 0 and seq % ATT_TILE == 0 and seq % RET_CHUNK == 0
    assert seq % MOBA_BLOCK == 0 and MOBA_BLOCK == ATT_TILE
    t = batch * seq
    xf = x.reshape(t, d)

    bias = _bias_tiles(rel_bias.astype(F32), _bucket_index_tiles())
    grp = _head_group_matrix()
    q_scale = HEAD_DIM ** -0.5 * LOG2E

    for i in range(depth):
        kind, j = i % N_MIXERS, i // N_MIXERS
        nw1 = norm1[i].reshape(1, d)
        nw2 = norm2[i].reshape(1, d)
        wu = w_up[i].astype(BF16)
        wd = w_down[i].astype(BF16)
        if kind == 0:
            cos, sin = _rotary_tables(seq)
            dm, rs, ks, cd = _retention_decay_tables()
            q, k, v, g = _ret_proj(xf, nw1, ret_w_in[j].astype(BF16), cos, sin, seq)
            o = _ret_core(q, k, v, dm, rs, ks, cd, batch, seq)
            xf = _out_mlp(o, g, ret_w_out[j].astype(BF16), xf, nw2, wu, wd)
        elif kind == 1:
            qn = (jnp.tile(moba_q_norm[j], ATTN_HEADS) * q_scale).reshape(1, d)
            kn = jnp.tile(moba_k_norm[j], ATTN_HEADS).reshape(1, d)
            qt, k, vt = _attn_proj(xf, nw1, moba_w_in[j].astype(BF16), qn, kn, grp)
            ot = _attention("moba", qt, k, vt, bias, (_block_mean_matrix(seq),), 0.0, batch, seq)
            xf = _out_mlp(ot, None, moba_w_out[j].astype(BF16), xf, nw2, wu, wd)
        else:
            lambda_init = 0.8 - 0.6 * math.exp(-0.3 * i)
            qn = (jnp.tile(diff_q_norm[j], ATTN_HEADS) * q_scale).reshape(1, d)
            kn = jnp.tile(diff_k_norm[j], ATTN_HEADS).reshape(1, d)
            qt, k, vt = _attn_proj(xf, nw1, diff_w_in[j].astype(BF16), qn, kn, grp)
            extras = (diff_lambda[j].astype(F32), diff_subln[j].reshape(LANES, 1))
            ot = _attention("diff", qt, k, vt, bias, extras, lambda_init, batch, seq)
            xf = _out_mlp(ot, None, diff_w_out[j].astype(BF16), xf, nw2, wu, wd)
    return xf.reshape(batch, seq, d)
```

```python
import functools
import math

import numpy as np
import jax
import jax.numpy as jnp
from jax import lax
from jax.experimental import pallas as pl
from jax.experimental.pallas import tpu as pltpu

F32 = jnp.float32
BF16 = jnp.bfloat16

D_MODEL = 1024
N_MIXERS = 3
RET_HEADS = 4
RET_DK = D_MODEL // RET_HEADS
RET_DV = 2 * RET_DK
ROPE_BASE = 10000.0
ATTN_HEADS = 16
HEAD_DIM = D_MODEL // ATTN_HEADS
MOBA_BLOCK = 256
MOBA_TOPK = 3
REL_BUCKETS = 32
REL_MAX_EXACT = REL_BUCKETS // 2
REL_MAX_DISTANCE = 1024
D_FF = 4 * D_MODEL
EPS = 1e-6
NEG = -1e30
LOG2E = math.log2(math.e)

LANES = 128
BF16_SUBLANES = 16
VMEM_LIMIT_BYTES = 56 * 1024 * 1024

ROW_TILE = 512
COL_TILE = 256
FF_TILE = 1024
RET_CHUNK = 256
ATT_TILE = 256
BIAS_TILES = 6
HEADS_PER_PAIR = LANES // HEAD_DIM
PAIRS_PER_STEP = 2
HEADS_PER_STEP = HEADS_PER_PAIR * PAIRS_PER_STEP
ONES_ROWS = BF16_SUBLANES


def _value_group(kind):
    return HEAD_DIM if kind == "moba" else LANES


def _params(*sem):
    return pltpu.CompilerParams(dimension_semantics=sem, vmem_limit_bytes=VMEM_LIMIT_BYTES)


def _rms(xf, w):
    ms = jnp.mean(xf * xf, axis=-1, keepdims=True)
    return xf * lax.rsqrt(ms + EPS) * w


def _dot(a, b):
    return jnp.dot(a, b, preferred_element_type=F32)


def _dot_nt(a, b):
    return lax.dot_general(a, b, (((1,), (1,)), ((), ())), preferred_element_type=F32)


def _dot_tn(a, b):
    return lax.dot_general(a, b, (((0,), (0,)), ((), ())), preferred_element_type=F32)


def _ret_proj_kernel(x_ref, nw_ref, w_ref, cos_ref, sin_ref, q_ref, k_ref, v_ref, g_ref):
    h = _rms(x_ref[...], nw_ref[...]).astype(BF16)
    cos = cos_ref[...]
    sin = sin_ref[...]
    half = RET_DK // 2
    for part, out_ref, scale in ((0, q_ref, 1.0), (1, k_ref, RET_DK ** -0.5)):
        for hd in range(RET_HEADS):
            base = part * RET_HEADS * RET_DK + hd * RET_DK
            acc = _dot(h, w_ref[:, base:base + RET_DK])
            x1 = acc[:, :half]
            x2 = acc[:, half:]
            r1 = x1 * cos - x2 * sin
            r2 = x1 * sin + x2 * cos
            if scale != 1.0:
                r1 = r1 * scale
                r2 = r2 * scale
            out_ref[:, hd * RET_DK:hd * RET_DK + half] = r1.astype(BF16)
            out_ref[:, hd * RET_DK + half:(hd + 1) * RET_DK] = r2.astype(BF16)
    vg_base = 2 * RET_HEADS * RET_DK
    n_v = RET_HEADS * RET_DV
    for part, out_ref in ((0, v_ref), (1, g_ref)):
        for c in range(n_v // COL_TILE):
            base = vg_base + part * n_v + c * COL_TILE
            acc = _dot(h, w_ref[:, base:base + COL_TILE])
            out_ref[:, c * COL_TILE:(c + 1) * COL_TILE] = acc.astype(BF16)


def _ret_proj(x, nw, w, layer, cos, sin, seq):
    t = x.shape[0]
    n_in = w.shape[2]
    tiles_per_seq = seq // ROW_TILE
    row = lambda i: (i, 0)
    const = lambda i: (0, 0)
    pos = lambda i: (i % tiles_per_seq, 0)
    n_qk = RET_HEADS * RET_DK
    n_v = RET_HEADS * RET_DV
    return pl.pallas_call(
        _ret_proj_kernel,
        grid=(t // ROW_TILE,),
        in_specs=[
            pl.BlockSpec((ROW_TILE, D_MODEL), row),
            pl.BlockSpec((1, D_MODEL), const),
            pl.BlockSpec((None, D_MODEL, n_in), lambda i: (layer, 0, 0)),
            pl.BlockSpec((ROW_TILE, RET_DK // 2), pos),
            pl.BlockSpec((ROW_TILE, RET_DK // 2), pos),
        ],
        out_specs=[
            pl.BlockSpec((ROW_TILE, n_qk), row),
            pl.BlockSpec((ROW_TILE, n_qk), row),
            pl.BlockSpec((ROW_TILE, n_v), row),
            pl.BlockSpec((ROW_TILE, n_v), row),
        ],
        out_shape=[
            jax.ShapeDtypeStruct((t, n_qk), BF16),
            jax.ShapeDtypeStruct((t, n_qk), BF16),
            jax.ShapeDtypeStruct((t, n_v), BF16),
            jax.ShapeDtypeStruct((t, n_v), BF16),
        ],
        compiler_params=_params("parallel"),
        name="ret_proj",
    )(x, nw, w, cos, sin)


def _attn_proj_kernel(v_group, x_ref, nw_ref, w_ref, qn_ref, kn_ref, grp_ref, qt_ref, k_ref, vt_ref):
    h = _rms(x_ref[...], nw_ref[...]).astype(BF16)
    grp = grp_ref[...]
    groups_per_chunk = COL_TILE // v_group
    chunks_per_part = D_MODEL // COL_TILE
    n_chunks = 3 * chunks_per_part

    def project(n):
        return _dot(h, w_ref[:, n * COL_TILE:(n + 1) * COL_TILE])

    acc_next = project(0)
    for n in range(n_chunks):
        acc = acc_next
        if n + 1 < n_chunks:
            acc_next = project(n + 1)
        part, c = divmod(n, chunks_per_part)
        cols = slice(c * COL_TILE, (c + 1) * COL_TILE)
        if part < 2:
            hw_ref = qn_ref if part == 0 else kn_ref
            ms = _dot((acc * acc).astype(BF16), grp)
            acc = acc * lax.rsqrt(ms + EPS) * hw_ref[:, cols]
        if part == 0:
            qt_ref[cols, :] = acc.T.astype(BF16)
        elif part == 1:
            k_ref[:, cols] = acc.astype(BF16)
        else:
            acc_t = acc.T.astype(BF16)
            for p in range(groups_per_chunk):
                row0 = (c * groups_per_chunk + p) * (v_group + ONES_ROWS)
                vt_ref[row0:row0 + v_group, :] = acc_t[p * v_group:(p + 1) * v_group, :]
                vt_ref[row0 + v_group:row0 + v_group + ONES_ROWS, :] = jnp.ones((ONES_ROWS, acc_t.shape[1]), BF16)


def _attn_proj(x, nw, w, layer, qn, kn, grp, v_group):
    t = x.shape[0]
    row = lambda i: (i, 0)
    col = lambda i: (0, i)
    const = lambda i: (0, 0)
    vt_rows = (D_MODEL // v_group) * (v_group + ONES_ROWS)
    return pl.pallas_call(
        functools.partial(_attn_proj_kernel, v_group),
        grid=(t // ROW_TILE,),
        in_specs=[
            pl.BlockSpec((ROW_TILE, D_MODEL), row),
            pl.BlockSpec((1, D_MODEL), const),
            pl.BlockSpec((None, D_MODEL, 3 * D_MODEL), lambda i: (layer, 0, 0)),
            pl.BlockSpec((1, D_MODEL), const),
            pl.BlockSpec((1, D_MODEL), const),
            pl.BlockSpec((COL_TILE, COL_TILE), const),
        ],
        out_specs=[
            pl.BlockSpec((D_MODEL, ROW_TILE), col),
            pl.BlockSpec((ROW_TILE, D_MODEL), row),
            pl.BlockSpec((vt_rows, ROW_TILE), col),
        ],
        out_shape=[
            jax.ShapeDtypeStruct((D_MODEL, t), BF16),
            jax.ShapeDtypeStruct((t, D_MODEL), BF16),
            jax.ShapeDtypeStruct((vt_rows, t), BF16),
        ],
        compiler_params=_params("parallel"),
        name="attn_proj",
    )(x, nw, w, qn, kn, grp)


def _ret_core_kernel(q_ref, k_ref, v_ref, dm_ref, rs_ref, ks_ref, cd_ref, o_ref, state_ref):
    c_len = RET_CHUNK
    n_chunks = q_ref.shape[0] // c_len
    state_ref[...] = jnp.zeros_like(state_ref)

    def body(c, carry):
        sl = pl.ds(pl.multiple_of(c * c_len, c_len), c_len)
        q = q_ref[sl, :]
        k = k_ref[sl, :]
        v = v_ref[sl, :]
        s = _dot_nt(q, k) * dm_ref[0]
        inner = _dot(s.astype(BF16), v)
        state = state_ref[...]
        cross = _dot(q, state.astype(BF16)) * rs_ref[0]
        o = inner + cross
        o = o * lax.rsqrt(jnp.mean(o * o, axis=-1, keepdims=True) + EPS)
        o_ref[sl, :] = o.astype(BF16)
        kd = (k.astype(F32) * ks_ref[0]).astype(BF16)
        state_ref[...] = state * cd_ref[0] + _dot_tn(kd, v)
        return carry

    lax.fori_loop(0, n_chunks, body, 0)


def _ret_core(q, k, v, dm, rs, ks, cd, batch, seq):
    t = q.shape[0]
    c_len = RET_CHUNK
    tok = lambda b, h: (b, h)
    head3 = lambda b, h: (h, 0, 0)
    return pl.pallas_call(
        _ret_core_kernel,
        grid=(batch, RET_HEADS),
        in_specs=[
            pl.BlockSpec((seq, RET_DK), tok),
            pl.BlockSpec((seq, RET_DK), tok),
            pl.BlockSpec((seq, RET_DV), tok),
            pl.BlockSpec((1, c_len, c_len), head3),
            pl.BlockSpec((1, c_len, 1), head3),
            pl.BlockSpec((1, c_len, 1), head3),
            pl.BlockSpec((1, 1, 1), head3),
        ],
        out_specs=pl.BlockSpec((seq, RET_DV), tok),
        out_shape=jax.ShapeDtypeStruct((t, RET_HEADS * RET_DV), BF16),
        scratch_shapes=[pltpu.VMEM((RET_DK, RET_DV), F32)],
        compiler_params=_params("parallel", "parallel"),
        name="ret_core",
    )(q, k, v, dm, rs, ks, cd)


def _out_mlp_kernel(gated, *refs):
    if gated:
        a_ref, g_ref, wo_ref, x_ref, nw_ref, wu_ref, wd_ref, o_ref = refs
        g = g_ref[...].astype(F32)
        a = (g * (1.0 / (1.0 + jnp.exp(-g))) * a_ref[...].astype(F32)).astype(BF16)
        x = x_ref[...] + _dot(a, wo_ref[...])
    else:
        a_ref, wo_ref, x_ref, nw_ref, wu_ref, wd_ref, o_ref = refs
        x = x_ref[...] + _dot_tn(a_ref[...], wo_ref[...])
    h = _rms(x, nw_ref[...]).astype(BF16)
    acc = x
    for c in range(D_FF // FF_TILE):
        u = _dot(h, wu_ref[:, c * FF_TILE:(c + 1) * FF_TILE])
        u = jnp.maximum(u, 0.0)
        acc = acc + _dot((u * u).astype(BF16), wd_ref[c * FF_TILE:(c + 1) * FF_TILE, :])
    o_ref[...] = acc


def _out_mlp(a, g, wo, mixer_layer, x, nw, wu, wd, layer):
    t = x.shape[0]
    ka = wo.shape[1]
    row = lambda i: (i, 0)
    const = lambda i: (0, 0)
    gated = g is not None
    if gated:
        in_specs = [pl.BlockSpec((ROW_TILE, ka), row), pl.BlockSpec((ROW_TILE, ka), row)]
        args = [a, g]
    else:
        in_specs = [pl.BlockSpec((ka, ROW_TILE), lambda i: (0, i))]
        args = [a]
    in_specs += [
        pl.BlockSpec((None, ka, D_MODEL), lambda i: (mixer_layer, 0, 0)),
        pl.BlockSpec((ROW_TILE, D_MODEL), row),
        pl.BlockSpec((1, D_MODEL), const),
        pl.BlockSpec((None, D_MODEL, D_FF), lambda i: (layer, 0, 0), pipeline_mode=pl.Buffered(1)),
        pl.BlockSpec((None, D_FF, D_MODEL), lambda i: (layer, 0, 0), pipeline_mode=pl.Buffered(1)),
    ]
    args += [wo, x, nw, wu, wd]
    return pl.pallas_call(
        functools.partial(_out_mlp_kernel, gated),
        grid=(t // ROW_TILE,),
        in_specs=in_specs,
        out_specs=pl.BlockSpec((ROW_TILE, D_MODEL), row),
        out_shape=jax.ShapeDtypeStruct((t, D_MODEL), F32),
        compiler_params=_params("parallel"),
        name="out_mlp_gated" if gated else "out_mlp",
    )(*args)


def _bias_tiles_kernel(bucket_ranges, rb_ref, idx_ref, o_ref):
    head = pl.program_id(0)
    for d, (lo, hi) in enumerate(bucket_ranges):
        idx = idx_ref[d]
        acc = jnp.full(idx.shape, NEG, F32)
        for b in range(lo, hi + 1):
            acc = jnp.where(idx == b, rb_ref[b, head] * LOG2E, acc)
        o_ref[0, d] = acc


def _bucket_ranges():
    ranges = []
    for d in range(BIAS_TILES):
        lo_dist = max(d * ATT_TILE - (ATT_TILE - 1), 0)
        hi_dist = d * ATT_TILE + (ATT_TILE - 1)

        def bucket(n):
            if n < REL_MAX_EXACT:
                return n
            return min(REL_MAX_EXACT + int(math.log(n / REL_MAX_EXACT) / math.log(REL_MAX_DISTANCE / REL_MAX_EXACT)
                                           * (REL_BUCKETS - REL_MAX_EXACT)), REL_BUCKETS - 1)

        ranges.append((max(bucket(lo_dist) - 1, 0), min(bucket(hi_dist) + 1, REL_BUCKETS - 1)))
    return tuple(ranges)


def _bias_tiles(rel_bias, bucket_idx):
    nd, tq, tk = bucket_idx.shape
    return pl.pallas_call(
        functools.partial(_bias_tiles_kernel, _bucket_ranges()),
        grid=(ATTN_HEADS,),
        in_specs=[
            pl.BlockSpec(memory_space=pltpu.SMEM),
            pl.BlockSpec((nd, tq, tk), lambda h: (0, 0, 0)),
        ],
        out_specs=pl.BlockSpec((1, nd, tq, tk), lambda h: (h, 0, 0, 0)),
        out_shape=jax.ShapeDtypeStruct((ATTN_HEADS, nd, tq, tk), F32),
        compiler_params=_params("parallel"),
        name="bias_tiles",
    )(rel_bias, bucket_idx)


def _rel_bucket(dist):
    n = jnp.maximum(dist, 0)
    nf = jnp.maximum(n, 1).astype(F32)
    large = REL_MAX_EXACT + (jnp.log(nf / REL_MAX_EXACT) / math.log(REL_MAX_DISTANCE / REL_MAX_EXACT)
                             * (REL_BUCKETS - REL_MAX_EXACT)).astype(jnp.int32)
    large = jnp.minimum(large, REL_BUCKETS - 1)
    return jnp.where(n < REL_MAX_EXACT, n, large)


def _bucket_index_tiles():
    r = np.arange(ATT_TILE)
    dist = (np.arange(BIAS_TILES)[:, None, None] * ATT_TILE + r[None, None, :] - r[None, :, None])
    dist = jnp.asarray(dist, jnp.int32)
    return jnp.where(dist >= 0, _rel_bucket(dist), REL_BUCKETS).astype(jnp.int32)


TAB_Q, TAB_K, TAB_DELTA, TAB_FIRST, TAB_ACC = range(5)
PIPE_LAG = 2
PIPE_UNROLL = 6
QUERY_HALVES = tuple(slice(h * LANES, (h + 1) * LANES) for h in range(ATT_TILE // LANES))


def _tile_schedule(nq, own_first):
    rows = []
    for i in range(nq):
        keys = ([i] + list(range(i))) if own_first else list(range(i + 1))
        for n, j in enumerate(keys):
            rows.append((i, j, min(i - j, BIAS_TILES - 1), int(n == 0), i))
    n_iters = -(-(len(rows) + PIPE_LAG) // PIPE_UNROLL) * PIPE_UNROLL
    idle = (0, 0, 0, 1, nq)
    cols = [idle] * PIPE_LAG + rows
    cols += [idle] * (n_iters + PIPE_LAG - len(cols))
    return jnp.asarray(np.array(cols, np.int32).T), n_iters


def _tile_slice(idx):
    return pl.ds(pl.multiple_of(idx * ATT_TILE, ATT_TILE), ATT_TILE)


def _head_row_mask(h):
    row = lax.broadcasted_iota(jnp.int32, (LANES, 1), 0)
    return (row >= h * HEAD_DIM) & (row < (h + 1) * HEAD_DIM)


def _pair_rows(pair, n=LANES):
    return slice(pair * n, (pair + 1) * n)


def _attn_kernel(kind, lambda_init, n_iters, tab_ref, qt_ref, k_ref, vt_ref, bias_ref, *rest):
    if kind == "moba":
        blk_ref, ot_ref, qts_ref, s_ref, p_ref, acc_ref, neg_ref = rest
    else:
        lam_ref, sw_ref, ot_ref, qts_ref, s_ref, p_ref, acc_ref = rest
    heads = range(HEADS_PER_STEP)
    seq = qt_ref.shape[1]
    acc_rows = acc_ref.shape[2]

    for e in heads:
        pair, h = divmod(e, HEADS_PER_PAIR)
        qf = qt_ref[_pair_rows(pair), :].astype(F32)
        qts_ref[e] = jnp.where(_head_row_mask(h), qf, 0.0).astype(BF16)
    s_ref[...] = jnp.zeros_like(s_ref)
    p_ref[...] = jnp.zeros_like(p_ref)
    acc_ref[...] = jnp.zeros_like(acc_ref)

    if kind == "moba":
        nb = blk_ref.shape[0]
        kmean = _dot(blk_ref[...], k_ref[...])
        km_hi = kmean.astype(BF16)
        km_lo = (kmean - km_hi.astype(F32)).astype(BF16)
        blk_id = lax.broadcasted_iota(jnp.int32, (nb, seq), 0)
        own_blk = lax.broadcasted_iota(jnp.int32, (nb, seq), 1) // MOBA_BLOCK
        past = blk_id < own_blk
        for e in heads:
            pair = e // HEADS_PER_PAIR
            q_e = qts_ref[e]
            gate = _dot(km_hi[:, _pair_rows(pair)], q_e) + _dot(km_lo[:, _pair_rows(pair)], q_e)
            gate = jnp.where(past, gate, NEG)
            rank = jnp.zeros((nb, seq), jnp.int32)
            for c in range(nb):
                gc = gate[c:c + 1, :]
                tie = jnp.where(blk_id > c, 1, 0)
                rank = rank + jnp.where(gc > gate, 1, jnp.where(gc == gate, tie, 0))
            chosen = jnp.where(rank < MOBA_TOPK, jnp.where(past, 1, 0), 0)
            keep = jnp.maximum(chosen, jnp.where(blk_id == own_blk, 1, 0))
            neg_ref[e] = jnp.where(keep == 1, 0.0, NEG)

    def step(t, cur, carry):
        nxt = 1 - cur
        m_prev, alpha_prev, mtile_prev = carry
        ic = tab_ref[TAB_ACC, t]
        jc = tab_ref[TAB_K, t]
        ib = tab_ref[TAB_Q, t + 1]
        jb = tab_ref[TAB_K, t + 1]
        first = tab_ref[TAB_FIRST, t + 1] != 0
        ia = tab_ref[TAB_Q, t + 2]
        ja = tab_ref[TAB_K, t + 2]
        da = tab_ref[TAB_DELTA, t + 2]
        m_new, alpha_new, mtile_new = [], [], []
        for e in heads:
            pair = e // HEADS_PER_PAIR

            vt = vt_ref[_pair_rows(e if kind == "moba" else pair, acc_rows), _tile_slice(jc)]
            pv = _dot(vt, p_ref[cur, e])
            for hf, lanes in enumerate(QUERY_HALVES):
                acc_ref[ic, e, :, lanes] = alpha_prev[e][hf] * acc_ref[ic, e, :, lanes] + pv[:, lanes]

            kt = k_ref[_tile_slice(ja), _pair_rows(pair)]
            sb = _dot(kt, qts_ref[e, :, _tile_slice(ia)]) + bias_ref[e, da]
            s_ref[nxt, e] = sb
            mtile_new.append(tuple(jnp.max(sb[:, lanes], axis=0, keepdims=True) for lanes in QUERY_HALVES))

            m_parts, alpha_parts = [], []
            if kind == "moba":
                neg_row = neg_ref[e, pl.ds(jb, 1), _tile_slice(ib)]
            for hf, lanes in enumerate(QUERY_HALVES):
                s = s_ref[cur, e, :, lanes]
                m_in = jnp.where(first, -jnp.inf, m_prev[e][hf])
                m_tile = mtile_prev[e][hf]
                if kind == "moba":
                    neg = neg_row[:, lanes]
                    m_e = jnp.maximum(m_in, m_tile + neg)
                    shift = m_e - neg
                else:
                    m_e = jnp.maximum(m_in, m_tile)
                    shift = m_e
                p_ref[nxt, e, :, lanes] = jnp.exp2(s - shift).astype(BF16)
                alpha_parts.append(jnp.exp2(m_in - m_e))
                m_parts.append(m_e)
            alpha_new.append(tuple(alpha_parts))
            m_new.append(tuple(m_parts))

        return tuple(m_new), tuple(alpha_new), tuple(mtile_new)

    zeros = tuple(tuple(jnp.zeros((1, LANES), F32) for _ in QUERY_HALVES) for _ in heads)

    def body(u, carry):
        for r in range(PIPE_UNROLL):
            carry = step(PIPE_UNROLL * u + r, r % 2, carry)
        return carry

    lax.fori_loop(0, n_iters // PIPE_UNROLL, body, (zeros, zeros, zeros))

    def emit(i, carry):
        cols = _tile_slice(i)
        v_group = acc_ref.shape[2] - ONES_ROWS
        for pair in range(PAIRS_PER_STEP):
            outs = []
            for h in range(HEADS_PER_PAIR):
                e = pair * HEADS_PER_PAIR + h
                outs.append(acc_ref[i, e, :v_group, :] * (1.0 / acc_ref[i, e, v_group:v_group + 1, :]))
            if kind == "moba":
                o = jnp.concatenate(outs, axis=0)
            else:
                lam = lam_ref[...]
                lam_full = (jnp.exp(jnp.sum(lam[0:1] * lam[1:2], axis=-1, keepdims=True))
                            - jnp.exp(jnp.sum(lam[2:3] * lam[3:4], axis=-1, keepdims=True)) + lambda_init)
                o = outs[0] - lam_full * outs[1]
                o = (o * lax.rsqrt(jnp.mean(o * o, axis=0, keepdims=True) + EPS)
                     * sw_ref[...] * (1.0 - lambda_init))
            ot_ref[_pair_rows(pair), cols] = o.astype(BF16)
        return carry

    lax.fori_loop(0, seq // ATT_TILE, emit, 0)


def _attention(kind, qt, k, vt_ext, bias, extras, lambda_init, batch, seq):
    v_group = _value_group(kind)
    acc_rows = v_group + ONES_ROWS
    vt_block_rows = acc_rows * (LANES * PAIRS_PER_STEP // v_group)
    t = k.shape[0]
    nq = seq // ATT_TILE
    tab, n_iters = _tile_schedule(nq, own_first=(kind == "moba"))
    groups = ATTN_HEADS // HEADS_PER_STEP
    rows = LANES * PAIRS_PER_STEP
    in_specs = [
        pl.BlockSpec(memory_space=pltpu.SMEM),
        pl.BlockSpec((rows, seq), lambda b, g: (g, b)),
        pl.BlockSpec((seq, rows), lambda b, g: (b, g)),
        pl.BlockSpec((vt_block_rows, seq), lambda b, g: (g, b)),
        pl.BlockSpec((HEADS_PER_STEP, BIAS_TILES, ATT_TILE, ATT_TILE), lambda b, g: (g, 0, 0, 0)),
    ]
    scratch = [
        pltpu.VMEM((HEADS_PER_STEP, LANES, seq), BF16),
        pltpu.VMEM((2, HEADS_PER_STEP, ATT_TILE, ATT_TILE), F32),
        pltpu.VMEM((2, HEADS_PER_STEP, ATT_TILE, ATT_TILE), BF16),
        pltpu.VMEM((nq + 1, HEADS_PER_STEP, acc_rows, ATT_TILE), F32),
    ]
    if kind == "moba":
        (blk,) = extras
        nb = blk.shape[0]
        in_specs.append(pl.BlockSpec((nb, seq), lambda b, g: (0, 0)))
        scratch.append(pltpu.VMEM((HEADS_PER_STEP, nb, seq), F32))
    else:
        lam, sw = extras
        in_specs += [
            pl.BlockSpec((4, HEAD_DIM), lambda b, g: (0, 0)),
            pl.BlockSpec((LANES, 1), lambda b, g: (0, 0)),
        ]
    return pl.pallas_call(
        functools.partial(_attn_kernel, kind, lambda_init, n_iters),
        grid=(batch, groups),
        in_specs=in_specs,
        out_specs=pl.BlockSpec((rows, seq), lambda b, g: (g, b)),
        out_shape=jax.ShapeDtypeStruct((D_MODEL, t), BF16),
        scratch_shapes=scratch,
        compiler_params=_params("parallel", "parallel"),
        name=kind + "_attn",
    )(tab, qt, k, vt_ext, bias, *extras)


def _rotary_tables(seq):
    d = RET_DK
    inv_freq = ROPE_BASE ** (-np.arange(0, d, 2, dtype=np.float64) / d)
    ang = np.arange(seq, dtype=np.float64)[:, None] * inv_freq[None, :]
    return jnp.asarray(np.cos(ang), F32), jnp.asarray(np.sin(ang), F32)


def _retention_decay_tables():
    c_len = RET_CHUNK
    log_gamma = np.log(1.0 - 2.0 ** (-5.0 - np.arange(RET_HEADS, dtype=np.float64)))
    pos = np.arange(c_len, dtype=np.float64)
    rel = pos[:, None] - pos[None, :]
    dm = np.where(rel >= 0, np.exp(np.maximum(rel, 0.0)[None] * log_gamma[:, None, None]), 0.0)
    rs = np.exp((pos + 1.0)[None, :] * log_gamma[:, None])[:, :, None]
    ks = np.exp((c_len - 1.0 - pos)[None, :] * log_gamma[:, None])[:, :, None]
    cd = np.exp(c_len * log_gamma)[:, None, None]
    return tuple(jnp.asarray(a, F32) for a in (dm, rs, ks, cd))


def _block_mean_matrix(seq):
    nb = seq // MOBA_BLOCK
    m = (np.arange(seq)[None, :] // MOBA_BLOCK == np.arange(nb)[:, None]) / float(MOBA_BLOCK)
    return jnp.asarray(m, BF16)


def _head_group_matrix():
    g = np.arange(COL_TILE)[:, None] // HEAD_DIM == np.arange(COL_TILE)[None, :] // HEAD_DIM
    return jnp.asarray(g / float(HEAD_DIM), BF16)


def kernel(x, rel_bias, norm1, norm2, w_up, w_down, ret_w_in, ret_w_out,
           moba_w_in, moba_q_norm, moba_k_norm, moba_w_out,
           diff_w_in, diff_q_norm, diff_k_norm, diff_lambda, diff_subln, diff_w_out):
    batch, seq, d = x.shape
    depth = norm1.shape[0]
    assert d == D_MODEL and seq % ROW_TILE == 0 and seq % ATT_TILE == 0 and seq % RET_CHUNK == 0
    assert seq % MOBA_BLOCK == 0 and MOBA_BLOCK == ATT_TILE
    t = batch * seq
    xf = x.reshape(t, d)

    bias = _bias_tiles(rel_bias.astype(F32), _bucket_index_tiles())
    grp = _head_group_matrix()
    q_scale = HEAD_DIM ** -0.5 * LOG2E
    wu, wd = w_up.astype(BF16), w_down.astype(BF16)
    ret_wi, ret_wo = ret_w_in.astype(BF16), ret_w_out.astype(BF16)
    moba_wi, moba_wo = moba_w_in.astype(BF16), moba_w_out.astype(BF16)
    diff_wi, diff_wo = diff_w_in.astype(BF16), diff_w_out.astype(BF16)

    for i in range(depth):
        kind, j = i % N_MIXERS, i // N_MIXERS
        nw1 = norm1[i].reshape(1, d)
        nw2 = norm2[i].reshape(1, d)
        if kind == 0:
            cos, sin = _rotary_tables(seq)
            dm, rs, ks, cd = _retention_decay_tables()
            q, k, v, g = _ret_proj(xf, nw1, ret_wi, j, cos, sin, seq)
            o = _ret_core(q, k, v, dm, rs, ks, cd, batch, seq)
            xf = _out_mlp(o, g, ret_wo, j, xf, nw2, wu, wd, i)
        elif kind == 1:
            qn = (jnp.tile(moba_q_norm[j], ATTN_HEADS) * q_scale).reshape(1, d)
            kn = jnp.tile(moba_k_norm[j], ATTN_HEADS).reshape(1, d)
            qt, k, vt = _attn_proj(xf, nw1, moba_wi, j, qn, kn, grp, _value_group("moba"))
            ot = _attention("moba", qt, k, vt, bias, (_block_mean_matrix(seq),), 0.0, batch, seq)
            xf = _out_mlp(ot, None, moba_wo, j, xf, nw2, wu, wd, i)
        else:
            lambda_init = 0.8 - 0.6 * math.exp(-0.3 * i)
            qn = (jnp.tile(diff_q_norm[j], ATTN_HEADS) * q_scale).reshape(1, d)
            kn = jnp.tile(diff_k_norm[j], ATTN_HEADS).reshape(1, d)
            qt, k, vt = _attn_proj(xf, nw1, diff_wi, j, qn, kn, grp, _value_group("diff"))
            extras = (diff_lambda[j].astype(F32), diff_subln[j].reshape(LANES, 1))
            ot = _attention("diff", qt, k, vt, bias, extras, lambda_init, batch, seq)
            xf = _out_mlp(ot, None, diff_wo, j, xf, nw2, wu, wd, i)
    return xf.reshape(batch, seq, d)
```

```python
import functools
import math

import numpy as np
import jax
import jax.numpy as jnp
from jax import lax
from jax.experimental import pallas as pl
from jax.experimental.pallas import tpu as pltpu

F32 = jnp.float32
BF16 = jnp.bfloat16

D_MODEL = 1024
N_MIXERS = 3
RET_HEADS = 4
RET_DK = D_MODEL // RET_HEADS
RET_DV = 2 * RET_DK
ROPE_BASE = 10000.0
ATTN_HEADS = 16
HEAD_DIM = D_MODEL // ATTN_HEADS
MOBA_BLOCK = 256
MOBA_TOPK = 3
REL_BUCKETS = 32
REL_MAX_EXACT = REL_BUCKETS // 2
REL_MAX_DISTANCE = 1024
D_FF = 4 * D_MODEL
EPS = 1e-6
NEG = -1e30
LOG2E = math.log2(math.e)

LANES = 128
BF16_SUBLANES = 16
VMEM_LIMIT_BYTES = 56 * 1024 * 1024

ROW_TILE = 512
COL_TILE = 256
FF_TILE = 1024
GATE_TILE = 512
RET_CHUNK = 256
ATT_TILE = 256
BIAS_TILES = 6
HEADS_PER_PAIR = LANES // HEAD_DIM
PAIRS_PER_STEP = 2
HEADS_PER_STEP = HEADS_PER_PAIR * PAIRS_PER_STEP
ONES_ROWS = BF16_SUBLANES


def _value_group(kind):
    return HEAD_DIM if kind == "moba" else LANES


def _params(*sem):
    return pltpu.CompilerParams(dimension_semantics=sem, vmem_limit_bytes=VMEM_LIMIT_BYTES)


def _rms(xf, w):
    ms = jnp.mean(xf * xf, axis=-1, keepdims=True)
    return xf * lax.rsqrt(ms + EPS) * w


def _dot(a, b):
    return jnp.dot(a, b, preferred_element_type=F32)


def _dot_nt(a, b):
    return lax.dot_general(a, b, (((1,), (1,)), ((), ())), preferred_element_type=F32)


def _dot_tn(a, b):
    return lax.dot_general(a, b, (((0,), (0,)), ((), ())), preferred_element_type=F32)


def _ret_proj_kernel(x_ref, nw_ref, w_ref, cos_ref, sin_ref, q_ref, k_ref, v_ref, g_ref):
    h = _rms(x_ref[...], nw_ref[...]).astype(BF16)
    cos = cos_ref[...]
    sin = sin_ref[...]
    half = RET_DK // 2
    chunks = []
    for out_ref, scale in ((q_ref, 1.0), (k_ref, RET_DK ** -0.5)):
        chunks += [(out_ref, hd * RET_DK, scale) for hd in range(RET_HEADS)]
    for out_ref in (v_ref, g_ref):
        chunks += [(out_ref, c * COL_TILE, None) for c in range(RET_HEADS * RET_DV // COL_TILE)]
    assert RET_DK == COL_TILE

    def project(n):
        return _dot(h, w_ref[:, n * COL_TILE:(n + 1) * COL_TILE])

    acc_next = project(0)
    for n, (out_ref, col0, scale) in enumerate(chunks):
        acc = acc_next
        if n + 1 < len(chunks):
            acc_next = project(n + 1)
        if scale is None:
            out_ref[:, col0:col0 + COL_TILE] = acc.astype(BF16)
        else:
            x1 = acc[:, :half]
            x2 = acc[:, half:]
            r1 = x1 * cos - x2 * sin
            r2 = x1 * sin + x2 * cos
            if scale != 1.0:
                r1 = r1 * scale
                r2 = r2 * scale
            out_ref[:, col0:col0 + half] = r1.astype(BF16)
            out_ref[:, col0 + half:col0 + RET_DK] = r2.astype(BF16)


def _ret_proj(x, nw, w, layer, cos, sin, seq):
    t = x.shape[0]
    n_in = w.shape[2]
    tiles_per_seq = seq // ROW_TILE
    row = lambda i: (i, 0)
    const = lambda i: (0, 0)
    pos = lambda i: (i % tiles_per_seq, 0)
    n_qk = RET_HEADS * RET_DK
    n_v = RET_HEADS * RET_DV
    return pl.pallas_call(
        _ret_proj_kernel,
        grid=(t // ROW_TILE,),
        in_specs=[
            pl.BlockSpec((ROW_TILE, D_MODEL), row),
            pl.BlockSpec((1, D_MODEL), const),
            pl.BlockSpec((None, D_MODEL, n_in), lambda i: (layer, 0, 0)),
            pl.BlockSpec((ROW_TILE, RET_DK // 2), pos),
            pl.BlockSpec((ROW_TILE, RET_DK // 2), pos),
        ],
        out_specs=[
            pl.BlockSpec((ROW_TILE, n_qk), row),
            pl.BlockSpec((ROW_TILE, n_qk), row),
            pl.BlockSpec((ROW_TILE, n_v), row),
            pl.BlockSpec((ROW_TILE, n_v), row),
        ],
        out_shape=[
            jax.ShapeDtypeStruct((t, n_qk), BF16),
            jax.ShapeDtypeStruct((t, n_qk), BF16),
            jax.ShapeDtypeStruct((t, n_v), BF16),
            jax.ShapeDtypeStruct((t, n_v), BF16),
        ],
        compiler_params=_params("parallel"),
        name="ret_proj",
    )(x, nw, w, cos, sin)


def _attn_proj_kernel(v_group, x_ref, nw_ref, w_ref, qn_ref, kn_ref, grp_ref, qt_ref, k_ref, vt_ref):
    h = _rms(x_ref[...], nw_ref[...]).astype(BF16)
    grp = grp_ref[...]
    groups_per_chunk = COL_TILE // v_group
    chunks_per_part = D_MODEL // COL_TILE
    n_chunks = 3 * chunks_per_part

    def project(n):
        return _dot(h, w_ref[:, n * COL_TILE:(n + 1) * COL_TILE])

    acc_next = project(0)
    for n in range(n_chunks):
        acc = acc_next
        if n + 1 < n_chunks:
            acc_next = project(n + 1)
        part, c = divmod(n, chunks_per_part)
        cols = slice(c * COL_TILE, (c + 1) * COL_TILE)
        if part < 2:
            hw_ref = qn_ref if part == 0 else kn_ref
            ms = _dot((acc * acc).astype(BF16), grp)
            acc = acc * lax.rsqrt(ms + EPS) * hw_ref[:, cols]
        if part == 0:
            qt_ref[cols, :] = acc.T.astype(BF16)
        elif part == 1:
            k_ref[:, cols] = acc.astype(BF16)
        else:
            acc_t = acc.T.astype(BF16)
            for p in range(groups_per_chunk):
                row0 = (c * groups_per_chunk + p) * (v_group + ONES_ROWS)
                vt_ref[row0:row0 + v_group, :] = acc_t[p * v_group:(p + 1) * v_group, :]
                vt_ref[row0 + v_group:row0 + v_group + ONES_ROWS, :] = jnp.ones((ONES_ROWS, acc_t.shape[1]), BF16)


def _attn_proj(x, nw, w, layer, qn, kn, grp, v_group):
    t = x.shape[0]
    row = lambda i: (i, 0)
    col = lambda i: (0, i)
    const = lambda i: (0, 0)
    vt_rows = (D_MODEL // v_group) * (v_group + ONES_ROWS)
    return pl.pallas_call(
        functools.partial(_attn_proj_kernel, v_group),
        grid=(t // ROW_TILE,),
        in_specs=[
            pl.BlockSpec((ROW_TILE, D_MODEL), row),
            pl.BlockSpec((1, D_MODEL), const),
            pl.BlockSpec((None, D_MODEL, 3 * D_MODEL), lambda i: (layer, 0, 0)),
            pl.BlockSpec((1, D_MODEL), const),
            pl.BlockSpec((1, D_MODEL), const),
            pl.BlockSpec((COL_TILE, COL_TILE), const),
        ],
        out_specs=[
            pl.BlockSpec((D_MODEL, ROW_TILE), col),
            pl.BlockSpec((ROW_TILE, D_MODEL), row),
            pl.BlockSpec((vt_rows, ROW_TILE), col),
        ],
        out_shape=[
            jax.ShapeDtypeStruct((D_MODEL, t), BF16),
            jax.ShapeDtypeStruct((t, D_MODEL), BF16),
            jax.ShapeDtypeStruct((vt_rows, t), BF16),
        ],
        compiler_params=_params("parallel"),
        name="attn_proj",
    )(x, nw, w, qn, kn, grp)


def _ret_core_kernel(q_ref, k_ref, v_ref, dm_ref, rs_ref, ks_ref, cd_ref, o_ref, state_ref, raw_ref):
    c_len = RET_CHUNK
    n_chunks = q_ref.shape[0] // c_len
    state_ref[...] = jnp.zeros_like(state_ref)
    raw_ref[...] = jnp.zeros_like(raw_ref)

    def chunk(c):
        return pl.ds(pl.multiple_of(c * c_len, c_len), c_len)

    def masked_scores(c):
        return (_dot_nt(q_ref[chunk(c), :], k_ref[chunk(c), :]) * dm_ref[0]).astype(BF16)

    def normalise(c, slot):
        o = raw_ref[slot]
        o_ref[chunk(c), :] = (o * lax.rsqrt(jnp.mean(o * o, axis=-1, keepdims=True) + EPS)).astype(BF16)

    def step(c, slot, s_cur):
        sl = chunk(c)
        q = q_ref[sl, :]
        k = k_ref[sl, :]
        v = v_ref[sl, :]
        state = state_ref[...]
        kd = (k.astype(F32) * ks_ref[0]).astype(BF16)
        kv = _dot_tn(kd, v)
        inner = _dot(s_cur, v)
        cross = _dot(q, state.astype(BF16))
        s_next = masked_scores(jnp.minimum(c + 1, n_chunks - 1))
        normalise(jnp.maximum(c - 1, 0), 1 - slot)
        raw_ref[slot] = inner + cross * rs_ref[0]
        state_ref[...] = state * cd_ref[0] + kv
        return s_next

    def body(u, s_cur):
        return step(2 * u + 1, 1, step(2 * u, 0, s_cur))

    lax.fori_loop(0, n_chunks // 2, body, masked_scores(0))
    normalise(n_chunks - 1, 1)


def _ret_core(q, k, v, dm, rs, ks, cd, batch, seq):
    t = q.shape[0]
    c_len = RET_CHUNK
    tok = lambda b, h: (b, h)
    head3 = lambda b, h: (h, 0, 0)
    return pl.pallas_call(
        _ret_core_kernel,
        grid=(batch, RET_HEADS),
        in_specs=[
            pl.BlockSpec((seq, RET_DK), tok),
            pl.BlockSpec((seq, RET_DK), tok),
            pl.BlockSpec((seq, RET_DV), tok),
            pl.BlockSpec((1, c_len, c_len), head3),
            pl.BlockSpec((1, c_len, 1), head3),
            pl.BlockSpec((1, c_len, 1), head3),
            pl.BlockSpec((1, 1, 1), head3),
        ],
        out_specs=pl.BlockSpec((seq, RET_DV), tok),
        out_shape=jax.ShapeDtypeStruct((t, RET_HEADS * RET_DV), BF16),
        scratch_shapes=[pltpu.VMEM((RET_DK, RET_DV), F32), pltpu.VMEM((2, RET_CHUNK, RET_DV), F32)],
        compiler_params=_params("parallel", "parallel"),
        name="ret_core",
    )(q, k, v, dm, rs, ks, cd)


def _out_mlp_kernel(gated, *refs):
    if gated:
        a_ref, g_ref, wo_ref, x_ref, nw_ref, wu_ref, wd_ref, o_ref = refs
        n_chunks = a_ref.shape[1] // GATE_TILE

        def gate(c):
            cols = slice(c * GATE_TILE, (c + 1) * GATE_TILE)
            g = g_ref[:, cols].astype(F32)
            return (g * (1.0 / (1.0 + jnp.exp(-g))) * a_ref[:, cols].astype(F32)).astype(BF16)

        a_next = gate(0)
        mix = None
        for c in range(n_chunks):
            a = a_next
            if c + 1 < n_chunks:
                a_next = gate(c + 1)
            part = _dot(a, wo_ref[c * GATE_TILE:(c + 1) * GATE_TILE, :])
            mix = part if mix is None else mix + part
    else:
        a_ref, wo_ref, x_ref, nw_ref, wu_ref, wd_ref, o_ref = refs
        mix = _dot_tn(a_ref[...], wo_ref[...])
    x = x_ref[...] + mix
    h = _rms(x, nw_ref[...]).astype(BF16)
    acc = x
    for c in range(D_FF // FF_TILE):
        u = _dot(h, wu_ref[:, c * FF_TILE:(c + 1) * FF_TILE])
        u = jnp.maximum(u, 0.0)
        acc = acc + _dot((u * u).astype(BF16), wd_ref[c * FF_TILE:(c + 1) * FF_TILE, :])
    o_ref[...] = acc


def _out_mlp(a, g, wo, mixer_layer, x, nw, wu, wd, layer):
    t = x.shape[0]
    ka = wo.shape[1]
    row = lambda i: (i, 0)
    const = lambda i: (0, 0)
    gated = g is not None
    if gated:
        mix_specs = [pl.BlockSpec((ROW_TILE, ka), row), pl.BlockSpec((ROW_TILE, ka), row)]
        mix_args = [a, g]
    else:
        mix_specs = [pl.BlockSpec((ka, ROW_TILE), lambda i: (0, i))]
        mix_args = [a]
    return pl.pallas_call(
        functools.partial(_out_mlp_kernel, gated),
        grid=(t // ROW_TILE,),
        in_specs=mix_specs + [
            pl.BlockSpec((None, ka, D_MODEL), lambda i: (mixer_layer, 0, 0)),
            pl.BlockSpec((ROW_TILE, D_MODEL), row),
            pl.BlockSpec((1, D_MODEL), const),
            pl.BlockSpec((None, D_MODEL, D_FF), lambda i: (layer, 0, 0), pipeline_mode=pl.Buffered(1)),
            pl.BlockSpec((None, D_FF, D_MODEL), lambda i: (layer, 0, 0), pipeline_mode=pl.Buffered(1)),
        ],
        out_specs=pl.BlockSpec((ROW_TILE, D_MODEL), row),
        out_shape=jax.ShapeDtypeStruct((t, D_MODEL), F32),
        compiler_params=_params("parallel"),
        name="out_mlp_gated" if gated else "out_mlp",
    )(*mix_args, wo, x, nw, wu, wd)


def _bias_tiles_kernel(bucket_ranges, rb_ref, idx_ref, o_ref):
    head = pl.program_id(0)
    for d, (lo, hi) in enumerate(bucket_ranges):
        idx = idx_ref[d]
        acc = jnp.full(idx.shape, NEG, F32)
        for b in range(lo, hi + 1):
            acc = jnp.where(idx == b, rb_ref[b, head] * LOG2E, acc)
        o_ref[0, d] = acc


def _bucket_ranges():
    ranges = []
    for d in range(BIAS_TILES):
        lo_dist = max(d * ATT_TILE - (ATT_TILE - 1), 0)
        hi_dist = d * ATT_TILE + (ATT_TILE - 1)

        def bucket(n):
            if n < REL_MAX_EXACT:
                return n
            return min(REL_MAX_EXACT + int(math.log(n / REL_MAX_EXACT) / math.log(REL_MAX_DISTANCE / REL_MAX_EXACT)
                                           * (REL_BUCKETS - REL_MAX_EXACT)), REL_BUCKETS - 1)

        ranges.append((max(bucket(lo_dist) - 1, 0), min(bucket(hi_dist) + 1, REL_BUCKETS - 1)))
    return tuple(ranges)


def _bias_tiles(rel_bias, bucket_idx):
    nd, tq, tk = bucket_idx.shape
    return pl.pallas_call(
        functools.partial(_bias_tiles_kernel, _bucket_ranges()),
        grid=(ATTN_HEADS,),
        in_specs=[
            pl.BlockSpec(memory_space=pltpu.SMEM),
            pl.BlockSpec((nd, tq, tk), lambda h: (0, 0, 0)),
        ],
        out_specs=pl.BlockSpec((1, nd, tq, tk), lambda h: (h, 0, 0, 0)),
        out_shape=jax.ShapeDtypeStruct((ATTN_HEADS, nd, tq, tk), F32),
        compiler_params=_params("parallel"),
        name="bias_tiles",
    )(rel_bias, bucket_idx)


def _rel_bucket(dist):
    n = jnp.maximum(dist, 0)
    nf = jnp.maximum(n, 1).astype(F32)
    large = REL_MAX_EXACT + (jnp.log(nf / REL_MAX_EXACT) / math.log(REL_MAX_DISTANCE / REL_MAX_EXACT)
                             * (REL_BUCKETS - REL_MAX_EXACT)).astype(jnp.int32)
    large = jnp.minimum(large, REL_BUCKETS - 1)
    return jnp.where(n < REL_MAX_EXACT, n, large)


def _bucket_index_tiles():
    r = np.arange(ATT_TILE)
    dist = (np.arange(BIAS_TILES)[:, None, None] * ATT_TILE + r[None, None, :] - r[None, :, None])
    dist = jnp.asarray(dist, jnp.int32)
    return jnp.where(dist >= 0, _rel_bucket(dist), REL_BUCKETS).astype(jnp.int32)


TAB_Q, TAB_K, TAB_DELTA, TAB_FIRST, TAB_ACC = range(5)
PIPE_LAG = 2
PIPE_UNROLL = 6
QUERY_HALVES = tuple(slice(h * LANES, (h + 1) * LANES) for h in range(ATT_TILE // LANES))


def _tile_schedule(nq, own_first):
    rows = []
    for i in range(nq):
        keys = ([i] + list(range(i))) if own_first else list(range(i + 1))
        for n, j in enumerate(keys):
            rows.append((i, j, min(i - j, BIAS_TILES - 1), int(n == 0), i))
    n_iters = -(-(len(rows) + PIPE_LAG) // PIPE_UNROLL) * PIPE_UNROLL
    idle = (0, 0, 0, 1, nq)
    cols = [idle] * PIPE_LAG + rows
    cols += [idle] * (n_iters + PIPE_LAG - len(cols))
    return jnp.asarray(np.array(cols, np.int32).T), n_iters


def _tile_slice(idx):
    return pl.ds(pl.multiple_of(idx * ATT_TILE, ATT_TILE), ATT_TILE)


def _head_row_mask(h):
    row = lax.broadcasted_iota(jnp.int32, (LANES, 1), 0)
    return (row >= h * HEAD_DIM) & (row < (h + 1) * HEAD_DIM)


def _pair_rows(pair, n=LANES):
    return slice(pair * n, (pair + 1) * n)


def _attn_kernel(kind, lambda_init, n_iters, tab_ref, qt_ref, k_ref, vt_ref, bias_ref, *rest):
    if kind == "moba":
        blk_ref, ot_ref, qts_ref, s_ref, p_ref, acc_ref, neg_ref = rest
    else:
        lam_ref, sw_ref, ot_ref, qts_ref, s_ref, p_ref, acc_ref = rest
    heads = range(HEADS_PER_STEP)
    seq = qt_ref.shape[1]
    acc_rows = acc_ref.shape[2]

    for e in heads:
        pair, h = divmod(e, HEADS_PER_PAIR)
        qf = qt_ref[_pair_rows(pair), :].astype(F32)
        qts_ref[e] = jnp.where(_head_row_mask(h), qf, 0.0).astype(BF16)
    s_ref[...] = jnp.zeros_like(s_ref)
    p_ref[...] = jnp.zeros_like(p_ref)
    acc_ref[...] = jnp.zeros_like(acc_ref)

    if kind == "moba":
        nb = blk_ref.shape[0]
        kmean = _dot(blk_ref[...], k_ref[...])
        km_hi = kmean.astype(BF16)
        km_lo = (kmean - km_hi.astype(F32)).astype(BF16)
        blk_id = lax.broadcasted_iota(jnp.int32, (nb, seq), 0)
        own_blk = lax.broadcasted_iota(jnp.int32, (nb, seq), 1) // MOBA_BLOCK
        past = blk_id < own_blk
        for e in heads:
            pair = e // HEADS_PER_PAIR
            q_e = qts_ref[e]
            gate = _dot(km_hi[:, _pair_rows(pair)], q_e) + _dot(km_lo[:, _pair_rows(pair)], q_e)
            gate = jnp.where(past, gate, NEG)
            rank = jnp.zeros((nb, seq), jnp.int32)
            for c in range(nb):
                gc = gate[c:c + 1, :]
                tie = jnp.where(blk_id > c, 1, 0)
                rank = rank + jnp.where(gc > gate, 1, jnp.where(gc == gate, tie, 0))
            chosen = jnp.where(rank < MOBA_TOPK, jnp.where(past, 1, 0), 0)
            keep = jnp.maximum(chosen, jnp.where(blk_id == own_blk, 1, 0))
            neg_ref[e] = jnp.where(keep == 1, 0.0, NEG)

    def step(t, cur, carry):
        nxt = 1 - cur
        m_prev, alpha_prev, mtile_prev = carry
        ic = tab_ref[TAB_ACC, t]
        jc = tab_ref[TAB_K, t]
        ib = tab_ref[TAB_Q, t + 1]
        jb = tab_ref[TAB_K, t + 1]
        first = tab_ref[TAB_FIRST, t + 1] != 0
        ia = tab_ref[TAB_Q, t + 2]
        ja = tab_ref[TAB_K, t + 2]
        da = tab_ref[TAB_DELTA, t + 2]
        m_new, alpha_new, mtile_new = [], [], []
        for e in heads:
            pair = e // HEADS_PER_PAIR

            vt = vt_ref[_pair_rows(e if kind == "moba" else pair, acc_rows), _tile_slice(jc)]
            pv = _dot(vt, p_ref[cur, e])
            for hf, lanes in enumerate(QUERY_HALVES):
                acc_ref[ic, e, :, lanes] = alpha_prev[e][hf] * acc_ref[ic, e, :, lanes] + pv[:, lanes]

            kt = k_ref[_tile_slice(ja), _pair_rows(pair)]
            sb = _dot(kt, qts_ref[e, :, _tile_slice(ia)]) + bias_ref[e, da]
            s_ref[nxt, e] = sb
            mtile_new.append(tuple(jnp.max(sb[:, lanes], axis=0, keepdims=True) for lanes in QUERY_HALVES))

            m_parts, alpha_parts = [], []
            if kind == "moba":
                neg_row = neg_ref[e, pl.ds(jb, 1), _tile_slice(ib)]
            for hf, lanes in enumerate(QUERY_HALVES):
                s = s_ref[cur, e, :, lanes]
                m_in = jnp.where(first, -jnp.inf, m_prev[e][hf])
                m_tile = mtile_prev[e][hf]
                if kind == "moba":
                    neg = neg_row[:, lanes]
                    m_e = jnp.maximum(m_in, m_tile + neg)
                    shift = m_e - neg
                else:
                    m_e = jnp.maximum(m_in, m_tile)
                    shift = m_e
                p_ref[nxt, e, :, lanes] = jnp.exp2(s - shift).astype(BF16)
                alpha_parts.append(jnp.exp2(m_in - m_e))
                m_parts.append(m_e)
            alpha_new.append(tuple(alpha_parts))
            m_new.append(tuple(m_parts))

        return tuple(m_new), tuple(alpha_new), tuple(mtile_new)

    zeros = tuple(tuple(jnp.zeros((1, LANES), F32) for _ in QUERY_HALVES) for _ in heads)

    def body(u, carry):
        for r in range(PIPE_UNROLL):
            carry = step(PIPE_UNROLL * u + r, r % 2, carry)
        return carry

    lax.fori_loop(0, n_iters // PIPE_UNROLL, body, (zeros, zeros, zeros))

    def emit(i, carry):
        cols = _tile_slice(i)
        v_group = acc_ref.shape[2] - ONES_ROWS
        for pair in range(PAIRS_PER_STEP):
            outs = []
            for h in range(HEADS_PER_PAIR):
                e = pair * HEADS_PER_PAIR + h
                outs.append(acc_ref[i, e, :v_group, :] * (1.0 / acc_ref[i, e, v_group:v_group + 1, :]))
            if kind == "moba":
                o = jnp.concatenate(outs, axis=0)
            else:
                lam = lam_ref[...]
                lam_full = (jnp.exp(jnp.sum(lam[0:1] * lam[1:2], axis=-1, keepdims=True))
                            - jnp.exp(jnp.sum(lam[2:3] * lam[3:4], axis=-1, keepdims=True)) + lambda_init)
                o = outs[0] - lam_full * outs[1]
                o = (o * lax.rsqrt(jnp.mean(o * o, axis=0, keepdims=True) + EPS)
                     * sw_ref[...] * (1.0 - lambda_init))
            ot_ref[_pair_rows(pair), cols] = o.astype(BF16)
        return carry

    lax.fori_loop(0, seq // ATT_TILE, emit, 0)


def _attention(kind, qt, k, vt_ext, bias, extras, lambda_init, batch, seq):
    v_group = _value_group(kind)
    acc_rows = v_group + ONES_ROWS
    vt_block_rows = acc_rows * (LANES * PAIRS_PER_STEP // v_group)
    t = k.shape[0]
    nq = seq // ATT_TILE
    tab, n_iters = _tile_schedule(nq, own_first=(kind == "moba"))
    groups = ATTN_HEADS // HEADS_PER_STEP
    rows = LANES * PAIRS_PER_STEP
    in_specs = [
        pl.BlockSpec(memory_space=pltpu.SMEM),
        pl.BlockSpec((rows, seq), lambda g, b: (g, b)),
        pl.BlockSpec((seq, rows), lambda g, b: (b, g)),
        pl.BlockSpec((vt_block_rows, seq), lambda g, b: (g, b)),
        pl.BlockSpec((HEADS_PER_STEP, BIAS_TILES, ATT_TILE, ATT_TILE), lambda g, b: (g, 0, 0, 0)),
    ]
    scratch = [
        pltpu.VMEM((HEADS_PER_STEP, LANES, seq), BF16),
        pltpu.VMEM((2, HEADS_PER_STEP, ATT_TILE, ATT_TILE), F32),
        pltpu.VMEM((2, HEADS_PER_STEP, ATT_TILE, ATT_TILE), BF16),
        pltpu.VMEM((nq + 1, HEADS_PER_STEP, acc_rows, ATT_TILE), F32),
    ]
    if kind == "moba":
        (blk,) = extras
        nb = blk.shape[0]
        in_specs.append(pl.BlockSpec((nb, seq), lambda g, b: (0, 0)))
        scratch.append(pltpu.VMEM((HEADS_PER_STEP, nb, seq), F32))
    else:
        lam, sw = extras
        in_specs += [
            pl.BlockSpec((4, HEAD_DIM), lambda g, b: (0, 0)),
            pl.BlockSpec((LANES, 1), lambda g, b: (0, 0)),
        ]
    return pl.pallas_call(
        functools.partial(_attn_kernel, kind, lambda_init, n_iters),
        grid=(groups, batch),
        in_specs=in_specs,
        out_specs=pl.BlockSpec((rows, seq), lambda g, b: (g, b)),
        out_shape=jax.ShapeDtypeStruct((D_MODEL, t), BF16),
        scratch_shapes=scratch,
        compiler_params=_params("parallel", "parallel"),
        name=kind + "_attn",
    )(tab, qt, k, vt_ext, bias, *extras)


def _rotary_tables(seq):
    d = RET_DK
    inv_freq = ROPE_BASE ** (-np.arange(0, d, 2, dtype=np.float64) / d)
    ang = np.arange(seq, dtype=np.float64)[:, None] * inv_freq[None, :]
    return jnp.asarray(np.cos(ang), F32), jnp.asarray(np.sin(ang), F32)


def _retention_decay_tables():
    c_len = RET_CHUNK
    log_gamma = np.log(1.0 - 2.0 ** (-5.0 - np.arange(RET_HEADS, dtype=np.float64)))
    pos = np.arange(c_len, dtype=np.float64)
    rel = pos[:, None] - pos[None, :]
    dm = np.where(rel >= 0, np.exp(np.maximum(rel, 0.0)[None] * log_gamma[:, None, None]), 0.0)
    rs = np.exp((pos + 1.0)[None, :] * log_gamma[:, None])[:, :, None]
    ks = np.exp((c_len - 1.0 - pos)[None, :] * log_gamma[:, None])[:, :, None]
    cd = np.exp(c_len * log_gamma)[:, None, None]
    return tuple(jnp.asarray(a, F32) for a in (dm, rs, ks, cd))


def _block_mean_matrix(seq):
    nb = seq // MOBA_BLOCK
    m = (np.arange(seq)[None, :] // MOBA_BLOCK == np.arange(nb)[:, None]) / float(MOBA_BLOCK)
    return jnp.asarray(m, BF16)


def _head_group_matrix():
    g = np.arange(COL_TILE)[:, None] // HEAD_DIM == np.arange(COL_TILE)[None, :] // HEAD_DIM
    return jnp.asarray(g / float(HEAD_DIM), BF16)


def kernel(x, rel_bias, norm1, norm2, w_up, w_down, ret_w_in, ret_w_out,
           moba_w_in, moba_q_norm, moba_k_norm, moba_w_out,
           diff_w_in, diff_q_norm, diff_k_norm, diff_lambda, diff_subln, diff_w_out):
    batch, seq, d = x.shape
    depth = norm1.shape[0]
    assert d == D_MODEL and seq % ROW_TILE == 0 and seq % ATT_TILE == 0 and seq % (2 * RET_CHUNK) == 0
    assert seq % MOBA_BLOCK == 0 and MOBA_BLOCK == ATT_TILE
    t = batch * seq
    xf = x.reshape(t, d)

    bias = _bias_tiles(rel_bias.astype(F32), _bucket_index_tiles())
    grp = _head_group_matrix()
    q_scale = HEAD_DIM ** -0.5 * LOG2E
    wu, wd = w_up.astype(BF16), w_down.astype(BF16)
    ret_wi, ret_wo = ret_w_in.astype(BF16), ret_w_out.astype(BF16)
    moba_wi, moba_wo = moba_w_in.astype(BF16), moba_w_out.astype(BF16)
    diff_wi, diff_wo = diff_w_in.astype(BF16), diff_w_out.astype(BF16)

    for i in range(depth):
        kind, j = i % N_MIXERS, i // N_MIXERS
        nw1 = norm1[i].reshape(1, d)
        nw2 = norm2[i].reshape(1, d)
        if kind == 0:
            cos, sin = _rotary_tables(seq)
            dm, rs, ks, cd = _retention_decay_tables()
            q, k, v, g = _ret_proj(xf, nw1, ret_wi, j, cos, sin, seq)
            o = _ret_core(q, k, v, dm, rs, ks, cd, batch, seq)
            xf = _out_mlp(o, g, ret_wo, j, xf, nw2, wu, wd, i)
        elif kind == 1:
            qn = (jnp.tile(moba_q_norm[j], ATTN_HEADS) * q_scale).reshape(1, d)
            kn = jnp.tile(moba_k_norm[j], ATTN_HEADS).reshape(1, d)
            qt, k, vt = _attn_proj(xf, nw1, moba_wi, j, qn, kn, grp, _value_group("moba"))
            ot = _attention("moba", qt, k, vt, bias, (_block_mean_matrix(seq),), 0.0, batch, seq)
            xf = _out_mlp(ot, None, moba_wo, j, xf, nw2, wu, wd, i)
        else:
            lambda_init = 0.8 - 0.6 * math.exp(-0.3 * i)
            qn = (jnp.tile(diff_q_norm[j], ATTN_HEADS) * q_scale).reshape(1, d)
            kn = jnp.tile(diff_k_norm[j], ATTN_HEADS).reshape(1, d)
            qt, k, vt = _attn_proj(xf, nw1, diff_wi, j, qn, kn, grp, _value_group("diff"))
            extras = (diff_lambda[j].astype(F32), diff_subln[j].reshape(LANES, 1))
            ot = _attention("diff", qt, k, vt, bias, extras, lambda_init, batch, seq)
            xf = _out_mlp(ot, None, diff_wo, j, xf, nw2, wu, wd, i)
    return xf.reshape(batch, seq, d)
```

```python
import functools
import math

import numpy as np
import jax
import jax.numpy as jnp
from jax import lax
from jax.experimental import pallas as pl
from jax.experimental.pallas import tpu as pltpu

F32 = jnp.float32
BF16 = jnp.bfloat16

D_MODEL = 1024
N_MIXERS = 3
RET_HEADS = 4
RET_DK = D_MODEL // RET_HEADS
RET_DV = 2 * RET_DK
ROPE_BASE = 10000.0
ATTN_HEADS = 16
HEAD_DIM = D_MODEL // ATTN_HEADS
MOBA_BLOCK = 256
MOBA_TOPK = 3
REL_BUCKETS = 32
REL_MAX_EXACT = REL_BUCKETS // 2
REL_MAX_DISTANCE = 1024
D_FF = 4 * D_MODEL
EPS = 1e-6
NEG = -1e30
LOG2E = math.log2(math.e)

LANES = 128
BF16_SUBLANES = 16
VMEM_LIMIT_BYTES = 56 * 1024 * 1024

ROW_TILE = 512
COL_TILE = 256
FF_TILE = 1024
GATE_TILE = 512
RET_CHUNK = 256
ATT_TILE = 256
BIAS_TILES = 6
HEADS_PER_PAIR = LANES // HEAD_DIM
PAIRS_PER_STEP = 2
HEADS_PER_STEP = HEADS_PER_PAIR * PAIRS_PER_STEP
ONES_ROWS = BF16_SUBLANES


def _value_group(kind):
    return HEAD_DIM if kind == "moba" else LANES


def _params(*sem):
    return pltpu.CompilerParams(dimension_semantics=sem, vmem_limit_bytes=VMEM_LIMIT_BYTES)


def _rms(xf, w):
    ms = jnp.mean(xf * xf, axis=-1, keepdims=True)
    return xf * lax.rsqrt(ms + EPS) * w


def _dot(a, b):
    return jnp.dot(a, b, preferred_element_type=F32)


def _dot_nt(a, b):
    return lax.dot_general(a, b, (((1,), (1,)), ((), ())), preferred_element_type=F32)


def _dot_tn(a, b):
    return lax.dot_general(a, b, (((0,), (0,)), ((), ())), preferred_element_type=F32)


def _ret_proj_kernel(x_ref, nw_ref, w_ref, cos_ref, sin_ref, q_ref, k_ref, v_ref, g_ref):
    h = _rms(x_ref[...], nw_ref[...]).astype(BF16)
    cos = cos_ref[...]
    sin = sin_ref[...]
    half = RET_DK // 2
    chunks = []
    for out_ref, scale in ((q_ref, 1.0), (k_ref, RET_DK ** -0.5)):
        chunks += [(out_ref, hd * RET_DK, scale) for hd in range(RET_HEADS)]
    for out_ref in (v_ref, g_ref):
        chunks += [(out_ref, c * COL_TILE, None) for c in range(RET_HEADS * RET_DV // COL_TILE)]
    assert RET_DK == COL_TILE

    def project(n):
        return _dot(h, w_ref[:, n * COL_TILE:(n + 1) * COL_TILE])

    acc_next = project(0)
    for n, (out_ref, col0, scale) in enumerate(chunks):
        acc = acc_next
        if n + 1 < len(chunks):
            acc_next = project(n + 1)
        if scale is None:
            out_ref[:, col0:col0 + COL_TILE] = acc.astype(BF16)
        else:
            x1 = acc[:, :half]
            x2 = acc[:, half:]
            r1 = x1 * cos - x2 * sin
            r2 = x1 * sin + x2 * cos
            if scale != 1.0:
                r1 = r1 * scale
                r2 = r2 * scale
            out_ref[:, col0:col0 + half] = r1.astype(BF16)
            out_ref[:, col0 + half:col0 + RET_DK] = r2.astype(BF16)


def _ret_proj(x, nw, w, layer, cos, sin, seq):
    t = x.shape[0]
    n_in = w.shape[2]
    tiles_per_seq = seq // ROW_TILE
    row = lambda i: (i, 0)
    const = lambda i: (0, 0)
    pos = lambda i: (i % tiles_per_seq, 0)
    n_qk = RET_HEADS * RET_DK
    n_v = RET_HEADS * RET_DV
    return pl.pallas_call(
        _ret_proj_kernel,
        grid=(t // ROW_TILE,),
        in_specs=[
            pl.BlockSpec((ROW_TILE, D_MODEL), row),
            pl.BlockSpec((1, D_MODEL), const),
            pl.BlockSpec((None, D_MODEL, n_in), lambda i: (layer, 0, 0)),
            pl.BlockSpec((ROW_TILE, RET_DK // 2), pos),
            pl.BlockSpec((ROW_TILE, RET_DK // 2), pos),
        ],
        out_specs=[
            pl.BlockSpec((ROW_TILE, n_qk), row),
            pl.BlockSpec((ROW_TILE, n_qk), row),
            pl.BlockSpec((ROW_TILE, n_v), row),
            pl.BlockSpec((ROW_TILE, n_v), row),
        ],
        out_shape=[
            jax.ShapeDtypeStruct((t, n_qk), BF16),
            jax.ShapeDtypeStruct((t, n_qk), BF16),
            jax.ShapeDtypeStruct((t, n_v), BF16),
            jax.ShapeDtypeStruct((t, n_v), BF16),
        ],
        compiler_params=_params("parallel"),
        name="ret_proj",
    )(x, nw, w, cos, sin)


def _attn_proj_kernel(v_group, x_ref, nw_ref, w_ref, qn_ref, kn_ref, grp_ref, qt_ref, k_ref, vt_ref):
    h = _rms(x_ref[...], nw_ref[...]).astype(BF16)
    grp = grp_ref[...]
    groups_per_chunk = COL_TILE // v_group
    chunks_per_part = D_MODEL // COL_TILE
    n_chunks = 3 * chunks_per_part

    def project(n):
        return _dot(h, w_ref[:, n * COL_TILE:(n + 1) * COL_TILE])

    acc_next = project(0)
    for n in range(n_chunks):
        acc = acc_next
        if n + 1 < n_chunks:
            acc_next = project(n + 1)
        part, c = divmod(n, chunks_per_part)
        cols = slice(c * COL_TILE, (c + 1) * COL_TILE)
        if part < 2:
            hw_ref = qn_ref if part == 0 else kn_ref
            ms = _dot((acc * acc).astype(BF16), grp)
            acc = acc * lax.rsqrt(ms + EPS) * hw_ref[:, cols]
        if part == 0:
            qt_ref[cols, :] = acc.T.astype(BF16)
        elif part == 1:
            k_ref[:, cols] = acc.astype(BF16)
        else:
            acc_t = acc.T.astype(BF16)
            for p in range(groups_per_chunk):
                row0 = (c * groups_per_chunk + p) * (v_group + ONES_ROWS)
                vt_ref[row0:row0 + v_group, :] = acc_t[p * v_group:(p + 1) * v_group, :]
                vt_ref[row0 + v_group:row0 + v_group + ONES_ROWS, :] = jnp.ones((ONES_ROWS, acc_t.shape[1]), BF16)


def _attn_proj(x, nw, w, layer, qn, kn, grp, v_group):
    t = x.shape[0]
    row = lambda i: (i, 0)
    col = lambda i: (0, i)
    const = lambda i: (0, 0)
    vt_rows = (D_MODEL // v_group) * (v_group + ONES_ROWS)
    return pl.pallas_call(
        functools.partial(_attn_proj_kernel, v_group),
        grid=(t // ROW_TILE,),
        in_specs=[
            pl.BlockSpec((ROW_TILE, D_MODEL), row),
            pl.BlockSpec((1, D_MODEL), const),
            pl.BlockSpec((None, D_MODEL, 3 * D_MODEL), lambda i: (layer, 0, 0)),
            pl.BlockSpec((1, D_MODEL), const),
            pl.BlockSpec((1, D_MODEL), const),
            pl.BlockSpec((COL_TILE, COL_TILE), const),
        ],
        out_specs=[
            pl.BlockSpec((D_MODEL, ROW_TILE), col),
            pl.BlockSpec((ROW_TILE, D_MODEL), row),
            pl.BlockSpec((vt_rows, ROW_TILE), col),
        ],
        out_shape=[
            jax.ShapeDtypeStruct((D_MODEL, t), BF16),
            jax.ShapeDtypeStruct((t, D_MODEL), BF16),
            jax.ShapeDtypeStruct((vt_rows, t), BF16),
        ],
        compiler_params=_params("parallel"),
        name="attn_proj",
    )(x, nw, w, qn, kn, grp)


def _ret_core_kernel(q_ref, k_ref, v_ref, dm_ref, rs_ref, ks_ref, cd_ref, o_ref, state_ref, raw_ref):
    c_len = RET_CHUNK
    n_chunks = q_ref.shape[0] // c_len
    state_ref[...] = jnp.zeros_like(state_ref)
    raw_ref[...] = jnp.zeros_like(raw_ref)

    def chunk(c):
        return pl.ds(pl.multiple_of(c * c_len, c_len), c_len)

    def masked_scores(c):
        return (_dot_nt(q_ref[chunk(c), :], k_ref[chunk(c), :]) * dm_ref[0]).astype(BF16)

    def normalise(c, slot):
        o = raw_ref[slot]
        o_ref[chunk(c), :] = (o * lax.rsqrt(jnp.mean(o * o, axis=-1, keepdims=True) + EPS)).astype(BF16)

    def step(c, slot, s_cur):
        sl = chunk(c)
        q = q_ref[sl, :]
        k = k_ref[sl, :]
        v = v_ref[sl, :]
        state = state_ref[...]
        kd = (k.astype(F32) * ks_ref[0]).astype(BF16)
        kv = _dot_tn(kd, v)
        inner = _dot(s_cur, v)
        cross = _dot(q, state.astype(BF16))
        s_next = masked_scores(jnp.minimum(c + 1, n_chunks - 1))
        normalise(jnp.maximum(c - 1, 0), 1 - slot)
        raw_ref[slot] = inner + cross * rs_ref[0]
        state_ref[...] = state * cd_ref[0] + kv
        return s_next

    def body(u, s_cur):
        return step(2 * u + 1, 1, step(2 * u, 0, s_cur))

    lax.fori_loop(0, n_chunks // 2, body, masked_scores(0))
    normalise(n_chunks - 1, 1)


def _ret_core(q, k, v, dm, rs, ks, cd, batch, seq):
    t = q.shape[0]
    c_len = RET_CHUNK
    tok = lambda b, h: (b, h)
    head3 = lambda b, h: (h, 0, 0)
    return pl.pallas_call(
        _ret_core_kernel,
        grid=(batch, RET_HEADS),
        in_specs=[
            pl.BlockSpec((seq, RET_DK), tok),
            pl.BlockSpec((seq, RET_DK), tok),
            pl.BlockSpec((seq, RET_DV), tok),
            pl.BlockSpec((1, c_len, c_len), head3),
            pl.BlockSpec((1, c_len, 1), head3),
            pl.BlockSpec((1, c_len, 1), head3),
            pl.BlockSpec((1, 1, 1), head3),
        ],
        out_specs=pl.BlockSpec((seq, RET_DV), tok),
        out_shape=jax.ShapeDtypeStruct((t, RET_HEADS * RET_DV), BF16),
        scratch_shapes=[pltpu.VMEM((RET_DK, RET_DV), F32), pltpu.VMEM((2, RET_CHUNK, RET_DV), F32)],
        compiler_params=_params("parallel", "parallel"),
        name="ret_core",
    )(q, k, v, dm, rs, ks, cd)


def _out_mlp_kernel(gated, *refs):
    if gated:
        a_ref, g_ref, wo_ref, x_ref, nw_ref, wu_ref, wd_ref, o_ref = refs
        n_chunks = a_ref.shape[1] // GATE_TILE

        def gate(c):
            cols = slice(c * GATE_TILE, (c + 1) * GATE_TILE)
            g = g_ref[:, cols].astype(F32)
            return (g * (1.0 / (1.0 + jnp.exp(-g))) * a_ref[:, cols].astype(F32)).astype(BF16)

        a_next = gate(0)
        mix = None
        for c in range(n_chunks):
            a = a_next
            if c + 1 < n_chunks:
                a_next = gate(c + 1)
            part = _dot(a, wo_ref[c * GATE_TILE:(c + 1) * GATE_TILE, :])
            mix = part if mix is None else mix + part
    else:
        a_ref, wo_ref, x_ref, nw_ref, wu_ref, wd_ref, o_ref = refs
        mix = _dot_tn(a_ref[...], wo_ref[...])
    x = x_ref[...] + mix
    h = _rms(x, nw_ref[...]).astype(BF16)
    acc = x
    for c in range(D_FF // FF_TILE):
        u = _dot(h, wu_ref[:, c * FF_TILE:(c + 1) * FF_TILE])
        u = jnp.maximum(u, 0.0)
        acc = acc + _dot((u * u).astype(BF16), wd_ref[c * FF_TILE:(c + 1) * FF_TILE, :])
    o_ref[...] = acc


def _out_mlp(a, g, wo, mixer_layer, x, nw, wu, wd, layer):
    t = x.shape[0]
    ka = wo.shape[1]
    row = lambda i: (i, 0)
    const = lambda i: (0, 0)
    gated = g is not None
    if gated:
        mix_specs = [pl.BlockSpec((ROW_TILE, ka), row), pl.BlockSpec((ROW_TILE, ka), row)]
        mix_args = [a, g]
    else:
        mix_specs = [pl.BlockSpec((ka, ROW_TILE), lambda i: (0, i))]
        mix_args = [a]
    return pl.pallas_call(
        functools.partial(_out_mlp_kernel, gated),
        grid=(t // ROW_TILE,),
        in_specs=mix_specs + [
            pl.BlockSpec((None, ka, D_MODEL), lambda i: (mixer_layer, 0, 0)),
            pl.BlockSpec((ROW_TILE, D_MODEL), row),
            pl.BlockSpec((1, D_MODEL), const),
            pl.BlockSpec((None, D_MODEL, D_FF), lambda i: (layer, 0, 0), pipeline_mode=pl.Buffered(1)),
            pl.BlockSpec((None, D_FF, D_MODEL), lambda i: (layer, 0, 0), pipeline_mode=pl.Buffered(1)),
        ],
        out_specs=pl.BlockSpec((ROW_TILE, D_MODEL), row),
        out_shape=jax.ShapeDtypeStruct((t, D_MODEL), F32),
        compiler_params=_params("parallel"),
        name="out_mlp_gated" if gated else "out_mlp",
    )(*mix_args, wo, x, nw, wu, wd)


def _bias_tiles_kernel(bucket_ranges, rb_ref, idx_ref, o_ref):
    head = pl.program_id(0)
    for d, (lo, hi) in enumerate(bucket_ranges):
        idx = idx_ref[d]
        acc = jnp.full(idx.shape, NEG, F32)
        for b in range(lo, hi + 1):
            acc = jnp.where(idx == b, rb_ref[b, head] * LOG2E, acc)
        o_ref[0, d] = acc


def _bucket_ranges():
    ranges = []
    for d in range(BIAS_TILES):
        lo_dist = max(d * ATT_TILE - (ATT_TILE - 1), 0)
        hi_dist = d * ATT_TILE + (ATT_TILE - 1)

        def bucket(n):
            if n < REL_MAX_EXACT:
                return n
            return min(REL_MAX_EXACT + int(math.log(n / REL_MAX_EXACT) / math.log(REL_MAX_DISTANCE / REL_MAX_EXACT)
                                           * (REL_BUCKETS - REL_MAX_EXACT)), REL_BUCKETS - 1)

        ranges.append((max(bucket(lo_dist) - 1, 0), min(bucket(hi_dist) + 1, REL_BUCKETS - 1)))
    return tuple(ranges)


def _bias_tiles(rel_bias, bucket_idx):
    nd, tq, tk = bucket_idx.shape
    return pl.pallas_call(
        functools.partial(_bias_tiles_kernel, _bucket_ranges()),
        grid=(ATTN_HEADS,),
        in_specs=[
            pl.BlockSpec(memory_space=pltpu.SMEM),
            pl.BlockSpec((nd, tq, tk), lambda h: (0, 0, 0)),
        ],
        out_specs=pl.BlockSpec((1, nd, tq, tk), lambda h: (h, 0, 0, 0)),
        out_shape=jax.ShapeDtypeStruct((ATTN_HEADS, nd, tq, tk), F32),
        compiler_params=_params("parallel"),
        name="bias_tiles",
    )(rel_bias, bucket_idx)


def _rel_bucket(dist):
    n = jnp.maximum(dist, 0)
    nf = jnp.maximum(n, 1).astype(F32)
    large = REL_MAX_EXACT + (jnp.log(nf / REL_MAX_EXACT) / math.log(REL_MAX_DISTANCE / REL_MAX_EXACT)
                             * (REL_BUCKETS - REL_MAX_EXACT)).astype(jnp.int32)
    large = jnp.minimum(large, REL_BUCKETS - 1)
    return jnp.where(n < REL_MAX_EXACT, n, large)


def _bucket_index_tiles():
    r = np.arange(ATT_TILE)
    dist = (np.arange(BIAS_TILES)[:, None, None] * ATT_TILE + r[None, None, :] - r[None, :, None])
    dist = jnp.asarray(dist, jnp.int32)
    return jnp.where(dist >= 0, _rel_bucket(dist), REL_BUCKETS).astype(jnp.int32)


TAB_Q, TAB_K, TAB_DELTA, TAB_FIRST, TAB_ACC = range(5)
PIPE_LAG = 2
PIPE_UNROLL = 6
QUERY_HALVES = tuple(slice(h * LANES, (h + 1) * LANES) for h in range(ATT_TILE // LANES))


def _tile_schedule(nq, own_first):
    rows = []
    for i in range(nq):
        keys = ([i] + list(range(i))) if own_first else list(range(i + 1))
        for n, j in enumerate(keys):
            rows.append((i, j, min(i - j, BIAS_TILES - 1), int(n == 0), i))
    n_iters = -(-(len(rows) + PIPE_LAG) // PIPE_UNROLL) * PIPE_UNROLL
    idle = (0, 0, 0, 1, nq)
    cols = [idle] * PIPE_LAG + rows
    cols += [idle] * (n_iters + PIPE_LAG - len(cols))
    return jnp.asarray(np.array(cols, np.int32).T), n_iters


def _tile_slice(idx):
    return pl.ds(pl.multiple_of(idx * ATT_TILE, ATT_TILE), ATT_TILE)


def _head_row_mask(h):
    row = lax.broadcasted_iota(jnp.int32, (LANES, 1), 0)
    return (row >= h * HEAD_DIM) & (row < (h + 1) * HEAD_DIM)


def _pair_rows(pair, n=LANES):
    return slice(pair * n, (pair + 1) * n)


def _attn_kernel(kind, lambda_init, n_iters, tab_ref, qt_ref, k_ref, vt_ref, bias_ref, *rest):
    if kind == "moba":
        blk_ref, ot_ref, qts_ref, vts_ref, s_ref, p_ref, acc_ref, neg_ref = rest
    else:
        lam_ref, sw_ref, ot_ref, qts_ref, vts_ref, s_ref, p_ref, acc_ref = rest
    heads = range(HEADS_PER_STEP)
    seq = qt_ref.shape[1]
    acc_rows = acc_ref.shape[2]

    n_tiles = seq // ATT_TILE
    q_heads = []
    for e in heads:
        pair, h = divmod(e, HEADS_PER_PAIR)
        qf = qt_ref[_pair_rows(pair), :].astype(F32)
        q_heads.append(jnp.where(_head_row_mask(h), qf, 0.0).astype(BF16))
        for i in range(n_tiles):
            qts_ref[i, e] = q_heads[e][:, i * ATT_TILE:(i + 1) * ATT_TILE]
    for j in range(n_tiles):
        vts_ref[j] = vt_ref[:, j * ATT_TILE:(j + 1) * ATT_TILE]
    s_ref[...] = jnp.zeros_like(s_ref)
    p_ref[...] = jnp.zeros_like(p_ref)
    acc_ref[...] = jnp.zeros_like(acc_ref)

    if kind == "moba":
        nb = blk_ref.shape[0]
        kmean = _dot(blk_ref[...], k_ref[...])
        km_hi = kmean.astype(BF16)
        km_lo = (kmean - km_hi.astype(F32)).astype(BF16)
        blk_id = lax.broadcasted_iota(jnp.int32, (nb, seq), 0)
        own_blk = lax.broadcasted_iota(jnp.int32, (nb, seq), 1) // MOBA_BLOCK
        past = blk_id < own_blk
        for e in heads:
            pair = e // HEADS_PER_PAIR
            q_e = q_heads[e]
            gate = _dot(km_hi[:, _pair_rows(pair)], q_e) + _dot(km_lo[:, _pair_rows(pair)], q_e)
            gate = jnp.where(past, gate, NEG)
            rank = jnp.zeros((nb, seq), jnp.int32)
            for c in range(nb):
                gc = gate[c:c + 1, :]
                tie = jnp.where(blk_id > c, 1, 0)
                rank = rank + jnp.where(gc > gate, 1, jnp.where(gc == gate, tie, 0))
            chosen = jnp.where(rank < MOBA_TOPK, jnp.where(past, 1, 0), 0)
            keep = jnp.maximum(chosen, jnp.where(blk_id == own_blk, 1, 0))
            neg_ref[e] = jnp.where(keep == 1, 0.0, NEG)

    def step(t, cur, carry):
        nxt = 1 - cur
        m_prev, alpha_prev, mtile_prev = carry
        ic = tab_ref[TAB_ACC, t]
        jc = tab_ref[TAB_K, t]
        ib = tab_ref[TAB_Q, t + 1]
        jb = tab_ref[TAB_K, t + 1]
        first = tab_ref[TAB_FIRST, t + 1] != 0
        ia = tab_ref[TAB_Q, t + 2]
        ja = tab_ref[TAB_K, t + 2]
        da = tab_ref[TAB_DELTA, t + 2]
        m_new, alpha_new, mtile_new = [], [], []
        for e in heads:
            pair = e // HEADS_PER_PAIR

            vt = vts_ref[jc, _pair_rows(e if kind == "moba" else pair, acc_rows), :]
            pv = _dot(vt, p_ref[cur, e])
            for hf, lanes in enumerate(QUERY_HALVES):
                acc_ref[ic, e, :, lanes] = alpha_prev[e][hf] * acc_ref[ic, e, :, lanes] + pv[:, lanes]

            kt = k_ref[_tile_slice(ja), _pair_rows(pair)]
            sb = _dot(kt, qts_ref[ia, e]) + bias_ref[e, da]
            s_ref[nxt, e] = sb
            mtile_new.append(tuple(jnp.max(sb[:, lanes], axis=0, keepdims=True) for lanes in QUERY_HALVES))

            m_parts, alpha_parts = [], []
            if kind == "moba":
                neg_row = neg_ref[e, pl.ds(jb, 1), _tile_slice(ib)]
            for hf, lanes in enumerate(QUERY_HALVES):
                s = s_ref[cur, e, :, lanes]
                m_in = jnp.where(first, -jnp.inf, m_prev[e][hf])
                m_tile = mtile_prev[e][hf]
                if kind == "moba":
                    neg = neg_row[:, lanes]
                    m_e = jnp.maximum(m_in, m_tile + neg)
                    shift = m_e - neg
                else:
                    m_e = jnp.maximum(m_in, m_tile)
                    shift = m_e
                p_ref[nxt, e, :, lanes] = jnp.exp2(s - shift).astype(BF16)
                alpha_parts.append(jnp.exp2(m_in - m_e))
                m_parts.append(m_e)
            alpha_new.append(tuple(alpha_parts))
            m_new.append(tuple(m_parts))

        return tuple(m_new), tuple(alpha_new), tuple(mtile_new)

    zeros = tuple(tuple(jnp.zeros((1, LANES), F32) for _ in QUERY_HALVES) for _ in heads)

    def body(u, carry):
        for r in range(PIPE_UNROLL):
            carry = step(PIPE_UNROLL * u + r, r % 2, carry)
        return carry

    lax.fori_loop(0, n_iters // PIPE_UNROLL, body, (zeros, zeros, zeros))

    def emit(i, carry):
        cols = _tile_slice(i)
        v_group = acc_ref.shape[2] - ONES_ROWS
        for pair in range(PAIRS_PER_STEP):
            outs = []
            for h in range(HEADS_PER_PAIR):
                e = pair * HEADS_PER_PAIR + h
                outs.append(acc_ref[i, e, :v_group, :] * (1.0 / acc_ref[i, e, v_group:v_group + 1, :]))
            if kind == "moba":
                o = jnp.concatenate(outs, axis=0)
            else:
                lam = lam_ref[...]
                lam_full = (jnp.exp(jnp.sum(lam[0:1] * lam[1:2], axis=-1, keepdims=True))
                            - jnp.exp(jnp.sum(lam[2:3] * lam[3:4], axis=-1, keepdims=True)) + lambda_init)
                o = outs[0] - lam_full * outs[1]
                o = (o * lax.rsqrt(jnp.mean(o * o, axis=0, keepdims=True) + EPS)
                     * sw_ref[...] * (1.0 - lambda_init))
            ot_ref[_pair_rows(pair), cols] = o.astype(BF16)
        return carry

    lax.fori_loop(0, seq // ATT_TILE, emit, 0)


def _attention(kind, qt, k, vt_ext, bias, extras, lambda_init, batch, seq):
    v_group = _value_group(kind)
    acc_rows = v_group + ONES_ROWS
    vt_block_rows = acc_rows * (LANES * PAIRS_PER_STEP // v_group)
    t = k.shape[0]
    nq = seq // ATT_TILE
    tab, n_iters = _tile_schedule(nq, own_first=(kind == "moba"))
    groups = ATTN_HEADS // HEADS_PER_STEP
    rows = LANES * PAIRS_PER_STEP
    in_specs = [
        pl.BlockSpec(memory_space=pltpu.SMEM),
        pl.BlockSpec((rows, seq), lambda g, b: (g, b)),
        pl.BlockSpec((seq, rows), lambda g, b: (b, g)),
        pl.BlockSpec((vt_block_rows, seq), lambda g, b: (g, b)),
        pl.BlockSpec((HEADS_PER_STEP, BIAS_TILES, ATT_TILE, ATT_TILE), lambda g, b: (g, 0, 0, 0)),
    ]
    scratch = [
        pltpu.VMEM((nq, HEADS_PER_STEP, LANES, ATT_TILE), BF16),
        pltpu.VMEM((nq, vt_block_rows, ATT_TILE), BF16),
        pltpu.VMEM((2, HEADS_PER_STEP, ATT_TILE, ATT_TILE), F32),
        pltpu.VMEM((2, HEADS_PER_STEP, ATT_TILE, ATT_TILE), BF16),
        pltpu.VMEM((nq + 1, HEADS_PER_STEP, acc_rows, ATT_TILE), F32),
    ]
    if kind == "moba":
        (blk,) = extras
        nb = blk.shape[0]
        in_specs.append(pl.BlockSpec((nb, seq), lambda g, b: (0, 0)))
        scratch.append(pltpu.VMEM((HEADS_PER_STEP, nb, seq), F32))
    else:
        lam, sw = extras
        in_specs += [
            pl.BlockSpec((4, HEAD_DIM), lambda g, b: (0, 0)),
            pl.BlockSpec((LANES, 1), lambda g, b: (0, 0)),
        ]
    return pl.pallas_call(
        functools.partial(_attn_kernel, kind, lambda_init, n_iters),
        grid=(groups, batch),
        in_specs=in_specs,
        out_specs=pl.BlockSpec((rows, seq), lambda g, b: (g, b)),
        out_shape=jax.ShapeDtypeStruct((D_MODEL, t), BF16),
        scratch_shapes=scratch,
        compiler_params=_params("parallel", "parallel"),
        name=kind + "_attn",
    )(tab, qt, k, vt_ext, bias, *extras)


def _rotary_tables(seq):
    d = RET_DK
    inv_freq = ROPE_BASE ** (-np.arange(0, d, 2, dtype=np.float64) / d)
    ang = np.arange(seq, dtype=np.float64)[:, None] * inv_freq[None, :]
    return jnp.asarray(np.cos(ang), F32), jnp.asarray(np.sin(ang), F32)


def _retention_decay_tables():
    c_len = RET_CHUNK
    log_gamma = np.log(1.0 - 2.0 ** (-5.0 - np.arange(RET_HEADS, dtype=np.float64)))
    pos = np.arange(c_len, dtype=np.float64)
    rel = pos[:, None] - pos[None, :]
    dm = np.where(rel >= 0, np.exp(np.maximum(rel, 0.0)[None] * log_gamma[:, None, None]), 0.0)
    rs = np.exp((pos + 1.0)[None, :] * log_gamma[:, None])[:, :, None]
    ks = np.exp((c_len - 1.0 - pos)[None, :] * log_gamma[:, None])[:, :, None]
    cd = np.exp(c_len * log_gamma)[:, None, None]
    return tuple(jnp.asarray(a, F32) for a in (dm, rs, ks, cd))


def _block_mean_matrix(seq):
    nb = seq // MOBA_BLOCK
    m = (np.arange(seq)[None, :] // MOBA_BLOCK == np.arange(nb)[:, None]) / float(MOBA_BLOCK)
    return jnp.asarray(m, BF16)


def _head_group_matrix():
    g = np.arange(COL_TILE)[:, None] // HEAD_DIM == np.arange(COL_TILE)[None, :] // HEAD_DIM
    return jnp.asarray(g / float(HEAD_DIM), BF16)


def kernel(x, rel_bias, norm1, norm2, w_up, w_down, ret_w_in, ret_w_out,
           moba_w_in, moba_q_norm, moba_k_norm, moba_w_out,
           diff_w_in, diff_q_norm, diff_k_norm, diff_lambda, diff_subln, diff_w_out):
    batch, seq, d = x.shape
    depth = norm1.shape[0]
    assert d == D_MODEL and seq % ROW_TILE == 0 and seq % ATT_TILE == 0 and seq % (2 * RET_CHUNK) == 0
    assert seq % MOBA_BLOCK == 0 and MOBA_BLOCK == ATT_TILE
    t = batch * seq
    xf = x.reshape(t, d)

    bias = _bias_tiles(rel_bias.astype(F32), _bucket_index_tiles())
    grp = _head_group_matrix()
    q_scale = HEAD_DIM ** -0.5 * LOG2E
    wu, wd = w_up.astype(BF16), w_down.astype(BF16)
    ret_wi, ret_wo = ret_w_in.astype(BF16), ret_w_out.astype(BF16)
    moba_wi, moba_wo = moba_w_in.astype(BF16), moba_w_out.astype(BF16)
    diff_wi, diff_wo = diff_w_in.astype(BF16), diff_w_out.astype(BF16)

    for i in range(depth):
        kind, j = i % N_MIXERS, i // N_MIXERS
        nw1 = norm1[i].reshape(1, d)
        nw2 = norm2[i].reshape(1, d)
        if kind == 0:
            cos, sin = _rotary_tables(seq)
            dm, rs, ks, cd = _retention_decay_tables()
            q, k, v, g = _ret_proj(xf, nw1, ret_wi, j, cos, sin, seq)
            o = _ret_core(q, k, v, dm, rs, ks, cd, batch, seq)
            xf = _out_mlp(o, g, ret_wo, j, xf, nw2, wu, wd, i)
        elif kind == 1:
            qn = (jnp.tile(moba_q_norm[j], ATTN_HEADS) * q_scale).reshape(1, d)
            kn = jnp.tile(moba_k_norm[j], ATTN_HEADS).reshape(1, d)
            qt, k, vt = _attn_proj(xf, nw1, moba_wi, j, qn, kn, grp, _value_group("moba"))
            ot = _attention("moba", qt, k, vt, bias, (_block_mean_matrix(seq),), 0.0, batch, seq)
            xf = _out_mlp(ot, None, moba_wo, j, xf, nw2, wu, wd, i)
        else:
            lambda_init = 0.8 - 0.6 * math.exp(-0.3 * i)
            qn = (jnp.tile(diff_q_norm[j], ATTN_HEADS) * q_scale).reshape(1, d)
            kn = jnp.tile(diff_k_norm[j], ATTN_HEADS).reshape(1, d)
            qt, k, vt = _attn_proj(xf, nw1, diff_wi, j, qn, kn, grp, _value_group("diff"))
            extras = (diff_lambda[j].astype(F32), diff_subln[j].reshape(LANES, 1))
            ot = _attention("diff", qt, k, vt, bias, extras, lambda_init, batch, seq)
            xf = _out_mlp(ot, None, diff_wo, j, xf, nw2, wu, wd, i)
    return xf.reshape(batch, seq, d)
```

```python
import functools
import math

import numpy as np
import jax
import jax.numpy as jnp
from jax import lax
from jax.experimental import pallas as pl
from jax.experimental.pallas import tpu as pltpu

F32 = jnp.float32
BF16 = jnp.bfloat16

D_MODEL = 1024
N_MIXERS = 3
RET_HEADS = 4
RET_DK = D_MODEL // RET_HEADS
RET_DV = 2 * RET_DK
ROPE_BASE = 10000.0
ATTN_HEADS = 16
HEAD_DIM = D_MODEL // ATTN_HEADS
MOBA_BLOCK = 256
MOBA_TOPK = 3
REL_BUCKETS = 32
REL_MAX_EXACT = REL_BUCKETS // 2
REL_MAX_DISTANCE = 1024
D_FF = 4 * D_MODEL
EPS = 1e-6
NEG = -1e30
LOG2E = math.log2(math.e)

LANES = 128
BF16_SUBLANES = 16
VMEM_LIMIT_BYTES = 56 * 1024 * 1024

ROW_TILE = 512
COL_TILE = 256
FF_TILE = 1024
GATE_TILE = 512
RET_CHUNK = 256
ATT_TILE = 256
BIAS_TILES = 5
HEADS_PER_PAIR = LANES // HEAD_DIM
PAIRS_PER_STEP = 2
HEADS_PER_STEP = HEADS_PER_PAIR * PAIRS_PER_STEP
ONES_ROWS = BF16_SUBLANES


def _value_group(kind):
    return HEAD_DIM if kind == "moba" else LANES


def _params(*sem):
    return pltpu.CompilerParams(dimension_semantics=sem, vmem_limit_bytes=VMEM_LIMIT_BYTES)


def _rms(xf, w):
    ms = jnp.mean(xf * xf, axis=-1, keepdims=True)
    return xf * lax.rsqrt(ms + EPS) * w


def _dot(a, b):
    return jnp.dot(a, b, preferred_element_type=F32)


def _dot_nt(a, b):
    return lax.dot_general(a, b, (((1,), (1,)), ((), ())), preferred_element_type=F32)


def _dot_tn(a, b):
    return lax.dot_general(a, b, (((0,), (0,)), ((), ())), preferred_element_type=F32)


def _ret_proj_kernel(x_ref, nw_ref, w_ref, cos_ref, sin_ref, q_ref, k_ref, v_ref, g_ref):
    h = _rms(x_ref[...], nw_ref[...]).astype(BF16)
    cos = cos_ref[...]
    sin = sin_ref[...]
    half = RET_DK // 2
    chunks = []
    for out_ref, scale in ((q_ref, 1.0), (k_ref, RET_DK ** -0.5)):
        chunks += [(out_ref, hd * RET_DK, scale) for hd in range(RET_HEADS)]
    for out_ref in (v_ref, g_ref):
        chunks += [(out_ref, c * COL_TILE, None) for c in range(RET_HEADS * RET_DV // COL_TILE)]
    assert RET_DK == COL_TILE

    def project(n):
        return _dot(h, w_ref[:, n * COL_TILE:(n + 1) * COL_TILE])

    acc_next = project(0)
    for n, (out_ref, col0, scale) in enumerate(chunks):
        acc = acc_next
        if n + 1 < len(chunks):
            acc_next = project(n + 1)
        if scale is None:
            out_ref[:, col0:col0 + COL_TILE] = acc.astype(BF16)
        else:
            x1 = acc[:, :half]
            x2 = acc[:, half:]
            r1 = x1 * cos - x2 * sin
            r2 = x1 * sin + x2 * cos
            if scale != 1.0:
                r1 = r1 * scale
                r2 = r2 * scale
            out_ref[:, col0:col0 + half] = r1.astype(BF16)
            out_ref[:, col0 + half:col0 + RET_DK] = r2.astype(BF16)


def _ret_proj(x, nw, w, layer, cos, sin, seq):
    t = x.shape[0]
    n_in = w.shape[2]
    tiles_per_seq = seq // ROW_TILE
    row = lambda i: (i, 0)
    const = lambda i: (0, 0)
    pos = lambda i: (i % tiles_per_seq, 0)
    n_qk = RET_HEADS * RET_DK
    n_v = RET_HEADS * RET_DV
    return pl.pallas_call(
        _ret_proj_kernel,
        grid=(t // ROW_TILE,),
        in_specs=[
            pl.BlockSpec((ROW_TILE, D_MODEL), row),
            pl.BlockSpec((1, D_MODEL), const),
            pl.BlockSpec((None, D_MODEL, n_in), lambda i: (layer, 0, 0)),
            pl.BlockSpec((ROW_TILE, RET_DK // 2), pos),
            pl.BlockSpec((ROW_TILE, RET_DK // 2), pos),
        ],
        out_specs=[
            pl.BlockSpec((ROW_TILE, n_qk), row),
            pl.BlockSpec((ROW_TILE, n_qk), row),
            pl.BlockSpec((ROW_TILE, n_v), row),
            pl.BlockSpec((ROW_TILE, n_v), row),
        ],
        out_shape=[
            jax.ShapeDtypeStruct((t, n_qk), BF16),
            jax.ShapeDtypeStruct((t, n_qk), BF16),
            jax.ShapeDtypeStruct((t, n_v), BF16),
            jax.ShapeDtypeStruct((t, n_v), BF16),
        ],
        compiler_params=_params("parallel"),
        name="ret_proj",
    )(x, nw, w, cos, sin)


def _attn_proj_kernel(v_group, x_ref, nw_ref, w_ref, qn_ref, kn_ref, grp_ref, qt_ref, k_ref, vt_ref):
    h = _rms(x_ref[...], nw_ref[...]).astype(BF16)
    grp = grp_ref[...]
    groups_per_chunk = COL_TILE // v_group
    chunks_per_part = D_MODEL // COL_TILE
    n_chunks = 3 * chunks_per_part

    def project(n):
        return _dot(h, w_ref[:, n * COL_TILE:(n + 1) * COL_TILE])

    acc_next = project(0)
    for n in range(n_chunks):
        acc = acc_next
        if n + 1 < n_chunks:
            acc_next = project(n + 1)
        part, c = divmod(n, chunks_per_part)
        cols = slice(c * COL_TILE, (c + 1) * COL_TILE)
        if part < 2:
            hw_ref = qn_ref if part == 0 else kn_ref
            ms = _dot((acc * acc).astype(BF16), grp)
            acc = acc * lax.rsqrt(ms + EPS) * hw_ref[:, cols]
        if part == 0:
            qt_ref[cols, :] = acc.T.astype(BF16)
        elif part == 1:
            k_ref[:, cols] = acc.astype(BF16)
        else:
            acc_t = acc.T.astype(BF16)
            for p in range(groups_per_chunk):
                row0 = (c * groups_per_chunk + p) * (v_group + ONES_ROWS)
                vt_ref[row0:row0 + v_group, :] = acc_t[p * v_group:(p + 1) * v_group, :]
                vt_ref[row0 + v_group:row0 + v_group + ONES_ROWS, :] = jnp.ones((ONES_ROWS, acc_t.shape[1]), BF16)


def _attn_proj(x, nw, w, layer, qn, kn, grp, v_group):
    t = x.shape[0]
    row = lambda i: (i, 0)
    col = lambda i: (0, i)
    const = lambda i: (0, 0)
    vt_rows = (D_MODEL // v_group) * (v_group + ONES_ROWS)
    return pl.pallas_call(
        functools.partial(_attn_proj_kernel, v_group),
        grid=(t // ROW_TILE,),
        in_specs=[
            pl.BlockSpec((ROW_TILE, D_MODEL), row),
            pl.BlockSpec((1, D_MODEL), const),
            pl.BlockSpec((None, D_MODEL, 3 * D_MODEL), lambda i: (layer, 0, 0)),
            pl.BlockSpec((1, D_MODEL), const),
            pl.BlockSpec((1, D_MODEL), const),
            pl.BlockSpec((COL_TILE, COL_TILE), const),
        ],
        out_specs=[
            pl.BlockSpec((D_MODEL, ROW_TILE), col),
            pl.BlockSpec((ROW_TILE, D_MODEL), row),
            pl.BlockSpec((vt_rows, ROW_TILE), col),
        ],
        out_shape=[
            jax.ShapeDtypeStruct((D_MODEL, t), BF16),
            jax.ShapeDtypeStruct((t, D_MODEL), BF16),
            jax.ShapeDtypeStruct((vt_rows, t), BF16),
        ],
        compiler_params=_params("parallel"),
        name="attn_proj",
    )(x, nw, w, qn, kn, grp)


def _ret_core_kernel(q_ref, k_ref, v_ref, dm_ref, rs_ref, ks_ref, cd_ref, o_ref, state_ref, raw_ref):
    c_len = RET_CHUNK
    n_chunks = q_ref.shape[0] // c_len
    state_ref[...] = jnp.zeros_like(state_ref)
    raw_ref[...] = jnp.zeros_like(raw_ref)

    def chunk(c):
        return pl.ds(pl.multiple_of(c * c_len, c_len), c_len)

    def masked_scores(c):
        return (_dot_nt(q_ref[chunk(c), :], k_ref[chunk(c), :]) * dm_ref[0]).astype(BF16)

    def normalise(c, slot):
        o = raw_ref[slot]
        o_ref[chunk(c), :] = (o * lax.rsqrt(jnp.mean(o * o, axis=-1, keepdims=True) + EPS)).astype(BF16)

    def step(c, slot, s_cur):
        sl = chunk(c)
        q = q_ref[sl, :]
        k = k_ref[sl, :]
        v = v_ref[sl, :]
        state = state_ref[...]
        kd = (k.astype(F32) * ks_ref[0]).astype(BF16)
        kv = _dot_tn(kd, v)
        inner = _dot(s_cur, v)
        cross = _dot(q, state.astype(BF16))
        s_next = masked_scores(jnp.minimum(c + 1, n_chunks - 1))
        normalise(jnp.maximum(c - 1, 0), 1 - slot)
        raw_ref[slot] = inner + cross * rs_ref[0]
        state_ref[...] = state * cd_ref[0] + kv
        return s_next

    def body(u, s_cur):
        return step(2 * u + 1, 1, step(2 * u, 0, s_cur))

    lax.fori_loop(0, n_chunks // 2, body, masked_scores(0))
    normalise(n_chunks - 1, 1)


def _ret_core(q, k, v, dm, rs, ks, cd, batch, seq):
    t = q.shape[0]
    c_len = RET_CHUNK
    tok = lambda b, h: (b, h)
    head3 = lambda b, h: (h, 0, 0)
    return pl.pallas_call(
        _ret_core_kernel,
        grid=(batch, RET_HEADS),
        in_specs=[
            pl.BlockSpec((seq, RET_DK), tok),
            pl.BlockSpec((seq, RET_DK), tok),
            pl.BlockSpec((seq, RET_DV), tok),
            pl.BlockSpec((1, c_len, c_len), head3),
            pl.BlockSpec((1, c_len, 1), head3),
            pl.BlockSpec((1, c_len, 1), head3),
            pl.BlockSpec((1, 1, 1), head3),
        ],
        out_specs=pl.BlockSpec((seq, RET_DV), tok),
        out_shape=jax.ShapeDtypeStruct((t, RET_HEADS * RET_DV), BF16),
        scratch_shapes=[pltpu.VMEM((RET_DK, RET_DV), F32), pltpu.VMEM((2, RET_CHUNK, RET_DV), F32)],
        compiler_params=_params("parallel", "parallel"),
        name="ret_core",
    )(q, k, v, dm, rs, ks, cd)


def _out_mlp_kernel(gated, *refs):
    if gated:
        a_ref, g_ref, wo_ref, x_ref, nw_ref, wu_ref, wd_ref, o_ref = refs
        n_chunks = a_ref.shape[1] // GATE_TILE

        def gate(c):
            cols = slice(c * GATE_TILE, (c + 1) * GATE_TILE)
            g = g_ref[:, cols].astype(F32)
            return (g * (1.0 / (1.0 + jnp.exp(-g))) * a_ref[:, cols].astype(F32)).astype(BF16)

        a_next = gate(0)
        mix = None
        for c in range(n_chunks):
            a = a_next
            if c + 1 < n_chunks:
                a_next = gate(c + 1)
            part = _dot(a, wo_ref[c * GATE_TILE:(c + 1) * GATE_TILE, :])
            mix = part if mix is None else mix + part
    else:
        a_ref, wo_ref, x_ref, nw_ref, wu_ref, wd_ref, o_ref = refs
        mix = _dot_tn(a_ref[...], wo_ref[...])
    x = x_ref[...] + mix
    h = _rms(x, nw_ref[...]).astype(BF16)
    acc = x
    for c in range(D_FF // FF_TILE):
        u = _dot(h, wu_ref[:, c * FF_TILE:(c + 1) * FF_TILE])
        u = jnp.maximum(u, 0.0)
        acc = acc + _dot((u * u).astype(BF16), wd_ref[c * FF_TILE:(c + 1) * FF_TILE, :])
    o_ref[...] = acc


def _out_mlp(a, g, wo, mixer_layer, x, nw, wu, wd, layer):
    t = x.shape[0]
    ka = wo.shape[1]
    row = lambda i: (i, 0)
    const = lambda i: (0, 0)
    gated = g is not None
    if gated:
        mix_specs = [pl.BlockSpec((ROW_TILE, ka), row), pl.BlockSpec((ROW_TILE, ka), row)]
        mix_args = [a, g]
    else:
        mix_specs = [pl.BlockSpec((ka, ROW_TILE), lambda i: (0, i))]
        mix_args = [a]
    return pl.pallas_call(
        functools.partial(_out_mlp_kernel, gated),
        grid=(t // ROW_TILE,),
        in_specs=mix_specs + [
            pl.BlockSpec((None, ka, D_MODEL), lambda i: (mixer_layer, 0, 0)),
            pl.BlockSpec((ROW_TILE, D_MODEL), row),
            pl.BlockSpec((1, D_MODEL), const),
            pl.BlockSpec((None, D_MODEL, D_FF), lambda i: (layer, 0, 0), pipeline_mode=pl.Buffered(1)),
            pl.BlockSpec((None, D_FF, D_MODEL), lambda i: (layer, 0, 0), pipeline_mode=pl.Buffered(1)),
        ],
        out_specs=pl.BlockSpec((ROW_TILE, D_MODEL), row),
        out_shape=jax.ShapeDtypeStruct((t, D_MODEL), F32),
        compiler_params=_params("parallel"),
        name="out_mlp_gated" if gated else "out_mlp",
    )(*mix_args, wo, x, nw, wu, wd)


def _bias_tiles_kernel(bucket_ranges, rb_ref, idx_ref, o_ref):
    head = pl.program_id(0)
    for d, (lo, hi) in enumerate(bucket_ranges):
        idx = idx_ref[d]
        acc = jnp.full(idx.shape, NEG, F32)
        for b in range(lo, hi + 1):
            acc = jnp.where(idx == b, rb_ref[b, head] * LOG2E, acc)
        o_ref[0, d] = acc


def _bucket_ranges():
    ranges = []
    for d in range(BIAS_TILES):
        lo_dist = max(d * ATT_TILE - (ATT_TILE - 1), 0)
        hi_dist = d * ATT_TILE + (ATT_TILE - 1)

        def bucket(n):
            if n < REL_MAX_EXACT:
                return n
            return min(REL_MAX_EXACT + int(math.log(n / REL_MAX_EXACT) / math.log(REL_MAX_DISTANCE / REL_MAX_EXACT)
                                           * (REL_BUCKETS - REL_MAX_EXACT)), REL_BUCKETS - 1)

        ranges.append((max(bucket(lo_dist) - 1, 0), min(bucket(hi_dist) + 1, REL_BUCKETS - 1)))
    return tuple(ranges)


def _bias_tiles(rel_bias, bucket_idx):
    nd, tq, tk = bucket_idx.shape
    return pl.pallas_call(
        functools.partial(_bias_tiles_kernel, _bucket_ranges()),
        grid=(ATTN_HEADS,),
        in_specs=[
            pl.BlockSpec(memory_space=pltpu.SMEM),
            pl.BlockSpec((nd, tq, tk), lambda h: (0, 0, 0)),
        ],
        out_specs=pl.BlockSpec((1, nd, tq, tk), lambda h: (h, 0, 0, 0)),
        out_shape=jax.ShapeDtypeStruct((ATTN_HEADS, nd, tq, tk), F32),
        compiler_params=_params("parallel"),
        name="bias_tiles",
    )(rel_bias, bucket_idx)


def _rel_bucket(dist):
    n = jnp.maximum(dist, 0)
    nf = jnp.maximum(n, 1).astype(F32)
    large = REL_MAX_EXACT + (jnp.log(nf / REL_MAX_EXACT) / math.log(REL_MAX_DISTANCE / REL_MAX_EXACT)
                             * (REL_BUCKETS - REL_MAX_EXACT)).astype(jnp.int32)
    large = jnp.minimum(large, REL_BUCKETS - 1)
    return jnp.where(n < REL_MAX_EXACT, n, large)


def _bucket_index_tiles():
    r = np.arange(ATT_TILE)
    dist = (np.arange(BIAS_TILES)[:, None, None] * ATT_TILE + r[None, None, :] - r[None, :, None])
    dist = jnp.asarray(dist, jnp.int32)
    return jnp.where(dist >= 0, _rel_bucket(dist), REL_BUCKETS).astype(jnp.int32)


TAB_Q, TAB_K, TAB_DELTA, TAB_FIRST, TAB_ACC = range(5)
PIPE_LAG = 2
NEAR_UNROLL = 12
FAR_UNROLL = 12
QUERY_HALVES = tuple(slice(h * LANES, (h + 1) * LANES) for h in range(ATT_TILE // LANES))


def _tile_schedule(nq, own_first, far, unroll):
    rows = []
    for i in range(nq):
        keys = [j for j in range(i + 1) if (i - j >= BIAS_TILES) == far]
        if own_first and not far:
            keys = [i] + keys[:-1]
        for n, j in enumerate(keys):
            rows.append((i, j, i - j if not far else 0, int(n == 0 and not far), i))
    if not rows:
        return None, 0
    n_iters = -(-(len(rows) + PIPE_LAG) // unroll) * unroll
    idle = (0, 0, 0, 1, nq)
    cols = [idle] * PIPE_LAG + rows
    cols += [idle] * (n_iters + PIPE_LAG - len(cols))
    return jnp.asarray(np.array(cols, np.int32).T), n_iters


def _tile_slice(idx):
    return pl.ds(pl.multiple_of(idx * ATT_TILE, ATT_TILE), ATT_TILE)


def _head_row_mask(h):
    row = lax.broadcasted_iota(jnp.int32, (LANES, 1), 0)
    return (row >= h * HEAD_DIM) & (row < (h + 1) * HEAD_DIM)


def _pair_rows(pair, n=LANES):
    return slice(pair * n, (pair + 1) * n)


def _attn_kernel(kind, lambda_init, n_near, n_far, near_ref, far_ref, cfar_ref, qt_ref, k_ref, vt_ref, bias_ref, *rest):
    if kind == "moba":
        blk_ref, ot_ref, qts_ref, vts_ref, s_ref, p_ref, acc_ref, m_ref, neg_ref = rest
    else:
        lam_ref, sw_ref, ot_ref, qts_ref, vts_ref, s_ref, p_ref, acc_ref, m_ref = rest
    heads = range(HEADS_PER_STEP)
    seq = qt_ref.shape[1]
    acc_rows = acc_ref.shape[2]

    n_tiles = seq // ATT_TILE
    q_heads = []
    for e in heads:
        pair, h = divmod(e, HEADS_PER_PAIR)
        qf = qt_ref[_pair_rows(pair), :].astype(F32)
        q_heads.append(jnp.where(_head_row_mask(h), qf, 0.0).astype(BF16))
        for i in range(n_tiles):
            qts_ref[i, e] = q_heads[e][:, i * ATT_TILE:(i + 1) * ATT_TILE]
    for j in range(n_tiles):
        vts_ref[j] = vt_ref[:, j * ATT_TILE:(j + 1) * ATT_TILE]
    s_ref[...] = jnp.zeros_like(s_ref)
    p_ref[...] = jnp.zeros_like(p_ref)
    acc_ref[...] = jnp.zeros_like(acc_ref)
    m_ref[...] = jnp.zeros_like(m_ref)

    if kind == "moba":
        nb = blk_ref.shape[0]
        kmean = _dot(blk_ref[...], k_ref[...])
        km_hi = kmean.astype(BF16)
        km_lo = (kmean - km_hi.astype(F32)).astype(BF16)
        blk_id = lax.broadcasted_iota(jnp.int32, (nb, seq), 0)
        own_blk = lax.broadcasted_iota(jnp.int32, (nb, seq), 1) // MOBA_BLOCK
        past = blk_id < own_blk
        for e in heads:
            pair = e // HEADS_PER_PAIR
            q_e = q_heads[e]
            gate = _dot(km_hi[:, _pair_rows(pair)], q_e) + _dot(km_lo[:, _pair_rows(pair)], q_e)
            gate = jnp.where(past, gate, NEG)
            rank = jnp.zeros((nb, seq), jnp.int32)
            for c in range(nb):
                gc = gate[c:c + 1, :]
                tie = jnp.where(blk_id > c, 1, 0)
                rank = rank + jnp.where(gc > gate, 1, jnp.where(gc == gate, tie, 0))
            chosen = jnp.where(rank < MOBA_TOPK, jnp.where(past, 1, 0), 0)
            keep = jnp.maximum(chosen, jnp.where(blk_id == own_blk, 1, 0))
            neg_ref[e] = jnp.where(keep == 1, 0.0, NEG)

    def step(tab_ref, far, t, cur, carry):
        nxt = 1 - cur
        alpha_prev, mtile_prev = carry
        ic = tab_ref[TAB_ACC, t]
        rb = tab_ref[TAB_ACC, t + 1]
        jc = tab_ref[TAB_K, t]
        ib = tab_ref[TAB_Q, t + 1]
        jb = tab_ref[TAB_K, t + 1]
        first = tab_ref[TAB_FIRST, t + 1] != 0
        ia = tab_ref[TAB_Q, t + 2]
        ja = tab_ref[TAB_K, t + 2]
        da = tab_ref[TAB_DELTA, t + 2]
        alpha_new, mtile_new = [], []
        for e in heads:
            pair = e // HEADS_PER_PAIR

            vt = vts_ref[jc, _pair_rows(e if kind == "moba" else pair, acc_rows), :]
            pv = _dot(vt, p_ref[cur, e])
            for hf, lanes in enumerate(QUERY_HALVES):
                acc_ref[ic, e, :, lanes] = alpha_prev[e][hf] * acc_ref[ic, e, :, lanes] + pv[:, lanes]

            kt = k_ref[_tile_slice(ja), _pair_rows(pair)]
            sb = _dot(kt, qts_ref[ia, e])
            if not far:
                sb = sb + bias_ref[e, da]
            s_ref[nxt, e] = sb
            mtile_new.append(tuple(jnp.max(sb[:, lanes], axis=0, keepdims=True) for lanes in QUERY_HALVES))

            alpha_parts = []
            col_shift = None
            if kind == "moba":
                col_shift = neg_ref[e, pl.ds(jb, 1), _tile_slice(ib)]
            if far:
                c_far = cfar_ref[pl.program_id(0) * HEADS_PER_STEP + e]
                col_shift = c_far if col_shift is None else col_shift + c_far
            m_row = m_ref[rb, e]
            m_parts = []
            for hf, lanes in enumerate(QUERY_HALVES):
                s = s_ref[cur, e, :, lanes]
                m_in = jnp.where(first, -jnp.inf, m_row[:, lanes])
                m_tile = mtile_prev[e][hf]
                if col_shift is None:
                    m_e = jnp.maximum(m_in, m_tile)
                    shift = m_e
                else:
                    cs = col_shift if far and kind != "moba" else col_shift[:, lanes]
                    m_e = jnp.maximum(m_in, m_tile + cs)
                    shift = m_e - cs
                p_ref[nxt, e, :, lanes] = jnp.exp2(s - shift).astype(BF16)
                alpha_parts.append(jnp.exp2(m_in - m_e))
                m_parts.append(m_e)
            m_ref[rb, e] = jnp.concatenate(m_parts, axis=1)
            alpha_new.append(tuple(alpha_parts))

        return tuple(alpha_new), tuple(mtile_new)

    zeros = tuple(tuple(jnp.zeros((1, LANES), F32) for _ in QUERY_HALVES) for _ in heads)

    def run(tab_ref, far, n_iters, unroll, carry):
        def body(u, carry):
            for r in range(unroll):
                carry = step(tab_ref, far, unroll * u + r, r % 2, carry)
            return carry

        return lax.fori_loop(0, n_iters // unroll, body, carry)

    carry = run(near_ref, False, n_near, NEAR_UNROLL, (zeros, zeros))
    if n_far:
        run(far_ref, True, n_far, FAR_UNROLL, carry)

    def emit(i, carry):
        cols = _tile_slice(i)
        v_group = acc_ref.shape[2] - ONES_ROWS
        for pair in range(PAIRS_PER_STEP):
            outs = []
            for h in range(HEADS_PER_PAIR):
                e = pair * HEADS_PER_PAIR + h
                outs.append(acc_ref[i, e, :v_group, :] * (1.0 / acc_ref[i, e, v_group:v_group + 1, :]))
            if kind == "moba":
                o = jnp.concatenate(outs, axis=0)
            else:
                lam = lam_ref[...]
                lam_full = (jnp.exp(jnp.sum(lam[0:1] * lam[1:2], axis=-1, keepdims=True))
                            - jnp.exp(jnp.sum(lam[2:3] * lam[3:4], axis=-1, keepdims=True)) + lambda_init)
                o = outs[0] - lam_full * outs[1]
                o = (o * lax.rsqrt(jnp.mean(o * o, axis=0, keepdims=True) + EPS)
                     * sw_ref[...] * (1.0 - lambda_init))
            ot_ref[_pair_rows(pair), cols] = o.astype(BF16)
        return carry

    lax.fori_loop(0, seq // ATT_TILE, emit, 0)


def _attention(kind, qt, k, vt_ext, bias, bias_far, extras, lambda_init, batch, seq):
    v_group = _value_group(kind)
    acc_rows = v_group + ONES_ROWS
    vt_block_rows = acc_rows * (LANES * PAIRS_PER_STEP // v_group)
    t = k.shape[0]
    nq = seq // ATT_TILE
    near_tab, n_near = _tile_schedule(nq, kind == "moba", False, NEAR_UNROLL)
    far_tab, n_far = _tile_schedule(nq, kind == "moba", True, FAR_UNROLL)
    if far_tab is None:
        far_tab = near_tab
    groups = ATTN_HEADS // HEADS_PER_STEP
    rows = LANES * PAIRS_PER_STEP
    in_specs = [
        pl.BlockSpec(memory_space=pltpu.SMEM),
        pl.BlockSpec(memory_space=pltpu.SMEM),
        pl.BlockSpec(memory_space=pltpu.SMEM),
        pl.BlockSpec((rows, seq), lambda g, b: (g, b)),
        pl.BlockSpec((seq, rows), lambda g, b: (b, g)),
        pl.BlockSpec((vt_block_rows, seq), lambda g, b: (g, b)),
        pl.BlockSpec((HEADS_PER_STEP, BIAS_TILES, ATT_TILE, ATT_TILE), lambda g, b: (g, 0, 0, 0)),
    ]
    scratch = [
        pltpu.VMEM((nq, HEADS_PER_STEP, LANES, ATT_TILE), BF16),
        pltpu.VMEM((nq, vt_block_rows, ATT_TILE), BF16),
        pltpu.VMEM((2, HEADS_PER_STEP, ATT_TILE, ATT_TILE), F32),
        pltpu.VMEM((2, HEADS_PER_STEP, ATT_TILE, ATT_TILE), BF16),
        pltpu.VMEM((nq + 1, HEADS_PER_STEP, acc_rows, ATT_TILE), F32),
        pltpu.VMEM((nq + 1, HEADS_PER_STEP, 1, ATT_TILE), F32),
    ]
    if kind == "moba":
        (blk,) = extras
        nb = blk.shape[0]
        in_specs.append(pl.BlockSpec((nb, seq), lambda g, b: (0, 0)))
        scratch.append(pltpu.VMEM((HEADS_PER_STEP, nb, seq), F32))
    else:
        lam, sw = extras
        in_specs += [
            pl.BlockSpec((4, HEAD_DIM), lambda g, b: (0, 0)),
            pl.BlockSpec((LANES, 1), lambda g, b: (0, 0)),
        ]
    return pl.pallas_call(
        functools.partial(_attn_kernel, kind, lambda_init, n_near, n_far),
        grid=(groups, batch),
        in_specs=in_specs,
        out_specs=pl.BlockSpec((rows, seq), lambda g, b: (g, b)),
        out_shape=jax.ShapeDtypeStruct((D_MODEL, t), BF16),
        scratch_shapes=scratch,
        compiler_params=_params("parallel", "parallel"),
        name=kind + "_attn",
    )(near_tab, far_tab, bias_far, qt, k, vt_ext, bias, *extras)


def _rotary_tables(seq):
    d = RET_DK
    inv_freq = ROPE_BASE ** (-np.arange(0, d, 2, dtype=np.float64) / d)
    ang = np.arange(seq, dtype=np.float64)[:, None] * inv_freq[None, :]
    return jnp.asarray(np.cos(ang), F32), jnp.asarray(np.sin(ang), F32)


def _retention_decay_tables():
    c_len = RET_CHUNK
    log_gamma = np.log(1.0 - 2.0 ** (-5.0 - np.arange(RET_HEADS, dtype=np.float64)))
    pos = np.arange(c_len, dtype=np.float64)
    rel = pos[:, None] - pos[None, :]
    dm = np.where(rel >= 0, np.exp(np.maximum(rel, 0.0)[None] * log_gamma[:, None, None]), 0.0)
    rs = np.exp((pos + 1.0)[None, :] * log_gamma[:, None])[:, :, None]
    ks = np.exp((c_len - 1.0 - pos)[None, :] * log_gamma[:, None])[:, :, None]
    cd = np.exp(c_len * log_gamma)[:, None, None]
    return tuple(jnp.asarray(a, F32) for a in (dm, rs, ks, cd))


def _block_mean_matrix(seq):
    nb = seq // MOBA_BLOCK
    m = (np.arange(seq)[None, :] // MOBA_BLOCK == np.arange(nb)[:, None]) / float(MOBA_BLOCK)
    return jnp.asarray(m, BF16)


def _head_group_matrix():
    g = np.arange(COL_TILE)[:, None] // HEAD_DIM == np.arange(COL_TILE)[None, :] // HEAD_DIM
    return jnp.asarray(g / float(HEAD_DIM), BF16)


def kernel(x, rel_bias, norm1, norm2, w_up, w_down, ret_w_in, ret_w_out,
           moba_w_in, moba_q_norm, moba_k_norm, moba_w_out,
           diff_w_in, diff_q_norm, diff_k_norm, diff_lambda, diff_subln, diff_w_out):
    batch, seq, d = x.shape
    depth = norm1.shape[0]
    assert d == D_MODEL and seq % ROW_TILE == 0 and seq % ATT_TILE == 0 and seq % (2 * RET_CHUNK) == 0
    assert seq % MOBA_BLOCK == 0 and MOBA_BLOCK == ATT_TILE
    t = batch * seq
    xf = x.reshape(t, d)

    bias = _bias_tiles(rel_bias.astype(F32), _bucket_index_tiles())
    bias_far = rel_bias[REL_BUCKETS - 1].astype(F32) * LOG2E
    grp = _head_group_matrix()
    q_scale = HEAD_DIM ** -0.5 * LOG2E
    wu, wd = w_up.astype(BF16), w_down.astype(BF16)
    ret_wi, ret_wo = ret_w_in.astype(BF16), ret_w_out.astype(BF16)
    moba_wi, moba_wo = moba_w_in.astype(BF16), moba_w_out.astype(BF16)
    diff_wi, diff_wo = diff_w_in.astype(BF16), diff_w_out.astype(BF16)

    for i in range(depth):
        kind, j = i % N_MIXERS, i // N_MIXERS
        nw1 = norm1[i].reshape(1, d)
        nw2 = norm2[i].reshape(1, d)
        if kind == 0:
            cos, sin = _rotary_tables(seq)
            dm, rs, ks, cd = _retention_decay_tables()
            q, k, v, g = _ret_proj(xf, nw1, ret_wi, j, cos, sin, seq)
            o = _ret_core(q, k, v, dm, rs, ks, cd, batch, seq)
            xf = _out_mlp(o, g, ret_wo, j, xf, nw2, wu, wd, i)
        elif kind == 1:
            qn = (jnp.tile(moba_q_norm[j], ATTN_HEADS) * q_scale).reshape(1, d)
            kn = jnp.tile(moba_k_norm[j], ATTN_HEADS).reshape(1, d)
            qt, k, vt = _attn_proj(xf, nw1, moba_wi, j, qn, kn, grp, _value_group("moba"))
            ot = _attention("moba", qt, k, vt, bias, bias_far, (_block_mean_matrix(seq),), 0.0, batch, seq)
            xf = _out_mlp(ot, None, moba_wo, j, xf, nw2, wu, wd, i)
        else:
            lambda_init = 0.8 - 0.6 * math.exp(-0.3 * i)
            qn = (jnp.tile(diff_q_norm[j], ATTN_HEADS) * q_scale).reshape(1, d)
            kn = jnp.tile(diff_k_norm[j], ATTN_HEADS).reshape(1, d)
            qt, k, vt = _attn_proj(xf, nw1, diff_wi, j, qn, kn, grp, _value_group("diff"))
            extras = (diff_lambda[j].astype(F32), diff_subln[j].reshape(LANES, 1))
            ot = _attention("diff", qt, k, vt, bias, bias_far, extras, lambda_init, batch, seq)
            xf = _out_mlp(ot, None, diff_wo, j, xf, nw2, wu, wd, i)
    return xf.reshape(batch, seq, d)
```

```python
import functools
import math

import numpy as np
import jax
import jax.numpy as jnp
from jax import lax
from jax.experimental import pallas as pl
from jax.experimental.pallas import tpu as pltpu

F32 = jnp.float32
BF16 = jnp.bfloat16

D_MODEL = 1024
N_MIXERS = 3
RET_HEADS = 4
RET_DK = D_MODEL // RET_HEADS
RET_DV = 2 * RET_DK
ROPE_BASE = 10000.0
ATTN_HEADS = 16
HEAD_DIM = D_MODEL // ATTN_HEADS
MOBA_BLOCK = 256
MOBA_TOPK = 3
REL_BUCKETS = 32
REL_MAX_EXACT = REL_BUCKETS // 2
REL_MAX_DISTANCE = 1024
D_FF = 4 * D_MODEL
EPS = 1e-6
NEG = -1e30
LOG2E = math.log2(math.e)

LANES = 128
BF16_SUBLANES = 16
VMEM_LIMIT_BYTES = 56 * 1024 * 1024

ROW_TILE = 512
COL_TILE = 256
FF_TILE = 1024
GATE_TILE = 512
RET_CHUNK = 256
ATT_TILE = 256
BIAS_TILES = 6
HEADS_PER_PAIR = LANES // HEAD_DIM
PAIRS_PER_STEP = 1
HEADS_PER_STEP = HEADS_PER_PAIR * PAIRS_PER_STEP
ONES_ROWS = BF16_SUBLANES


def _value_group(kind):
    return HEAD_DIM if kind == "moba" else LANES


def _params(*sem):
    return pltpu.CompilerParams(dimension_semantics=sem, vmem_limit_bytes=VMEM_LIMIT_BYTES)


def _rms(xf, w):
    ms = jnp.mean(xf * xf, axis=-1, keepdims=True)
    return xf * lax.rsqrt(ms + EPS) * w


def _dot(a, b):
    return jnp.dot(a, b, preferred_element_type=F32)


def _dot_nt(a, b):
    return lax.dot_general(a, b, (((1,), (1,)), ((), ())), preferred_element_type=F32)


def _dot_tn(a, b):
    return lax.dot_general(a, b, (((0,), (0,)), ((), ())), preferred_element_type=F32)


def _ret_proj_kernel(x_ref, nw_ref, w_ref, cos_ref, sin_ref, q_ref, k_ref, v_ref, g_ref):
    h = _rms(x_ref[...], nw_ref[...]).astype(BF16)
    cos = cos_ref[...]
    sin = sin_ref[...]
    half = RET_DK // 2
    chunks = []
    for out_ref, scale in ((q_ref, 1.0), (k_ref, RET_DK ** -0.5)):
        chunks += [(out_ref, hd * RET_DK, scale) for hd in range(RET_HEADS)]
    for out_ref in (v_ref, g_ref):
        chunks += [(out_ref, c * COL_TILE, None) for c in range(RET_HEADS * RET_DV // COL_TILE)]
    assert RET_DK == COL_TILE

    def project(n):
        return _dot(h, w_ref[:, n * COL_TILE:(n + 1) * COL_TILE])

    acc_next = project(0)
    for n, (out_ref, col0, scale) in enumerate(chunks):
        acc = acc_next
        if n + 1 < len(chunks):
            acc_next = project(n + 1)
        if scale is None:
            out_ref[:, col0:col0 + COL_TILE] = acc.astype(BF16)
        else:
            x1 = acc[:, :half]
            x2 = acc[:, half:]
            r1 = x1 * cos - x2 * sin
            r2 = x1 * sin + x2 * cos
            if scale != 1.0:
                r1 = r1 * scale
                r2 = r2 * scale
            out_ref[:, col0:col0 + half] = r1.astype(BF16)
            out_ref[:, col0 + half:col0 + RET_DK] = r2.astype(BF16)


def _ret_proj(x, nw, w, layer, cos, sin, seq):
    t = x.shape[0]
    n_in = w.shape[2]
    tiles_per_seq = seq // ROW_TILE
    row = lambda i: (i, 0)
    const = lambda i: (0, 0)
    pos = lambda i: (i % tiles_per_seq, 0)
    n_qk = RET_HEADS * RET_DK
    n_v = RET_HEADS * RET_DV
    return pl.pallas_call(
        _ret_proj_kernel,
        grid=(t // ROW_TILE,),
        in_specs=[
            pl.BlockSpec((ROW_TILE, D_MODEL), row),
            pl.BlockSpec((1, D_MODEL), const),
            pl.BlockSpec((None, D_MODEL, n_in), lambda i: (layer, 0, 0)),
            pl.BlockSpec((ROW_TILE, RET_DK // 2), pos),
            pl.BlockSpec((ROW_TILE, RET_DK // 2), pos),
        ],
        out_specs=[
            pl.BlockSpec((ROW_TILE, n_qk), row),
            pl.BlockSpec((ROW_TILE, n_qk), row),
            pl.BlockSpec((ROW_TILE, n_v), row),
            pl.BlockSpec((ROW_TILE, n_v), row),
        ],
        out_shape=[
            jax.ShapeDtypeStruct((t, n_qk), BF16),
            jax.ShapeDtypeStruct((t, n_qk), BF16),
            jax.ShapeDtypeStruct((t, n_v), BF16),
            jax.ShapeDtypeStruct((t, n_v), BF16),
        ],
        compiler_params=_params("parallel"),
        name="ret_proj",
    )(x, nw, w, cos, sin)


def _attn_proj_kernel(v_group, x_ref, nw_ref, w_ref, qn_ref, kn_ref, grp_ref, qt_ref, k_ref, vt_ref):
    h = _rms(x_ref[...], nw_ref[...]).astype(BF16)
    grp = grp_ref[...]
    groups_per_chunk = COL_TILE // v_group
    chunks_per_part = D_MODEL // COL_TILE
    n_chunks = 3 * chunks_per_part

    def project(n):
        return _dot(h, w_ref[:, n * COL_TILE:(n + 1) * COL_TILE])

    acc_next = project(0)
    for n in range(n_chunks):
        acc = acc_next
        if n + 1 < n_chunks:
            acc_next = project(n + 1)
        part, c = divmod(n, chunks_per_part)
        cols = slice(c * COL_TILE, (c + 1) * COL_TILE)
        if part < 2:
            hw_ref = qn_ref if part == 0 else kn_ref
            ms = _dot((acc * acc).astype(BF16), grp)
            acc = acc * lax.rsqrt(ms + EPS) * hw_ref[:, cols]
        if part == 0:
            qt_ref[cols, :] = acc.T.astype(BF16)
        elif part == 1:
            k_ref[:, cols] = acc.astype(BF16)
        else:
            acc_t = acc.T.astype(BF16)
            for p in range(groups_per_chunk):
                row0 = (c * groups_per_chunk + p) * (v_group + ONES_ROWS)
                vt_ref[row0:row0 + v_group, :] = acc_t[p * v_group:(p + 1) * v_group, :]
                vt_ref[row0 + v_group:row0 + v_group + ONES_ROWS, :] = jnp.ones((ONES_ROWS, acc_t.shape[1]), BF16)


def _attn_proj(x, nw, w, layer, qn, kn, grp, v_group):
    t = x.shape[0]
    row = lambda i: (i, 0)
    col = lambda i: (0, i)
    const = lambda i: (0, 0)
    vt_rows = (D_MODEL // v_group) * (v_group + ONES_ROWS)
    return pl.pallas_call(
        functools.partial(_attn_proj_kernel, v_group),
        grid=(t // ROW_TILE,),
        in_specs=[
            pl.BlockSpec((ROW_TILE, D_MODEL), row),
            pl.BlockSpec((1, D_MODEL), const),
            pl.BlockSpec((None, D_MODEL, 3 * D_MODEL), lambda i: (layer, 0, 0)),
            pl.BlockSpec((1, D_MODEL), const),
            pl.BlockSpec((1, D_MODEL), const),
            pl.BlockSpec((COL_TILE, COL_TILE), const),
        ],
        out_specs=[
            pl.BlockSpec((D_MODEL, ROW_TILE), col),
            pl.BlockSpec((ROW_TILE, D_MODEL), row),
            pl.BlockSpec((vt_rows, ROW_TILE), col),
        ],
        out_shape=[
            jax.ShapeDtypeStruct((D_MODEL, t), BF16),
            jax.ShapeDtypeStruct((t, D_MODEL), BF16),
            jax.ShapeDtypeStruct((vt_rows, t), BF16),
        ],
        compiler_params=_params("parallel"),
        name="attn_proj",
    )(x, nw, w, qn, kn, grp)


def _ret_core_kernel(q_ref, k_ref, v_ref, dm_ref, rs_ref, ks_ref, cd_ref, o_ref, state_ref, raw_ref):
    c_len = RET_CHUNK
    n_chunks = q_ref.shape[0] // c_len
    state_ref[...] = jnp.zeros_like(state_ref)
    raw_ref[...] = jnp.zeros_like(raw_ref)

    def chunk(c):
        return pl.ds(pl.multiple_of(c * c_len, c_len), c_len)

    def masked_scores(c):
        return (_dot_nt(q_ref[chunk(c), :], k_ref[chunk(c), :]) * dm_ref[0]).astype(BF16)

    def normalise(c, slot):
        o = raw_ref[slot]
        o_ref[chunk(c), :] = (o * lax.rsqrt(jnp.mean(o * o, axis=-1, keepdims=True) + EPS)).astype(BF16)

    def step(c, slot, s_cur):
        sl = chunk(c)
        q = q_ref[sl, :]
        k = k_ref[sl, :]
        v = v_ref[sl, :]
        state = state_ref[...]
        kd = (k.astype(F32) * ks_ref[0]).astype(BF16)
        kv = _dot_tn(kd, v)
        inner = _dot(s_cur, v)
        cross = _dot(q, state.astype(BF16))
        s_next = masked_scores(jnp.minimum(c + 1, n_chunks - 1))
        normalise(jnp.maximum(c - 1, 0), 1 - slot)
        raw_ref[slot] = inner + cross * rs_ref[0]
        state_ref[...] = state * cd_ref[0] + kv
        return s_next

    def body(u, s_cur):
        return step(2 * u + 1, 1, step(2 * u, 0, s_cur))

    lax.fori_loop(0, n_chunks // 2, body, masked_scores(0))
    normalise(n_chunks - 1, 1)


def _ret_core(q, k, v, dm, rs, ks, cd, batch, seq):
    t = q.shape[0]
    c_len = RET_CHUNK
    tok = lambda b, h: (b, h)
    head3 = lambda b, h: (h, 0, 0)
    return pl.pallas_call(
        _ret_core_kernel,
        grid=(batch, RET_HEADS),
        in_specs=[
            pl.BlockSpec((seq, RET_DK), tok),
            pl.BlockSpec((seq, RET_DK), tok),
            pl.BlockSpec((seq, RET_DV), tok),
            pl.BlockSpec((1, c_len, c_len), head3),
            pl.BlockSpec((1, c_len, 1), head3),
            pl.BlockSpec((1, c_len, 1), head3),
            pl.BlockSpec((1, 1, 1), head3),
        ],
        out_specs=pl.BlockSpec((seq, RET_DV), tok),
        out_shape=jax.ShapeDtypeStruct((t, RET_HEADS * RET_DV), BF16),
        scratch_shapes=[pltpu.VMEM((RET_DK, RET_DV), F32), pltpu.VMEM((2, RET_CHUNK, RET_DV), F32)],
        compiler_params=_params("parallel", "parallel"),
        name="ret_core",
    )(q, k, v, dm, rs, ks, cd)


def _out_mlp_kernel(gated, *refs):
    if gated:
        a_ref, g_ref, wo_ref, x_ref, nw_ref, wu_ref, wd_ref, o_ref = refs
        n_chunks = a_ref.shape[1] // GATE_TILE

        def gate(c):
            cols = slice(c * GATE_TILE, (c + 1) * GATE_TILE)
            g = g_ref[:, cols].astype(F32)
            return (g * (1.0 / (1.0 + jnp.exp(-g))) * a_ref[:, cols].astype(F32)).astype(BF16)

        a_next = gate(0)
        mix = None
        for c in range(n_chunks):
            a = a_next
            if c + 1 < n_chunks:
                a_next = gate(c + 1)
            part = _dot(a, wo_ref[c * GATE_TILE:(c + 1) * GATE_TILE, :])
            mix = part if mix is None else mix + part
    else:
        a_ref, wo_ref, x_ref, nw_ref, wu_ref, wd_ref, o_ref = refs
        mix = _dot_tn(a_ref[...], wo_ref[...])
    x = x_ref[...] + mix
    h = _rms(x, nw_ref[...]).astype(BF16)
    acc = x
    for c in range(D_FF // FF_TILE):
        u = _dot(h, wu_ref[:, c * FF_TILE:(c + 1) * FF_TILE])
        u = jnp.maximum(u, 0.0)
        acc = acc + _dot((u * u).astype(BF16), wd_ref[c * FF_TILE:(c + 1) * FF_TILE, :])
    o_ref[...] = acc


def _out_mlp(a, g, wo, mixer_layer, x, nw, wu, wd, layer):
    t = x.shape[0]
    ka = wo.shape[1]
    row = lambda i: (i, 0)
    const = lambda i: (0, 0)
    gated = g is not None
    if gated:
        mix_specs = [pl.BlockSpec((ROW_TILE, ka), row), pl.BlockSpec((ROW_TILE, ka), row)]
        mix_args = [a, g]
    else:
        mix_specs = [pl.BlockSpec((ka, ROW_TILE), lambda i: (0, i))]
        mix_args = [a]
    return pl.pallas_call(
        functools.partial(_out_mlp_kernel, gated),
        grid=(t // ROW_TILE,),
        in_specs=mix_specs + [
            pl.BlockSpec((None, ka, D_MODEL), lambda i: (mixer_layer, 0, 0)),
            pl.BlockSpec((ROW_TILE, D_MODEL), row),
            pl.BlockSpec((1, D_MODEL), const),
            pl.BlockSpec((None, D_MODEL, D_FF), lambda i: (layer, 0, 0), pipeline_mode=pl.Buffered(1)),
            pl.BlockSpec((None, D_FF, D_MODEL), lambda i: (layer, 0, 0), pipeline_mode=pl.Buffered(1)),
        ],
        out_specs=pl.BlockSpec((ROW_TILE, D_MODEL), row),
        out_shape=jax.ShapeDtypeStruct((t, D_MODEL), F32),
        compiler_params=_params("parallel"),
        name="out_mlp_gated" if gated else "out_mlp",
    )(*mix_args, wo, x, nw, wu, wd)


def _bias_tiles_kernel(bucket_ranges, rb_ref, idx_ref, o_ref):
    head = pl.program_id(0)
    for d, (lo, hi) in enumerate(bucket_ranges):
        idx = idx_ref[d]
        acc = jnp.full(idx.shape, NEG, F32)
        for b in range(lo, hi + 1):
            acc = jnp.where(idx == b, rb_ref[b, head] * LOG2E, acc)
        o_ref[0, d] = acc


def _bucket_ranges():
    ranges = []
    for d in range(BIAS_TILES):
        lo_dist = max(d * ATT_TILE - (ATT_TILE - 1), 0)
        hi_dist = d * ATT_TILE + (ATT_TILE - 1)

        def bucket(n):
            if n < REL_MAX_EXACT:
                return n
            return min(REL_MAX_EXACT + int(math.log(n / REL_MAX_EXACT) / math.log(REL_MAX_DISTANCE / REL_MAX_EXACT)
                                           * (REL_BUCKETS - REL_MAX_EXACT)), REL_BUCKETS - 1)

        ranges.append((max(bucket(lo_dist) - 1, 0), min(bucket(hi_dist) + 1, REL_BUCKETS - 1)))
    return tuple(ranges)


def _bias_tiles(rel_bias, bucket_idx):
    nd, tq, tk = bucket_idx.shape
    return pl.pallas_call(
        functools.partial(_bias_tiles_kernel, _bucket_ranges()),
        grid=(ATTN_HEADS,),
        in_specs=[
            pl.BlockSpec(memory_space=pltpu.SMEM),
            pl.BlockSpec((nd, tq, tk), lambda h: (0, 0, 0)),
        ],
        out_specs=pl.BlockSpec((1, nd, tq, tk), lambda h: (h, 0, 0, 0)),
        out_shape=jax.ShapeDtypeStruct((ATTN_HEADS, nd, tq, tk), F32),
        compiler_params=_params("parallel"),
        name="bias_tiles",
    )(rel_bias, bucket_idx)


def _rel_bucket(dist):
    n = jnp.maximum(dist, 0)
    nf = jnp.maximum(n, 1).astype(F32)
    large = REL_MAX_EXACT + (jnp.log(nf / REL_MAX_EXACT) / math.log(REL_MAX_DISTANCE / REL_MAX_EXACT)
                             * (REL_BUCKETS - REL_MAX_EXACT)).astype(jnp.int32)
    large = jnp.minimum(large, REL_BUCKETS - 1)
    return jnp.where(n < REL_MAX_EXACT, n, large)


def _bucket_index_tiles():
    r = np.arange(ATT_TILE)
    dist = (np.arange(BIAS_TILES)[:, None, None] * ATT_TILE + r[None, None, :] - r[None, :, None])
    dist = jnp.asarray(dist, jnp.int32)
    return jnp.where(dist >= 0, _rel_bucket(dist), REL_BUCKETS).astype(jnp.int32)


TAB_Q, TAB_K, TAB_DELTA, TAB_FIRST, TAB_ACC = range(5)
PIPE_LAG = 2
PIPE_UNROLL = 6
QUERY_HALVES = tuple(slice(h * LANES, (h + 1) * LANES) for h in range(ATT_TILE // LANES))


def _tile_schedule(nq, own_first):
    rows = []
    for i in range(nq):
        keys = ([i] + list(range(i))) if own_first else list(range(i + 1))
        for n, j in enumerate(keys):
            rows.append((i, j, min(i - j, BIAS_TILES - 1), int(n == 0), i))
    n_iters = -(-(len(rows) + PIPE_LAG) // PIPE_UNROLL) * PIPE_UNROLL
    idle = (0, 0, 0, 1, nq)
    cols = [idle] * PIPE_LAG + rows
    cols += [idle] * (n_iters + PIPE_LAG - len(cols))
    return jnp.asarray(np.array(cols, np.int32).T), n_iters


def _tile_slice(idx):
    return pl.ds(pl.multiple_of(idx * ATT_TILE, ATT_TILE), ATT_TILE)


def _head_row_mask(h):
    row = lax.broadcasted_iota(jnp.int32, (LANES, 1), 0)
    return (row >= h * HEAD_DIM) & (row < (h + 1) * HEAD_DIM)


def _pair_rows(pair, n=LANES):
    return slice(pair * n, (pair + 1) * n)


def _attn_kernel(kind, lambda_init, n_iters, tab_ref, qt_ref, k_ref, vt_ref, bias_ref, *rest):
    if kind == "moba":
        blk_ref, ot_ref, qts_ref, vts_ref, s_ref, p_ref, acc_ref, neg_ref = rest
    else:
        lam_ref, sw_ref, ot_ref, qts_ref, vts_ref, s_ref, p_ref, acc_ref = rest
    heads = range(HEADS_PER_STEP)
    seq = qt_ref.shape[1]
    acc_rows = acc_ref.shape[2]

    n_tiles = seq // ATT_TILE
    q_heads = []
    for e in heads:
        pair, h = divmod(e, HEADS_PER_PAIR)
        qf = qt_ref[_pair_rows(pair), :].astype(F32)
        q_heads.append(jnp.where(_head_row_mask(h), qf, 0.0).astype(BF16))
        for i in range(n_tiles):
            qts_ref[i, e] = q_heads[e][:, i * ATT_TILE:(i + 1) * ATT_TILE]
    for j in range(n_tiles):
        vts_ref[j] = vt_ref[:, j * ATT_TILE:(j + 1) * ATT_TILE]
    s_ref[...] = jnp.zeros_like(s_ref)
    p_ref[...] = jnp.zeros_like(p_ref)
    acc_ref[...] = jnp.zeros_like(acc_ref)

    if kind == "moba":
        nb = blk_ref.shape[0]
        kmean = _dot(blk_ref[...], k_ref[...])
        km_hi = kmean.astype(BF16)
        km_lo = (kmean - km_hi.astype(F32)).astype(BF16)
        blk_id = lax.broadcasted_iota(jnp.int32, (nb, seq), 0)
        own_blk = lax.broadcasted_iota(jnp.int32, (nb, seq), 1) // MOBA_BLOCK
        past = blk_id < own_blk
        for e in heads:
            pair = e // HEADS_PER_PAIR
            q_e = q_heads[e]
            gate = _dot(km_hi[:, _pair_rows(pair)], q_e) + _dot(km_lo[:, _pair_rows(pair)], q_e)
            gate = jnp.where(past, gate, NEG)
            rank = jnp.zeros((nb, seq), jnp.int32)
            for c in range(nb):
                gc = gate[c:c + 1, :]
                tie = jnp.where(blk_id > c, 1, 0)
                rank = rank + jnp.where(gc > gate, 1, jnp.where(gc == gate, tie, 0))
            chosen = jnp.where(rank < MOBA_TOPK, jnp.where(past, 1, 0), 0)
            keep = jnp.maximum(chosen, jnp.where(blk_id == own_blk, 1, 0))
            neg_ref[e] = jnp.where(keep == 1, 0.0, NEG)

    def step(t, cur, carry):
        nxt = 1 - cur
        m_prev, alpha_prev, mtile_prev = carry
        ic = tab_ref[TAB_ACC, t]
        jc = tab_ref[TAB_K, t]
        ib = tab_ref[TAB_Q, t + 1]
        jb = tab_ref[TAB_K, t + 1]
        first = tab_ref[TAB_FIRST, t + 1] != 0
        ia = tab_ref[TAB_Q, t + 2]
        ja = tab_ref[TAB_K, t + 2]
        da = tab_ref[TAB_DELTA, t + 2]
        m_new, alpha_new, mtile_new = [], [], []
        for e in heads:
            pair = e // HEADS_PER_PAIR

            vt = vts_ref[jc, _pair_rows(e if kind == "moba" else pair, acc_rows), :]
            pv = _dot(vt, p_ref[cur, e])
            for hf, lanes in enumerate(QUERY_HALVES):
                acc_ref[ic, e, :, lanes] = alpha_prev[e][hf] * acc_ref[ic, e, :, lanes] + pv[:, lanes]

            kt = k_ref[_tile_slice(ja), _pair_rows(pair)]
            sb = _dot(kt, qts_ref[ia, e]) + bias_ref[e, da]
            s_ref[nxt, e] = sb
            mtile_new.append(tuple(jnp.max(sb[:, lanes], axis=0, keepdims=True) for lanes in QUERY_HALVES))

            m_parts, alpha_parts = [], []
            if kind == "moba":
                neg_row = neg_ref[e, pl.ds(jb, 1), _tile_slice(ib)]
            for hf, lanes in enumerate(QUERY_HALVES):
                s = s_ref[cur, e, :, lanes]
                m_in = jnp.where(first, -jnp.inf, m_prev[e][hf])
                m_tile = mtile_prev[e][hf]
                if kind == "moba":
                    neg = neg_row[:, lanes]
                    m_e = jnp.maximum(m_in, m_tile + neg)
                    shift = m_e - neg
                else:
                    m_e = jnp.maximum(m_in, m_tile)
                    shift = m_e
                p_ref[nxt, e, :, lanes] = jnp.exp2(s - shift).astype(BF16)
                alpha_parts.append(jnp.exp2(m_in - m_e))
                m_parts.append(m_e)
            alpha_new.append(tuple(alpha_parts))
            m_new.append(tuple(m_parts))

        return tuple(m_new), tuple(alpha_new), tuple(mtile_new)

    zeros = tuple(tuple(jnp.zeros((1, LANES), F32) for _ in QUERY_HALVES) for _ in heads)

    def body(u, carry):
        for r in range(PIPE_UNROLL):
            carry = step(PIPE_UNROLL * u + r, r % 2, carry)
        return carry

    lax.fori_loop(0, n_iters // PIPE_UNROLL, body, (zeros, zeros, zeros))

    def emit(i, carry):
        cols = _tile_slice(i)
        v_group = acc_ref.shape[2] - ONES_ROWS
        for pair in range(PAIRS_PER_STEP):
            outs = []
            for h in range(HEADS_PER_PAIR):
                e = pair * HEADS_PER_PAIR + h
                outs.append(acc_ref[i, e, :v_group, :] * (1.0 / acc_ref[i, e, v_group:v_group + 1, :]))
            if kind == "moba":
                o = jnp.concatenate(outs, axis=0)
            else:
                lam = lam_ref[...]
                lam_full = (jnp.exp(jnp.sum(lam[0:1] * lam[1:2], axis=-1, keepdims=True))
                            - jnp.exp(jnp.sum(lam[2:3] * lam[3:4], axis=-1, keepdims=True)) + lambda_init)
                o = outs[0] - lam_full * outs[1]
                o = (o * lax.rsqrt(jnp.mean(o * o, axis=0, keepdims=True) + EPS)
                     * sw_ref[...] * (1.0 - lambda_init))
            ot_ref[_pair_rows(pair), cols] = o.astype(BF16)
        return carry

    lax.fori_loop(0, seq // ATT_TILE, emit, 0)


def _attention(kind, qt, k, vt_ext, bias, extras, lambda_init, batch, seq):
    v_group = _value_group(kind)
    acc_rows = v_group + ONES_ROWS
    vt_block_rows = acc_rows * (LANES * PAIRS_PER_STEP // v_group)
    t = k.shape[0]
    nq = seq // ATT_TILE
    tab, n_iters = _tile_schedule(nq, own_first=(kind == "moba"))
    groups = ATTN_HEADS // HEADS_PER_STEP
    rows = LANES * PAIRS_PER_STEP
    in_specs = [
        pl.BlockSpec(memory_space=pltpu.SMEM),
        pl.BlockSpec((rows, seq), lambda g, b: (g, b)),
        pl.BlockSpec((seq, rows), lambda g, b: (b, g)),
        pl.BlockSpec((vt_block_rows, seq), lambda g, b: (g, b)),
        pl.BlockSpec((HEADS_PER_STEP, BIAS_TILES, ATT_TILE, ATT_TILE), lambda g, b: (g, 0, 0, 0)),
    ]
    scratch = [
        pltpu.VMEM((nq, HEADS_PER_STEP, LANES, ATT_TILE), BF16),
        pltpu.VMEM((nq, vt_block_rows, ATT_TILE), BF16),
        pltpu.VMEM((2, HEADS_PER_STEP, ATT_TILE, ATT_TILE), F32),
        pltpu.VMEM((2, HEADS_PER_STEP, ATT_TILE, ATT_TILE), BF16),
        pltpu.VMEM((nq + 1, HEADS_PER_STEP, acc_rows, ATT_TILE), F32),
    ]
    if kind == "moba":
        (blk,) = extras
        nb = blk.shape[0]
        in_specs.append(pl.BlockSpec((nb, seq), lambda g, b: (0, 0)))
        scratch.append(pltpu.VMEM((HEADS_PER_STEP, nb, seq), F32))
    else:
        lam, sw = extras
        in_specs += [
            pl.BlockSpec((4, HEAD_DIM), lambda g, b: (0, 0)),
            pl.BlockSpec((LANES, 1), lambda g, b: (0, 0)),
        ]
    return pl.pallas_call(
        functools.partial(_attn_kernel, kind, lambda_init, n_iters),
        grid=(groups, batch),
        in_specs=in_specs,
        out_specs=pl.BlockSpec((rows, seq), lambda g, b: (g, b)),
        out_shape=jax.ShapeDtypeStruct((D_MODEL, t), BF16),
        scratch_shapes=scratch,
        compiler_params=_params("parallel", "parallel"),
        name=kind + "_attn",
    )(tab, qt, k, vt_ext, bias, *extras)


def _rotary_tables(seq):
    d = RET_DK
    inv_freq = ROPE_BASE ** (-np.arange(0, d, 2, dtype=np.float64) / d)
    ang = np.arange(seq, dtype=np.float64)[:, None] * inv_freq[None, :]
    return jnp.asarray(np.cos(ang), F32), jnp.asarray(np.sin(ang), F32)


def _retention_decay_tables():
    c_len = RET_CHUNK
    log_gamma = np.log(1.0 - 2.0 ** (-5.0 - np.arange(RET_HEADS, dtype=np.float64)))
    pos = np.arange(c_len, dtype=np.float64)
    rel = pos[:, None] - pos[None, :]
    dm = np.where(rel >= 0, np.exp(np.maximum(rel, 0.0)[None] * log_gamma[:, None, None]), 0.0)
    rs = np.exp((pos + 1.0)[None, :] * log_gamma[:, None])[:, :, None]
    ks = np.exp((c_len - 1.0 - pos)[None, :] * log_gamma[:, None])[:, :, None]
    cd = np.exp(c_len * log_gamma)[:, None, None]
    return tuple(jnp.asarray(a, F32) for a in (dm, rs, ks, cd))


def _block_mean_matrix(seq):
    nb = seq // MOBA_BLOCK
    m = (np.arange(seq)[None, :] // MOBA_BLOCK == np.arange(nb)[:, None]) / float(MOBA_BLOCK)
    return jnp.asarray(m, BF16)


def _head_group_matrix():
    g = np.arange(COL_TILE)[:, None] // HEAD_DIM == np.arange(COL_TILE)[None, :] // HEAD_DIM
    return jnp.asarray(g / float(HEAD_DIM), BF16)


def kernel(x, rel_bias, norm1, norm2, w_up, w_down, ret_w_in, ret_w_out,
           moba_w_in, moba_q_norm, moba_k_norm, moba_w_out,
           diff_w_in, diff_q_norm, diff_k_norm, diff_lambda, diff_subln, diff_w_out):
    batch, seq, d = x.shape
    depth = norm1.shape[0]
    assert d == D_MODEL and seq % ROW_TILE == 0 and seq % ATT_TILE == 0 and seq % (2 * RET_CHUNK) == 0
    assert seq % MOBA_BLOCK == 0 and MOBA_BLOCK == ATT_TILE
    t = batch * seq
    xf = x.reshape(t, d)

    bias = _bias_tiles(rel_bias.astype(F32), _bucket_index_tiles())
    grp = _head_group_matrix()
    q_scale = HEAD_DIM ** -0.5 * LOG2E
    wu, wd = w_up.astype(BF16), w_down.astype(BF16)
    ret_wi, ret_wo = ret_w_in.astype(BF16), ret_w_out.astype(BF16)
    moba_wi, moba_wo = moba_w_in.astype(BF16), moba_w_out.astype(BF16)
    diff_wi, diff_wo = diff_w_in.astype(BF16), diff_w_out.astype(BF16)

    for i in range(depth):
        kind, j = i % N_MIXERS, i // N_MIXERS
        nw1 = norm1[i].reshape(1, d)
        nw2 = norm2[i].reshape(1, d)
        if kind == 0:
            cos, sin = _rotary_tables(seq)
            dm, rs, ks, cd = _retention_decay_tables()
            q, k, v, g = _ret_proj(xf, nw1, ret_wi, j, cos, sin, seq)
            o = _ret_core(q, k, v, dm, rs, ks, cd, batch, seq)
            xf = _out_mlp(o, g, ret_wo, j, xf, nw2, wu, wd, i)
        elif kind == 1:
            qn = (jnp.tile(moba_q_norm[j], ATTN_HEADS) * q_scale).reshape(1, d)
            kn = jnp.tile(moba_k_norm[j], ATTN_HEADS).reshape(1, d)
            qt, k, vt = _attn_proj(xf, nw1, moba_wi, j, qn, kn, grp, _value_group("moba"))
            ot = _attention("moba", qt, k, vt, bias, (_block_mean_matrix(seq),), 0.0, batch, seq)
            xf = _out_mlp(ot, None, moba_wo, j, xf, nw2, wu, wd, i)
        else:
            lambda_init = 0.8 - 0.6 * math.exp(-0.3 * i)
            qn = (jnp.tile(diff_q_norm[j], ATTN_HEADS) * q_scale).reshape(1, d)
            kn = jnp.tile(diff_k_norm[j], ATTN_HEADS).reshape(1, d)
            qt, k, vt = _attn_proj(xf, nw1, diff_wi, j, qn, kn, grp, _value_group("diff"))
            extras = (diff_lambda[j].astype(F32), diff_subln[j].reshape(LANES, 1))
            ot = _attention("diff", qt, k, vt, bias, extras, lambda_init, batch, seq)
            xf = _out_mlp(ot, None, diff_wo, j, xf, nw2, wu, wd, i)
    return xf.reshape(batch, seq, d)
```

```python
import functools
import math

import numpy as np
import jax
import jax.numpy as jnp
from jax import lax
from jax.experimental import pallas as pl
from jax.experimental.pallas import tpu as pltpu

F32 = jnp.float32
BF16 = jnp.bfloat16

D_MODEL = 1024
N_MIXERS = 3
RET_HEADS = 4
RET_DK = D_MODEL // RET_HEADS
RET_DV = 2 * RET_DK
ROPE_BASE = 10000.0
ATTN_HEADS = 16
HEAD_DIM = D_MODEL // ATTN_HEADS
MOBA_BLOCK = 256
MOBA_TOPK = 3
REL_BUCKETS = 32
REL_MAX_EXACT = REL_BUCKETS // 2
REL_MAX_DISTANCE = 1024
D_FF = 4 * D_MODEL
EPS = 1e-6
NEG = -1e30
LOG2E = math.log2(math.e)

LANES = 128
BF16_SUBLANES = 16
VMEM_LIMIT_BYTES = 56 * 1024 * 1024

ROW_TILE = 512
COL_TILE = 256
FF_TILE = 1024
GATE_TILE = 512
RET_CHUNK = 256
RET_UNROLL = 4
ATT_TILE = 256
BIAS_TILES = 6
HEADS_PER_PAIR = LANES // HEAD_DIM
PAIRS_PER_STEP = 2
HEADS_PER_STEP = HEADS_PER_PAIR * PAIRS_PER_STEP
ONES_ROWS = BF16_SUBLANES


def _value_group(kind):
    return HEAD_DIM if kind == "moba" else LANES


def _params(*sem):
    return pltpu.CompilerParams(dimension_semantics=sem, vmem_limit_bytes=VMEM_LIMIT_BYTES)


def _rms(xf, w):
    ms = jnp.mean(xf * xf, axis=-1, keepdims=True)
    return xf * lax.rsqrt(ms + EPS) * w


def _dot(a, b):
    return jnp.dot(a, b, preferred_element_type=F32)


def _dot_nt(a, b):
    return lax.dot_general(a, b, (((1,), (1,)), ((), ())), preferred_element_type=F32)


def _dot_tn(a, b):
    return lax.dot_general(a, b, (((0,), (0,)), ((), ())), preferred_element_type=F32)


def _ret_proj_kernel(x_ref, nw_ref, w_ref, cos_ref, sin_ref, q_ref, k_ref, v_ref, g_ref):
    h = _rms(x_ref[...], nw_ref[...]).astype(BF16)
    cos = cos_ref[...]
    sin = sin_ref[...]
    half = RET_DK // 2
    chunks = []
    for out_ref, scale in ((q_ref, 1.0), (k_ref, RET_DK ** -0.5)):
        chunks += [(out_ref, hd * RET_DK, scale) for hd in range(RET_HEADS)]
    for out_ref in (v_ref, g_ref):
        chunks += [(out_ref, c * COL_TILE, None) for c in range(RET_HEADS * RET_DV // COL_TILE)]
    assert RET_DK == COL_TILE

    def project(n):
        return _dot(h, w_ref[:, n * COL_TILE:(n + 1) * COL_TILE])

    acc_next = project(0)
    for n, (out_ref, col0, scale) in enumerate(chunks):
        acc = acc_next
        if n + 1 < len(chunks):
            acc_next = project(n + 1)
        if scale is None:
            out_ref[:, col0:col0 + COL_TILE] = acc.astype(BF16)
        else:
            x1 = acc[:, :half]
            x2 = acc[:, half:]
            r1 = x1 * cos - x2 * sin
            r2 = x1 * sin + x2 * cos
            if scale != 1.0:
                r1 = r1 * scale
                r2 = r2 * scale
            out_ref[:, col0:col0 + half] = r1.astype(BF16)
            out_ref[:, col0 + half:col0 + RET_DK] = r2.astype(BF16)


def _ret_proj(x, nw, w, layer, cos, sin, seq):
    t = x.shape[0]
    n_in = w.shape[2]
    tiles_per_seq = seq // ROW_TILE
    row = lambda i: (i, 0)
    const = lambda i: (0, 0)
    pos = lambda i: (i % tiles_per_seq, 0)
    n_qk = RET_HEADS * RET_DK
    n_v = RET_HEADS * RET_DV
    return pl.pallas_call(
        _ret_proj_kernel,
        grid=(t // ROW_TILE,),
        in_specs=[
            pl.BlockSpec((ROW_TILE, D_MODEL), row),
            pl.BlockSpec((1, D_MODEL), const),
            pl.BlockSpec((None, D_MODEL, n_in), lambda i: (layer, 0, 0)),
            pl.BlockSpec((ROW_TILE, RET_DK // 2), pos),
            pl.BlockSpec((ROW_TILE, RET_DK // 2), pos),
        ],
        out_specs=[
            pl.BlockSpec((ROW_TILE, n_qk), row),
            pl.BlockSpec((ROW_TILE, n_qk), row),
            pl.BlockSpec((ROW_TILE, n_v), row),
            pl.BlockSpec((ROW_TILE, n_v), row),
        ],
        out_shape=[
            jax.ShapeDtypeStruct((t, n_qk), BF16),
            jax.ShapeDtypeStruct((t, n_qk), BF16),
            jax.ShapeDtypeStruct((t, n_v), BF16),
            jax.ShapeDtypeStruct((t, n_v), BF16),
        ],
        compiler_params=_params("parallel"),
        name="ret_proj",
    )(x, nw, w, cos, sin)


def _attn_proj_kernel(v_group, x_ref, nw_ref, w_ref, qn_ref, kn_ref, grp_ref, qt_ref, k_ref, vt_ref):
    h = _rms(x_ref[...], nw_ref[...]).astype(BF16)
    grp = grp_ref[...]
    groups_per_chunk = COL_TILE // v_group
    chunks_per_part = D_MODEL // COL_TILE
    n_chunks = 3 * chunks_per_part

    def project(n):
        return _dot(h, w_ref[:, n * COL_TILE:(n + 1) * COL_TILE])

    acc_next = project(0)
    for n in range(n_chunks):
        acc = acc_next
        if n + 1 < n_chunks:
            acc_next = project(n + 1)
        part, c = divmod(n, chunks_per_part)
        cols = slice(c * COL_TILE, (c + 1) * COL_TILE)
        if part < 2:
            hw_ref = qn_ref if part == 0 else kn_ref
            ms = _dot((acc * acc).astype(BF16), grp)
            acc = acc * lax.rsqrt(ms + EPS) * hw_ref[:, cols]
        if part == 0:
            qt_ref[cols, :] = acc.T.astype(BF16)
        elif part == 1:
            k_ref[:, cols] = acc.astype(BF16)
        else:
            acc_t = acc.T.astype(BF16)
            for p in range(groups_per_chunk):
                row0 = (c * groups_per_chunk + p) * (v_group + ONES_ROWS)
                vt_ref[row0:row0 + v_group, :] = acc_t[p * v_group:(p + 1) * v_group, :]
                vt_ref[row0 + v_group:row0 + v_group + ONES_ROWS, :] = jnp.ones((ONES_ROWS, acc_t.shape[1]), BF16)


def _attn_proj(x, nw, w, layer, qn, kn, grp, v_group):
    t = x.shape[0]
    row = lambda i: (i, 0)
    col = lambda i: (0, i)
    const = lambda i: (0, 0)
    vt_rows = (D_MODEL // v_group) * (v_group + ONES_ROWS)
    return pl.pallas_call(
        functools.partial(_attn_proj_kernel, v_group),
        grid=(t // ROW_TILE,),
        in_specs=[
            pl.BlockSpec((ROW_TILE, D_MODEL), row),
            pl.BlockSpec((1, D_MODEL), const),
            pl.BlockSpec((None, D_MODEL, 3 * D_MODEL), lambda i: (layer, 0, 0)),
            pl.BlockSpec((1, D_MODEL), const),
            pl.BlockSpec((1, D_MODEL), const),
            pl.BlockSpec((COL_TILE, COL_TILE), const),
        ],
        out_specs=[
            pl.BlockSpec((D_MODEL, ROW_TILE), col),
            pl.BlockSpec((ROW_TILE, D_MODEL), row),
            pl.BlockSpec((vt_rows, ROW_TILE), col),
        ],
        out_shape=[
            jax.ShapeDtypeStruct((D_MODEL, t), BF16),
            jax.ShapeDtypeStruct((t, D_MODEL), BF16),
            jax.ShapeDtypeStruct((vt_rows, t), BF16),
        ],
        compiler_params=_params("parallel"),
        name="attn_proj",
    )(x, nw, w, qn, kn, grp)


def _ret_core_kernel(q_ref, k_ref, v_ref, dm_ref, rs_ref, ks_ref, cd_ref, o_ref, state_ref, raw_ref):
    c_len = RET_CHUNK
    n_chunks = q_ref.shape[0] // c_len
    state_ref[...] = jnp.zeros_like(state_ref)
    raw_ref[...] = jnp.zeros_like(raw_ref)

    def chunk(c):
        return pl.ds(pl.multiple_of(c * c_len, c_len), c_len)

    def masked_scores(c):
        return (_dot_nt(q_ref[chunk(c), :], k_ref[chunk(c), :]) * dm_ref[0]).astype(BF16)

    def normalise(c, slot):
        o = raw_ref[slot]
        o_ref[chunk(c), :] = (o * lax.rsqrt(jnp.mean(o * o, axis=-1, keepdims=True) + EPS)).astype(BF16)

    def step(c, slot, s_cur):
        sl = chunk(c)
        q = q_ref[sl, :]
        k = k_ref[sl, :]
        v = v_ref[sl, :]
        state = state_ref[...]
        kd = (k.astype(F32) * ks_ref[0]).astype(BF16)
        kv = _dot_tn(kd, v)
        inner = _dot(s_cur, v)
        cross = _dot(q, state.astype(BF16))
        s_next = masked_scores(jnp.minimum(c + 1, n_chunks - 1))
        normalise(jnp.maximum(c - 1, 0), 1 - slot)
        raw_ref[slot] = inner + cross * rs_ref[0]
        state_ref[...] = state * cd_ref[0] + kv
        return s_next

    def body(u, s_cur):
        for r in range(RET_UNROLL):
            s_cur = step(RET_UNROLL * u + r, r % 2, s_cur)
        return s_cur

    lax.fori_loop(0, n_chunks // RET_UNROLL, body, masked_scores(0))
    normalise(n_chunks - 1, 1)


def _ret_core(q, k, v, dm, rs, ks, cd, batch, seq):
    t = q.shape[0]
    c_len = RET_CHUNK
    tok = lambda b, h: (b, h)
    head3 = lambda b, h: (h, 0, 0)
    return pl.pallas_call(
        _ret_core_kernel,
        grid=(batch, RET_HEADS),
        in_specs=[
            pl.BlockSpec((seq, RET_DK), tok),
            pl.BlockSpec((seq, RET_DK), tok),
            pl.BlockSpec((seq, RET_DV), tok),
            pl.BlockSpec((1, c_len, c_len), head3),
            pl.BlockSpec((1, c_len, 1), head3),
            pl.BlockSpec((1, c_len, 1), head3),
            pl.BlockSpec((1, 1, 1), head3),
        ],
        out_specs=pl.BlockSpec((seq, RET_DV), tok),
        out_shape=jax.ShapeDtypeStruct((t, RET_HEADS * RET_DV), BF16),
        scratch_shapes=[pltpu.VMEM((RET_DK, RET_DV), F32), pltpu.VMEM((2, RET_CHUNK, RET_DV), F32)],
        compiler_params=_params("parallel", "parallel"),
        name="ret_core",
    )(q, k, v, dm, rs, ks, cd)


def _out_mlp_kernel(gated, *refs):
    if gated:
        a_ref, g_ref, wo_ref, x_ref, nw_ref, wu_ref, wd_ref, o_ref = refs
        n_chunks = a_ref.shape[1] // GATE_TILE

        def gate(c):
            cols = slice(c * GATE_TILE, (c + 1) * GATE_TILE)
            g = g_ref[:, cols].astype(F32)
            return (g * (1.0 / (1.0 + jnp.exp(-g))) * a_ref[:, cols].astype(F32)).astype(BF16)

        a_next = gate(0)
        mix = None
        for c in range(n_chunks):
            a = a_next
            if c + 1 < n_chunks:
                a_next = gate(c + 1)
            part = _dot(a, wo_ref[c * GATE_TILE:(c + 1) * GATE_TILE, :])
            mix = part if mix is None else mix + part
    else:
        a_ref, wo_ref, x_ref, nw_ref, wu_ref, wd_ref, o_ref = refs
        mix = _dot_tn(a_ref[...], wo_ref[...])
    x = x_ref[...] + mix
    h = _rms(x, nw_ref[...]).astype(BF16)
    acc = x
    for c in range(D_FF // FF_TILE):
        u = _dot(h, wu_ref[:, c * FF_TILE:(c + 1) * FF_TILE])
        u = jnp.maximum(u, 0.0)
        acc = acc + _dot((u * u).astype(BF16), wd_ref[c * FF_TILE:(c + 1) * FF_TILE, :])
    o_ref[...] = acc


def _out_mlp(a, g, wo, mixer_layer, x, nw, wu, wd, layer):
    t = x.shape[0]
    ka = wo.shape[1]
    row = lambda i: (i, 0)
    const = lambda i: (0, 0)
    gated = g is not None
    if gated:
        mix_specs = [pl.BlockSpec((ROW_TILE, ka), row), pl.BlockSpec((ROW_TILE, ka), row)]
        mix_args = [a, g]
    else:
        mix_specs = [pl.BlockSpec((ka, ROW_TILE), lambda i: (0, i))]
        mix_args = [a]
    return pl.pallas_call(
        functools.partial(_out_mlp_kernel, gated),
        grid=(t // ROW_TILE,),
        in_specs=mix_specs + [
            pl.BlockSpec((None, ka, D_MODEL), lambda i: (mixer_layer, 0, 0)),
            pl.BlockSpec((ROW_TILE, D_MODEL), row),
            pl.BlockSpec((1, D_MODEL), const),
            pl.BlockSpec((None, D_MODEL, D_FF), lambda i: (layer, 0, 0), pipeline_mode=pl.Buffered(1)),
            pl.BlockSpec((None, D_FF, D_MODEL), lambda i: (layer, 0, 0), pipeline_mode=pl.Buffered(1)),
        ],
        out_specs=pl.BlockSpec((ROW_TILE, D_MODEL), row),
        out_shape=jax.ShapeDtypeStruct((t, D_MODEL), F32),
        compiler_params=_params("parallel"),
        name="out_mlp_gated" if gated else "out_mlp",
    )(*mix_args, wo, x, nw, wu, wd)


def _bias_tiles_kernel(bucket_ranges, rb_ref, idx_ref, o_ref):
    head = pl.program_id(0)
    for d, (lo, hi) in enumerate(bucket_ranges):
        idx = idx_ref[d]
        acc = jnp.full(idx.shape, NEG, F32)
        for b in range(lo, hi + 1):
            acc = jnp.where(idx == b, rb_ref[b, head] * LOG2E, acc)
        o_ref[0, d] = acc


def _bucket_ranges():
    ranges = []
    for d in range(BIAS_TILES):
        lo_dist = max(d * ATT_TILE - (ATT_TILE - 1), 0)
        hi_dist = d * ATT_TILE + (ATT_TILE - 1)

        def bucket(n):
            if n < REL_MAX_EXACT:
                return n
            return min(REL_MAX_EXACT + int(math.log(n / REL_MAX_EXACT) / math.log(REL_MAX_DISTANCE / REL_MAX_EXACT)
                                           * (REL_BUCKETS - REL_MAX_EXACT)), REL_BUCKETS - 1)

        ranges.append((max(bucket(lo_dist) - 1, 0), min(bucket(hi_dist) + 1, REL_BUCKETS - 1)))
    return tuple(ranges)


def _bias_tiles(rel_bias, bucket_idx):
    nd, tq, tk = bucket_idx.shape
    return pl.pallas_call(
        functools.partial(_bias_tiles_kernel, _bucket_ranges()),
        grid=(ATTN_HEADS,),
        in_specs=[
            pl.BlockSpec(memory_space=pltpu.SMEM),
            pl.BlockSpec((nd, tq, tk), lambda h: (0, 0, 0)),
        ],
        out_specs=pl.BlockSpec((1, nd, tq, tk), lambda h: (h, 0, 0, 0)),
        out_shape=jax.ShapeDtypeStruct((ATTN_HEADS, nd, tq, tk), F32),
        compiler_params=_params("parallel"),
        name="bias_tiles",
    )(rel_bias, bucket_idx)


def _rel_bucket(dist):
    n = jnp.maximum(dist, 0)
    nf = jnp.maximum(n, 1).astype(F32)
    large = REL_MAX_EXACT + (jnp.log(nf / REL_MAX_EXACT) / math.log(REL_MAX_DISTANCE / REL_MAX_EXACT)
                             * (REL_BUCKETS - REL_MAX_EXACT)).astype(jnp.int32)
    large = jnp.minimum(large, REL_BUCKETS - 1)
    return jnp.where(n < REL_MAX_EXACT, n, large)


def _bucket_index_tiles():
    r = np.arange(ATT_TILE)
    dist = (np.arange(BIAS_TILES)[:, None, None] * ATT_TILE + r[None, None, :] - r[None, :, None])
    dist = jnp.asarray(dist, jnp.int32)
    return jnp.where(dist >= 0, _rel_bucket(dist), REL_BUCKETS).astype(jnp.int32)


TAB_Q, TAB_K, TAB_DELTA, TAB_FIRST, TAB_ACC = range(5)
PIPE_LAG = 2
PIPE_UNROLL = 6
QUERY_HALVES = tuple(slice(h * LANES, (h + 1) * LANES) for h in range(ATT_TILE // LANES))


def _tile_schedule(nq, own_first):
    rows = []
    for i in range(nq):
        keys = ([i] + list(range(i))) if own_first else list(range(i + 1))
        for n, j in enumerate(keys):
            rows.append((i, j, min(i - j, BIAS_TILES - 1), int(n == 0), i))
    n_iters = -(-(len(rows) + PIPE_LAG) // PIPE_UNROLL) * PIPE_UNROLL
    idle = (0, 0, 0, 1, nq)
    cols = [idle] * PIPE_LAG + rows
    cols += [idle] * (n_iters + PIPE_LAG - len(cols))
    return jnp.asarray(np.array(cols, np.int32).T), n_iters


def _tile_slice(idx):
    return pl.ds(pl.multiple_of(idx * ATT_TILE, ATT_TILE), ATT_TILE)


def _head_row_mask(h):
    row = lax.broadcasted_iota(jnp.int32, (LANES, 1), 0)
    return (row >= h * HEAD_DIM) & (row < (h + 1) * HEAD_DIM)


def _pair_rows(pair, n=LANES):
    return slice(pair * n, (pair + 1) * n)


def _attn_kernel(kind, lambda_init, n_iters, tab_ref, qt_ref, k_ref, vt_ref, bias_ref, *rest):
    if kind == "moba":
        blk_ref, ot_ref, qts_ref, vts_ref, s_ref, p_ref, acc_ref, neg_ref = rest
    else:
        lam_ref, sw_ref, ot_ref, qts_ref, vts_ref, s_ref, p_ref, acc_ref = rest
    heads = range(HEADS_PER_STEP)
    seq = qt_ref.shape[1]
    acc_rows = acc_ref.shape[2]

    n_tiles = seq // ATT_TILE
    q_heads = []
    for e in heads:
        pair, h = divmod(e, HEADS_PER_PAIR)
        qf = qt_ref[_pair_rows(pair), :].astype(F32)
        q_heads.append(jnp.where(_head_row_mask(h), qf, 0.0).astype(BF16))
        for i in range(n_tiles):
            qts_ref[i, e] = q_heads[e][:, i * ATT_TILE:(i + 1) * ATT_TILE]
    for j in range(n_tiles):
        vts_ref[j] = vt_ref[:, j * ATT_TILE:(j + 1) * ATT_TILE]
    s_ref[...] = jnp.zeros_like(s_ref)
    p_ref[...] = jnp.zeros_like(p_ref)
    acc_ref[...] = jnp.zeros_like(acc_ref)

    if kind == "moba":
        nb = blk_ref.shape[0]
        kmean = _dot(blk_ref[...], k_ref[...])
        km_hi = kmean.astype(BF16)
        km_lo = (kmean - km_hi.astype(F32)).astype(BF16)
        blk_id = lax.broadcasted_iota(jnp.int32, (nb, ATT_TILE), 0)
        for e in heads:
            pair = e // HEADS_PER_PAIR
            q_e = q_heads[e]
            gate_all = _dot(km_hi[:, _pair_rows(pair)], q_e) + _dot(km_lo[:, _pair_rows(pair)], q_e)
            for i in range(n_tiles):
                cols = slice(i * ATT_TILE, (i + 1) * ATT_TILE)
                past = blk_id < i
                gate = jnp.where(past, gate_all[:, cols], NEG)
                rank = jnp.zeros((nb, ATT_TILE), jnp.int32)
                for c in range(i):
                    gc = gate[c:c + 1, :]
                    tie = jnp.where(blk_id > c, 1, 0)
                    rank = rank + jnp.where(gc > gate, 1, jnp.where(gc == gate, tie, 0))
                chosen = jnp.where(rank < MOBA_TOPK, jnp.where(past, 1, 0), 0)
                keep = jnp.maximum(chosen, jnp.where(blk_id == i, 1, 0))
                neg_ref[e, :, cols] = jnp.where(keep == 1, 0.0, NEG)

    def step(t, cur, carry):
        nxt = 1 - cur
        m_prev, alpha_prev, mtile_prev = carry
        ic = tab_ref[TAB_ACC, t]
        jc = tab_ref[TAB_K, t]
        ib = tab_ref[TAB_Q, t + 1]
        jb = tab_ref[TAB_K, t + 1]
        first = tab_ref[TAB_FIRST, t + 1] != 0
        ia = tab_ref[TAB_Q, t + 2]
        ja = tab_ref[TAB_K, t + 2]
        da = tab_ref[TAB_DELTA, t + 2]
        m_new, alpha_new, mtile_new = [], [], []
        for e in heads:
            pair = e // HEADS_PER_PAIR

            vt = vts_ref[jc, _pair_rows(e if kind == "moba" else pair, acc_rows), :]
            pv = _dot(vt, p_ref[cur, e])
            for hf, lanes in enumerate(QUERY_HALVES):
                acc_ref[ic, e, :, lanes] = alpha_prev[e][hf] * acc_ref[ic, e, :, lanes] + pv[:, lanes]

            kt = k_ref[_tile_slice(ja), _pair_rows(pair)]
            sb = _dot(kt, qts_ref[ia, e]) + bias_ref[e, da]
            s_ref[nxt, e] = sb
            mtile_new.append(tuple(jnp.max(sb[:, lanes], axis=0, keepdims=True) for lanes in QUERY_HALVES))

            m_parts, alpha_parts = [], []
            if kind == "moba":
                neg_row = neg_ref[e, pl.ds(jb, 1), _tile_slice(ib)]
            for hf, lanes in enumerate(QUERY_HALVES):
                s = s_ref[cur, e, :, lanes]
                m_in = jnp.where(first, -jnp.inf, m_prev[e][hf])
                m_tile = mtile_prev[e][hf]
                if kind == "moba":
                    neg = neg_row[:, lanes]
                    m_e = jnp.maximum(m_in, m_tile + neg)
                    shift = m_e - neg
                else:
                    m_e = jnp.maximum(m_in, m_tile)
                    shift = m_e
                p_ref[nxt, e, :, lanes] = jnp.exp2((s - shift).astype(BF16))
                alpha_parts.append(jnp.exp2(m_in - m_e))
                m_parts.append(m_e)
            alpha_new.append(tuple(alpha_parts))
            m_new.append(tuple(m_parts))

        return tuple(m_new), tuple(alpha_new), tuple(mtile_new)

    zeros = tuple(tuple(jnp.zeros((1, LANES), F32) for _ in QUERY_HALVES) for _ in heads)

    def body(u, carry):
        for r in range(PIPE_UNROLL):
            carry = step(PIPE_UNROLL * u + r, r % 2, carry)
        return carry

    lax.fori_loop(0, n_iters // PIPE_UNROLL, body, (zeros, zeros, zeros))

    def emit(i, carry):
        cols = _tile_slice(i)
        v_group = acc_ref.shape[2] - ONES_ROWS
        for pair in range(PAIRS_PER_STEP):
            outs = []
            for h in range(HEADS_PER_PAIR):
                e = pair * HEADS_PER_PAIR + h
                outs.append(acc_ref[i, e, :v_group, :] * (1.0 / acc_ref[i, e, v_group:v_group + 1, :]))
            if kind == "moba":
                o = jnp.concatenate(outs, axis=0)
            else:
                lam = lam_ref[...]
                lam_full = (jnp.exp(jnp.sum(lam[0:1] * lam[1:2], axis=-1, keepdims=True))
                            - jnp.exp(jnp.sum(lam[2:3] * lam[3:4], axis=-1, keepdims=True)) + lambda_init)
                o = outs[0] - lam_full * outs[1]
                o = (o * lax.rsqrt(jnp.mean(o * o, axis=0, keepdims=True) + EPS)
                     * sw_ref[...] * (1.0 - lambda_init))
            ot_ref[_pair_rows(pair), cols] = o.astype(BF16)
        return carry

    lax.fori_loop(0, seq // ATT_TILE, emit, 0)


def _attention(kind, qt, k, vt_ext, bias, extras, lambda_init, batch, seq):
    v_group = _value_group(kind)
    acc_rows = v_group + ONES_ROWS
    vt_block_rows = acc_rows * (LANES * PAIRS_PER_STEP // v_group)
    t = k.shape[0]
    nq = seq // ATT_TILE
    tab, n_iters = _tile_schedule(nq, own_first=(kind == "moba"))
    groups = ATTN_HEADS // HEADS_PER_STEP
    rows = LANES * PAIRS_PER_STEP
    in_specs = [
        pl.BlockSpec(memory_space=pltpu.SMEM),
        pl.BlockSpec((rows, seq), lambda g, b: (g, b)),
        pl.BlockSpec((seq, rows), lambda g, b: (b, g)),
        pl.BlockSpec((vt_block_rows, seq), lambda g, b: (g, b)),
        pl.BlockSpec((HEADS_PER_STEP, BIAS_TILES, ATT_TILE, ATT_TILE), lambda g, b: (g, 0, 0, 0)),
    ]
    scratch = [
        pltpu.VMEM((nq, HEADS_PER_STEP, LANES, ATT_TILE), BF16),
        pltpu.VMEM((nq, vt_block_rows, ATT_TILE), BF16),
        pltpu.VMEM((2, HEADS_PER_STEP, ATT_TILE, ATT_TILE), F32),
        pltpu.VMEM((2, HEADS_PER_STEP, ATT_TILE, ATT_TILE), BF16),
        pltpu.VMEM((nq + 1, HEADS_PER_STEP, acc_rows, ATT_TILE), F32),
    ]
    if kind == "moba":
        (blk,) = extras
        nb = blk.shape[0]
        in_specs.append(pl.BlockSpec((nb, seq), lambda g, b: (0, 0)))
        scratch.append(pltpu.VMEM((HEADS_PER_STEP, nb, seq), F32))
    else:
        lam, sw = extras
        in_specs += [
            pl.BlockSpec((4, HEAD_DIM), lambda g, b: (0, 0)),
            pl.BlockSpec((LANES, 1), lambda g, b: (0, 0)),
        ]
    return pl.pallas_call(
        functools.partial(_attn_kernel, kind, lambda_init, n_iters),
        grid=(groups, batch),
        in_specs=in_specs,
        out_specs=pl.BlockSpec((rows, seq), lambda g, b: (g, b)),
        out_shape=jax.ShapeDtypeStruct((D_MODEL, t), BF16),
        scratch_shapes=scratch,
        compiler_params=_params("parallel", "parallel"),
        name=kind + "_attn",
    )(tab, qt, k, vt_ext, bias, *extras)


def _rotary_tables(seq):
    d = RET_DK
    inv_freq = ROPE_BASE ** (-np.arange(0, d, 2, dtype=np.float64) / d)
    ang = np.arange(seq, dtype=np.float64)[:, None] * inv_freq[None, :]
    return jnp.asarray(np.cos(ang), F32), jnp.asarray(np.sin(ang), F32)


def _retention_decay_tables():
    c_len = RET_CHUNK
    log_gamma = np.log(1.0 - 2.0 ** (-5.0 - np.arange(RET_HEADS, dtype=np.float64)))
    pos = np.arange(c_len, dtype=np.float64)
    rel = pos[:, None] - pos[None, :]
    dm = np.where(rel >= 0, np.exp(np.maximum(rel, 0.0)[None] * log_gamma[:, None, None]), 0.0)
    rs = np.exp((pos + 1.0)[None, :] * log_gamma[:, None])[:, :, None]
    ks = np.exp((c_len - 1.0 - pos)[None, :] * log_gamma[:, None])[:, :, None]
    cd = np.exp(c_len * log_gamma)[:, None, None]
    return tuple(jnp.asarray(a, F32) for a in (dm, rs, ks, cd))


def _block_mean_matrix(seq):
    nb = seq // MOBA_BLOCK
    m = (np.arange(seq)[None, :] // MOBA_BLOCK == np.arange(nb)[:, None]) / float(MOBA_BLOCK)
    return jnp.asarray(m, BF16)


def _head_group_matrix():
    g = np.arange(COL_TILE)[:, None] // HEAD_DIM == np.arange(COL_TILE)[None, :] // HEAD_DIM
    return jnp.asarray(g / float(HEAD_DIM), BF16)


def kernel(x, rel_bias, norm1, norm2, w_up, w_down, ret_w_in, ret_w_out,
           moba_w_in, moba_q_norm, moba_k_norm, moba_w_out,
           diff_w_in, diff_q_norm, diff_k_norm, diff_lambda, diff_subln, diff_w_out):
    batch, seq, d = x.shape
    depth = norm1.shape[0]
    assert d == D_MODEL and seq % ROW_TILE == 0 and seq % ATT_TILE == 0 and seq % (RET_UNROLL * RET_CHUNK) == 0
    assert seq % MOBA_BLOCK == 0 and MOBA_BLOCK == ATT_TILE
    t = batch * seq
    xf = x.reshape(t, d)

    bias = _bias_tiles(rel_bias.astype(F32), _bucket_index_tiles())
    grp = _head_group_matrix()
    q_scale = HEAD_DIM ** -0.5 * LOG2E
    wu, wd = w_up.astype(BF16), w_down.astype(BF16)
    ret_wi, ret_wo = ret_w_in.astype(BF16), ret_w_out.astype(BF16)
    moba_wi, moba_wo = moba_w_in.astype(BF16), moba_w_out.astype(BF16)
    diff_wi, diff_wo = diff_w_in.astype(BF16), diff_w_out.astype(BF16)

    for i in range(depth):
        kind, j = i % N_MIXERS, i // N_MIXERS
        nw1 = norm1[i].reshape(1, d)
        nw2 = norm2[i].reshape(1, d)
        if kind == 0:
            cos, sin = _rotary_tables(seq)
            dm, rs, ks, cd = _retention_decay_tables()
            q, k, v, g = _ret_proj(xf, nw1, ret_wi, j, cos, sin, seq)
            o = _ret_core(q, k, v, dm, rs, ks, cd, batch, seq)
            xf = _out_mlp(o, g, ret_wo, j, xf, nw2, wu, wd, i)
        elif kind == 1:
            qn = (jnp.tile(moba_q_norm[j], ATTN_HEADS) * q_scale).reshape(1, d)
            kn = jnp.tile(moba_k_norm[j], ATTN_HEADS).reshape(1, d)
            qt, k, vt = _attn_proj(xf, nw1, moba_wi, j, qn, kn, grp, _value_group("moba"))
            ot = _attention("moba", qt, k, vt, bias, (_block_mean_matrix(seq),), 0.0, batch, seq)
            xf = _out_mlp(ot, None, moba_wo, j, xf, nw2, wu, wd, i)
        else:
            lambda_init = 0.8 - 0.6 * math.exp(-0.3 * i)
            qn = (jnp.tile(diff_q_norm[j], ATTN_HEADS) * q_scale).reshape(1, d)
            kn = jnp.tile(diff_k_norm[j], ATTN_HEADS).reshape(1, d)
            qt, k, vt = _attn_proj(xf, nw1, diff_wi, j, qn, kn, grp, _value_group("diff"))
            extras = (diff_lambda[j].astype(F32), diff_subln[j].reshape(LANES, 1))
            ot = _attention("diff", qt, k, vt, bias, extras, lambda_init, batch, seq)
            xf = _out_mlp(ot, None, diff_wo, j, xf, nw2, wu, wd, i)
    return xf.reshape(batch, seq, d)
```

```python
import functools
import math

import numpy as np
import jax
import jax.numpy as jnp
from jax import lax
from jax.experimental import pallas as pl
from jax.experimental.pallas import tpu as pltpu

F32 = jnp.float32
BF16 = jnp.bfloat16

D_MODEL = 1024
N_MIXERS = 3
RET_HEADS = 4
RET_DK = D_MODEL // RET_HEADS
RET_DV = 2 * RET_DK
ROPE_BASE = 10000.0
ATTN_HEADS = 16
HEAD_DIM = D_MODEL // ATTN_HEADS
MOBA_BLOCK = 256
MOBA_TOPK = 3
REL_BUCKETS = 32
REL_MAX_EXACT = REL_BUCKETS // 2
REL_MAX_DISTANCE = 1024
D_FF = 4 * D_MODEL
EPS = 1e-6
NEG = -1e30
LOG2E = math.log2(math.e)

LANES = 128
BF16_SUBLANES = 16
VMEM_LIMIT_BYTES = 56 * 1024 * 1024

ROW_TILE = 512
COL_TILE = 256
FF_TILE = 1024
GATE_TILE = 512
RET_CHUNK = 256
RET_UNROLL = 4
ATT_TILE = 256
BIAS_TILES = 6
HEADS_PER_PAIR = LANES // HEAD_DIM
PAIRS_PER_STEP = 2
HEADS_PER_STEP = HEADS_PER_PAIR * PAIRS_PER_STEP
ONES_ROWS = BF16_SUBLANES


def _value_group(kind):
    return HEAD_DIM if kind == "moba" else LANES


def _params(*sem):
    return pltpu.CompilerParams(dimension_semantics=sem, vmem_limit_bytes=VMEM_LIMIT_BYTES)


def _rms(xf, w):
    ms = jnp.mean(xf * xf, axis=-1, keepdims=True)
    return xf * lax.rsqrt(ms + EPS) * w


def _dot(a, b):
    return jnp.dot(a, b, preferred_element_type=F32)


def _dot_nt(a, b):
    return lax.dot_general(a, b, (((1,), (1,)), ((), ())), preferred_element_type=F32)


def _dot_tn(a, b):
    return lax.dot_general(a, b, (((0,), (0,)), ((), ())), preferred_element_type=F32)


def _ret_proj_kernel(x_ref, nw_ref, w_ref, cos_ref, sin_ref, q_ref, k_ref, v_ref, g_ref):
    h = _rms(x_ref[...], nw_ref[...]).astype(BF16)
    cos = cos_ref[...]
    sin = sin_ref[...]
    half = RET_DK // 2
    chunks = []
    for out_ref, scale in ((q_ref, 1.0), (k_ref, RET_DK ** -0.5)):
        chunks += [(out_ref, hd * RET_DK, scale) for hd in range(RET_HEADS)]
    for out_ref in (v_ref, g_ref):
        chunks += [(out_ref, c * COL_TILE, None) for c in range(RET_HEADS * RET_DV // COL_TILE)]
    assert RET_DK == COL_TILE

    def project(n):
        return _dot(h, w_ref[:, n * COL_TILE:(n + 1) * COL_TILE])

    acc_next = project(0)
    for n, (out_ref, col0, scale) in enumerate(chunks):
        acc = acc_next
        if n + 1 < len(chunks):
            acc_next = project(n + 1)
        if scale is None:
            out_ref[:, col0:col0 + COL_TILE] = acc.astype(BF16)
        else:
            x1 = acc[:, :half]
            x2 = acc[:, half:]
            r1 = x1 * cos - x2 * sin
            r2 = x1 * sin + x2 * cos
            if scale != 1.0:
                r1 = r1 * scale
                r2 = r2 * scale
            out_ref[:, col0:col0 + half] = r1.astype(BF16)
            out_ref[:, col0 + half:col0 + RET_DK] = r2.astype(BF16)


def _ret_proj(x, nw, w, layer, cos, sin, seq):
    t = x.shape[0]
    n_in = w.shape[2]
    tiles_per_seq = seq // ROW_TILE
    row = lambda i: (i, 0)
    const = lambda i: (0, 0)
    pos = lambda i: (i % tiles_per_seq, 0)
    n_qk = RET_HEADS * RET_DK
    n_v = RET_HEADS * RET_DV
    return pl.pallas_call(
        _ret_proj_kernel,
        grid=(t // ROW_TILE,),
        in_specs=[
            pl.BlockSpec((ROW_TILE, D_MODEL), row),
            pl.BlockSpec((1, D_MODEL), const),
            pl.BlockSpec((None, D_MODEL, n_in), lambda i: (layer, 0, 0)),
            pl.BlockSpec((ROW_TILE, RET_DK // 2), pos),
            pl.BlockSpec((ROW_TILE, RET_DK // 2), pos),
        ],
        out_specs=[
            pl.BlockSpec((ROW_TILE, n_qk), row),
            pl.BlockSpec((ROW_TILE, n_qk), row),
            pl.BlockSpec((ROW_TILE, n_v), row),
            pl.BlockSpec((ROW_TILE, n_v), row),
        ],
        out_shape=[
            jax.ShapeDtypeStruct((t, n_qk), BF16),
            jax.ShapeDtypeStruct((t, n_qk), BF16),
            jax.ShapeDtypeStruct((t, n_v), BF16),
            jax.ShapeDtypeStruct((t, n_v), BF16),
        ],
        compiler_params=_params("parallel"),
        name="ret_proj",
    )(x, nw, w, cos, sin)


def _attn_proj_kernel(v_group, x_ref, nw_ref, w_ref, qn_ref, kn_ref, grp_ref, qt_ref, k_ref, vt_ref):
    h = _rms(x_ref[...], nw_ref[...]).astype(BF16)
    grp = grp_ref[...]
    groups_per_chunk = COL_TILE // v_group
    chunks_per_part = D_MODEL // COL_TILE
    n_chunks = 3 * chunks_per_part

    def project(n):
        return _dot(h, w_ref[:, n * COL_TILE:(n + 1) * COL_TILE])

    acc_next = project(0)
    for n in range(n_chunks):
        acc = acc_next
        if n + 1 < n_chunks:
            acc_next = project(n + 1)
        part, c = divmod(n, chunks_per_part)
        cols = slice(c * COL_TILE, (c + 1) * COL_TILE)
        if part < 2:
            hw_ref = qn_ref if part == 0 else kn_ref
            ms = _dot((acc * acc).astype(BF16), grp)
            acc = acc * lax.rsqrt(ms + EPS) * hw_ref[:, cols]
        if part == 0:
            qt_ref[cols, :] = acc.T.astype(BF16)
        elif part == 1:
            k_ref[:, cols] = acc.astype(BF16)
        else:
            acc_t = acc.T.astype(BF16)
            for p in range(groups_per_chunk):
                row0 = (c * groups_per_chunk + p) * (v_group + ONES_ROWS)
                vt_ref[row0:row0 + v_group, :] = acc_t[p * v_group:(p + 1) * v_group, :]
                vt_ref[row0 + v_group:row0 + v_group + ONES_ROWS, :] = jnp.ones((ONES_ROWS, acc_t.shape[1]), BF16)


def _attn_proj(x, nw, w, layer, qn, kn, grp, v_group):
    t = x.shape[0]
    row = lambda i: (i, 0)
    col = lambda i: (0, i)
    const = lambda i: (0, 0)
    vt_rows = (D_MODEL // v_group) * (v_group + ONES_ROWS)
    return pl.pallas_call(
        functools.partial(_attn_proj_kernel, v_group),
        grid=(t // ROW_TILE,),
        in_specs=[
            pl.BlockSpec((ROW_TILE, D_MODEL), row),
            pl.BlockSpec((1, D_MODEL), const),
            pl.BlockSpec((None, D_MODEL, 3 * D_MODEL), lambda i: (layer, 0, 0)),
            pl.BlockSpec((1, D_MODEL), const),
            pl.BlockSpec((1, D_MODEL), const),
            pl.BlockSpec((COL_TILE, COL_TILE), const),
        ],
        out_specs=[
            pl.BlockSpec((D_MODEL, ROW_TILE), col),
            pl.BlockSpec((ROW_TILE, D_MODEL), row),
            pl.BlockSpec((vt_rows, ROW_TILE), col),
        ],
        out_shape=[
            jax.ShapeDtypeStruct((D_MODEL, t), BF16),
            jax.ShapeDtypeStruct((t, D_MODEL), BF16),
            jax.ShapeDtypeStruct((vt_rows, t), BF16),
        ],
        compiler_params=_params("parallel"),
        name="attn_proj",
    )(x, nw, w, qn, kn, grp)


def _ret_core_kernel(q_ref, k_ref, v_ref, dm_ref, rs_ref, ks_ref, cd_ref, o_ref, state_ref, raw_ref):
    c_len = RET_CHUNK
    n_chunks = q_ref.shape[0] // c_len
    state_ref[...] = jnp.zeros_like(state_ref)
    raw_ref[...] = jnp.zeros_like(raw_ref)

    def chunk(c):
        return pl.ds(pl.multiple_of(c * c_len, c_len), c_len)

    def masked_scores(c):
        return (_dot_nt(q_ref[chunk(c), :], k_ref[chunk(c), :]) * dm_ref[0]).astype(BF16)

    def normalise(c, slot):
        o = raw_ref[slot]
        o_ref[chunk(c), :] = (o * lax.rsqrt(jnp.mean(o * o, axis=-1, keepdims=True) + EPS)).astype(BF16)

    def step(c, slot, s_cur):
        sl = chunk(c)
        q = q_ref[sl, :]
        k = k_ref[sl, :]
        v = v_ref[sl, :]
        state = state_ref[...]
        kd = (k.astype(F32) * ks_ref[0]).astype(BF16)
        kv = _dot_tn(kd, v)
        inner = _dot(s_cur, v)
        cross = _dot(q, state.astype(BF16))
        s_next = masked_scores(jnp.minimum(c + 1, n_chunks - 1))
        normalise(jnp.maximum(c - 1, 0), 1 - slot)
        raw_ref[slot] = inner + cross * rs_ref[0]
        state_ref[...] = state * cd_ref[0] + kv
        return s_next

    def body(u, s_cur):
        for r in range(RET_UNROLL):
            s_cur = step(RET_UNROLL * u + r, r % 2, s_cur)
        return s_cur

    lax.fori_loop(0, n_chunks // RET_UNROLL, body, masked_scores(0))
    normalise(n_chunks - 1, 1)


def _ret_core(q, k, v, dm, rs, ks, cd, batch, seq):
    t = q.shape[0]
    c_len = RET_CHUNK
    tok = lambda b, h: (b, h)
    head3 = lambda b, h: (h, 0, 0)
    return pl.pallas_call(
        _ret_core_kernel,
        grid=(batch, RET_HEADS),
        in_specs=[
            pl.BlockSpec((seq, RET_DK), tok),
            pl.BlockSpec((seq, RET_DK), tok),
            pl.BlockSpec((seq, RET_DV), tok),
            pl.BlockSpec((1, c_len, c_len), head3),
            pl.BlockSpec((1, c_len, 1), head3),
            pl.BlockSpec((1, c_len, 1), head3),
            pl.BlockSpec((1, 1, 1), head3),
        ],
        out_specs=pl.BlockSpec((seq, RET_DV), tok),
        out_shape=jax.ShapeDtypeStruct((t, RET_HEADS * RET_DV), BF16),
        scratch_shapes=[pltpu.VMEM((RET_DK, RET_DV), F32), pltpu.VMEM((2, RET_CHUNK, RET_DV), F32)],
        compiler_params=_params("parallel", "parallel"),
        name="ret_core",
    )(q, k, v, dm, rs, ks, cd)


def _out_mlp_kernel(gated, *refs):
    if gated:
        a_ref, g_ref, wo_ref, x_ref, nw_ref, wu_ref, wd_ref, o_ref = refs
        n_chunks = a_ref.shape[1] // GATE_TILE

        def gate(c):
            cols = slice(c * GATE_TILE, (c + 1) * GATE_TILE)
            g = g_ref[:, cols].astype(F32)
            return (g * (1.0 / (1.0 + jnp.exp(-g))) * a_ref[:, cols].astype(F32)).astype(BF16)

        a_next = gate(0)
        mix = None
        for c in range(n_chunks):
            a = a_next
            if c + 1 < n_chunks:
                a_next = gate(c + 1)
            part = _dot(a, wo_ref[c * GATE_TILE:(c + 1) * GATE_TILE, :])
            mix = part if mix is None else mix + part
    else:
        a_ref, wo_ref, x_ref, nw_ref, wu_ref, wd_ref, o_ref = refs
        mix = _dot_tn(a_ref[...], wo_ref[...])
    x = x_ref[...] + mix
    h = _rms(x, nw_ref[...]).astype(BF16)
    acc = x
    for c in range(D_FF // FF_TILE):
        u = _dot(h, wu_ref[:, c * FF_TILE:(c + 1) * FF_TILE])
        u = jnp.maximum(u, 0.0)
        acc = acc + _dot((u * u).astype(BF16), wd_ref[c * FF_TILE:(c + 1) * FF_TILE, :])
    o_ref[...] = acc


def _out_mlp(a, g, wo, mixer_layer, x, nw, wu, wd, layer):
    t = x.shape[0]
    ka = wo.shape[1]
    row = lambda i: (i, 0)
    const = lambda i: (0, 0)
    gated = g is not None
    if gated:
        mix_specs = [pl.BlockSpec((ROW_TILE, ka), row), pl.BlockSpec((ROW_TILE, ka), row)]
        mix_args = [a, g]
    else:
        mix_specs = [pl.BlockSpec((ka, ROW_TILE), lambda i: (0, i))]
        mix_args = [a]
    return pl.pallas_call(
        functools.partial(_out_mlp_kernel, gated),
        grid=(t // ROW_TILE,),
        in_specs=mix_specs + [
            pl.BlockSpec((None, ka, D_MODEL), lambda i: (mixer_layer, 0, 0)),
            pl.BlockSpec((ROW_TILE, D_MODEL), row),
            pl.BlockSpec((1, D_MODEL), const),
            pl.BlockSpec((None, D_MODEL, D_FF), lambda i: (layer, 0, 0), pipeline_mode=pl.Buffered(1)),
            pl.BlockSpec((None, D_FF, D_MODEL), lambda i: (layer, 0, 0), pipeline_mode=pl.Buffered(1)),
        ],
        out_specs=pl.BlockSpec((ROW_TILE, D_MODEL), row),
        out_shape=jax.ShapeDtypeStruct((t, D_MODEL), F32),
        compiler_params=_params("parallel"),
        name="out_mlp_gated" if gated else "out_mlp",
    )(*mix_args, wo, x, nw, wu, wd)


def _bias_tiles_kernel(bucket_ranges, rb_ref, idx_ref, o_ref):
    head = pl.program_id(0)
    for d, (lo, hi) in enumerate(bucket_ranges):
        idx = idx_ref[d]
        acc = jnp.full(idx.shape, NEG, F32)
        for b in range(lo, hi + 1):
            acc = jnp.where(idx == b, rb_ref[b, head] * LOG2E, acc)
        o_ref[0, d] = acc


def _bucket_ranges():
    ranges = []
    for d in range(BIAS_TILES):
        lo_dist = max(d * ATT_TILE - (ATT_TILE - 1), 0)
        hi_dist = d * ATT_TILE + (ATT_TILE - 1)

        def bucket(n):
            if n < REL_MAX_EXACT:
                return n
            return min(REL_MAX_EXACT + int(math.log(n / REL_MAX_EXACT) / math.log(REL_MAX_DISTANCE / REL_MAX_EXACT)
                                           * (REL_BUCKETS - REL_MAX_EXACT)), REL_BUCKETS - 1)

        ranges.append((max(bucket(lo_dist) - 1, 0), min(bucket(hi_dist) + 1, REL_BUCKETS - 1)))
    return tuple(ranges)


def _bias_tiles(rel_bias, bucket_idx):
    nd, tq, tk = bucket_idx.shape
    return pl.pallas_call(
        functools.partial(_bias_tiles_kernel, _bucket_ranges()),
        grid=(ATTN_HEADS,),
        in_specs=[
            pl.BlockSpec(memory_space=pltpu.SMEM),
            pl.BlockSpec((nd, tq, tk), lambda h: (0, 0, 0)),
        ],
        out_specs=pl.BlockSpec((1, nd, tq, tk), lambda h: (h, 0, 0, 0)),
        out_shape=jax.ShapeDtypeStruct((ATTN_HEADS, nd, tq, tk), F32),
        compiler_params=_params("parallel"),
        name="bias_tiles",
    )(rel_bias, bucket_idx)


def _rel_bucket(dist):
    n = jnp.maximum(dist, 0)
    nf = jnp.maximum(n, 1).astype(F32)
    large = REL_MAX_EXACT + (jnp.log(nf / REL_MAX_EXACT) / math.log(REL_MAX_DISTANCE / REL_MAX_EXACT)
                             * (REL_BUCKETS - REL_MAX_EXACT)).astype(jnp.int32)
    large = jnp.minimum(large, REL_BUCKETS - 1)
    return jnp.where(n < REL_MAX_EXACT, n, large)


def _bucket_index_tiles():
    r = np.arange(ATT_TILE)
    dist = (np.arange(BIAS_TILES)[:, None, None] * ATT_TILE + r[None, None, :] - r[None, :, None])
    dist = jnp.asarray(dist, jnp.int32)
    return jnp.where(dist >= 0, _rel_bucket(dist), REL_BUCKETS).astype(jnp.int32)


TAB_Q, TAB_K, TAB_DELTA, TAB_FIRST, TAB_ACC = range(5)
PIPE_LAG = 2
PIPE_UNROLL = 6
QUERY_HALVES = tuple(slice(h * LANES, (h + 1) * LANES) for h in range(ATT_TILE // LANES))


def _tile_schedule(nq, own_first):
    rows = []
    for i in range(nq):
        keys = ([i] + list(range(i))) if own_first else list(range(i + 1))
        for n, j in enumerate(keys):
            rows.append((i, j, min(i - j, BIAS_TILES - 1), int(n == 0), i))
    n_iters = -(-(len(rows) + PIPE_LAG) // PIPE_UNROLL) * PIPE_UNROLL
    idle = (0, 0, 0, 1, nq)
    cols = [idle] * PIPE_LAG + rows
    cols += [idle] * (n_iters + PIPE_LAG - len(cols))
    return jnp.asarray(np.array(cols, np.int32).T), n_iters


def _tile_slice(idx):
    return pl.ds(pl.multiple_of(idx * ATT_TILE, ATT_TILE), ATT_TILE)


def _head_row_mask(h):
    row = lax.broadcasted_iota(jnp.int32, (LANES, 1), 0)
    return (row >= h * HEAD_DIM) & (row < (h + 1) * HEAD_DIM)


def _pair_rows(pair, n=LANES):
    return slice(pair * n, (pair + 1) * n)


def _attn_kernel(kind, lambda_init, n_iters, tab_ref, qt_ref, k_ref, vt_ref, bias_ref, *rest):
    if kind == "moba":
        blk_ref, ot_ref, qts_ref, vts_ref, s_ref, p_ref, acc_ref, neg_ref = rest
    else:
        lam_ref, sw_ref, ot_ref, qts_ref, vts_ref, s_ref, p_ref, acc_ref = rest
    heads = range(HEADS_PER_STEP)
    seq = qt_ref.shape[1]
    acc_rows = acc_ref.shape[2]

    n_tiles = seq // ATT_TILE
    q_heads = []
    for e in heads:
        pair, h = divmod(e, HEADS_PER_PAIR)
        qf = qt_ref[_pair_rows(pair), :].astype(F32)
        q_heads.append(jnp.where(_head_row_mask(h), qf, 0.0).astype(BF16))
        for i in range(n_tiles):
            qts_ref[i, e] = q_heads[e][:, i * ATT_TILE:(i + 1) * ATT_TILE]
    for j in range(n_tiles):
        vts_ref[j] = vt_ref[:, j * ATT_TILE:(j + 1) * ATT_TILE]
    s_ref[...] = jnp.zeros_like(s_ref)
    p_ref[...] = jnp.zeros_like(p_ref)
    acc_ref[...] = jnp.zeros_like(acc_ref)

    if kind == "moba":
        nb = blk_ref.shape[0]
        kmean = _dot(blk_ref[...], k_ref[...])
        km_hi = kmean.astype(BF16)
        km_lo = (kmean - km_hi.astype(F32)).astype(BF16)
        blk_id = lax.broadcasted_iota(jnp.int32, (nb, ATT_TILE), 0)
        for e in heads:
            pair = e // HEADS_PER_PAIR
            q_e = q_heads[e]
            gate_all = _dot(km_hi[:, _pair_rows(pair)], q_e) + _dot(km_lo[:, _pair_rows(pair)], q_e)
            for i in range(n_tiles):
                cols = slice(i * ATT_TILE, (i + 1) * ATT_TILE)
                past = blk_id < i
                gate = jnp.where(past, gate_all[:, cols], NEG)
                rank = jnp.zeros((nb, ATT_TILE), jnp.int32)
                for c in range(i):
                    gc = gate[c:c + 1, :]
                    tie = jnp.where(blk_id > c, 1, 0)
                    rank = rank + jnp.where(gc > gate, 1, jnp.where(gc == gate, tie, 0))
                chosen = jnp.where(rank < MOBA_TOPK, jnp.where(past, 1, 0), 0)
                keep = jnp.maximum(chosen, jnp.where(blk_id == i, 1, 0))
                neg_ref[e, :, cols] = jnp.where(keep == 1, 0.0, NEG)

    def step(t, cur, carry):
        nxt = 1 - cur
        m_prev, alpha_prev, mtile_prev = carry
        ic = tab_ref[TAB_ACC, t]
        jc = tab_ref[TAB_K, t]
        ib = tab_ref[TAB_Q, t + 1]
        jb = tab_ref[TAB_K, t + 1]
        first = tab_ref[TAB_FIRST, t + 1] != 0
        ia = tab_ref[TAB_Q, t + 2]
        ja = tab_ref[TAB_K, t + 2]
        da = tab_ref[TAB_DELTA, t + 2]
        m_new, alpha_new, mtile_new = [], [], []
        for e in heads:
            pair = e // HEADS_PER_PAIR

            def stage_a():
                kt = k_ref[_tile_slice(ja), _pair_rows(pair)]
                sb = _dot(kt, qts_ref[ia, e]) + bias_ref[e, da]
                s_ref[nxt, e] = sb
                mtile_new.append(tuple(jnp.max(sb[:, lanes], axis=0, keepdims=True) for lanes in QUERY_HALVES))

            def stage_c():
                vt = vts_ref[jc, _pair_rows(e if kind == "moba" else pair, acc_rows), :]
                pv = _dot(vt, p_ref[cur, e])
                for hf, lanes in enumerate(QUERY_HALVES):
                    acc_ref[ic, e, :, lanes] = alpha_prev[e][hf] * acc_ref[ic, e, :, lanes] + pv[:, lanes]

            for stage in ((stage_a, stage_c) if kind == "moba" else (stage_c, stage_a)):
                stage()

            m_parts, alpha_parts = [], []
            if kind == "moba":
                neg_row = neg_ref[e, pl.ds(jb, 1), _tile_slice(ib)]
            for hf, lanes in enumerate(QUERY_HALVES):
                s = s_ref[cur, e, :, lanes]
                m_in = jnp.where(first, -jnp.inf, m_prev[e][hf])
                m_tile = mtile_prev[e][hf]
                if kind == "moba":
                    neg = neg_row[:, lanes]
                    m_e = jnp.maximum(m_in, m_tile + neg)
                    shift = m_e - neg
                else:
                    m_e = jnp.maximum(m_in, m_tile)
                    shift = m_e
                p_ref[nxt, e, :, lanes] = jnp.exp2(s - shift).astype(BF16)
                alpha_parts.append(jnp.exp2(m_in - m_e))
                m_parts.append(m_e)
            alpha_new.append(tuple(alpha_parts))
            m_new.append(tuple(m_parts))

        return tuple(m_new), tuple(alpha_new), tuple(mtile_new)

    zeros = tuple(tuple(jnp.zeros((1, LANES), F32) for _ in QUERY_HALVES) for _ in heads)

    def body(u, carry):
        for r in range(PIPE_UNROLL):
            carry = step(PIPE_UNROLL * u + r, r % 2, carry)
        return carry

    lax.fori_loop(0, n_iters // PIPE_UNROLL, body, (zeros, zeros, zeros))

    def emit(i, carry):
        cols = _tile_slice(i)
        v_group = acc_ref.shape[2] - ONES_ROWS
        for pair in range(PAIRS_PER_STEP):
            outs = []
            for h in range(HEADS_PER_PAIR):
                e = pair * HEADS_PER_PAIR + h
                outs.append(acc_ref[i, e, :v_group, :] * (1.0 / acc_ref[i, e, v_group:v_group + 1, :]))
            if kind == "moba":
                o = jnp.concatenate(outs, axis=0)
            else:
                lam = lam_ref[...]
                lam_full = (jnp.exp(jnp.sum(lam[0:1] * lam[1:2], axis=-1, keepdims=True))
                            - jnp.exp(jnp.sum(lam[2:3] * lam[3:4], axis=-1, keepdims=True)) + lambda_init)
                o = outs[0] - lam_full * outs[1]
                o = (o * lax.rsqrt(jnp.mean(o * o, axis=0, keepdims=True) + EPS)
                     * sw_ref[...] * (1.0 - lambda_init))
            ot_ref[_pair_rows(pair), cols] = o.astype(BF16)
        return carry

    lax.fori_loop(0, seq // ATT_TILE, emit, 0)


def _attention(kind, qt, k, vt_ext, bias, extras, lambda_init, batch, seq):
    v_group = _value_group(kind)
    acc_rows = v_group + ONES_ROWS
    vt_block_rows = acc_rows * (LANES * PAIRS_PER_STEP // v_group)
    t = k.shape[0]
    nq = seq // ATT_TILE
    tab, n_iters = _tile_schedule(nq, own_first=(kind == "moba"))
    groups = ATTN_HEADS // HEADS_PER_STEP
    rows = LANES * PAIRS_PER_STEP
    in_specs = [
        pl.BlockSpec(memory_space=pltpu.SMEM),
        pl.BlockSpec((rows, seq), lambda g, b: (g, b)),
        pl.BlockSpec((seq, rows), lambda g, b: (b, g)),
        pl.BlockSpec((vt_block_rows, seq), lambda g, b: (g, b)),
        pl.BlockSpec((HEADS_PER_STEP, BIAS_TILES, ATT_TILE, ATT_TILE), lambda g, b: (g, 0, 0, 0)),
    ]
    scratch = [
        pltpu.VMEM((nq, HEADS_PER_STEP, LANES, ATT_TILE), BF16),
        pltpu.VMEM((nq, vt_block_rows, ATT_TILE), BF16),
        pltpu.VMEM((2, HEADS_PER_STEP, ATT_TILE, ATT_TILE), F32),
        pltpu.VMEM((2, HEADS_PER_STEP, ATT_TILE, ATT_TILE), BF16),
        pltpu.VMEM((nq + 1, HEADS_PER_STEP, acc_rows, ATT_TILE), F32),
    ]
    if kind == "moba":
        (blk,) = extras
        nb = blk.shape[0]
        in_specs.append(pl.BlockSpec((nb, seq), lambda g, b: (0, 0)))
        scratch.append(pltpu.VMEM((HEADS_PER_STEP, nb, seq), F32))
    else:
        lam, sw = extras
        in_specs += [
            pl.BlockSpec((4, HEAD_DIM), lambda g, b: (0, 0)),
            pl.BlockSpec((LANES, 1), lambda g, b: (0, 0)),
        ]
    return pl.pallas_call(
        functools.partial(_attn_kernel, kind, lambda_init, n_iters),
        grid=(groups, batch),
        in_specs=in_specs,
        out_specs=pl.BlockSpec((rows, seq), lambda g, b: (g, b)),
        out_shape=jax.ShapeDtypeStruct((D_MODEL, t), BF16),
        scratch_shapes=scratch,
        compiler_params=_params("parallel", "parallel"),
        name=kind + "_attn",
    )(tab, qt, k, vt_ext, bias, *extras)


def _rotary_tables(seq):
    d = RET_DK
    inv_freq = ROPE_BASE ** (-np.arange(0, d, 2, dtype=np.float64) / d)
    ang = np.arange(seq, dtype=np.float64)[:, None] * inv_freq[None, :]
    return jnp.asarray(np.cos(ang), F32), jnp.asarray(np.sin(ang), F32)


def _retention_decay_tables():
    c_len = RET_CHUNK
    log_gamma = np.log(1.0 - 2.0 ** (-5.0 - np.arange(RET_HEADS, dtype=np.float64)))
    pos = np.arange(c_len, dtype=np.float64)
    rel = pos[:, None] - pos[None, :]
    dm = np.where(rel >= 0, np.exp(np.maximum(rel, 0.0)[None] * log_gamma[:, None, None]), 0.0)
    rs = np.exp((pos + 1.0)[None, :] * log_gamma[:, None])[:, :, None]
    ks = np.exp((c_len - 1.0 - pos)[None, :] * log_gamma[:, None])[:, :, None]
    cd = np.exp(c_len * log_gamma)[:, None, None]
    return tuple(jnp.asarray(a, F32) for a in (dm, rs, ks, cd))


def _block_mean_matrix(seq):
    nb = seq // MOBA_BLOCK
    m = (np.arange(seq)[None, :] // MOBA_BLOCK == np.arange(nb)[:, None]) / float(MOBA_BLOCK)
    return jnp.asarray(m, BF16)


def _head_group_matrix():
    g = np.arange(COL_TILE)[:, None] // HEAD_DIM == np.arange(COL_TILE)[None, :] // HEAD_DIM
    return jnp.asarray(g / float(HEAD_DIM), BF16)


def kernel(x, rel_bias, norm1, norm2, w_up, w_down, ret_w_in, ret_w_out,
           moba_w_in, moba_q_norm, moba_k_norm, moba_w_out,
           diff_w_in, diff_q_norm, diff_k_norm, diff_lambda, diff_subln, diff_w_out):
    batch, seq, d = x.shape
    depth = norm1.shape[0]
    assert d == D_MODEL and seq % ROW_TILE == 0 and seq % ATT_TILE == 0 and seq % (RET_UNROLL * RET_CHUNK) == 0
    assert seq % MOBA_BLOCK == 0 and MOBA_BLOCK == ATT_TILE
    t = batch * seq
    xf = x.reshape(t, d)

    bias = _bias_tiles(rel_bias.astype(F32), _bucket_index_tiles())
    grp = _head_group_matrix()
    q_scale = HEAD_DIM ** -0.5 * LOG2E
    wu, wd = w_up.astype(BF16), w_down.astype(BF16)
    ret_wi, ret_wo = ret_w_in.astype(BF16), ret_w_out.astype(BF16)
    moba_wi, moba_wo = moba_w_in.astype(BF16), moba_w_out.astype(BF16)
    diff_wi, diff_wo = diff_w_in.astype(BF16), diff_w_out.astype(BF16)

    for i in range(depth):
        kind, j = i % N_MIXERS, i // N_MIXERS
        nw1 = norm1[i].reshape(1, d)
        nw2 = norm2[i].reshape(1, d)
        if kind == 0:
            cos, sin = _rotary_tables(seq)
            dm, rs, ks, cd = _retention_decay_tables()
            q, k, v, g = _ret_proj(xf, nw1, ret_wi, j, cos, sin, seq)
            o = _ret_core(q, k, v, dm, rs, ks, cd, batch, seq)
            xf = _out_mlp(o, g, ret_wo, j, xf, nw2, wu, wd, i)
        elif kind == 1:
            qn = (jnp.tile(moba_q_norm[j], ATTN_HEADS) * q_scale).reshape(1, d)
            kn = jnp.tile(moba_k_norm[j], ATTN_HEADS).reshape(1, d)
            qt, k, vt = _attn_proj(xf, nw1, moba_wi, j, qn, kn, grp, _value_group("moba"))
            ot = _attention("moba", qt, k, vt, bias, (_block_mean_matrix(seq),), 0.0, batch, seq)
            xf = _out_mlp(ot, None, moba_wo, j, xf, nw2, wu, wd, i)
        else:
            lambda_init = 0.8 - 0.6 * math.exp(-0.3 * i)
            qn = (jnp.tile(diff_q_norm[j], ATTN_HEADS) * q_scale).reshape(1, d)
            kn = jnp.tile(diff_k_norm[j], ATTN_HEADS).reshape(1, d)
            qt, k, vt = _attn_proj(xf, nw1, diff_wi, j, qn, kn, grp, _value_group("diff"))
            extras = (diff_lambda[j].astype(F32), diff_subln[j].reshape(LANES, 1))
            ot = _attention("diff", qt, k, vt, bias, extras, lambda_init, batch, seq)
            xf = _out_mlp(ot, None, diff_wo, j, xf, nw2, wu, wd, i)
    return xf.reshape(batch, seq, d)
```

```python
import functools
import math

import numpy as np
import jax
import jax.numpy as jnp
from jax import lax
from jax.experimental import pallas as pl
from jax.experimental.pallas import tpu as pltpu

F32 = jnp.float32
BF16 = jnp.bfloat16

D_MODEL = 1024
N_MIXERS = 3
RET_HEADS = 4
RET_DK = D_MODEL // RET_HEADS
RET_DV = 2 * RET_DK
ROPE_BASE = 10000.0
ATTN_HEADS = 16
HEAD_DIM = D_MODEL // ATTN_HEADS
MOBA_BLOCK = 256
MOBA_TOPK = 3
REL_BUCKETS = 32
REL_MAX_EXACT = REL_BUCKETS // 2
REL_MAX_DISTANCE = 1024
D_FF = 4 * D_MODEL
EPS = 1e-6
NEG = -1e30
LOG2E = math.log2(math.e)

LANES = 128
BF16_SUBLANES = 16
VMEM_LIMIT_BYTES = 56 * 1024 * 1024

ROW_TILE = 512
COL_TILE = 256
FF_TILE = 1024
GATE_TILE = 512
RET_CHUNK = 256
RET_UNROLL = 4
ATT_TILE = 256
BIAS_TILES = 6
HEADS_PER_PAIR = LANES // HEAD_DIM
PAIRS_PER_STEP = 2
HEADS_PER_STEP = HEADS_PER_PAIR * PAIRS_PER_STEP
ONES_ROWS = BF16_SUBLANES


def _value_group(kind):
    return HEAD_DIM if kind == "moba" else LANES


def _params(*sem):
    return pltpu.CompilerParams(dimension_semantics=sem, vmem_limit_bytes=VMEM_LIMIT_BYTES)


def _rms(xf, w):
    ms = jnp.mean(xf * xf, axis=-1, keepdims=True)
    return xf * lax.rsqrt(ms + EPS) * w


def _dot(a, b):
    return jnp.dot(a, b, preferred_element_type=F32)


def _dot_nt(a, b):
    return lax.dot_general(a, b, (((1,), (1,)), ((), ())), preferred_element_type=F32)


def _dot_tn(a, b):
    return lax.dot_general(a, b, (((0,), (0,)), ((), ())), preferred_element_type=F32)


def _ret_proj_kernel(x_ref, nw_ref, w_ref, cos_ref, sin_ref, q_ref, k_ref, v_ref, g_ref):
    h = _rms(x_ref[...], nw_ref[...]).astype(BF16)
    cos = cos_ref[...]
    sin = sin_ref[...]
    half = RET_DK // 2
    chunks = []
    for out_ref, scale in ((q_ref, 1.0), (k_ref, RET_DK ** -0.5)):
        chunks += [(out_ref, hd * RET_DK, scale) for hd in range(RET_HEADS)]
    for out_ref in (v_ref, g_ref):
        chunks += [(out_ref, c * COL_TILE, None) for c in range(RET_HEADS * RET_DV // COL_TILE)]
    assert RET_DK == COL_TILE

    def project(n):
        return _dot(h, w_ref[:, n * COL_TILE:(n + 1) * COL_TILE])

    acc_next = project(0)
    for n, (out_ref, col0, scale) in enumerate(chunks):
        acc = acc_next
        if n + 1 < len(chunks):
            acc_next = project(n + 1)
        if scale is None:
            out_ref[:, col0:col0 + COL_TILE] = acc.astype(BF16)
        else:
            x1 = acc[:, :half]
            x2 = acc[:, half:]
            r1 = x1 * cos - x2 * sin
            r2 = x1 * sin + x2 * cos
            if scale != 1.0:
                r1 = r1 * scale
                r2 = r2 * scale
            out_ref[:, col0:col0 + half] = r1.astype(BF16)
            out_ref[:, col0 + half:col0 + RET_DK] = r2.astype(BF16)


def _ret_proj(x, nw, w, layer, cos, sin, seq):
    t = x.shape[0]
    n_in = w.shape[2]
    tiles_per_seq = seq // ROW_TILE
    row = lambda i: (i, 0)
    const = lambda i: (0, 0)
    pos = lambda i: (i % tiles_per_seq, 0)
    n_qk = RET_HEADS * RET_DK
    n_v = RET_HEADS * RET_DV
    return pl.pallas_call(
        _ret_proj_kernel,
        grid=(t // ROW_TILE,),
        in_specs=[
            pl.BlockSpec((ROW_TILE, D_MODEL), row),
            pl.BlockSpec((1, D_MODEL), const),
            pl.BlockSpec((None, D_MODEL, n_in), lambda i: (layer, 0, 0)),
            pl.BlockSpec((ROW_TILE, RET_DK // 2), pos),
            pl.BlockSpec((ROW_TILE, RET_DK // 2), pos),
        ],
        out_specs=[
            pl.BlockSpec((ROW_TILE, n_qk), row),
            pl.BlockSpec((ROW_TILE, n_qk), row),
            pl.BlockSpec((ROW_TILE, n_v), row),
            pl.BlockSpec((ROW_TILE, n_v), row),
        ],
        out_shape=[
            jax.ShapeDtypeStruct((t, n_qk), BF16),
            jax.ShapeDtypeStruct((t, n_qk), BF16),
            jax.ShapeDtypeStruct((t, n_v), BF16),
            jax.ShapeDtypeStruct((t, n_v), BF16),
        ],
        compiler_params=_params("parallel"),
        name="ret_proj",
    )(x, nw, w, cos, sin)


def _attn_proj_kernel(v_group, x_ref, nw_ref, w_ref, qn_ref, kn_ref, grp_ref, qt_ref, k_ref, vt_ref):
    h = _rms(x_ref[...], nw_ref[...]).astype(BF16)
    grp = grp_ref[...]
    groups_per_chunk = COL_TILE // v_group
    chunks_per_part = D_MODEL // COL_TILE
    n_chunks = 3 * chunks_per_part

    def project(n):
        return _dot(h, w_ref[:, n * COL_TILE:(n + 1) * COL_TILE])

    acc_next = project(0)
    for n in range(n_chunks):
        acc = acc_next
        if n + 1 < n_chunks:
            acc_next = project(n + 1)
        part, c = divmod(n, chunks_per_part)
        cols = slice(c * COL_TILE, (c + 1) * COL_TILE)
        if part < 2:
            hw_ref = qn_ref if part == 0 else kn_ref
            ms = _dot((acc * acc).astype(BF16), grp)
            acc = acc * lax.rsqrt(ms + EPS) * hw_ref[:, cols]
        if part == 0:
            qt_ref[cols, :] = acc.T.astype(BF16)
        elif part == 1:
            k_ref[:, cols] = acc.astype(BF16)
        else:
            acc_t = acc.T.astype(BF16)
            for p in range(groups_per_chunk):
                row0 = (c * groups_per_chunk + p) * (v_group + ONES_ROWS)
                vt_ref[row0:row0 + v_group, :] = acc_t[p * v_group:(p + 1) * v_group, :]
                vt_ref[row0 + v_group:row0 + v_group + ONES_ROWS, :] = jnp.ones((ONES_ROWS, acc_t.shape[1]), BF16)


def _attn_proj(x, nw, w, layer, qn, kn, grp, v_group):
    t = x.shape[0]
    row = lambda i: (i, 0)
    col = lambda i: (0, i)
    const = lambda i: (0, 0)
    vt_rows = (D_MODEL // v_group) * (v_group + ONES_ROWS)
    return pl.pallas_call(
        functools.partial(_attn_proj_kernel, v_group),
        grid=(t // ROW_TILE,),
        in_specs=[
            pl.BlockSpec((ROW_TILE, D_MODEL), row),
            pl.BlockSpec((1, D_MODEL), const),
            pl.BlockSpec((None, D_MODEL, 3 * D_MODEL), lambda i: (layer, 0, 0)),
            pl.BlockSpec((1, D_MODEL), const),
            pl.BlockSpec((1, D_MODEL), const),
            pl.BlockSpec((COL_TILE, COL_TILE), const),
        ],
        out_specs=[
            pl.BlockSpec((D_MODEL, ROW_TILE), col),
            pl.BlockSpec((ROW_TILE, D_MODEL), row),
            pl.BlockSpec((vt_rows, ROW_TILE), col),
        ],
        out_shape=[
            jax.ShapeDtypeStruct((D_MODEL, t), BF16),
            jax.ShapeDtypeStruct((t, D_MODEL), BF16),
            jax.ShapeDtypeStruct((vt_rows, t), BF16),
        ],
        compiler_params=_params("parallel"),
        name="attn_proj",
    )(x, nw, w, qn, kn, grp)


def _ret_core_kernel(q_ref, k_ref, v_ref, dm_ref, rs_ref, ks_ref, cd_ref, o_ref, state_ref, raw_ref):
    c_len = RET_CHUNK
    n_chunks = q_ref.shape[0] // c_len
    state_ref[...] = jnp.zeros_like(state_ref)
    raw_ref[...] = jnp.zeros_like(raw_ref)

    def chunk(c):
        return pl.ds(pl.multiple_of(c * c_len, c_len), c_len)

    def masked_scores(c):
        return (_dot_nt(q_ref[chunk(c), :], k_ref[chunk(c), :]) * dm_ref[0]).astype(BF16)

    def normalise(c, slot):
        o = raw_ref[slot]
        o_ref[chunk(c), :] = (o * lax.rsqrt(jnp.mean(o * o, axis=-1, keepdims=True) + EPS)).astype(BF16)

    def step(c, slot, s_cur):
        sl = chunk(c)
        q = q_ref[sl, :]
        k = k_ref[sl, :]
        v = v_ref[sl, :]
        state = state_ref[...]
        kd = (k.astype(F32) * ks_ref[0]).astype(BF16)
        kv = _dot_tn(kd, v)
        inner = _dot(s_cur, v)
        cross = _dot(q, state.astype(BF16))
        s_next = masked_scores(jnp.minimum(c + 1, n_chunks - 1))
        normalise(jnp.maximum(c - 1, 0), 1 - slot)
        raw_ref[slot] = inner + cross * rs_ref[0]
        state_ref[...] = state * cd_ref[0] + kv
        return s_next

    def body(u, s_cur):
        for r in range(RET_UNROLL):
            s_cur = step(RET_UNROLL * u + r, r % 2, s_cur)
        return s_cur

    lax.fori_loop(0, n_chunks // RET_UNROLL, body, masked_scores(0))
    normalise(n_chunks - 1, 1)


def _ret_core(q, k, v, dm, rs, ks, cd, batch, seq):
    t = q.shape[0]
    c_len = RET_CHUNK
    tok = lambda b, h: (b, h)
    head3 = lambda b, h: (h, 0, 0)
    return pl.pallas_call(
        _ret_core_kernel,
        grid=(batch, RET_HEADS),
        in_specs=[
            pl.BlockSpec((seq, RET_DK), tok),
            pl.BlockSpec((seq, RET_DK), tok),
            pl.BlockSpec((seq, RET_DV), tok),
            pl.BlockSpec((1, c_len, c_len), head3),
            pl.BlockSpec((1, c_len, 1), head3),
            pl.BlockSpec((1, c_len, 1), head3),
            pl.BlockSpec((1, 1, 1), head3),
        ],
        out_specs=pl.BlockSpec((seq, RET_DV), tok),
        out_shape=jax.ShapeDtypeStruct((t, RET_HEADS * RET_DV), BF16),
        scratch_shapes=[pltpu.VMEM((RET_DK, RET_DV), F32), pltpu.VMEM((2, RET_CHUNK, RET_DV), F32)],
        compiler_params=_params("parallel", "parallel"),
        name="ret_core",
    )(q, k, v, dm, rs, ks, cd)


def _out_mlp_kernel(gated, n_cast, *refs):
    n_in = len(refs) - 1 - 2 * n_cast
    cast_in = refs[n_in:n_in + n_cast]
    cast_out = refs[n_in + n_cast + 1:]
    refs = refs[:n_in] + (refs[n_in + n_cast],)
    for src, dst in zip(cast_in, cast_out):
        dst[...] = src[...].astype(BF16)
    if gated:
        a_ref, g_ref, wo_ref, x_ref, nw_ref, wu_ref, wd_ref, o_ref = refs
        n_chunks = a_ref.shape[1] // GATE_TILE

        def gate(c):
            cols = slice(c * GATE_TILE, (c + 1) * GATE_TILE)
            g = g_ref[:, cols].astype(F32)
            return (g * (1.0 / (1.0 + jnp.exp(-g))) * a_ref[:, cols].astype(F32)).astype(BF16)

        a_next = gate(0)
        mix = None
        for c in range(n_chunks):
            a = a_next
            if c + 1 < n_chunks:
                a_next = gate(c + 1)
            part = _dot(a, wo_ref[c * GATE_TILE:(c + 1) * GATE_TILE, :])
            mix = part if mix is None else mix + part
    else:
        a_ref, wo_ref, x_ref, nw_ref, wu_ref, wd_ref, o_ref = refs
        mix = _dot_tn(a_ref[...], wo_ref[...])
    x = x_ref[...] + mix
    h = _rms(x, nw_ref[...]).astype(BF16)
    acc = x
    for c in range(D_FF // FF_TILE):
        u = _dot(h, wu_ref[:, c * FF_TILE:(c + 1) * FF_TILE])
        u = jnp.maximum(u, 0.0)
        acc = acc + _dot((u * u).astype(BF16), wd_ref[c * FF_TILE:(c + 1) * FF_TILE, :])
    o_ref[...] = acc


def _out_mlp(a, g, wo, mixer_layer, x, nw, wu, wd, layer, cast_jobs=()):
    t = x.shape[0]
    steps = t // ROW_TILE
    cast_in_specs, cast_out_specs, cast_shapes = [], [], []
    for w, w_layer in cast_jobs:
        _, rows, cols = w.shape
        slab = rows // steps
        assert slab * steps == rows and slab % BF16_SUBLANES == 0
        cast_in_specs.append(pl.BlockSpec((None, slab, cols), lambda i, w_layer=w_layer: (w_layer, i, 0)))
        cast_out_specs.append(pl.BlockSpec((None, slab, cols), lambda i: (0, i, 0)))
        cast_shapes.append(jax.ShapeDtypeStruct((1, rows, cols), BF16))
    ka = wo.shape[1]
    row = lambda i: (i, 0)
    const = lambda i: (0, 0)
    gated = g is not None
    if gated:
        mix_specs = [pl.BlockSpec((ROW_TILE, ka), row), pl.BlockSpec((ROW_TILE, ka), row)]
        mix_args = [a, g]
    else:
        mix_specs = [pl.BlockSpec((ka, ROW_TILE), lambda i: (0, i))]
        mix_args = [a]
    outs = pl.pallas_call(
        functools.partial(_out_mlp_kernel, gated, len(cast_jobs)),
        grid=(steps,),
        in_specs=mix_specs + [
            pl.BlockSpec((None, ka, D_MODEL), lambda i: (mixer_layer, 0, 0)),
            pl.BlockSpec((ROW_TILE, D_MODEL), row),
            pl.BlockSpec((1, D_MODEL), const),
            pl.BlockSpec((None, D_MODEL, D_FF), lambda i: (layer, 0, 0), pipeline_mode=pl.Buffered(1)),
            pl.BlockSpec((None, D_FF, D_MODEL), lambda i: (layer, 0, 0), pipeline_mode=pl.Buffered(1)),
        ] + cast_in_specs,
        out_specs=[pl.BlockSpec((ROW_TILE, D_MODEL), row)] + cast_out_specs,
        out_shape=[jax.ShapeDtypeStruct((t, D_MODEL), F32)] + cast_shapes,
        compiler_params=_params("parallel"),
        name="out_mlp_gated" if gated else "out_mlp",
    )(*mix_args, wo, x, nw, wu, wd, *[w for w, _ in cast_jobs])
    return outs[0], outs[1:]


def _bias_tiles_kernel(bucket_ranges, rb_ref, idx_ref, o_ref):
    head = pl.program_id(0)
    for d, (lo, hi) in enumerate(bucket_ranges):
        idx = idx_ref[d]
        acc = jnp.full(idx.shape, NEG, F32)
        for b in range(lo, hi + 1):
            acc = jnp.where(idx == b, rb_ref[b, head] * LOG2E, acc)
        o_ref[0, d] = acc


def _bucket_ranges():
    ranges = []
    for d in range(BIAS_TILES):
        lo_dist = max(d * ATT_TILE - (ATT_TILE - 1), 0)
        hi_dist = d * ATT_TILE + (ATT_TILE - 1)

        def bucket(n):
            if n < REL_MAX_EXACT:
                return n
            return min(REL_MAX_EXACT + int(math.log(n / REL_MAX_EXACT) / math.log(REL_MAX_DISTANCE / REL_MAX_EXACT)
                                           * (REL_BUCKETS - REL_MAX_EXACT)), REL_BUCKETS - 1)

        ranges.append((max(bucket(lo_dist) - 1, 0), min(bucket(hi_dist) + 1, REL_BUCKETS - 1)))
    return tuple(ranges)


def _bias_tiles(rel_bias, bucket_idx):
    nd, tq, tk = bucket_idx.shape
    return pl.pallas_call(
        functools.partial(_bias_tiles_kernel, _bucket_ranges()),
        grid=(ATTN_HEADS,),
        in_specs=[
            pl.BlockSpec(memory_space=pltpu.SMEM),
            pl.BlockSpec((nd, tq, tk), lambda h: (0, 0, 0)),
        ],
        out_specs=pl.BlockSpec((1, nd, tq, tk), lambda h: (h, 0, 0, 0)),
        out_shape=jax.ShapeDtypeStruct((ATTN_HEADS, nd, tq, tk), F32),
        compiler_params=_params("parallel"),
        name="bias_tiles",
    )(rel_bias, bucket_idx)


def _rel_bucket(dist):
    n = jnp.maximum(dist, 0)
    nf = jnp.maximum(n, 1).astype(F32)
    large = REL_MAX_EXACT + (jnp.log(nf / REL_MAX_EXACT) / math.log(REL_MAX_DISTANCE / REL_MAX_EXACT)
                             * (REL_BUCKETS - REL_MAX_EXACT)).astype(jnp.int32)
    large = jnp.minimum(large, REL_BUCKETS - 1)
    return jnp.where(n < REL_MAX_EXACT, n, large)


def _bucket_index_tiles():
    r = np.arange(ATT_TILE)
    dist = (np.arange(BIAS_TILES)[:, None, None] * ATT_TILE + r[None, None, :] - r[None, :, None])
    dist = jnp.asarray(dist, jnp.int32)
    return jnp.where(dist >= 0, _rel_bucket(dist), REL_BUCKETS).astype(jnp.int32)


TAB_Q, TAB_K, TAB_DELTA, TAB_FIRST, TAB_ACC = range(5)
PIPE_LAG = 2
PIPE_UNROLL = 6
QUERY_HALVES = tuple(slice(h * LANES, (h + 1) * LANES) for h in range(ATT_TILE // LANES))


def _tile_schedule(nq, own_first):
    rows = []
    for i in range(nq):
        keys = ([i] + list(range(i))) if own_first else list(range(i + 1))
        for n, j in enumerate(keys):
            rows.append((i, j, min(i - j, BIAS_TILES - 1), int(n == 0), i))
    n_iters = -(-(len(rows) + PIPE_LAG) // PIPE_UNROLL) * PIPE_UNROLL
    idle = (0, 0, 0, 1, nq)
    cols = [idle] * PIPE_LAG + rows
    cols += [idle] * (n_iters + PIPE_LAG - len(cols))
    return jnp.asarray(np.array(cols, np.int32).T), n_iters


def _tile_slice(idx):
    return pl.ds(pl.multiple_of(idx * ATT_TILE, ATT_TILE), ATT_TILE)


def _head_row_mask(h):
    row = lax.broadcasted_iota(jnp.int32, (LANES, 1), 0)
    return (row >= h * HEAD_DIM) & (row < (h + 1) * HEAD_DIM)


def _pair_rows(pair, n=LANES):
    return slice(pair * n, (pair + 1) * n)


def _attn_kernel(kind, lambda_init, n_iters, tab_ref, qt_ref, k_ref, vt_ref, bias_ref, *rest):
    if kind == "moba":
        blk_ref, ot_ref, qts_ref, vts_ref, s_ref, p_ref, acc_ref, neg_ref = rest
    else:
        lam_ref, sw_ref, ot_ref, qts_ref, vts_ref, s_ref, p_ref, acc_ref = rest
    heads = range(HEADS_PER_STEP)
    seq = qt_ref.shape[1]
    acc_rows = acc_ref.shape[2]

    n_tiles = seq // ATT_TILE
    q_heads = []
    for e in heads:
        pair, h = divmod(e, HEADS_PER_PAIR)
        qf = qt_ref[_pair_rows(pair), :].astype(F32)
        q_heads.append(jnp.where(_head_row_mask(h), qf, 0.0).astype(BF16))
        for i in range(n_tiles):
            qts_ref[i, e] = q_heads[e][:, i * ATT_TILE:(i + 1) * ATT_TILE]
    for j in range(n_tiles):
        vts_ref[j] = vt_ref[:, j * ATT_TILE:(j + 1) * ATT_TILE]
    s_ref[...] = jnp.zeros_like(s_ref)
    p_ref[...] = jnp.zeros_like(p_ref)
    acc_ref[...] = jnp.zeros_like(acc_ref)

    if kind == "moba":
        nb = blk_ref.shape[0]
        kmean = _dot(blk_ref[...], k_ref[...])
        km_hi = kmean.astype(BF16)
        km_lo = (kmean - km_hi.astype(F32)).astype(BF16)
        blk_id = lax.broadcasted_iota(jnp.int32, (nb, ATT_TILE), 0)
        for e in heads:
            pair = e // HEADS_PER_PAIR
            q_e = q_heads[e]
            gate_all = _dot(km_hi[:, _pair_rows(pair)], q_e) + _dot(km_lo[:, _pair_rows(pair)], q_e)
            for i in range(n_tiles):
                cols = slice(i * ATT_TILE, (i + 1) * ATT_TILE)
                past = blk_id < i
                gate = jnp.where(past, gate_all[:, cols], NEG)
                rank = jnp.zeros((nb, ATT_TILE), jnp.int32)
                for c in range(i):
                    gc = gate[c:c + 1, :]
                    tie = jnp.where(blk_id > c, 1, 0)
                    rank = rank + jnp.where(gc > gate, 1, jnp.where(gc == gate, tie, 0))
                chosen = jnp.where(rank < MOBA_TOPK, jnp.where(past, 1, 0), 0)
                keep = jnp.maximum(chosen, jnp.where(blk_id == i, 1, 0))
                neg_ref[e, :, cols] = jnp.where(keep == 1, 0.0, NEG)

    def step(t, cur, carry):
        nxt = 1 - cur
        m_prev, alpha_prev, mtile_prev = carry
        ic = tab_ref[TAB_ACC, t]
        jc = tab_ref[TAB_K, t]
        ib = tab_ref[TAB_Q, t + 1]
        jb = tab_ref[TAB_K, t + 1]
        first = tab_ref[TAB_FIRST, t + 1] != 0
        ia = tab_ref[TAB_Q, t + 2]
        ja = tab_ref[TAB_K, t + 2]
        da = tab_ref[TAB_DELTA, t + 2]
        m_new, alpha_new, mtile_new = [], [], []
        for e in heads:
            pair = e // HEADS_PER_PAIR

            def stage_a():
                kt = k_ref[_tile_slice(ja), _pair_rows(pair)]
                sb = _dot(kt, qts_ref[ia, e]) + bias_ref[e, da]
                s_ref[nxt, e] = sb
                mtile_new.append(tuple(jnp.max(sb[:, lanes], axis=0, keepdims=True) for lanes in QUERY_HALVES))

            def stage_c():
                vt = vts_ref[jc, _pair_rows(e if kind == "moba" else pair, acc_rows), :]
                pv = _dot(vt, p_ref[cur, e])
                for hf, lanes in enumerate(QUERY_HALVES):
                    acc_ref[ic, e, :, lanes] = alpha_prev[e][hf] * acc_ref[ic, e, :, lanes] + pv[:, lanes]

            for stage in ((stage_a, stage_c) if kind == "moba" else (stage_c, stage_a)):
                stage()

            m_parts, alpha_parts = [], []
            if kind == "moba":
                neg_row = neg_ref[e, pl.ds(jb, 1), _tile_slice(ib)]
            for hf, lanes in enumerate(QUERY_HALVES):
                s = s_ref[cur, e, :, lanes]
                m_in = jnp.where(first, -jnp.inf, m_prev[e][hf])
                m_tile = mtile_prev[e][hf]
                if kind == "moba":
                    neg = neg_row[:, lanes]
                    m_e = jnp.maximum(m_in, m_tile + neg)
                    shift = m_e - neg
                else:
                    m_e = jnp.maximum(m_in, m_tile)
                    shift = m_e
                p_ref[nxt, e, :, lanes] = jnp.exp2(s - shift).astype(BF16)
                alpha_parts.append(jnp.exp2(m_in - m_e))
                m_parts.append(m_e)
            alpha_new.append(tuple(alpha_parts))
            m_new.append(tuple(m_parts))

        return tuple(m_new), tuple(alpha_new), tuple(mtile_new)

    zeros = tuple(tuple(jnp.zeros((1, LANES), F32) for _ in QUERY_HALVES) for _ in heads)

    def body(u, carry):
        for r in range(PIPE_UNROLL):
            carry = step(PIPE_UNROLL * u + r, r % 2, carry)
        return carry

    lax.fori_loop(0, n_iters // PIPE_UNROLL, body, (zeros, zeros, zeros))

    def emit(i, carry):
        cols = _tile_slice(i)
        v_group = acc_ref.shape[2] - ONES_ROWS
        for pair in range(PAIRS_PER_STEP):
            outs = []
            for h in range(HEADS_PER_PAIR):
                e = pair * HEADS_PER_PAIR + h
                outs.append(acc_ref[i, e, :v_group, :] * (1.0 / acc_ref[i, e, v_group:v_group + 1, :]))
            if kind == "moba":
                o = jnp.concatenate(outs, axis=0)
            else:
                lam = lam_ref[...]
                lam_full = (jnp.exp(jnp.sum(lam[0:1] * lam[1:2], axis=-1, keepdims=True))
                            - jnp.exp(jnp.sum(lam[2:3] * lam[3:4], axis=-1, keepdims=True)) + lambda_init)
                o = outs[0] - lam_full * outs[1]
                o = (o * lax.rsqrt(jnp.mean(o * o, axis=0, keepdims=True) + EPS)
                     * sw_ref[...] * (1.0 - lambda_init))
            ot_ref[_pair_rows(pair), cols] = o.astype(BF16)
        return carry

    lax.fori_loop(0, seq // ATT_TILE, emit, 0)


def _attention(kind, qt, k, vt_ext, bias, extras, lambda_init, batch, seq):
    v_group = _value_group(kind)
    acc_rows = v_group + ONES_ROWS
    vt_block_rows = acc_rows * (LANES * PAIRS_PER_STEP // v_group)
    t = k.shape[0]
    nq = seq // ATT_TILE
    tab, n_iters = _tile_schedule(nq, own_first=(kind == "moba"))
    groups = ATTN_HEADS // HEADS_PER_STEP
    rows = LANES * PAIRS_PER_STEP
    in_specs = [
        pl.BlockSpec(memory_space=pltpu.SMEM),
        pl.BlockSpec((rows, seq), lambda g, b: (g, b)),
        pl.BlockSpec((seq, rows), lambda g, b: (b, g)),
        pl.BlockSpec((vt_block_rows, seq), lambda g, b: (g, b)),
        pl.BlockSpec((HEADS_PER_STEP, BIAS_TILES, ATT_TILE, ATT_TILE), lambda g, b: (g, 0, 0, 0)),
    ]
    scratch = [
        pltpu.VMEM((nq, HEADS_PER_STEP, LANES, ATT_TILE), BF16),
        pltpu.VMEM((nq, vt_block_rows, ATT_TILE), BF16),
        pltpu.VMEM((2, HEADS_PER_STEP, ATT_TILE, ATT_TILE), F32),
        pltpu.VMEM((2, HEADS_PER_STEP, ATT_TILE, ATT_TILE), BF16),
        pltpu.VMEM((nq + 1, HEADS_PER_STEP, acc_rows, ATT_TILE), F32),
    ]
    if kind == "moba":
        (blk,) = extras
        nb = blk.shape[0]
        in_specs.append(pl.BlockSpec((nb, seq), lambda g, b: (0, 0)))
        scratch.append(pltpu.VMEM((HEADS_PER_STEP, nb, seq), F32))
    else:
        lam, sw = extras
        in_specs += [
            pl.BlockSpec((4, HEAD_DIM), lambda g, b: (0, 0)),
            pl.BlockSpec((LANES, 1), lambda g, b: (0, 0)),
        ]
    return pl.pallas_call(
        functools.partial(_attn_kernel, kind, lambda_init, n_iters),
        grid=(groups, batch),
        in_specs=in_specs,
        out_specs=pl.BlockSpec((rows, seq), lambda g, b: (g, b)),
        out_shape=jax.ShapeDtypeStruct((D_MODEL, t), BF16),
        scratch_shapes=scratch,
        compiler_params=_params("parallel", "parallel"),
        name=kind + "_attn",
    )(tab, qt, k, vt_ext, bias, *extras)


def _rotary_tables(seq):
    d = RET_DK
    inv_freq = ROPE_BASE ** (-np.arange(0, d, 2, dtype=np.float64) / d)
    ang = np.arange(seq, dtype=np.float64)[:, None] * inv_freq[None, :]
    return jnp.asarray(np.cos(ang), F32), jnp.asarray(np.sin(ang), F32)


def _retention_decay_tables():
    c_len = RET_CHUNK
    log_gamma = np.log(1.0 - 2.0 ** (-5.0 - np.arange(RET_HEADS, dtype=np.float64)))
    pos = np.arange(c_len, dtype=np.float64)
    rel = pos[:, None] - pos[None, :]
    dm = np.where(rel >= 0, np.exp(np.maximum(rel, 0.0)[None] * log_gamma[:, None, None]), 0.0)
    rs = np.exp((pos + 1.0)[None, :] * log_gamma[:, None])[:, :, None]
    ks = np.exp((c_len - 1.0 - pos)[None, :] * log_gamma[:, None])[:, :, None]
    cd = np.exp(c_len * log_gamma)[:, None, None]
    return tuple(jnp.asarray(a, F32) for a in (dm, rs, ks, cd))


def _block_mean_matrix(seq):
    nb = seq // MOBA_BLOCK
    m = (np.arange(seq)[None, :] // MOBA_BLOCK == np.arange(nb)[:, None]) / float(MOBA_BLOCK)
    return jnp.asarray(m, BF16)


def _head_group_matrix():
    g = np.arange(COL_TILE)[:, None] // HEAD_DIM == np.arange(COL_TILE)[None, :] // HEAD_DIM
    return jnp.asarray(g / float(HEAD_DIM), BF16)


def kernel(x, rel_bias, norm1, norm2, w_up, w_down, ret_w_in, ret_w_out,
           moba_w_in, moba_q_norm, moba_k_norm, moba_w_out,
           diff_w_in, diff_q_norm, diff_k_norm, diff_lambda, diff_subln, diff_w_out):
    batch, seq, d = x.shape
    depth = norm1.shape[0]
    assert d == D_MODEL and seq % ROW_TILE == 0 and seq % ATT_TILE == 0 and seq % (RET_UNROLL * RET_CHUNK) == 0
    assert seq % MOBA_BLOCK == 0 and MOBA_BLOCK == ATT_TILE
    t = batch * seq
    xf = x.reshape(t, d)

    bias = _bias_tiles(rel_bias.astype(F32), _bucket_index_tiles())
    grp = _head_group_matrix()
    q_scale = HEAD_DIM ** -0.5 * LOG2E
    mixer_weights = ((ret_w_in, ret_w_out), (moba_w_in, moba_w_out), (diff_w_in, diff_w_out))

    def layer_weights_f32(i):
        kind, j = i % N_MIXERS, i // N_MIXERS
        w_in, w_out = mixer_weights[kind]
        return [(w_in, j), (w_out, j), (w_up, i), (w_down, i)]

    w_in, w_out, wu, wd = [w[l:l + 1].astype(BF16) for w, l in layer_weights_f32(0)]

    for i in range(depth):
        kind, j = i % N_MIXERS, i // N_MIXERS
        nw1 = norm1[i].reshape(1, d)
        nw2 = norm2[i].reshape(1, d)
        cast_jobs = layer_weights_f32(i + 1) if i + 1 < depth else ()
        if kind == 0:
            cos, sin = _rotary_tables(seq)
            dm, rs, ks, cd = _retention_decay_tables()
            q, k, v, g = _ret_proj(xf, nw1, w_in, 0, cos, sin, seq)
            o = _ret_core(q, k, v, dm, rs, ks, cd, batch, seq)
            xf, nxt = _out_mlp(o, g, w_out, 0, xf, nw2, wu, wd, 0, cast_jobs)
        elif kind == 1:
            qn = (jnp.tile(moba_q_norm[j], ATTN_HEADS) * q_scale).reshape(1, d)
            kn = jnp.tile(moba_k_norm[j], ATTN_HEADS).reshape(1, d)
            qt, k, vt = _attn_proj(xf, nw1, w_in, 0, qn, kn, grp, _value_group("moba"))
            ot = _attention("moba", qt, k, vt, bias, (_block_mean_matrix(seq),), 0.0, batch, seq)
            xf, nxt = _out_mlp(ot, None, w_out, 0, xf, nw2, wu, wd, 0, cast_jobs)
        else:
            lambda_init = 0.8 - 0.6 * math.exp(-0.3 * i)
            qn = (jnp.tile(diff_q_norm[j], ATTN_HEADS) * q_scale).reshape(1, d)
            kn = jnp.tile(diff_k_norm[j], ATTN_HEADS).reshape(1, d)
            qt, k, vt = _attn_proj(xf, nw1, w_in, 0, qn, kn, grp, _value_group("diff"))
            extras = (diff_lambda[j].astype(F32), diff_subln[j].reshape(LANES, 1))
            ot = _attention("diff", qt, k, vt, bias, extras, lambda_init, batch, seq)
            xf, nxt = _out_mlp(ot, None, w_out, 0, xf, nw2, wu, wd, 0, cast_jobs)
        if nxt:
            w_in, w_out, wu, wd = nxt
    return xf.reshape(batch, seq, d)
```

```python
import functools
import math

import numpy as np
import jax
import jax.numpy as jnp
from jax import lax
from jax.experimental import pallas as pl
from jax.experimental.pallas import tpu as pltpu

F32 = jnp.float32
BF16 = jnp.bfloat16

D_MODEL = 1024
N_MIXERS = 3
RET_HEADS = 4
RET_DK = D_MODEL // RET_HEADS
RET_DV = 2 * RET_DK
ROPE_BASE = 10000.0
ATTN_HEADS = 16
HEAD_DIM = D_MODEL // ATTN_HEADS
MOBA_BLOCK = 256
MOBA_TOPK = 3
REL_BUCKETS = 32
REL_MAX_EXACT = REL_BUCKETS // 2
REL_MAX_DISTANCE = 1024
D_FF = 4 * D_MODEL
EPS = 1e-6
NEG = -1e30
LOG2E = math.log2(math.e)

LANES = 128
BF16_SUBLANES = 16
VMEM_LIMIT_BYTES = 56 * 1024 * 1024

ROW_TILE = 512
COL_TILE = 256
FF_TILE = 1024
GATE_TILE = 512
RET_CHUNK = 256
RET_UNROLL = 4
ATT_TILE = 256
BIAS_TILES = 6
HEADS_PER_PAIR = LANES // HEAD_DIM
PAIRS_PER_STEP = 2
HEADS_PER_STEP = HEADS_PER_PAIR * PAIRS_PER_STEP
ONES_ROWS = BF16_SUBLANES


def _value_group(kind):
    return HEAD_DIM if kind == "moba" else LANES


def _params(*sem):
    return pltpu.CompilerParams(dimension_semantics=sem, vmem_limit_bytes=VMEM_LIMIT_BYTES)


def _rms(xf, w):
    ms = jnp.mean(xf * xf, axis=-1, keepdims=True)
    return xf * lax.rsqrt(ms + EPS) * w


def _dot(a, b):
    return jnp.dot(a, b, preferred_element_type=F32)


def _dot_nt(a, b):
    return lax.dot_general(a, b, (((1,), (1,)), ((), ())), preferred_element_type=F32)


def _dot_tn(a, b):
    return lax.dot_general(a, b, (((0,), (0,)), ((), ())), preferred_element_type=F32)


def _cast_specs(cast_jobs, steps):
    in_specs, out_specs, shapes = [], [], []
    for w, w_layer in cast_jobs:
        _, rows, cols = w.shape
        slab = rows // steps
        assert slab * steps == rows and slab % BF16_SUBLANES == 0
        in_specs.append(pl.BlockSpec((None, slab, cols), lambda i, w_layer=w_layer: (w_layer, i, 0)))
        out_specs.append(pl.BlockSpec((None, slab, cols), lambda i: (0, i, 0)))
        shapes.append(jax.ShapeDtypeStruct((1, rows, cols), BF16))
    return in_specs, out_specs, shapes


def _split_cast_refs(refs, n_outputs, n_cast):
    n_in = len(refs) - n_outputs - 2 * n_cast
    sources = refs[n_in:n_in + n_cast]
    outputs = refs[n_in + n_cast:n_in + n_cast + n_outputs]
    return refs[:n_in] + outputs, sources, refs[n_in + n_cast + n_outputs:]


def _run_casts(sources, destinations):
    for src, dst in zip(sources, destinations):
        dst[...] = src[...].astype(BF16)


def _ret_proj_kernel(n_cast, *refs):
    refs, cast_in, cast_out = _split_cast_refs(refs, 4, n_cast)
    _run_casts(cast_in, cast_out)
    x_ref, nw_ref, w_ref, cos_ref, sin_ref, q_ref, k_ref, v_ref, g_ref = refs
    h = _rms(x_ref[...], nw_ref[...]).astype(BF16)
    cos = cos_ref[...]
    sin = sin_ref[...]
    half = RET_DK // 2
    chunks = []
    for out_ref, scale in ((q_ref, 1.0), (k_ref, RET_DK ** -0.5)):
        chunks += [(out_ref, hd * RET_DK, scale) for hd in range(RET_HEADS)]
    for out_ref in (v_ref, g_ref):
        chunks += [(out_ref, c * COL_TILE, None) for c in range(RET_HEADS * RET_DV // COL_TILE)]
    assert RET_DK == COL_TILE

    def project(n):
        return _dot(h, w_ref[:, n * COL_TILE:(n + 1) * COL_TILE])

    acc_next = project(0)
    for n, (out_ref, col0, scale) in enumerate(chunks):
        acc = acc_next
        if n + 1 < len(chunks):
            acc_next = project(n + 1)
        if scale is None:
            out_ref[:, col0:col0 + COL_TILE] = acc.astype(BF16)
        else:
            x1 = acc[:, :half]
            x2 = acc[:, half:]
            r1 = x1 * cos - x2 * sin
            r2 = x1 * sin + x2 * cos
            if scale != 1.0:
                r1 = r1 * scale
                r2 = r2 * scale
            out_ref[:, col0:col0 + half] = r1.astype(BF16)
            out_ref[:, col0 + half:col0 + RET_DK] = r2.astype(BF16)


def _ret_proj(x, nw, w, layer, cos, sin, seq, cast_jobs=()):
    t = x.shape[0]
    cast_in_specs, cast_out_specs, cast_shapes = _cast_specs(cast_jobs, t // ROW_TILE)
    n_in = w.shape[2]
    tiles_per_seq = seq // ROW_TILE
    row = lambda i: (i, 0)
    const = lambda i: (0, 0)
    pos = lambda i: (i % tiles_per_seq, 0)
    n_qk = RET_HEADS * RET_DK
    n_v = RET_HEADS * RET_DV
    outs = pl.pallas_call(
        functools.partial(_ret_proj_kernel, len(cast_jobs)),
        grid=(t // ROW_TILE,),
        in_specs=[
            pl.BlockSpec((ROW_TILE, D_MODEL), row),
            pl.BlockSpec((1, D_MODEL), const),
            pl.BlockSpec((None, D_MODEL, n_in), lambda i: (layer, 0, 0)),
            pl.BlockSpec((ROW_TILE, RET_DK // 2), pos),
            pl.BlockSpec((ROW_TILE, RET_DK // 2), pos),
        ] + cast_in_specs,
        out_specs=[
            pl.BlockSpec((ROW_TILE, n_qk), row),
            pl.BlockSpec((ROW_TILE, n_qk), row),
            pl.BlockSpec((ROW_TILE, n_v), row),
            pl.BlockSpec((ROW_TILE, n_v), row),
        ] + cast_out_specs,
        out_shape=[
            jax.ShapeDtypeStruct((t, n_qk), BF16),
            jax.ShapeDtypeStruct((t, n_qk), BF16),
            jax.ShapeDtypeStruct((t, n_v), BF16),
            jax.ShapeDtypeStruct((t, n_v), BF16),
        ] + cast_shapes,
        compiler_params=_params("parallel"),
        name="ret_proj",
    )(x, nw, w, cos, sin, *[cw for cw, _ in cast_jobs])
    return outs[:4], outs[4:]


def _attn_proj_kernel(v_group, x_ref, nw_ref, w_ref, qn_ref, kn_ref, grp_ref, qt_ref, k_ref, vt_ref):
    h = _rms(x_ref[...], nw_ref[...]).astype(BF16)
    grp = grp_ref[...]
    groups_per_chunk = COL_TILE // v_group
    chunks_per_part = D_MODEL // COL_TILE
    n_chunks = 3 * chunks_per_part

    def project(n):
        return _dot(h, w_ref[:, n * COL_TILE:(n + 1) * COL_TILE])

    acc_next = project(0)
    for n in range(n_chunks):
        acc = acc_next
        if n + 1 < n_chunks:
            acc_next = project(n + 1)
        part, c = divmod(n, chunks_per_part)
        cols = slice(c * COL_TILE, (c + 1) * COL_TILE)
        if part < 2:
            hw_ref = qn_ref if part == 0 else kn_ref
            ms = _dot((acc * acc).astype(BF16), grp)
            acc = acc * lax.rsqrt(ms + EPS) * hw_ref[:, cols]
        if part == 0:
            qt_ref[cols, :] = acc.T.astype(BF16)
        elif part == 1:
            k_ref[:, cols] = acc.astype(BF16)
        else:
            acc_t = acc.T.astype(BF16)
            for p in range(groups_per_chunk):
                row0 = (c * groups_per_chunk + p) * (v_group + ONES_ROWS)
                vt_ref[row0:row0 + v_group, :] = acc_t[p * v_group:(p + 1) * v_group, :]
                vt_ref[row0 + v_group:row0 + v_group + ONES_ROWS, :] = jnp.ones((ONES_ROWS, acc_t.shape[1]), BF16)


def _attn_proj(x, nw, w, layer, qn, kn, grp, v_group):
    t = x.shape[0]
    row = lambda i: (i, 0)
    col = lambda i: (0, i)
    const = lambda i: (0, 0)
    vt_rows = (D_MODEL // v_group) * (v_group + ONES_ROWS)
    return pl.pallas_call(
        functools.partial(_attn_proj_kernel, v_group),
        grid=(t // ROW_TILE,),
        in_specs=[
            pl.BlockSpec((ROW_TILE, D_MODEL), row),
            pl.BlockSpec((1, D_MODEL), const),
            pl.BlockSpec((None, D_MODEL, 3 * D_MODEL), lambda i: (layer, 0, 0)),
            pl.BlockSpec((1, D_MODEL), const),
            pl.BlockSpec((1, D_MODEL), const),
            pl.BlockSpec((COL_TILE, COL_TILE), const),
        ],
        out_specs=[
            pl.BlockSpec((D_MODEL, ROW_TILE), col),
            pl.BlockSpec((ROW_TILE, D_MODEL), row),
            pl.BlockSpec((vt_rows, ROW_TILE), col),
        ],
        out_shape=[
            jax.ShapeDtypeStruct((D_MODEL, t), BF16),
            jax.ShapeDtypeStruct((t, D_MODEL), BF16),
            jax.ShapeDtypeStruct((vt_rows, t), BF16),
        ],
        compiler_params=_params("parallel"),
        name="attn_proj",
    )(x, nw, w, qn, kn, grp)


def _ret_core_kernel(q_ref, k_ref, v_ref, dm_ref, rs_ref, ks_ref, cd_ref, o_ref, state_ref, raw_ref):
    c_len = RET_CHUNK
    n_chunks = q_ref.shape[0] // c_len
    state_ref[...] = jnp.zeros_like(state_ref)
    raw_ref[...] = jnp.zeros_like(raw_ref)

    def chunk(c):
        return pl.ds(pl.multiple_of(c * c_len, c_len), c_len)

    def masked_scores(c):
        return (_dot_nt(q_ref[chunk(c), :], k_ref[chunk(c), :]) * dm_ref[0]).astype(BF16)

    def normalise(c, slot):
        o = raw_ref[slot]
        o_ref[chunk(c), :] = (o * lax.rsqrt(jnp.mean(o * o, axis=-1, keepdims=True) + EPS)).astype(BF16)

    def step(c, slot, s_cur):
        sl = chunk(c)
        q = q_ref[sl, :]
        k = k_ref[sl, :]
        v = v_ref[sl, :]
        state = state_ref[...]
        kd = (k.astype(F32) * ks_ref[0]).astype(BF16)
        kv = _dot_tn(kd, v)
        inner = _dot(s_cur, v)
        cross = _dot(q, state.astype(BF16))
        s_next = masked_scores(jnp.minimum(c + 1, n_chunks - 1))
        normalise(jnp.maximum(c - 1, 0), 1 - slot)
        raw_ref[slot] = inner + cross * rs_ref[0]
        state_ref[...] = state * cd_ref[0] + kv
        return s_next

    def body(u, s_cur):
        for r in range(RET_UNROLL):
            s_cur = step(RET_UNROLL * u + r, r % 2, s_cur)
        return s_cur

    lax.fori_loop(0, n_chunks // RET_UNROLL, body, masked_scores(0))
    normalise(n_chunks - 1, 1)


def _ret_core(q, k, v, dm, rs, ks, cd, batch, seq):
    t = q.shape[0]
    c_len = RET_CHUNK
    tok = lambda b, h: (b, h)
    head3 = lambda b, h: (h, 0, 0)
    return pl.pallas_call(
        _ret_core_kernel,
        grid=(batch, RET_HEADS),
        in_specs=[
            pl.BlockSpec((seq, RET_DK), tok),
            pl.BlockSpec((seq, RET_DK), tok),
            pl.BlockSpec((seq, RET_DV), tok),
            pl.BlockSpec((1, c_len, c_len), head3),
            pl.BlockSpec((1, c_len, 1), head3),
            pl.BlockSpec((1, c_len, 1), head3),
            pl.BlockSpec((1, 1, 1), head3),
        ],
        out_specs=pl.BlockSpec((seq, RET_DV), tok),
        out_shape=jax.ShapeDtypeStruct((t, RET_HEADS * RET_DV), BF16),
        scratch_shapes=[pltpu.VMEM((RET_DK, RET_DV), F32), pltpu.VMEM((2, RET_CHUNK, RET_DV), F32)],
        compiler_params=_params("parallel", "parallel"),
        name="ret_core",
    )(q, k, v, dm, rs, ks, cd)


def _out_mlp_kernel(gated, n_cast, *refs):
    refs, cast_in, cast_out = _split_cast_refs(refs, 1, n_cast)
    _run_casts(cast_in, cast_out)
    if gated:
        a_ref, g_ref, wo_ref, x_ref, nw_ref, wu_ref, wd_ref, o_ref = refs
        n_chunks = a_ref.shape[1] // GATE_TILE

        def gate(c):
            cols = slice(c * GATE_TILE, (c + 1) * GATE_TILE)
            g = g_ref[:, cols].astype(F32)
            return (g * (1.0 / (1.0 + jnp.exp(-g))) * a_ref[:, cols].astype(F32)).astype(BF16)

        a_next = gate(0)
        mix = None
        for c in range(n_chunks):
            a = a_next
            if c + 1 < n_chunks:
                a_next = gate(c + 1)
            part = _dot(a, wo_ref[c * GATE_TILE:(c + 1) * GATE_TILE, :])
            mix = part if mix is None else mix + part
    else:
        a_ref, wo_ref, x_ref, nw_ref, wu_ref, wd_ref, o_ref = refs
        mix = _dot_tn(a_ref[...], wo_ref[...])
    x = x_ref[...] + mix
    h = _rms(x, nw_ref[...]).astype(BF16)
    acc = x
    for c in range(D_FF // FF_TILE):
        u = _dot(h, wu_ref[:, c * FF_TILE:(c + 1) * FF_TILE])
        u = jnp.maximum(u, 0.0)
        acc = acc + _dot((u * u).astype(BF16), wd_ref[c * FF_TILE:(c + 1) * FF_TILE, :])
    o_ref[...] = acc


def _out_mlp(a, g, wo, mixer_layer, x, nw, wu, wd, layer, cast_jobs=()):
    t = x.shape[0]
    steps = t // ROW_TILE
    cast_in_specs, cast_out_specs, cast_shapes = _cast_specs(cast_jobs, steps)
    ka = wo.shape[1]
    row = lambda i: (i, 0)
    const = lambda i: (0, 0)
    gated = g is not None
    if gated:
        mix_specs = [pl.BlockSpec((ROW_TILE, ka), row), pl.BlockSpec((ROW_TILE, ka), row)]
        mix_args = [a, g]
    else:
        mix_specs = [pl.BlockSpec((ka, ROW_TILE), lambda i: (0, i))]
        mix_args = [a]
    outs = pl.pallas_call(
        functools.partial(_out_mlp_kernel, gated, len(cast_jobs)),
        grid=(steps,),
        in_specs=mix_specs + [
            pl.BlockSpec((None, ka, D_MODEL), lambda i: (mixer_layer, 0, 0)),
            pl.BlockSpec((ROW_TILE, D_MODEL), row),
            pl.BlockSpec((1, D_MODEL), const),
            pl.BlockSpec((None, D_MODEL, D_FF), lambda i: (layer, 0, 0), pipeline_mode=pl.Buffered(1)),
            pl.BlockSpec((None, D_FF, D_MODEL), lambda i: (layer, 0, 0), pipeline_mode=pl.Buffered(1)),
        ] + cast_in_specs,
        out_specs=[pl.BlockSpec((ROW_TILE, D_MODEL), row)] + cast_out_specs,
        out_shape=[jax.ShapeDtypeStruct((t, D_MODEL), F32)] + cast_shapes,
        compiler_params=_params("parallel"),
        name="out_mlp_gated" if gated else "out_mlp",
    )(*mix_args, wo, x, nw, wu, wd, *[w for w, _ in cast_jobs])
    return outs[0], outs[1:]


def _bias_tiles_kernel(bucket_ranges, rb_ref, idx_ref, o_ref):
    head = pl.program_id(0)
    for d, (lo, hi) in enumerate(bucket_ranges):
        idx = idx_ref[d]
        acc = jnp.full(idx.shape, NEG, F32)
        for b in range(lo, hi + 1):
            acc = jnp.where(idx == b, rb_ref[b, head] * LOG2E, acc)
        o_ref[0, d] = acc


def _bucket_ranges():
    ranges = []
    for d in range(BIAS_TILES):
        lo_dist = max(d * ATT_TILE - (ATT_TILE - 1), 0)
        hi_dist = d * ATT_TILE + (ATT_TILE - 1)

        def bucket(n):
            if n < REL_MAX_EXACT:
                return n
            return min(REL_MAX_EXACT + int(math.log(n / REL_MAX_EXACT) / math.log(REL_MAX_DISTANCE / REL_MAX_EXACT)
                                           * (REL_BUCKETS - REL_MAX_EXACT)), REL_BUCKETS - 1)

        ranges.append((max(bucket(lo_dist) - 1, 0), min(bucket(hi_dist) + 1, REL_BUCKETS - 1)))
    return tuple(ranges)


def _bias_tiles(rel_bias, bucket_idx):
    nd, tq, tk = bucket_idx.shape
    return pl.pallas_call(
        functools.partial(_bias_tiles_kernel, _bucket_ranges()),
        grid=(ATTN_HEADS,),
        in_specs=[
            pl.BlockSpec(memory_space=pltpu.SMEM),
            pl.BlockSpec((nd, tq, tk), lambda h: (0, 0, 0)),
        ],
        out_specs=pl.BlockSpec((1, nd, tq, tk), lambda h: (h, 0, 0, 0)),
        out_shape=jax.ShapeDtypeStruct((ATTN_HEADS, nd, tq, tk), F32),
        compiler_params=_params("parallel"),
        name="bias_tiles",
    )(rel_bias, bucket_idx)


def _rel_bucket(dist):
    n = jnp.maximum(dist, 0)
    nf = jnp.maximum(n, 1).astype(F32)
    large = REL_MAX_EXACT + (jnp.log(nf / REL_MAX_EXACT) / math.log(REL_MAX_DISTANCE / REL_MAX_EXACT)
                             * (REL_BUCKETS - REL_MAX_EXACT)).astype(jnp.int32)
    large = jnp.minimum(large, REL_BUCKETS - 1)
    return jnp.where(n < REL_MAX_EXACT, n, large)


def _bucket_index_tiles():
    r = np.arange(ATT_TILE)
    dist = (np.arange(BIAS_TILES)[:, None, None] * ATT_TILE + r[None, None, :] - r[None, :, None])
    dist = jnp.asarray(dist, jnp.int32)
    return jnp.where(dist >= 0, _rel_bucket(dist), REL_BUCKETS).astype(jnp.int32)


TAB_Q, TAB_K, TAB_DELTA, TAB_FIRST, TAB_ACC = range(5)
PIPE_LAG = 2
PIPE_UNROLL = 6
QUERY_HALVES = tuple(slice(h * LANES, (h + 1) * LANES) for h in range(ATT_TILE // LANES))


def _tile_schedule(nq, own_first):
    rows = []
    for i in range(nq):
        keys = ([i] + list(range(i))) if own_first else list(range(i + 1))
        for n, j in enumerate(keys):
            rows.append((i, j, min(i - j, BIAS_TILES - 1), int(n == 0), i))
    n_iters = -(-(len(rows) + PIPE_LAG) // PIPE_UNROLL) * PIPE_UNROLL
    idle = (0, 0, 0, 1, nq)
    cols = [idle] * PIPE_LAG + rows
    cols += [idle] * (n_iters + PIPE_LAG - len(cols))
    return jnp.asarray(np.array(cols, np.int32).T), n_iters


def _tile_slice(idx):
    return pl.ds(pl.multiple_of(idx * ATT_TILE, ATT_TILE), ATT_TILE)


def _head_row_mask(h):
    row = lax.broadcasted_iota(jnp.int32, (LANES, 1), 0)
    return (row >= h * HEAD_DIM) & (row < (h + 1) * HEAD_DIM)


def _pair_rows(pair, n=LANES):
    return slice(pair * n, (pair + 1) * n)


def _attn_kernel(kind, lambda_init, n_iters, tab_ref, qt_ref, k_ref, vt_ref, bias_ref, *rest):
    if kind == "moba":
        blk_ref, ot_ref, qts_ref, vts_ref, s_ref, p_ref, acc_ref, neg_ref = rest
    else:
        lam_ref, sw_ref, ot_ref, qts_ref, vts_ref, s_ref, p_ref, acc_ref = rest
    heads = range(HEADS_PER_STEP)
    seq = qt_ref.shape[1]
    acc_rows = acc_ref.shape[2]

    n_tiles = seq // ATT_TILE
    q_heads = []
    for e in heads:
        pair, h = divmod(e, HEADS_PER_PAIR)
        qf = qt_ref[_pair_rows(pair), :].astype(F32)
        q_heads.append(jnp.where(_head_row_mask(h), qf, 0.0).astype(BF16))
        for i in range(n_tiles):
            qts_ref[i, e] = q_heads[e][:, i * ATT_TILE:(i + 1) * ATT_TILE]
    for j in range(n_tiles):
        vts_ref[j] = vt_ref[:, j * ATT_TILE:(j + 1) * ATT_TILE]
    s_ref[...] = jnp.zeros_like(s_ref)
    p_ref[...] = jnp.zeros_like(p_ref)
    acc_ref[...] = jnp.zeros_like(acc_ref)

    if kind == "moba":
        nb = blk_ref.shape[0]
        kmean = _dot(blk_ref[...], k_ref[...])
        km_hi = kmean.astype(BF16)
        km_lo = (kmean - km_hi.astype(F32)).astype(BF16)
        blk_id = lax.broadcasted_iota(jnp.int32, (nb, ATT_TILE), 0)
        for e in heads:
            pair = e // HEADS_PER_PAIR
            q_e = q_heads[e]
            gate_all = _dot(km_hi[:, _pair_rows(pair)], q_e) + _dot(km_lo[:, _pair_rows(pair)], q_e)
            for i in range(n_tiles):
                cols = slice(i * ATT_TILE, (i + 1) * ATT_TILE)
                past = blk_id < i
                gate = jnp.where(past, gate_all[:, cols], NEG)
                rank = jnp.zeros((nb, ATT_TILE), jnp.int32)
                for c in range(i):
                    gc = gate[c:c + 1, :]
                    tie = jnp.where(blk_id > c, 1, 0)
                    rank = rank + jnp.where(gc > gate, 1, jnp.where(gc == gate, tie, 0))
                chosen = jnp.where(rank < MOBA_TOPK, jnp.where(past, 1, 0), 0)
                keep = jnp.maximum(chosen, jnp.where(blk_id == i, 1, 0))
                neg_ref[e, :, cols] = jnp.where(keep == 1, 0.0, NEG)

    def step(t, cur, carry):
        nxt = 1 - cur
        m_prev, alpha_prev, mtile_prev = carry
        ic = tab_ref[TAB_ACC, t]
        jc = tab_ref[TAB_K, t]
        ib = tab_ref[TAB_Q, t + 1]
        jb = tab_ref[TAB_K, t + 1]
        first = tab_ref[TAB_FIRST, t + 1] != 0
        ia = tab_ref[TAB_Q, t + 2]
        ja = tab_ref[TAB_K, t + 2]
        da = tab_ref[TAB_DELTA, t + 2]
        m_new, alpha_new, mtile_new = [], [], []
        for e in heads:
            pair = e // HEADS_PER_PAIR

            def stage_a():
                kt = k_ref[_tile_slice(ja), _pair_rows(pair)]
                sb = _dot(kt, qts_ref[ia, e]) + bias_ref[e, da]
                s_ref[nxt, e] = sb
                mtile_new.append(tuple(jnp.max(sb[:, lanes], axis=0, keepdims=True) for lanes in QUERY_HALVES))

            def stage_c():
                vt = vts_ref[jc, _pair_rows(e if kind == "moba" else pair, acc_rows), :]
                pv = _dot(vt, p_ref[cur, e])
                for hf, lanes in enumerate(QUERY_HALVES):
                    acc_ref[ic, e, :, lanes] = alpha_prev[e][hf] * acc_ref[ic, e, :, lanes] + pv[:, lanes]

            for stage in ((stage_a, stage_c) if kind == "moba" else (stage_c, stage_a)):
                stage()

            m_parts, alpha_parts = [], []
            if kind == "moba":
                neg_row = neg_ref[e, pl.ds(jb, 1), _tile_slice(ib)]
            for hf, lanes in enumerate(QUERY_HALVES):
                s = s_ref[cur, e, :, lanes]
                m_in = jnp.where(first, -jnp.inf, m_prev[e][hf])
                m_tile = mtile_prev[e][hf]
                if kind == "moba":
                    neg = neg_row[:, lanes]
                    m_e = jnp.maximum(m_in, m_tile + neg)
                    shift = m_e - neg
                else:
                    m_e = jnp.maximum(m_in, m_tile)
                    shift = m_e
                p_ref[nxt, e, :, lanes] = jnp.exp2(s - shift).astype(BF16)
                alpha_parts.append(jnp.exp2(m_in - m_e))
                m_parts.append(m_e)
            alpha_new.append(tuple(alpha_parts))
            m_new.append(tuple(m_parts))

        return tuple(m_new), tuple(alpha_new), tuple(mtile_new)

    zeros = tuple(tuple(jnp.zeros((1, LANES), F32) for _ in QUERY_HALVES) for _ in heads)

    def body(u, carry):
        for r in range(PIPE_UNROLL):
            carry = step(PIPE_UNROLL * u + r, r % 2, carry)
        return carry

    lax.fori_loop(0, n_iters // PIPE_UNROLL, body, (zeros, zeros, zeros))

    def emit(i, carry):
        cols = _tile_slice(i)
        v_group = acc_ref.shape[2] - ONES_ROWS
        for pair in range(PAIRS_PER_STEP):
            outs = []
            for h in range(HEADS_PER_PAIR):
                e = pair * HEADS_PER_PAIR + h
                outs.append(acc_ref[i, e, :v_group, :] * (1.0 / acc_ref[i, e, v_group:v_group + 1, :]))
            if kind == "moba":
                o = jnp.concatenate(outs, axis=0)
            else:
                lam = lam_ref[...]
                lam_full = (jnp.exp(jnp.sum(lam[0:1] * lam[1:2], axis=-1, keepdims=True))
                            - jnp.exp(jnp.sum(lam[2:3] * lam[3:4], axis=-1, keepdims=True)) + lambda_init)
                o = outs[0] - lam_full * outs[1]
                o = (o * lax.rsqrt(jnp.mean(o * o, axis=0, keepdims=True) + EPS)
                     * sw_ref[...] * (1.0 - lambda_init))
            ot_ref[_pair_rows(pair), cols] = o.astype(BF16)
        return carry

    lax.fori_loop(0, seq // ATT_TILE, emit, 0)


def _attention(kind, qt, k, vt_ext, bias, extras, lambda_init, batch, seq):
    v_group = _value_group(kind)
    acc_rows = v_group + ONES_ROWS
    vt_block_rows = acc_rows * (LANES * PAIRS_PER_STEP // v_group)
    t = k.shape[0]
    nq = seq // ATT_TILE
    tab, n_iters = _tile_schedule(nq, own_first=(kind == "moba"))
    groups = ATTN_HEADS // HEADS_PER_STEP
    rows = LANES * PAIRS_PER_STEP
    in_specs = [
        pl.BlockSpec(memory_space=pltpu.SMEM),
        pl.BlockSpec((rows, seq), lambda g, b: (g, b)),
        pl.BlockSpec((seq, rows), lambda g, b: (b, g)),
        pl.BlockSpec((vt_block_rows, seq), lambda g, b: (g, b)),
        pl.BlockSpec((HEADS_PER_STEP, BIAS_TILES, ATT_TILE, ATT_TILE), lambda g, b: (g, 0, 0, 0)),
    ]
    scratch = [
        pltpu.VMEM((nq, HEADS_PER_STEP, LANES, ATT_TILE), BF16),
        pltpu.VMEM((nq, vt_block_rows, ATT_TILE), BF16),
        pltpu.VMEM((2, HEADS_PER_STEP, ATT_TILE, ATT_TILE), F32),
        pltpu.VMEM((2, HEADS_PER_STEP, ATT_TILE, ATT_TILE), BF16),
        pltpu.VMEM((nq + 1, HEADS_PER_STEP, acc_rows, ATT_TILE), F32),
    ]
    if kind == "moba":
        (blk,) = extras
        nb = blk.shape[0]
        in_specs.append(pl.BlockSpec((nb, seq), lambda g, b: (0, 0)))
        scratch.append(pltpu.VMEM((HEADS_PER_STEP, nb, seq), F32))
    else:
        lam, sw = extras
        in_specs += [
            pl.BlockSpec((4, HEAD_DIM), lambda g, b: (0, 0)),
            pl.BlockSpec((LANES, 1), lambda g, b: (0, 0)),
        ]
    return pl.pallas_call(
        functools.partial(_attn_kernel, kind, lambda_init, n_iters),
        grid=(groups, batch),
        in_specs=in_specs,
        out_specs=pl.BlockSpec((rows, seq), lambda g, b: (g, b)),
        out_shape=jax.ShapeDtypeStruct((D_MODEL, t), BF16),
        scratch_shapes=scratch,
        compiler_params=_params("parallel", "parallel"),
        name=kind + "_attn",
    )(tab, qt, k, vt_ext, bias, *extras)


def _rotary_tables(seq):
    d = RET_DK
    inv_freq = ROPE_BASE ** (-np.arange(0, d, 2, dtype=np.float64) / d)
    ang = np.arange(seq, dtype=np.float64)[:, None] * inv_freq[None, :]
    return jnp.asarray(np.cos(ang), F32), jnp.asarray(np.sin(ang), F32)


def _retention_decay_tables():
    c_len = RET_CHUNK
    log_gamma = np.log(1.0 - 2.0 ** (-5.0 - np.arange(RET_HEADS, dtype=np.float64)))
    pos = np.arange(c_len, dtype=np.float64)
    rel = pos[:, None] - pos[None, :]
    dm = np.where(rel >= 0, np.exp(np.maximum(rel, 0.0)[None] * log_gamma[:, None, None]), 0.0)
    rs = np.exp((pos + 1.0)[None, :] * log_gamma[:, None])[:, :, None]
    ks = np.exp((c_len - 1.0 - pos)[None, :] * log_gamma[:, None])[:, :, None]
    cd = np.exp(c_len * log_gamma)[:, None, None]
    return tuple(jnp.asarray(a, F32) for a in (dm, rs, ks, cd))


def _block_mean_matrix(seq):
    nb = seq // MOBA_BLOCK
    m = (np.arange(seq)[None, :] // MOBA_BLOCK == np.arange(nb)[:, None]) / float(MOBA_BLOCK)
    return jnp.asarray(m, BF16)


def _head_group_matrix():
    g = np.arange(COL_TILE)[:, None] // HEAD_DIM == np.arange(COL_TILE)[None, :] // HEAD_DIM
    return jnp.asarray(g / float(HEAD_DIM), BF16)


def kernel(x, rel_bias, norm1, norm2, w_up, w_down, ret_w_in, ret_w_out,
           moba_w_in, moba_q_norm, moba_k_norm, moba_w_out,
           diff_w_in, diff_q_norm, diff_k_norm, diff_lambda, diff_subln, diff_w_out):
    batch, seq, d = x.shape
    depth = norm1.shape[0]
    assert d == D_MODEL and seq % ROW_TILE == 0 and seq % ATT_TILE == 0 and seq % (RET_UNROLL * RET_CHUNK) == 0
    assert seq % MOBA_BLOCK == 0 and MOBA_BLOCK == ATT_TILE
    t = batch * seq
    xf = x.reshape(t, d)

    bias = _bias_tiles(rel_bias.astype(F32), _bucket_index_tiles())
    grp = _head_group_matrix()
    q_scale = HEAD_DIM ** -0.5 * LOG2E
    mixer_weights = ((ret_w_in, ret_w_out), (moba_w_in, moba_w_out), (diff_w_in, diff_w_out))

    def layer_weights_f32(i):
        kind, j = i % N_MIXERS, i // N_MIXERS
        w_in, w_out = mixer_weights[kind]
        return [(w_in, j), (w_out, j), (w_up, i), (w_down, i)]

    first_jobs = layer_weights_f32(0)
    w_in = first_jobs[0][0][0:1].astype(BF16)
    w_out = wu = wd = None

    for i in range(depth):
        kind, j = i % N_MIXERS, i // N_MIXERS
        nw1 = norm1[i].reshape(1, d)
        nw2 = norm2[i].reshape(1, d)
        cast_jobs = layer_weights_f32(i + 1) if i + 1 < depth else ()
        if kind == 0:
            cos, sin = _rotary_tables(seq)
            dm, rs, ks, cd = _retention_decay_tables()
            (q, k, v, g), early = _ret_proj(xf, nw1, w_in, 0, cos, sin, seq, first_jobs[1:] if i == 0 else ())
            if early:
                w_out, wu, wd = early
            o = _ret_core(q, k, v, dm, rs, ks, cd, batch, seq)
            xf, nxt = _out_mlp(o, g, w_out, 0, xf, nw2, wu, wd, 0, cast_jobs)
        elif kind == 1:
            qn = (jnp.tile(moba_q_norm[j], ATTN_HEADS) * q_scale).reshape(1, d)
            kn = jnp.tile(moba_k_norm[j], ATTN_HEADS).reshape(1, d)
            qt, k, vt = _attn_proj(xf, nw1, w_in, 0, qn, kn, grp, _value_group("moba"))
            ot = _attention("moba", qt, k, vt, bias, (_block_mean_matrix(seq),), 0.0, batch, seq)
            xf, nxt = _out_mlp(ot, None, w_out, 0, xf, nw2, wu, wd, 0, cast_jobs)
        else:
            lambda_init = 0.8 - 0.6 * math.exp(-0.3 * i)
            qn = (jnp.tile(diff_q_norm[j], ATTN_HEADS) * q_scale).reshape(1, d)
            kn = jnp.tile(diff_k_norm[j], ATTN_HEADS).reshape(1, d)
            qt, k, vt = _attn_proj(xf, nw1, w_in, 0, qn, kn, grp, _value_group("diff"))
            extras = (diff_lambda[j].astype(F32), diff_subln[j].reshape(LANES, 1))
            ot = _attention("diff", qt, k, vt, bias, extras, lambda_init, batch, seq)
            xf, nxt = _out_mlp(ot, None, w_out, 0, xf, nw2, wu, wd, 0, cast_jobs)
        if nxt:
            w_in, w_out, wu, wd = nxt
    return xf.reshape(batch, seq, d)
```

```python
import functools
import math

import numpy as np
import jax
import jax.numpy as jnp
from jax import lax
from jax.experimental import pallas as pl
from jax.experimental.pallas import tpu as pltpu

F32 = jnp.float32
BF16 = jnp.bfloat16

D_MODEL = 1024
N_MIXERS = 3
RET_HEADS = 4
RET_DK = D_MODEL // RET_HEADS
RET_DV = 2 * RET_DK
ROPE_BASE = 10000.0
ATTN_HEADS = 16
HEAD_DIM = D_MODEL // ATTN_HEADS
MOBA_BLOCK = 256
MOBA_TOPK = 3
REL_BUCKETS = 32
REL_MAX_EXACT = REL_BUCKETS // 2
REL_MAX_DISTANCE = 1024
D_FF = 4 * D_MODEL
EPS = 1e-6
NEG = -1e30
LOG2E = math.log2(math.e)

LANES = 128
BF16_SUBLANES = 16
VMEM_LIMIT_BYTES = 56 * 1024 * 1024

ROW_TILE = 512
COL_TILE = 256
FF_TILE = 1024
GATE_TILE = 512
RET_CHUNK = 256
RET_UNROLL = 4
ATT_TILE = 256
BIAS_TILES = 6
HEADS_PER_PAIR = LANES // HEAD_DIM
PAIRS_PER_STEP = 2
HEADS_PER_STEP = HEADS_PER_PAIR * PAIRS_PER_STEP
ONES_ROWS = BF16_SUBLANES


def _value_group(kind):
    return HEAD_DIM if kind == "moba" else LANES


def _params(*sem):
    return pltpu.CompilerParams(dimension_semantics=sem, vmem_limit_bytes=VMEM_LIMIT_BYTES)


def _rms(xf, w):
    ms = jnp.mean(xf * xf, axis=-1, keepdims=True)
    return xf * lax.rsqrt(ms + EPS) * w


def _dot(a, b):
    return jnp.dot(a, b, preferred_element_type=F32)


def _dot_nt(a, b):
    return lax.dot_general(a, b, (((1,), (1,)), ((), ())), preferred_element_type=F32)


def _dot_tn(a, b):
    return lax.dot_general(a, b, (((0,), (0,)), ((), ())), preferred_element_type=F32)


def _cast_specs(cast_jobs, steps):
    in_specs, out_specs, shapes = [], [], []
    for w, w_layer in cast_jobs:
        _, rows, cols = w.shape
        slab = rows // steps
        assert slab * steps == rows and slab % BF16_SUBLANES == 0
        in_specs.append(pl.BlockSpec((None, slab, cols), lambda i, w_layer=w_layer: (w_layer, i, 0)))
        out_specs.append(pl.BlockSpec((None, slab, cols), lambda i: (0, i, 0)))
        shapes.append(jax.ShapeDtypeStruct((1, rows, cols), BF16))
    return in_specs, out_specs, shapes


def _split_cast_refs(refs, n_outputs, n_cast):
    n_in = len(refs) - n_outputs - 2 * n_cast
    sources = refs[n_in:n_in + n_cast]
    outputs = refs[n_in + n_cast:n_in + n_cast + n_outputs]
    return refs[:n_in] + outputs, sources, refs[n_in + n_cast + n_outputs:]


def _run_casts(sources, destinations):
    for src, dst in zip(sources, destinations):
        dst[...] = src[...].astype(BF16)


def _ret_proj_kernel(n_cast, *refs):
    refs, cast_in, cast_out = _split_cast_refs(refs, 4, n_cast)
    _run_casts(cast_in, cast_out)
    x_ref, nw_ref, w_ref, cos_ref, sin_ref, q_ref, k_ref, v_ref, g_ref = refs
    h = _rms(x_ref[...], nw_ref[...]).astype(BF16)
    cos = cos_ref[...]
    sin = sin_ref[...]
    half = RET_DK // 2
    chunks = []
    for out_ref, scale in ((q_ref, 1.0), (k_ref, RET_DK ** -0.5)):
        chunks += [(out_ref, hd * RET_DK, scale) for hd in range(RET_HEADS)]
    for out_ref in (v_ref, g_ref):
        chunks += [(out_ref, c * COL_TILE, None) for c in range(RET_HEADS * RET_DV // COL_TILE)]
    assert RET_DK == COL_TILE

    def project(n):
        return _dot(h, w_ref[:, n * COL_TILE:(n + 1) * COL_TILE])

    acc_next = project(0)
    for n, (out_ref, col0, scale) in enumerate(chunks):
        acc = acc_next
        if n + 1 < len(chunks):
            acc_next = project(n + 1)
        if scale is None:
            out_ref[:, col0:col0 + COL_TILE] = acc.astype(BF16)
        else:
            x1 = acc[:, :half]
            x2 = acc[:, half:]
            r1 = x1 * cos - x2 * sin
            r2 = x1 * sin + x2 * cos
            if scale != 1.0:
                r1 = r1 * scale
                r2 = r2 * scale
            out_ref[:, col0:col0 + half] = r1.astype(BF16)
            out_ref[:, col0 + half:col0 + RET_DK] = r2.astype(BF16)


def _ret_proj(x, nw, w, layer, cos, sin, seq, cast_jobs=()):
    t = x.shape[0]
    cast_in_specs, cast_out_specs, cast_shapes = _cast_specs(cast_jobs, t // ROW_TILE)
    n_in = w.shape[2]
    tiles_per_seq = seq // ROW_TILE
    row = lambda i: (i, 0)
    const = lambda i: (0, 0)
    pos = lambda i: (i % tiles_per_seq, 0)
    n_qk = RET_HEADS * RET_DK
    n_v = RET_HEADS * RET_DV
    outs = pl.pallas_call(
        functools.partial(_ret_proj_kernel, len(cast_jobs)),
        grid=(t // ROW_TILE,),
        in_specs=[
            pl.BlockSpec((ROW_TILE, D_MODEL), row),
            pl.BlockSpec((1, D_MODEL), const),
            pl.BlockSpec((None, D_MODEL, n_in), lambda i: (layer, 0, 0)),
            pl.BlockSpec((ROW_TILE, RET_DK // 2), pos),
            pl.BlockSpec((ROW_TILE, RET_DK // 2), pos),
        ] + cast_in_specs,
        out_specs=[
            pl.BlockSpec((ROW_TILE, n_qk), row),
            pl.BlockSpec((ROW_TILE, n_qk), row),
            pl.BlockSpec((ROW_TILE, n_v), row),
            pl.BlockSpec((ROW_TILE, n_v), row),
        ] + cast_out_specs,
        out_shape=[
            jax.ShapeDtypeStruct((t, n_qk), BF16),
            jax.ShapeDtypeStruct((t, n_qk), BF16),
            jax.ShapeDtypeStruct((t, n_v), BF16),
            jax.ShapeDtypeStruct((t, n_v), BF16),
        ] + cast_shapes,
        compiler_params=_params("parallel"),
        name="ret_proj",
    )(x, nw, w, cos, sin, *[cw for cw, _ in cast_jobs])
    return outs[:4], outs[4:]


def _attn_proj_kernel(v_group, x_ref, nw_ref, w_ref, qn_ref, kn_ref, grp_ref, qt_ref, k_ref, vt_ref):
    h = _rms(x_ref[...], nw_ref[...]).astype(BF16)
    grp = grp_ref[...]
    groups_per_chunk = COL_TILE // v_group
    chunks_per_part = D_MODEL // COL_TILE
    n_chunks = 3 * chunks_per_part

    def project(n):
        return _dot(h, w_ref[:, n * COL_TILE:(n + 1) * COL_TILE])

    acc_next = project(0)
    for n in range(n_chunks):
        acc = acc_next
        if n + 1 < n_chunks:
            acc_next = project(n + 1)
        part, c = divmod(n, chunks_per_part)
        cols = slice(c * COL_TILE, (c + 1) * COL_TILE)
        if part < 2:
            hw_ref = qn_ref if part == 0 else kn_ref
            ms = _dot((acc * acc).astype(BF16), grp)
            acc = acc * lax.rsqrt(ms + EPS) * hw_ref[:, cols]
        if part == 0:
            qt_ref[cols, :] = acc.T.astype(BF16)
        elif part == 1:
            k_ref[:, cols] = acc.astype(BF16)
        else:
            acc_t = acc.T.astype(BF16)
            for p in range(groups_per_chunk):
                row0 = (c * groups_per_chunk + p) * (v_group + ONES_ROWS)
                vt_ref[row0:row0 + v_group, :] = acc_t[p * v_group:(p + 1) * v_group, :]
                vt_ref[row0 + v_group:row0 + v_group + ONES_ROWS, :] = jnp.ones((ONES_ROWS, acc_t.shape[1]), BF16)


def _attn_proj(x, nw, w, layer, qn, kn, grp, v_group):
    t = x.shape[0]
    row = lambda i: (i, 0)
    col = lambda i: (0, i)
    const = lambda i: (0, 0)
    vt_rows = (D_MODEL // v_group) * (v_group + ONES_ROWS)
    return pl.pallas_call(
        functools.partial(_attn_proj_kernel, v_group),
        grid=(t // ROW_TILE,),
        in_specs=[
            pl.BlockSpec((ROW_TILE, D_MODEL), row),
            pl.BlockSpec((1, D_MODEL), const),
            pl.BlockSpec((None, D_MODEL, 3 * D_MODEL), lambda i: (layer, 0, 0)),
            pl.BlockSpec((1, D_MODEL), const),
            pl.BlockSpec((1, D_MODEL), const),
            pl.BlockSpec((COL_TILE, COL_TILE), const),
        ],
        out_specs=[
            pl.BlockSpec((D_MODEL, ROW_TILE), col),
            pl.BlockSpec((ROW_TILE, D_MODEL), row),
            pl.BlockSpec((vt_rows, ROW_TILE), col),
        ],
        out_shape=[
            jax.ShapeDtypeStruct((D_MODEL, t), BF16),
            jax.ShapeDtypeStruct((t, D_MODEL), BF16),
            jax.ShapeDtypeStruct((vt_rows, t), BF16),
        ],
        compiler_params=_params("parallel"),
        name="attn_proj",
    )(x, nw, w, qn, kn, grp)


def _ret_core_kernel(q_ref, k_ref, v_ref, dm_ref, rs_ref, ks_ref, cd_ref, o_ref, state_ref, raw_ref):
    c_len = RET_CHUNK
    n_chunks = q_ref.shape[0] // c_len
    state_ref[...] = jnp.zeros_like(state_ref)
    raw_ref[...] = jnp.zeros_like(raw_ref)

    def chunk(c):
        return pl.ds(pl.multiple_of(c * c_len, c_len), c_len)

    def masked_scores(c):
        return (_dot_nt(q_ref[chunk(c), :], k_ref[chunk(c), :]) * dm_ref[0]).astype(BF16)

    def normalise(c, slot):
        o = raw_ref[slot]
        o_ref[chunk(c), :] = (o * lax.rsqrt(jnp.mean(o * o, axis=-1, keepdims=True) + EPS)).astype(BF16)

    def step(c, slot, s_cur):
        sl = chunk(c)
        q = q_ref[sl, :]
        k = k_ref[sl, :]
        v = v_ref[sl, :]
        state = state_ref[...]
        kd = (k.astype(F32) * ks_ref[0]).astype(BF16)
        kv = _dot_tn(kd, v)
        inner = _dot(s_cur, v)
        cross = _dot(q, state.astype(BF16))
        s_next = masked_scores(jnp.minimum(c + 1, n_chunks - 1))
        normalise(jnp.maximum(c - 1, 0), 1 - slot)
        raw_ref[slot] = inner + cross * rs_ref[0]
        state_ref[...] = state * cd_ref[0] + kv
        return s_next

    def body(u, s_cur):
        for r in range(RET_UNROLL):
            s_cur = step(RET_UNROLL * u + r, r % 2, s_cur)
        return s_cur

    lax.fori_loop(0, n_chunks // RET_UNROLL, body, masked_scores(0))
    normalise(n_chunks - 1, 1)


def _ret_core(q, k, v, dm, rs, ks, cd, batch, seq):
    t = q.shape[0]
    c_len = RET_CHUNK
    tok = lambda b, h: (b, h)
    head3 = lambda b, h: (h, 0, 0)
    return pl.pallas_call(
        _ret_core_kernel,
        grid=(batch, RET_HEADS),
        in_specs=[
            pl.BlockSpec((seq, RET_DK), tok),
            pl.BlockSpec((seq, RET_DK), tok),
            pl.BlockSpec((seq, RET_DV), tok),
            pl.BlockSpec((1, c_len, c_len), head3),
            pl.BlockSpec((1, c_len, 1), head3),
            pl.BlockSpec((1, c_len, 1), head3),
            pl.BlockSpec((1, 1, 1), head3),
        ],
        out_specs=pl.BlockSpec((seq, RET_DV), tok),
        out_shape=jax.ShapeDtypeStruct((t, RET_HEADS * RET_DV), BF16),
        scratch_shapes=[pltpu.VMEM((RET_DK, RET_DV), F32), pltpu.VMEM((2, RET_CHUNK, RET_DV), F32)],
        compiler_params=_params("parallel", "parallel"),
        name="ret_core",
    )(q, k, v, dm, rs, ks, cd)


def _out_mlp_kernel(gated, n_cast, *refs):
    refs, cast_in, cast_out = _split_cast_refs(refs, 1, n_cast)
    _run_casts(cast_in, cast_out)
    if gated:
        a_ref, g_ref, wo_ref, x_ref, nw_ref, wu_ref, wd_ref, o_ref = refs
        n_chunks = a_ref.shape[1] // GATE_TILE

        def gate(c):
            cols = slice(c * GATE_TILE, (c + 1) * GATE_TILE)
            g = g_ref[:, cols].astype(F32)
            return (g * (1.0 / (1.0 + jnp.exp(-g))) * a_ref[:, cols].astype(F32)).astype(BF16)

        a_next = gate(0)
        mix = None
        for c in range(n_chunks):
            a = a_next
            if c + 1 < n_chunks:
                a_next = gate(c + 1)
            part = _dot(a, wo_ref[c * GATE_TILE:(c + 1) * GATE_TILE, :])
            mix = part if mix is None else mix + part
    else:
        a_ref, wo_ref, x_ref, nw_ref, wu_ref, wd_ref, o_ref = refs
        mix = _dot_tn(a_ref[...], wo_ref[...])
    x = x_ref[...] + mix
    h = _rms(x, nw_ref[...]).astype(BF16)
    acc = x
    for c in range(D_FF // FF_TILE):
        u = _dot(h, wu_ref[:, c * FF_TILE:(c + 1) * FF_TILE])
        u = jnp.maximum(u, 0.0)
        acc = acc + _dot((u * u).astype(BF16), wd_ref[c * FF_TILE:(c + 1) * FF_TILE, :])
    o_ref[...] = acc


def _out_mlp(a, g, wo, mixer_layer, x, nw, wu, wd, layer, cast_jobs=()):
    t = x.shape[0]
    steps = t // ROW_TILE
    cast_in_specs, cast_out_specs, cast_shapes = _cast_specs(cast_jobs, steps)
    ka = wo.shape[1]
    row = lambda i: (i, 0)
    const = lambda i: (0, 0)
    gated = g is not None
    if gated:
        mix_specs = [pl.BlockSpec((ROW_TILE, ka), row), pl.BlockSpec((ROW_TILE, ka), row)]
        mix_args = [a, g]
    else:
        mix_specs = [pl.BlockSpec((ka, ROW_TILE), lambda i: (0, i))]
        mix_args = [a]
    outs = pl.pallas_call(
        functools.partial(_out_mlp_kernel, gated, len(cast_jobs)),
        grid=(steps,),
        in_specs=mix_specs + [
            pl.BlockSpec((None, ka, D_MODEL), lambda i: (mixer_layer, 0, 0)),
            pl.BlockSpec((ROW_TILE, D_MODEL), row),
            pl.BlockSpec((1, D_MODEL), const),
            pl.BlockSpec((None, D_MODEL, D_FF), lambda i: (layer, 0, 0), pipeline_mode=pl.Buffered(1)),
            pl.BlockSpec((None, D_FF, D_MODEL), lambda i: (layer, 0, 0), pipeline_mode=pl.Buffered(1)),
        ] + cast_in_specs,
        out_specs=[pl.BlockSpec((ROW_TILE, D_MODEL), row)] + cast_out_specs,
        out_shape=[jax.ShapeDtypeStruct((t, D_MODEL), F32)] + cast_shapes,
        compiler_params=_params("parallel"),
        name="out_mlp_gated" if gated else "out_mlp",
    )(*mix_args, wo, x, nw, wu, wd, *[w for w, _ in cast_jobs])
    return outs[0], outs[1:]


def _bias_tiles_kernel(bucket_ranges, rb_ref, idx_ref, o_ref):
    head = pl.program_id(0)
    for d, (lo, hi) in enumerate(bucket_ranges):
        idx = idx_ref[d]
        acc = jnp.full(idx.shape, NEG, F32)
        for b in range(lo, hi + 1):
            acc = jnp.where(idx == b, rb_ref[b, head] * LOG2E, acc)
        o_ref[0, d] = acc


def _bucket_ranges():
    ranges = []
    for d in range(BIAS_TILES):
        lo_dist = max(d * ATT_TILE - (ATT_TILE - 1), 0)
        hi_dist = d * ATT_TILE + (ATT_TILE - 1)

        def bucket(n):
            if n < REL_MAX_EXACT:
                return n
            return min(REL_MAX_EXACT + int(math.log(n / REL_MAX_EXACT) / math.log(REL_MAX_DISTANCE / REL_MAX_EXACT)
                                           * (REL_BUCKETS - REL_MAX_EXACT)), REL_BUCKETS - 1)

        ranges.append((max(bucket(lo_dist) - 1, 0), min(bucket(hi_dist) + 1, REL_BUCKETS - 1)))
    return tuple(ranges)


def _bias_tiles(rel_bias, bucket_idx):
    nd, tq, tk = bucket_idx.shape
    return pl.pallas_call(
        functools.partial(_bias_tiles_kernel, _bucket_ranges()),
        grid=(ATTN_HEADS,),
        in_specs=[
            pl.BlockSpec(memory_space=pltpu.SMEM),
            pl.BlockSpec((nd, tq, tk), lambda h: (0, 0, 0)),
        ],
        out_specs=pl.BlockSpec((1, nd, tq, tk), lambda h: (h, 0, 0, 0)),
        out_shape=jax.ShapeDtypeStruct((ATTN_HEADS, nd, tq, tk), F32),
        compiler_params=_params("parallel"),
        name="bias_tiles",
    )(rel_bias, bucket_idx)


def _rel_bucket(dist):
    n = jnp.maximum(dist, 0)
    nf = jnp.maximum(n, 1).astype(F32)
    large = REL_MAX_EXACT + (jnp.log(nf / REL_MAX_EXACT) / math.log(REL_MAX_DISTANCE / REL_MAX_EXACT)
                             * (REL_BUCKETS - REL_MAX_EXACT)).astype(jnp.int32)
    large = jnp.minimum(large, REL_BUCKETS - 1)
    return jnp.where(n < REL_MAX_EXACT, n, large)


def _bucket_index_tiles():
    r = np.arange(ATT_TILE)
    dist = (np.arange(BIAS_TILES)[:, None, None] * ATT_TILE + r[None, None, :] - r[None, :, None])
    dist = jnp.asarray(dist, jnp.int32)
    return jnp.where(dist >= 0, _rel_bucket(dist), REL_BUCKETS).astype(jnp.int32)


TAB_Q, TAB_K, TAB_DELTA, TAB_FIRST, TAB_ACC = range(5)
PIPE_LAG = 2
PIPE_UNROLL = 6


def _tile_schedule(nq, own_first):
    rows = []
    for i in range(nq):
        keys = ([i] + list(range(i))) if own_first else list(range(i + 1))
        for n, j in enumerate(keys):
            rows.append((i, j, min(i - j, BIAS_TILES - 1), int(n == 0), i))
    n_iters = -(-(len(rows) + PIPE_LAG) // PIPE_UNROLL) * PIPE_UNROLL
    idle = (0, 0, 0, 1, nq)
    cols = [idle] * PIPE_LAG + rows
    cols += [idle] * (n_iters + PIPE_LAG - len(cols))
    return jnp.asarray(np.array(cols, np.int32).T), n_iters


def _tile_slice(idx):
    return pl.ds(pl.multiple_of(idx * ATT_TILE, ATT_TILE), ATT_TILE)


def _head_row_mask(h):
    row = lax.broadcasted_iota(jnp.int32, (LANES, 1), 0)
    return (row >= h * HEAD_DIM) & (row < (h + 1) * HEAD_DIM)


def _pair_rows(pair, n=LANES):
    return slice(pair * n, (pair + 1) * n)


def _attn_kernel(kind, lambda_init, n_iters, tab_ref, qt_ref, k_ref, vt_ref, bias_ref, *rest):
    if kind == "moba":
        blk_ref, ot_ref, qts_ref, vts_ref, s_ref, p_ref, acc_ref, neg_ref = rest
    else:
        lam_ref, sw_ref, ot_ref, qts_ref, vts_ref, s_ref, p_ref, acc_ref = rest
    heads = range(HEADS_PER_STEP)
    seq = qt_ref.shape[1]
    acc_rows = acc_ref.shape[2]

    n_tiles = seq // ATT_TILE
    q_heads = []
    for e in heads:
        pair, h = divmod(e, HEADS_PER_PAIR)
        qf = qt_ref[_pair_rows(pair), :].astype(F32)
        q_heads.append(jnp.where(_head_row_mask(h), qf, 0.0).astype(BF16))
        for i in range(n_tiles):
            qts_ref[i, e] = q_heads[e][:, i * ATT_TILE:(i + 1) * ATT_TILE]
    for j in range(n_tiles):
        vts_ref[j] = vt_ref[:, j * ATT_TILE:(j + 1) * ATT_TILE]
    s_ref[...] = jnp.zeros_like(s_ref)
    p_ref[...] = jnp.zeros_like(p_ref)
    acc_ref[...] = jnp.zeros_like(acc_ref)

    if kind == "moba":
        nb = blk_ref.shape[0]
        kmean = _dot(blk_ref[...], k_ref[...])
        km_hi = kmean.astype(BF16)
        km_lo = (kmean - km_hi.astype(F32)).astype(BF16)
        blk_id = lax.broadcasted_iota(jnp.int32, (nb, ATT_TILE), 0)
        for e in heads:
            pair = e // HEADS_PER_PAIR
            q_e = q_heads[e]
            gate_all = _dot(km_hi[:, _pair_rows(pair)], q_e) + _dot(km_lo[:, _pair_rows(pair)], q_e)
            for i in range(n_tiles):
                cols = slice(i * ATT_TILE, (i + 1) * ATT_TILE)
                past = blk_id < i
                gate = jnp.where(past, gate_all[:, cols], NEG)
                rank = jnp.zeros((nb, ATT_TILE), jnp.int32)
                for c in range(i):
                    gc = gate[c:c + 1, :]
                    tie = jnp.where(blk_id > c, 1, 0)
                    rank = rank + jnp.where(gc > gate, 1, jnp.where(gc == gate, tie, 0))
                chosen = jnp.where(rank < MOBA_TOPK, jnp.where(past, 1, 0), 0)
                keep = jnp.maximum(chosen, jnp.where(blk_id == i, 1, 0))
                neg_ref[e, :, cols] = jnp.where(keep == 1, 0.0, NEG)

    def step(t, cur, carry):
        nxt = 1 - cur
        m_prev, alpha_prev, mtile_prev = carry
        ic = tab_ref[TAB_ACC, t]
        jc = tab_ref[TAB_K, t]
        ib = tab_ref[TAB_Q, t + 1]
        jb = tab_ref[TAB_K, t + 1]
        first = tab_ref[TAB_FIRST, t + 1] != 0
        ia = tab_ref[TAB_Q, t + 2]
        ja = tab_ref[TAB_K, t + 2]
        da = tab_ref[TAB_DELTA, t + 2]
        m_new, alpha_new, mtile_new = [], [], []
        for e in heads:
            pair = e // HEADS_PER_PAIR

            def stage_a():
                kt = k_ref[_tile_slice(ja), _pair_rows(pair)]
                sb = _dot(kt, qts_ref[ia, e]) + bias_ref[e, da]
                s_ref[nxt, e] = sb
                mtile_new.append(jnp.max(sb, axis=0, keepdims=True))

            def stage_c():
                vt = vts_ref[jc, _pair_rows(e if kind == "moba" else pair, acc_rows), :]
                pv = _dot(vt, p_ref[cur, e])
                acc_ref[ic, e] = alpha_prev[e] * acc_ref[ic, e] + pv

            for stage in ((stage_a, stage_c) if kind == "moba" else (stage_c, stage_a)):
                stage()

            s = s_ref[cur, e]
            m_in = jnp.where(first, -jnp.inf, m_prev[e])
            if kind == "moba":
                neg = neg_ref[e, pl.ds(jb, 1), _tile_slice(ib)]
                m_e = jnp.maximum(m_in, mtile_prev[e] + neg)
                shift = m_e - neg
            else:
                m_e = jnp.maximum(m_in, mtile_prev[e])
                shift = m_e
            p_ref[nxt, e] = jnp.exp2(s - shift).astype(BF16)
            alpha_new.append(jnp.exp2(m_in - m_e))
            m_new.append(m_e)

        return tuple(m_new), tuple(alpha_new), tuple(mtile_new)

    zeros = tuple(jnp.zeros((1, ATT_TILE), F32) for _ in heads)

    def body(u, carry):
        for r in range(PIPE_UNROLL):
            carry = step(PIPE_UNROLL * u + r, r % 2, carry)
        return carry

    lax.fori_loop(0, n_iters // PIPE_UNROLL, body, (zeros, zeros, zeros))

    def emit(i, carry):
        cols = _tile_slice(i)
        v_group = acc_ref.shape[2] - ONES_ROWS
        for pair in range(PAIRS_PER_STEP):
            outs = []
            for h in range(HEADS_PER_PAIR):
                e = pair * HEADS_PER_PAIR + h
                outs.append(acc_ref[i, e, :v_group, :] * (1.0 / acc_ref[i, e, v_group:v_group + 1, :]))
            if kind == "moba":
                o = jnp.concatenate(outs, axis=0)
            else:
                lam = lam_ref[...]
                lam_full = (jnp.exp(jnp.sum(lam[0:1] * lam[1:2], axis=-1, keepdims=True))
                            - jnp.exp(jnp.sum(lam[2:3] * lam[3:4], axis=-1, keepdims=True)) + lambda_init)
                o = outs[0] - lam_full * outs[1]
                o = (o * lax.rsqrt(jnp.mean(o * o, axis=0, keepdims=True) + EPS)
                     * sw_ref[...] * (1.0 - lambda_init))
            ot_ref[_pair_rows(pair), cols] = o.astype(BF16)
        return carry

    lax.fori_loop(0, seq // ATT_TILE, emit, 0)


def _attention(kind, qt, k, vt_ext, bias, extras, lambda_init, batch, seq):
    v_group = _value_group(kind)
    acc_rows = v_group + ONES_ROWS
    vt_block_rows = acc_rows * (LANES * PAIRS_PER_STEP // v_group)
    t = k.shape[0]
    nq = seq // ATT_TILE
    tab, n_iters = _tile_schedule(nq, own_first=(kind == "moba"))
    groups = ATTN_HEADS // HEADS_PER_STEP
    rows = LANES * PAIRS_PER_STEP
    in_specs = [
        pl.BlockSpec(memory_space=pltpu.SMEM),
        pl.BlockSpec((rows, seq), lambda g, b: (g, b)),
        pl.BlockSpec((seq, rows), lambda g, b: (b, g)),
        pl.BlockSpec((vt_block_rows, seq), lambda g, b: (g, b)),
        pl.BlockSpec((HEADS_PER_STEP, BIAS_TILES, ATT_TILE, ATT_TILE), lambda g, b: (g, 0, 0, 0)),
    ]
    scratch = [
        pltpu.VMEM((nq, HEADS_PER_STEP, LANES, ATT_TILE), BF16),
        pltpu.VMEM((nq, vt_block_rows, ATT_TILE), BF16),
        pltpu.VMEM((2, HEADS_PER_STEP, ATT_TILE, ATT_TILE), F32),
        pltpu.VMEM((2, HEADS_PER_STEP, ATT_TILE, ATT_TILE), BF16),
        pltpu.VMEM((nq + 1, HEADS_PER_STEP, acc_rows, ATT_TILE), F32),
    ]
    if kind == "moba":
        (blk,) = extras
        nb = blk.shape[0]
        in_specs.append(pl.BlockSpec((nb, seq), lambda g, b: (0, 0)))
        scratch.append(pltpu.VMEM((HEADS_PER_STEP, nb, seq), F32))
    else:
        lam, sw = extras
        in_specs += [
            pl.BlockSpec((4, HEAD_DIM), lambda g, b: (0, 0)),
            pl.BlockSpec((LANES, 1), lambda g, b: (0, 0)),
        ]
    return pl.pallas_call(
        functools.partial(_attn_kernel, kind, lambda_init, n_iters),
        grid=(groups, batch),
        in_specs=in_specs,
        out_specs=pl.BlockSpec((rows, seq), lambda g, b: (g, b)),
        out_shape=jax.ShapeDtypeStruct((D_MODEL, t), BF16),
        scratch_shapes=scratch,
        compiler_params=_params("parallel", "parallel"),
        name=kind + "_attn",
    )(tab, qt, k, vt_ext, bias, *extras)


def _rotary_tables(seq):
    d = RET_DK
    inv_freq = ROPE_BASE ** (-np.arange(0, d, 2, dtype=np.float64) / d)
    ang = np.arange(seq, dtype=np.float64)[:, None] * inv_freq[None, :]
    return jnp.asarray(np.cos(ang), F32), jnp.asarray(np.sin(ang), F32)


def _retention_decay_tables():
    c_len = RET_CHUNK
    log_gamma = np.log(1.0 - 2.0 ** (-5.0 - np.arange(RET_HEADS, dtype=np.float64)))
    pos = np.arange(c_len, dtype=np.float64)
    rel = pos[:, None] - pos[None, :]
    dm = np.where(rel >= 0, np.exp(np.maximum(rel, 0.0)[None] * log_gamma[:, None, None]), 0.0)
    rs = np.exp((pos + 1.0)[None, :] * log_gamma[:, None])[:, :, None]
    ks = np.exp((c_len - 1.0 - pos)[None, :] * log_gamma[:, None])[:, :, None]
    cd = np.exp(c_len * log_gamma)[:, None, None]
    return tuple(jnp.asarray(a, F32) for a in (dm, rs, ks, cd))


def _block_mean_matrix(seq):
    nb = seq // MOBA_BLOCK
    m = (np.arange(seq)[None, :] // MOBA_BLOCK == np.arange(nb)[:, None]) / float(MOBA_BLOCK)
    return jnp.asarray(m, BF16)


def _head_group_matrix():
    g = np.arange(COL_TILE)[:, None] // HEAD_DIM == np.arange(COL_TILE)[None, :] // HEAD_DIM
    return jnp.asarray(g / float(HEAD_DIM), BF16)


def kernel(x, rel_bias, norm1, norm2, w_up, w_down, ret_w_in, ret_w_out,
           moba_w_in, moba_q_norm, moba_k_norm, moba_w_out,
           diff_w_in, diff_q_norm, diff_k_norm, diff_lambda, diff_subln, diff_w_out):
    batch, seq, d = x.shape
    depth = norm1.shape[0]
    assert d == D_MODEL and seq % ROW_TILE == 0 and seq % ATT_TILE == 0 and seq % (RET_UNROLL * RET_CHUNK) == 0
    assert seq % MOBA_BLOCK == 0 and MOBA_BLOCK == ATT_TILE
    t = batch * seq
    xf = x.reshape(t, d)

    bias = _bias_tiles(rel_bias.astype(F32), _bucket_index_tiles())
    grp = _head_group_matrix()
    q_scale = HEAD_DIM ** -0.5 * LOG2E
    mixer_weights = ((ret_w_in, ret_w_out), (moba_w_in, moba_w_out), (diff_w_in, diff_w_out))

    def layer_weights_f32(i):
        kind, j = i % N_MIXERS, i // N_MIXERS
        w_in, w_out = mixer_weights[kind]
        return [(w_in, j), (w_out, j), (w_up, i), (w_down, i)]

    first_jobs = layer_weights_f32(0)
    w_in = first_jobs[0][0][0:1].astype(BF16)
    w_out = wu = wd = None

    for i in range(depth):
        kind, j = i % N_MIXERS, i // N_MIXERS
        nw1 = norm1[i].reshape(1, d)
        nw2 = norm2[i].reshape(1, d)
        cast_jobs = layer_weights_f32(i + 1) if i + 1 < depth else ()
        if kind == 0:
            cos, sin = _rotary_tables(seq)
            dm, rs, ks, cd = _retention_decay_tables()
            (q, k, v, g), early = _ret_proj(xf, nw1, w_in, 0, cos, sin, seq, first_jobs[1:] if i == 0 else ())
            if early:
                w_out, wu, wd = early
            o = _ret_core(q, k, v, dm, rs, ks, cd, batch, seq)
            xf, nxt = _out_mlp(o, g, w_out, 0, xf, nw2, wu, wd, 0, cast_jobs)
        elif kind == 1:
            qn = (jnp.tile(moba_q_norm[j], ATTN_HEADS) * q_scale).reshape(1, d)
            kn = jnp.tile(moba_k_norm[j], ATTN_HEADS).reshape(1, d)
            qt, k, vt = _attn_proj(xf, nw1, w_in, 0, qn, kn, grp, _value_group("moba"))
            ot = _attention("moba", qt, k, vt, bias, (_block_mean_matrix(seq),), 0.0, batch, seq)
            xf, nxt = _out_mlp(ot, None, w_out, 0, xf, nw2, wu, wd, 0, cast_jobs)
        else:
            lambda_init = 0.8 - 0.6 * math.exp(-0.3 * i)
            qn = (jnp.tile(diff_q_norm[j], ATTN_HEADS) * q_scale).reshape(1, d)
            kn = jnp.tile(diff_k_norm[j], ATTN_HEADS).reshape(1, d)
            qt, k, vt = _attn_proj(xf, nw1, w_in, 0, qn, kn, grp, _value_group("diff"))
            extras = (diff_lambda[j].astype(F32), diff_subln[j].reshape(LANES, 1))
            ot = _attention("diff", qt, k, vt, bias, extras, lambda_init, batch, seq)
            xf, nxt = _out_mlp(ot, None, w_out, 0, xf, nw2, wu, wd, 0, cast_jobs)
        if nxt:
            w_in, w_out, wu, wd = nxt
    return xf.reshape(batch, seq, d)
```

```python
import functools
import math

import numpy as np
import jax
import jax.numpy as jnp
from jax import lax
from jax.experimental import pallas as pl
from jax.experimental.pallas import tpu as pltpu

F32 = jnp.float32
BF16 = jnp.bfloat16

D_MODEL = 1024
N_MIXERS = 3
RET_HEADS = 4
RET_DK = D_MODEL // RET_HEADS
RET_DV = 2 * RET_DK
ROPE_BASE = 10000.0
ATTN_HEADS = 16
HEAD_DIM = D_MODEL // ATTN_HEADS
MOBA_BLOCK = 256
MOBA_TOPK = 3
REL_BUCKETS = 32
REL_MAX_EXACT = REL_BUCKETS // 2
REL_MAX_DISTANCE = 1024
D_FF = 4 * D_MODEL
EPS = 1e-6
NEG = -1e30
LOG2E = math.log2(math.e)

LANES = 128
BF16_SUBLANES = 16
VMEM_LIMIT_BYTES = 56 * 1024 * 1024

ROW_TILE = 512
COL_TILE = 256
FF_TILE = 1024
GATE_TILE = 512
RET_CHUNK = 256
RET_UNROLL = 4
ATT_TILE = 256
BIAS_TILES = 6
HEADS_PER_PAIR = LANES // HEAD_DIM
PAIRS_PER_STEP = 2
HEADS_PER_STEP = HEADS_PER_PAIR * PAIRS_PER_STEP
ONES_ROWS = BF16_SUBLANES


def _value_group(kind):
    return HEAD_DIM if kind == "moba" else LANES


def _params(*sem):
    return pltpu.CompilerParams(dimension_semantics=sem, vmem_limit_bytes=VMEM_LIMIT_BYTES)


def _rms(xf, w):
    ms = jnp.mean(xf * xf, axis=-1, keepdims=True)
    return xf * lax.rsqrt(ms + EPS) * w


def _dot(a, b):
    return jnp.dot(a, b, preferred_element_type=F32)


def _dot_nt(a, b):
    return lax.dot_general(a, b, (((1,), (1,)), ((), ())), preferred_element_type=F32)


def _dot_tn(a, b):
    return lax.dot_general(a, b, (((0,), (0,)), ((), ())), preferred_element_type=F32)


def _cast_specs(cast_jobs, steps):
    in_specs, out_specs, shapes = [], [], []
    for w, w_layer in cast_jobs:
        _, rows, cols = w.shape
        slab = rows // steps
        assert slab * steps == rows and slab % BF16_SUBLANES == 0
        in_specs.append(pl.BlockSpec((None, slab, cols), lambda i, w_layer=w_layer: (w_layer, i, 0)))
        out_specs.append(pl.BlockSpec((None, slab, cols), lambda i: (0, i, 0)))
        shapes.append(jax.ShapeDtypeStruct((1, rows, cols), BF16))
    return in_specs, out_specs, shapes


def _split_cast_refs(refs, n_outputs, n_cast):
    n_in = len(refs) - n_outputs - 2 * n_cast
    sources = refs[n_in:n_in + n_cast]
    outputs = refs[n_in + n_cast:n_in + n_cast + n_outputs]
    return refs[:n_in] + outputs, sources, refs[n_in + n_cast + n_outputs:]


def _run_casts(sources, destinations):
    for src, dst in zip(sources, destinations):
        dst[...] = src[...].astype(BF16)


def _ret_proj_kernel(n_cast, *refs):
    refs, cast_in, cast_out = _split_cast_refs(refs, 4, n_cast)
    _run_casts(cast_in, cast_out)
    x_ref, nw_ref, w_ref, cos_ref, sin_ref, q_ref, k_ref, v_ref, g_ref = refs
    h = _rms(x_ref[...], nw_ref[...]).astype(BF16)
    cos = cos_ref[...]
    sin = sin_ref[...]
    half = RET_DK // 2
    chunks = []
    for out_ref, scale in ((q_ref, 1.0), (k_ref, RET_DK ** -0.5)):
        chunks += [(out_ref, hd * RET_DK, scale) for hd in range(RET_HEADS)]
    for out_ref in (v_ref, g_ref):
        chunks += [(out_ref, c * COL_TILE, None) for c in range(RET_HEADS * RET_DV // COL_TILE)]
    assert RET_DK == COL_TILE

    def project(n):
        return _dot(h, w_ref[:, n * COL_TILE:(n + 1) * COL_TILE])

    acc_next = project(0)
    for n, (out_ref, col0, scale) in enumerate(chunks):
        acc = acc_next
        if n + 1 < len(chunks):
            acc_next = project(n + 1)
        if scale is None:
            out_ref[:, col0:col0 + COL_TILE] = acc.astype(BF16)
        else:
            x1 = acc[:, :half]
            x2 = acc[:, half:]
            r1 = x1 * cos - x2 * sin
            r2 = x1 * sin + x2 * cos
            if scale != 1.0:
                r1 = r1 * scale
                r2 = r2 * scale
            out_ref[:, col0:col0 + half] = r1.astype(BF16)
            out_ref[:, col0 + half:col0 + RET_DK] = r2.astype(BF16)


def _ret_proj(x, nw, w, layer, cos, sin, seq, cast_jobs=()):
    t = x.shape[0]
    cast_in_specs, cast_out_specs, cast_shapes = _cast_specs(cast_jobs, t // ROW_TILE)
    n_in = w.shape[2]
    tiles_per_seq = seq // ROW_TILE
    row = lambda i: (i, 0)
    const = lambda i: (0, 0)
    pos = lambda i: (i % tiles_per_seq, 0)
    n_qk = RET_HEADS * RET_DK
    n_v = RET_HEADS * RET_DV
    outs = pl.pallas_call(
        functools.partial(_ret_proj_kernel, len(cast_jobs)),
        grid=(t // ROW_TILE,),
        in_specs=[
            pl.BlockSpec((ROW_TILE, D_MODEL), row),
            pl.BlockSpec((1, D_MODEL), const),
            pl.BlockSpec((None, D_MODEL, n_in), lambda i: (layer, 0, 0)),
            pl.BlockSpec((ROW_TILE, RET_DK // 2), pos),
            pl.BlockSpec((ROW_TILE, RET_DK // 2), pos),
        ] + cast_in_specs,
        out_specs=[
            pl.BlockSpec((ROW_TILE, n_qk), row),
            pl.BlockSpec((ROW_TILE, n_qk), row),
            pl.BlockSpec((ROW_TILE, n_v), row),
            pl.BlockSpec((ROW_TILE, n_v), row),
        ] + cast_out_specs,
        out_shape=[
            jax.ShapeDtypeStruct((t, n_qk), BF16),
            jax.ShapeDtypeStruct((t, n_qk), BF16),
            jax.ShapeDtypeStruct((t, n_v), BF16),
            jax.ShapeDtypeStruct((t, n_v), BF16),
        ] + cast_shapes,
        compiler_params=_params("parallel"),
        name="ret_proj",
    )(x, nw, w, cos, sin, *[cw for cw, _ in cast_jobs])
    return outs[:4], outs[4:]


def _attn_proj_kernel(v_group, x_ref, nw_ref, w_ref, qn_ref, kn_ref, grp_ref, qt_ref, k_ref, vt_ref):
    h = _rms(x_ref[...], nw_ref[...]).astype(BF16)
    grp = grp_ref[...]
    groups_per_chunk = COL_TILE // v_group
    chunks_per_part = D_MODEL // COL_TILE
    n_chunks = 3 * chunks_per_part

    def project(n):
        return _dot(h, w_ref[:, n * COL_TILE:(n + 1) * COL_TILE])

    acc_next = project(0)
    for n in range(n_chunks):
        acc = acc_next
        if n + 1 < n_chunks:
            acc_next = project(n + 1)
        part, c = divmod(n, chunks_per_part)
        cols = slice(c * COL_TILE, (c + 1) * COL_TILE)
        if part < 2:
            hw_ref = qn_ref if part == 0 else kn_ref
            ms = _dot((acc * acc).astype(BF16), grp)
            acc = acc * lax.rsqrt(ms + EPS) * hw_ref[:, cols]
        if part == 0:
            qt_ref[cols, :] = acc.T.astype(BF16)
        elif part == 1:
            k_ref[:, cols] = acc.astype(BF16)
        else:
            acc_t = acc.T.astype(BF16)
            for p in range(groups_per_chunk):
                row0 = (c * groups_per_chunk + p) * (v_group + ONES_ROWS)
                vt_ref[row0:row0 + v_group, :] = acc_t[p * v_group:(p + 1) * v_group, :]
                vt_ref[row0 + v_group:row0 + v_group + ONES_ROWS, :] = jnp.ones((ONES_ROWS, acc_t.shape[1]), BF16)


def _attn_proj(x, nw, w, layer, qn, kn, grp, v_group):
    t = x.shape[0]
    row = lambda i: (i, 0)
    col = lambda i: (0, i)
    const = lambda i: (0, 0)
    vt_rows = (D_MODEL // v_group) * (v_group + ONES_ROWS)
    return pl.pallas_call(
        functools.partial(_attn_proj_kernel, v_group),
        grid=(t // ROW_TILE,),
        in_specs=[
            pl.BlockSpec((ROW_TILE, D_MODEL), row),
            pl.BlockSpec((1, D_MODEL), const),
            pl.BlockSpec((None, D_MODEL, 3 * D_MODEL), lambda i: (layer, 0, 0)),
            pl.BlockSpec((1, D_MODEL), const),
            pl.BlockSpec((1, D_MODEL), const),
            pl.BlockSpec((COL_TILE, COL_TILE), const),
        ],
        out_specs=[
            pl.BlockSpec((D_MODEL, ROW_TILE), col),
            pl.BlockSpec((ROW_TILE, D_MODEL), row),
            pl.BlockSpec((vt_rows, ROW_TILE), col),
        ],
        out_shape=[
            jax.ShapeDtypeStruct((D_MODEL, t), BF16),
            jax.ShapeDtypeStruct((t, D_MODEL), BF16),
            jax.ShapeDtypeStruct((vt_rows, t), BF16),
        ],
        compiler_params=_params("parallel"),
        name="attn_proj",
    )(x, nw, w, qn, kn, grp)


def _ret_core_kernel(q_ref, k_ref, v_ref, dm_ref, rs_ref, ks_ref, cd_ref, o_ref, state_ref, raw_ref):
    c_len = RET_CHUNK
    n_chunks = q_ref.shape[0] // c_len
    state_ref[...] = jnp.zeros_like(state_ref)
    raw_ref[...] = jnp.zeros_like(raw_ref)

    def chunk(c):
        return pl.ds(pl.multiple_of(c * c_len, c_len), c_len)

    def masked_scores(c):
        return (_dot_nt(q_ref[chunk(c), :], k_ref[chunk(c), :]) * dm_ref[0]).astype(BF16)

    def normalise(c, slot):
        o = raw_ref[slot]
        o_ref[chunk(c), :] = (o * lax.rsqrt(jnp.mean(o * o, axis=-1, keepdims=True) + EPS)).astype(BF16)

    def step(c, slot, s_cur):
        sl = chunk(c)
        q = q_ref[sl, :]
        k = k_ref[sl, :]
        v = v_ref[sl, :]
        state = state_ref[...]
        kd = (k.astype(F32) * ks_ref[0]).astype(BF16)
        kv = _dot_tn(kd, v)
        inner = _dot(s_cur, v)
        cross = _dot(q, state.astype(BF16))
        s_next = masked_scores(jnp.minimum(c + 1, n_chunks - 1))
        normalise(jnp.maximum(c - 1, 0), 1 - slot)
        raw_ref[slot] = inner + cross * rs_ref[0]
        state_ref[...] = state * cd_ref[0] + kv
        return s_next

    def body(u, s_cur):
        for r in range(RET_UNROLL):
            s_cur = step(RET_UNROLL * u + r, r % 2, s_cur)
        return s_cur

    lax.fori_loop(0, n_chunks // RET_UNROLL, body, masked_scores(0))
    normalise(n_chunks - 1, 1)


def _ret_core(q, k, v, dm, rs, ks, cd, batch, seq):
    t = q.shape[0]
    c_len = RET_CHUNK
    tok = lambda b, h: (b, h)
    head3 = lambda b, h: (h, 0, 0)
    return pl.pallas_call(
        _ret_core_kernel,
        grid=(batch, RET_HEADS),
        in_specs=[
            pl.BlockSpec((seq, RET_DK), tok),
            pl.BlockSpec((seq, RET_DK), tok),
            pl.BlockSpec((seq, RET_DV), tok),
            pl.BlockSpec((1, c_len, c_len), head3),
            pl.BlockSpec((1, c_len, 1), head3),
            pl.BlockSpec((1, c_len, 1), head3),
            pl.BlockSpec((1, 1, 1), head3),
        ],
        out_specs=pl.BlockSpec((seq, RET_DV), tok),
        out_shape=jax.ShapeDtypeStruct((t, RET_HEADS * RET_DV), BF16),
        scratch_shapes=[pltpu.VMEM((RET_DK, RET_DV), F32), pltpu.VMEM((2, RET_CHUNK, RET_DV), F32)],
        compiler_params=_params("parallel", "parallel"),
        name="ret_core",
    )(q, k, v, dm, rs, ks, cd)


def _out_mlp_kernel(gated, n_cast, *refs):
    refs, cast_in, cast_out = _split_cast_refs(refs, 1, n_cast)
    _run_casts(cast_in, cast_out)
    if gated:
        a_ref, g_ref, wo_ref, x_ref, nw_ref, wu_ref, wd_ref, o_ref = refs
        n_chunks = a_ref.shape[1] // GATE_TILE

        def gate(c):
            cols = slice(c * GATE_TILE, (c + 1) * GATE_TILE)
            g = g_ref[:, cols].astype(F32)
            return (g * (1.0 / (1.0 + jnp.exp(-g))) * a_ref[:, cols].astype(F32)).astype(BF16)

        a_next = gate(0)
        mix = None
        for c in range(n_chunks):
            a = a_next
            if c + 1 < n_chunks:
                a_next = gate(c + 1)
            part = _dot(a, wo_ref[c * GATE_TILE:(c + 1) * GATE_TILE, :])
            mix = part if mix is None else mix + part
    else:
        a_ref, wo_ref, x_ref, nw_ref, wu_ref, wd_ref, o_ref = refs
        mix = _dot_tn(a_ref[...], wo_ref[...])
    x = x_ref[...] + mix
    h = _rms(x, nw_ref[...]).astype(BF16)
    acc = x
    for c in range(D_FF // FF_TILE):
        u = _dot(h, wu_ref[:, c * FF_TILE:(c + 1) * FF_TILE])
        u = jnp.maximum(u, 0.0)
        acc = acc + _dot((u * u).astype(BF16), wd_ref[c * FF_TILE:(c + 1) * FF_TILE, :])
    o_ref[...] = acc


def _out_mlp(a, g, wo, mixer_layer, x, nw, wu, wd, layer, cast_jobs=()):
    t = x.shape[0]
    steps = t // ROW_TILE
    cast_in_specs, cast_out_specs, cast_shapes = _cast_specs(cast_jobs, steps)
    ka = wo.shape[1]
    row = lambda i: (i, 0)
    const = lambda i: (0, 0)
    gated = g is not None
    if gated:
        mix_specs = [pl.BlockSpec((ROW_TILE, ka), row), pl.BlockSpec((ROW_TILE, ka), row)]
        mix_args = [a, g]
    else:
        mix_specs = [pl.BlockSpec((ka, ROW_TILE), lambda i: (0, i))]
        mix_args = [a]
    outs = pl.pallas_call(
        functools.partial(_out_mlp_kernel, gated, len(cast_jobs)),
        grid=(steps,),
        in_specs=mix_specs + [
            pl.BlockSpec((None, ka, D_MODEL), lambda i: (mixer_layer, 0, 0)),
            pl.BlockSpec((ROW_TILE, D_MODEL), row),
            pl.BlockSpec((1, D_MODEL), const),
            pl.BlockSpec((None, D_MODEL, D_FF), lambda i: (layer, 0, 0), pipeline_mode=pl.Buffered(1)),
            pl.BlockSpec((None, D_FF, D_MODEL), lambda i: (layer, 0, 0), pipeline_mode=pl.Buffered(1)),
        ] + cast_in_specs,
        out_specs=[pl.BlockSpec((ROW_TILE, D_MODEL), row)] + cast_out_specs,
        out_shape=[jax.ShapeDtypeStruct((t, D_MODEL), F32)] + cast_shapes,
        compiler_params=_params("parallel"),
        name="out_mlp_gated" if gated else "out_mlp",
    )(*mix_args, wo, x, nw, wu, wd, *[w for w, _ in cast_jobs])
    return outs[0], outs[1:]


def _bias_tiles_kernel(bucket_ranges, rb_ref, idx_ref, o_ref):
    head = pl.program_id(0)
    for d, (lo, hi) in enumerate(bucket_ranges):
        idx = idx_ref[d]
        acc = jnp.full(idx.shape, NEG, F32)
        for b in range(lo, hi + 1):
            acc = jnp.where(idx == b, rb_ref[b, head] * LOG2E, acc)
        o_ref[0, d] = acc


def _bucket_ranges():
    ranges = []
    for d in range(BIAS_TILES):
        lo_dist = max(d * ATT_TILE - (ATT_TILE - 1), 0)
        hi_dist = d * ATT_TILE + (ATT_TILE - 1)

        def bucket(n):
            if n < REL_MAX_EXACT:
                return n
            return min(REL_MAX_EXACT + int(math.log(n / REL_MAX_EXACT) / math.log(REL_MAX_DISTANCE / REL_MAX_EXACT)
                                           * (REL_BUCKETS - REL_MAX_EXACT)), REL_BUCKETS - 1)

        ranges.append((max(bucket(lo_dist) - 1, 0), min(bucket(hi_dist) + 1, REL_BUCKETS - 1)))
    return tuple(ranges)


def _bias_tiles(rel_bias, bucket_idx):
    nd, tq, tk = bucket_idx.shape
    return pl.pallas_call(
        functools.partial(_bias_tiles_kernel, _bucket_ranges()),
        grid=(ATTN_HEADS,),
        in_specs=[
            pl.BlockSpec(memory_space=pltpu.SMEM),
            pl.BlockSpec((nd, tq, tk), lambda h: (0, 0, 0)),
        ],
        out_specs=pl.BlockSpec((1, nd, tq, tk), lambda h: (h, 0, 0, 0)),
        out_shape=jax.ShapeDtypeStruct((ATTN_HEADS, nd, tq, tk), F32),
        compiler_params=_params("parallel"),
        name="bias_tiles",
    )(rel_bias, bucket_idx)


def _rel_bucket(dist):
    n = jnp.maximum(dist, 0)
    nf = jnp.maximum(n, 1).astype(F32)
    large = REL_MAX_EXACT + (jnp.log(nf / REL_MAX_EXACT) / math.log(REL_MAX_DISTANCE / REL_MAX_EXACT)
                             * (REL_BUCKETS - REL_MAX_EXACT)).astype(jnp.int32)
    large = jnp.minimum(large, REL_BUCKETS - 1)
    return jnp.where(n < REL_MAX_EXACT, n, large)


def _bucket_index_tiles():
    r = np.arange(ATT_TILE)
    dist = (np.arange(BIAS_TILES)[:, None, None] * ATT_TILE + r[None, None, :] - r[None, :, None])
    dist = jnp.asarray(dist, jnp.int32)
    return jnp.where(dist >= 0, _rel_bucket(dist), REL_BUCKETS).astype(jnp.int32)


TAB_Q, TAB_K, TAB_DELTA, TAB_FIRST, TAB_ACC = range(5)
PIPE_LAG = 2
PIPE_UNROLL = 6


def _tile_schedule(nq, own_first):
    rows = []
    for i in range(nq):
        keys = ([i] + list(range(i))) if own_first else list(range(i + 1))
        for n, j in enumerate(keys):
            rows.append((i, j, min(i - j, BIAS_TILES - 1), int(n == 0), i))
    n_iters = -(-(len(rows) + PIPE_LAG) // PIPE_UNROLL) * PIPE_UNROLL
    idle = (0, 0, 0, 1, nq)
    cols = [idle] * PIPE_LAG + rows
    cols += [idle] * (n_iters + PIPE_LAG - len(cols))
    return jnp.asarray(np.array(cols, np.int32).T), n_iters


def _tile_slice(idx):
    return pl.ds(pl.multiple_of(idx * ATT_TILE, ATT_TILE), ATT_TILE)


def _head_row_mask(h):
    row = lax.broadcasted_iota(jnp.int32, (LANES, 1), 0)
    return (row >= h * HEAD_DIM) & (row < (h + 1) * HEAD_DIM)


def _pair_rows(pair, n=LANES):
    return slice(pair * n, (pair + 1) * n)


def _attn_kernel(kind, lambda_init, n_iters, tab_ref, qt_ref, k_ref, vt_ref, bias_ref, *rest):
    if kind == "moba":
        blk_ref, ot_ref, qts_ref, vts_ref, s_ref, p_ref, acc_ref, neg_ref = rest
    else:
        lam_ref, sw_ref, ot_ref, qts_ref, vts_ref, s_ref, p_ref, acc_ref = rest
    heads = range(HEADS_PER_STEP)
    seq = qt_ref.shape[1]
    part = LANES if kind == "moba" else ATT_TILE
    query_parts = tuple(slice(c, c + part) for c in range(0, ATT_TILE, part))
    acc_rows = acc_ref.shape[2]

    n_tiles = seq // ATT_TILE
    q_heads = []
    for e in heads:
        pair, h = divmod(e, HEADS_PER_PAIR)
        qf = qt_ref[_pair_rows(pair), :].astype(F32)
        q_heads.append(jnp.where(_head_row_mask(h), qf, 0.0).astype(BF16))
        for i in range(n_tiles):
            qts_ref[i, e] = q_heads[e][:, i * ATT_TILE:(i + 1) * ATT_TILE]
    for j in range(n_tiles):
        vts_ref[j] = vt_ref[:, j * ATT_TILE:(j + 1) * ATT_TILE]
    s_ref[...] = jnp.zeros_like(s_ref)
    p_ref[...] = jnp.zeros_like(p_ref)
    acc_ref[...] = jnp.zeros_like(acc_ref)

    if kind == "moba":
        nb = blk_ref.shape[0]
        kmean = _dot(blk_ref[...], k_ref[...])
        km_hi = kmean.astype(BF16)
        km_lo = (kmean - km_hi.astype(F32)).astype(BF16)
        blk_id = lax.broadcasted_iota(jnp.int32, (nb, ATT_TILE), 0)
        for e in heads:
            pair = e // HEADS_PER_PAIR
            q_e = q_heads[e]
            gate_all = _dot(km_hi[:, _pair_rows(pair)], q_e) + _dot(km_lo[:, _pair_rows(pair)], q_e)
            for i in range(n_tiles):
                cols = slice(i * ATT_TILE, (i + 1) * ATT_TILE)
                past = blk_id < i
                gate = jnp.where(past, gate_all[:, cols], NEG)
                rank = jnp.zeros((nb, ATT_TILE), jnp.int32)
                for c in range(i):
                    gc = gate[c:c + 1, :]
                    tie = jnp.where(blk_id > c, 1, 0)
                    rank = rank + jnp.where(gc > gate, 1, jnp.where(gc == gate, tie, 0))
                chosen = jnp.where(rank < MOBA_TOPK, jnp.where(past, 1, 0), 0)
                keep = jnp.maximum(chosen, jnp.where(blk_id == i, 1, 0))
                neg_ref[e, :, cols] = jnp.where(keep == 1, 0.0, NEG)

    def step(t, cur, carry):
        nxt = 1 - cur
        m_prev, alpha_prev, mtile_prev = carry
        ic = tab_ref[TAB_ACC, t]
        jc = tab_ref[TAB_K, t]
        ib = tab_ref[TAB_Q, t + 1]
        jb = tab_ref[TAB_K, t + 1]
        first = tab_ref[TAB_FIRST, t + 1] != 0
        ia = tab_ref[TAB_Q, t + 2]
        ja = tab_ref[TAB_K, t + 2]
        da = tab_ref[TAB_DELTA, t + 2]
        m_new, alpha_new, mtile_new = [], [], []
        for e in heads:
            pair = e // HEADS_PER_PAIR

            def stage_a():
                kt = k_ref[_tile_slice(ja), _pair_rows(pair)]
                sb = _dot(kt, qts_ref[ia, e]) + bias_ref[e, da]
                s_ref[nxt, e] = sb
                mtile_new.append(jnp.max(sb, axis=0, keepdims=True))

            def stage_c():
                vt = vts_ref[jc, _pair_rows(e if kind == "moba" else pair, acc_rows), :]
                pv = _dot(vt, p_ref[cur, e])
                for lanes in query_parts:
                    acc_ref[ic, e, :, lanes] = alpha_prev[e][:, lanes] * acc_ref[ic, e, :, lanes] + pv[:, lanes]

            for stage in ((stage_a, stage_c) if kind == "moba" else (stage_c, stage_a)):
                stage()

            m_in = jnp.where(first, -jnp.inf, m_prev[e])
            if kind == "moba":
                neg = neg_ref[e, pl.ds(jb, 1), _tile_slice(ib)]
                m_e = jnp.maximum(m_in, mtile_prev[e] + neg)
                shift = m_e - neg
            else:
                m_e = jnp.maximum(m_in, mtile_prev[e])
                shift = m_e
            for lanes in query_parts:
                p_ref[nxt, e, :, lanes] = jnp.exp2(s_ref[cur, e, :, lanes] - shift[:, lanes]).astype(BF16)
            alpha_new.append(jnp.exp2(m_in - m_e))
            m_new.append(m_e)

        return tuple(m_new), tuple(alpha_new), tuple(mtile_new)

    zeros = tuple(jnp.zeros((1, ATT_TILE), F32) for _ in heads)

    def body(u, carry):
        for r in range(PIPE_UNROLL):
            carry = step(PIPE_UNROLL * u + r, r % 2, carry)
        return carry

    lax.fori_loop(0, n_iters // PIPE_UNROLL, body, (zeros, zeros, zeros))

    def emit(i, carry):
        cols = _tile_slice(i)
        v_group = acc_ref.shape[2] - ONES_ROWS
        for pair in range(PAIRS_PER_STEP):
            outs = []
            for h in range(HEADS_PER_PAIR):
                e = pair * HEADS_PER_PAIR + h
                outs.append(acc_ref[i, e, :v_group, :] * (1.0 / acc_ref[i, e, v_group:v_group + 1, :]))
            if kind == "moba":
                o = jnp.concatenate(outs, axis=0)
            else:
                lam = lam_ref[...]
                lam_full = (jnp.exp(jnp.sum(lam[0:1] * lam[1:2], axis=-1, keepdims=True))
                            - jnp.exp(jnp.sum(lam[2:3] * lam[3:4], axis=-1, keepdims=True)) + lambda_init)
                o = outs[0] - lam_full * outs[1]
                o = (o * lax.rsqrt(jnp.mean(o * o, axis=0, keepdims=True) + EPS)
                     * sw_ref[...] * (1.0 - lambda_init))
            ot_ref[_pair_rows(pair), cols] = o.astype(BF16)
        return carry

    lax.fori_loop(0, seq // ATT_TILE, emit, 0)


def _attention(kind, qt, k, vt_ext, bias, extras, lambda_init, batch, seq):
    v_group = _value_group(kind)
    acc_rows = v_group + ONES_ROWS
    vt_block_rows = acc_rows * (LANES * PAIRS_PER_STEP // v_group)
    t = k.shape[0]
    nq = seq // ATT_TILE
    tab, n_iters = _tile_schedule(nq, own_first=(kind == "moba"))
    groups = ATTN_HEADS // HEADS_PER_STEP
    rows = LANES * PAIRS_PER_STEP
    in_specs = [
        pl.BlockSpec(memory_space=pltpu.SMEM),
        pl.BlockSpec((rows, seq), lambda g, b: (g, b)),
        pl.BlockSpec((seq, rows), lambda g, b: (b, g)),
        pl.BlockSpec((vt_block_rows, seq), lambda g, b: (g, b)),
        pl.BlockSpec((HEADS_PER_STEP, BIAS_TILES, ATT_TILE, ATT_TILE), lambda g, b: (g, 0, 0, 0)),
    ]
    scratch = [
        pltpu.VMEM((nq, HEADS_PER_STEP, LANES, ATT_TILE), BF16),
        pltpu.VMEM((nq, vt_block_rows, ATT_TILE), BF16),
        pltpu.VMEM((2, HEADS_PER_STEP, ATT_TILE, ATT_TILE), F32),
        pltpu.VMEM((2, HEADS_PER_STEP, ATT_TILE, ATT_TILE), BF16),
        pltpu.VMEM((nq + 1, HEADS_PER_STEP, acc_rows, ATT_TILE), F32),
    ]
    if kind == "moba":
        (blk,) = extras
        nb = blk.shape[0]
        in_specs.append(pl.BlockSpec((nb, seq), lambda g, b: (0, 0)))
        scratch.append(pltpu.VMEM((HEADS_PER_STEP, nb, seq), F32))
    else:
        lam, sw = extras
        in_specs += [
            pl.BlockSpec((4, HEAD_DIM), lambda g, b: (0, 0)),
            pl.BlockSpec((LANES, 1), lambda g, b: (0, 0)),
        ]
    return pl.pallas_call(
        functools.partial(_attn_kernel, kind, lambda_init, n_iters),
        grid=(groups, batch),
        in_specs=in_specs,
        out_specs=pl.BlockSpec((rows, seq), lambda g, b: (g, b)),
        out_shape=jax.ShapeDtypeStruct((D_MODEL, t), BF16),
        scratch_shapes=scratch,
        compiler_params=_params("parallel", "parallel"),
        name=kind + "_attn",
    )(tab, qt, k, vt_ext, bias, *extras)


def _rotary_tables(seq):
    d = RET_DK
    inv_freq = ROPE_BASE ** (-np.arange(0, d, 2, dtype=np.float64) / d)
    ang = np.arange(seq, dtype=np.float64)[:, None] * inv_freq[None, :]
    return jnp.asarray(np.cos(ang), F32), jnp.asarray(np.sin(ang), F32)


def _retention_decay_tables():
    c_len = RET_CHUNK
    log_gamma = np.log(1.0 - 2.0 ** (-5.0 - np.arange(RET_HEADS, dtype=np.float64)))
    pos = np.arange(c_len, dtype=np.float64)
    rel = pos[:, None] - pos[None, :]
    dm = np.where(rel >= 0, np.exp(np.maximum(rel, 0.0)[None] * log_gamma[:, None, None]), 0.0)
    rs = np.exp((pos + 1.0)[None, :] * log_gamma[:, None])[:, :, None]
    ks = np.exp((c_len - 1.0 - pos)[None, :] * log_gamma[:, None])[:, :, None]
    cd = np.exp(c_len * log_gamma)[:, None, None]
    return tuple(jnp.asarray(a, F32) for a in (dm, rs, ks, cd))


def _block_mean_matrix(seq):
    nb = seq // MOBA_BLOCK
    m = (np.arange(seq)[None, :] // MOBA_BLOCK == np.arange(nb)[:, None]) / float(MOBA_BLOCK)
    return jnp.asarray(m, BF16)


def _head_group_matrix():
    g = np.arange(COL_TILE)[:, None] // HEAD_DIM == np.arange(COL_TILE)[None, :] // HEAD_DIM
    return jnp.asarray(g / float(HEAD_DIM), BF16)


def kernel(x, rel_bias, norm1, norm2, w_up, w_down, ret_w_in, ret_w_out,
           moba_w_in, moba_q_norm, moba_k_norm, moba_w_out,
           diff_w_in, diff_q_norm, diff_k_norm, diff_lambda, diff_subln, diff_w_out):
    batch, seq, d = x.shape
    depth = norm1.shape[0]
    assert d == D_MODEL and seq % ROW_TILE == 0 and seq % ATT_TILE == 0 and seq % (RET_UNROLL * RET_CHUNK) == 0
    assert seq % MOBA_BLOCK == 0 and MOBA_BLOCK == ATT_TILE
    t = batch * seq
    xf = x.reshape(t, d)

    bias = _bias_tiles(rel_bias.astype(F32), _bucket_index_tiles())
    grp = _head_group_matrix()
    q_scale = HEAD_DIM ** -0.5 * LOG2E
    mixer_weights = ((ret_w_in, ret_w_out), (moba_w_in, moba_w_out), (diff_w_in, diff_w_out))

    def layer_weights_f32(i):
        kind, j = i % N_MIXERS, i // N_MIXERS
        w_in, w_out = mixer_weights[kind]
        return [(w_in, j), (w_out, j), (w_up, i), (w_down, i)]

    first_jobs = layer_weights_f32(0)
    w_in = first_jobs[0][0][0:1].astype(BF16)
    w_out = wu = wd = None

    for i in range(depth):
        kind, j = i % N_MIXERS, i // N_MIXERS
        nw1 = norm1[i].reshape(1, d)
        nw2 = norm2[i].reshape(1, d)
        cast_jobs = layer_weights_f32(i + 1) if i + 1 < depth else ()
        if kind == 0:
            cos, sin = _rotary_tables(seq)
            dm, rs, ks, cd = _retention_decay_tables()
            (q, k, v, g), early = _ret_proj(xf, nw1, w_in, 0, cos, sin, seq, first_jobs[1:] if i == 0 else ())
            if early:
                w_out, wu, wd = early
            o = _ret_core(q, k, v, dm, rs, ks, cd, batch, seq)
            xf, nxt = _out_mlp(o, g, w_out, 0, xf, nw2, wu, wd, 0, cast_jobs)
        elif kind == 1:
            qn = (jnp.tile(moba_q_norm[j], ATTN_HEADS) * q_scale).reshape(1, d)
            kn = jnp.tile(moba_k_norm[j], ATTN_HEADS).reshape(1, d)
            qt, k, vt = _attn_proj(xf, nw1, w_in, 0, qn, kn, grp, _value_group("moba"))
            ot = _attention("moba", qt, k, vt, bias, (_block_mean_matrix(seq),), 0.0, batch, seq)
            xf, nxt = _out_mlp(ot, None, w_out, 0, xf, nw2, wu, wd, 0, cast_jobs)
        else:
            lambda_init = 0.8 - 0.6 * math.exp(-0.3 * i)
            qn = (jnp.tile(diff_q_norm[j], ATTN_HEADS) * q_scale).reshape(1, d)
            kn = jnp.tile(diff_k_norm[j], ATTN_HEADS).reshape(1, d)
            qt, k, vt = _attn_proj(xf, nw1, w_in, 0, qn, kn, grp, _value_group("diff"))
            extras = (diff_lambda[j].astype(F32), diff_subln[j].reshape(LANES, 1))
            ot = _attention("diff", qt, k, vt, bias, extras, lambda_init, batch, seq)
            xf, nxt = _out_mlp(ot, None, w_out, 0, xf, nw2, wu, wd, 0, cast_jobs)
        if nxt:
            w_in, w_out, wu, wd = nxt
    return xf.reshape(batch, seq, d)
```

```python
import functools
import math

import numpy as np
import jax
import jax.numpy as jnp
from jax import lax
from jax.experimental import pallas as pl
from jax.experimental.pallas import tpu as pltpu

F32 = jnp.float32
BF16 = jnp.bfloat16

D_MODEL = 1024
N_MIXERS = 3
RET_HEADS = 4
RET_DK = D_MODEL // RET_HEADS
RET_DV = 2 * RET_DK
ROPE_BASE = 10000.0
ATTN_HEADS = 16
HEAD_DIM = D_MODEL // ATTN_HEADS
MOBA_BLOCK = 256
MOBA_TOPK = 3
REL_BUCKETS = 32
REL_MAX_EXACT = REL_BUCKETS // 2
REL_MAX_DISTANCE = 1024
D_FF = 4 * D_MODEL
EPS = 1e-6
NEG = -1e30
LOG2E = math.log2(math.e)

LANES = 128
BF16_SUBLANES = 16
VMEM_LIMIT_BYTES = 56 * 1024 * 1024

ROW_TILE = 512
COL_TILE = 256
FF_TILE = 1024
GATE_TILE = 512
RET_CHUNK = 256
RET_UNROLL = 4
ATT_TILE = 256
BIAS_TILES = 6
HEADS_PER_PAIR = LANES // HEAD_DIM
PAIRS_PER_STEP = 2
HEADS_PER_STEP = HEADS_PER_PAIR * PAIRS_PER_STEP
ONES_ROWS = BF16_SUBLANES


def _value_group(kind):
    return HEAD_DIM if kind == "moba" else LANES


def _params(*sem):
    return pltpu.CompilerParams(dimension_semantics=sem, vmem_limit_bytes=VMEM_LIMIT_BYTES)


def _rms(xf, w):
    ms = jnp.mean(xf * xf, axis=-1, keepdims=True)
    return xf * lax.rsqrt(ms + EPS) * w


def _dot(a, b):
    return jnp.dot(a, b, preferred_element_type=F32)


def _dot_nt(a, b):
    return lax.dot_general(a, b, (((1,), (1,)), ((), ())), preferred_element_type=F32)


def _dot_tn(a, b):
    return lax.dot_general(a, b, (((0,), (0,)), ((), ())), preferred_element_type=F32)


def _cast_specs(cast_jobs, steps):
    in_specs, out_specs, shapes = [], [], []
    for w, w_layer in cast_jobs:
        _, rows, cols = w.shape
        slab = rows // steps
        assert slab * steps == rows and slab % BF16_SUBLANES == 0
        in_specs.append(pl.BlockSpec((None, slab, cols), lambda i, w_layer=w_layer: (w_layer, i, 0)))
        out_specs.append(pl.BlockSpec((None, slab, cols), lambda i: (0, i, 0)))
        shapes.append(jax.ShapeDtypeStruct((1, rows, cols), BF16))
    return in_specs, out_specs, shapes


def _split_cast_refs(refs, n_outputs, n_cast):
    n_in = len(refs) - n_outputs - 2 * n_cast
    sources = refs[n_in:n_in + n_cast]
    outputs = refs[n_in + n_cast:n_in + n_cast + n_outputs]
    return refs[:n_in] + outputs, sources, refs[n_in + n_cast + n_outputs:]


def _run_casts(sources, destinations):
    for src, dst in zip(sources, destinations):
        dst[...] = src[...].astype(BF16)


def _ret_proj_kernel(n_cast, *refs):
    refs, cast_in, cast_out = _split_cast_refs(refs, 4, n_cast)
    _run_casts(cast_in, cast_out)
    x_ref, nw_ref, w_ref, cos_ref, sin_ref, q_ref, k_ref, v_ref, g_ref = refs
    h = _rms(x_ref[...], nw_ref[...]).astype(BF16)
    cos = cos_ref[...]
    sin = sin_ref[...]
    half = RET_DK // 2
    chunks = []
    for out_ref, scale in ((q_ref, 1.0), (k_ref, RET_DK ** -0.5)):
        chunks += [(out_ref, hd * RET_DK, scale) for hd in range(RET_HEADS)]
    for out_ref in (v_ref, g_ref):
        chunks += [(out_ref, c * COL_TILE, None) for c in range(RET_HEADS * RET_DV // COL_TILE)]
    assert RET_DK == COL_TILE

    def project(n):
        return _dot(h, w_ref[:, n * COL_TILE:(n + 1) * COL_TILE])

    acc_next = project(0)
    for n, (out_ref, col0, scale) in enumerate(chunks):
        acc = acc_next
        if n + 1 < len(chunks):
            acc_next = project(n + 1)
        if scale is None:
            out_ref[:, col0:col0 + COL_TILE] = acc.astype(BF16)
        else:
            x1 = acc[:, :half]
            x2 = acc[:, half:]
            r1 = x1 * cos - x2 * sin
            r2 = x1 * sin + x2 * cos
            if scale != 1.0:
                r1 = r1 * scale
                r2 = r2 * scale
            out_ref[:, col0:col0 + half] = r1.astype(BF16)
            out_ref[:, col0 + half:col0 + RET_DK] = r2.astype(BF16)


def _ret_proj(x, nw, w, layer, cos, sin, seq, cast_jobs=()):
    t = x.shape[0]
    cast_in_specs, cast_out_specs, cast_shapes = _cast_specs(cast_jobs, t // ROW_TILE)
    n_in = w.shape[2]
    tiles_per_seq = seq // ROW_TILE
    row = lambda i: (i, 0)
    const = lambda i: (0, 0)
    pos = lambda i: (i % tiles_per_seq, 0)
    n_qk = RET_HEADS * RET_DK
    n_v = RET_HEADS * RET_DV
    outs = pl.pallas_call(
        functools.partial(_ret_proj_kernel, len(cast_jobs)),
        grid=(t // ROW_TILE,),
        in_specs=[
            pl.BlockSpec((ROW_TILE, D_MODEL), row),
            pl.BlockSpec((1, D_MODEL), const),
            pl.BlockSpec((None, D_MODEL, n_in), lambda i: (layer, 0, 0)),
            pl.BlockSpec((ROW_TILE, RET_DK // 2), pos),
            pl.BlockSpec((ROW_TILE, RET_DK // 2), pos),
        ] + cast_in_specs,
        out_specs=[
            pl.BlockSpec((ROW_TILE, n_qk), row),
            pl.BlockSpec((ROW_TILE, n_qk), row),
            pl.BlockSpec((ROW_TILE, n_v), row),
            pl.BlockSpec((ROW_TILE, n_v), row),
        ] + cast_out_specs,
        out_shape=[
            jax.ShapeDtypeStruct((t, n_qk), BF16),
            jax.ShapeDtypeStruct((t, n_qk), BF16),
            jax.ShapeDtypeStruct((t, n_v), BF16),
            jax.ShapeDtypeStruct((t, n_v), BF16),
        ] + cast_shapes,
        compiler_params=_params("parallel"),
        name="ret_proj",
    )(x, nw, w, cos, sin, *[cw for cw, _ in cast_jobs])
    return outs[:4], outs[4:]


def _attn_proj_kernel(v_group, x_ref, nw_ref, w_ref, qn_ref, kn_ref, grp_ref, qt_ref, k_ref, vt_ref):
    h = _rms(x_ref[...], nw_ref[...]).astype(BF16)
    grp = grp_ref[...]
    groups_per_chunk = COL_TILE // v_group
    chunks_per_part = D_MODEL // COL_TILE
    n_chunks = 3 * chunks_per_part

    def project(n):
        return _dot(h, w_ref[:, n * COL_TILE:(n + 1) * COL_TILE])

    acc_next = project(0)
    for n in range(n_chunks):
        acc = acc_next
        if n + 1 < n_chunks:
            acc_next = project(n + 1)
        part, c = divmod(n, chunks_per_part)
        cols = slice(c * COL_TILE, (c + 1) * COL_TILE)
        if part < 2:
            hw_ref = qn_ref if part == 0 else kn_ref
            ms = _dot((acc * acc).astype(BF16), grp)
            acc = acc * lax.rsqrt(ms + EPS) * hw_ref[:, cols]
        if part == 0:
            qt_ref[cols, :] = acc.T.astype(BF16)
        elif part == 1:
            k_ref[:, cols] = acc.astype(BF16)
        else:
            acc_t = acc.T.astype(BF16)
            for p in range(groups_per_chunk):
                row0 = (c * groups_per_chunk + p) * (v_group + ONES_ROWS)
                vt_ref[row0:row0 + v_group, :] = acc_t[p * v_group:(p + 1) * v_group, :]
                vt_ref[row0 + v_group:row0 + v_group + ONES_ROWS, :] = jnp.ones((ONES_ROWS, acc_t.shape[1]), BF16)


def _attn_proj(x, nw, w, layer, qn, kn, grp, v_group):
    t = x.shape[0]
    row = lambda i: (i, 0)
    col = lambda i: (0, i)
    const = lambda i: (0, 0)
    vt_rows = (D_MODEL // v_group) * (v_group + ONES_ROWS)
    return pl.pallas_call(
        functools.partial(_attn_proj_kernel, v_group),
        grid=(t // ROW_TILE,),
        in_specs=[
            pl.BlockSpec((ROW_TILE, D_MODEL), row),
            pl.BlockSpec((1, D_MODEL), const),
            pl.BlockSpec((None, D_MODEL, 3 * D_MODEL), lambda i: (layer, 0, 0)),
            pl.BlockSpec((1, D_MODEL), const),
            pl.BlockSpec((1, D_MODEL), const),
            pl.BlockSpec((COL_TILE, COL_TILE), const),
        ],
        out_specs=[
            pl.BlockSpec((D_MODEL, ROW_TILE), col),
            pl.BlockSpec((ROW_TILE, D_MODEL), row),
            pl.BlockSpec((vt_rows, ROW_TILE), col),
        ],
        out_shape=[
            jax.ShapeDtypeStruct((D_MODEL, t), BF16),
            jax.ShapeDtypeStruct((t, D_MODEL), BF16),
            jax.ShapeDtypeStruct((vt_rows, t), BF16),
        ],
        compiler_params=_params("parallel"),
        name="attn_proj",
    )(x, nw, w, qn, kn, grp)


def _ret_core_kernel(q_ref, k_ref, v_ref, dm_ref, rs_ref, ks_ref, cd_ref, o_ref, state_ref, raw_ref):
    c_len = RET_CHUNK
    n_chunks = q_ref.shape[0] // c_len
    state_ref[...] = jnp.zeros_like(state_ref)
    raw_ref[...] = jnp.zeros_like(raw_ref)

    def chunk(c):
        return pl.ds(pl.multiple_of(c * c_len, c_len), c_len)

    def masked_scores(c):
        return (_dot_nt(q_ref[chunk(c), :], k_ref[chunk(c), :]) * dm_ref[0]).astype(BF16)

    def normalise(c, slot):
        o = raw_ref[slot]
        o_ref[chunk(c), :] = (o * lax.rsqrt(jnp.mean(o * o, axis=-1, keepdims=True) + EPS)).astype(BF16)

    def step(c, slot, s_cur):
        sl = chunk(c)
        q = q_ref[sl, :]
        k = k_ref[sl, :]
        v = v_ref[sl, :]
        state = state_ref[...]
        kd = (k.astype(F32) * ks_ref[0]).astype(BF16)
        kv = _dot_tn(kd, v)
        inner = _dot(s_cur, v)
        cross = _dot(q, state.astype(BF16))
        s_next = masked_scores(jnp.minimum(c + 1, n_chunks - 1))
        normalise(jnp.maximum(c - 1, 0), 1 - slot)
        raw_ref[slot] = inner + cross * rs_ref[0]
        state_ref[...] = state * cd_ref[0] + kv
        return s_next

    def body(u, s_cur):
        for r in range(RET_UNROLL):
            s_cur = step(RET_UNROLL * u + r, r % 2, s_cur)
        return s_cur

    lax.fori_loop(0, n_chunks // RET_UNROLL, body, masked_scores(0))
    normalise(n_chunks - 1, 1)


def _ret_core(q, k, v, dm, rs, ks, cd, batch, seq):
    t = q.shape[0]
    c_len = RET_CHUNK
    tok = lambda b, h: (b, h)
    head3 = lambda b, h: (h, 0, 0)
    return pl.pallas_call(
        _ret_core_kernel,
        grid=(batch, RET_HEADS),
        in_specs=[
            pl.BlockSpec((seq, RET_DK), tok),
            pl.BlockSpec((seq, RET_DK), tok),
            pl.BlockSpec((seq, RET_DV), tok),
            pl.BlockSpec((1, c_len, c_len), head3),
            pl.BlockSpec((1, c_len, 1), head3),
            pl.BlockSpec((1, c_len, 1), head3),
            pl.BlockSpec((1, 1, 1), head3),
        ],
        out_specs=pl.BlockSpec((seq, RET_DV), tok),
        out_shape=jax.ShapeDtypeStruct((t, RET_HEADS * RET_DV), BF16),
        scratch_shapes=[pltpu.VMEM((RET_DK, RET_DV), F32), pltpu.VMEM((2, RET_CHUNK, RET_DV), F32)],
        compiler_params=_params("parallel", "parallel"),
        name="ret_core",
    )(q, k, v, dm, rs, ks, cd)


def _out_mlp_kernel(gated, n_cast, *refs):
    refs, cast_in, cast_out = _split_cast_refs(refs, 1, n_cast)
    _run_casts(cast_in, cast_out)
    if gated:
        a_ref, g_ref, wo_ref, x_ref, nw_ref, wu_ref, wd_ref, o_ref = refs
        n_chunks = a_ref.shape[1] // GATE_TILE

        def gate(c):
            cols = slice(c * GATE_TILE, (c + 1) * GATE_TILE)
            g = g_ref[:, cols].astype(F32)
            return (g * (1.0 / (1.0 + jnp.exp(-g))) * a_ref[:, cols].astype(F32)).astype(BF16)

        a_next = gate(0)
        mix = None
        for c in range(n_chunks):
            a = a_next
            if c + 1 < n_chunks:
                a_next = gate(c + 1)
            part = _dot(a, wo_ref[c * GATE_TILE:(c + 1) * GATE_TILE, :])
            mix = part if mix is None else mix + part
    else:
        a_ref, wo_ref, x_ref, nw_ref, wu_ref, wd_ref, o_ref = refs
        mix = _dot_tn(a_ref[...], wo_ref[...])
    x = x_ref[...] + mix
    h = _rms(x, nw_ref[...]).astype(BF16)
    acc = x
    for c in range(D_FF // FF_TILE):
        u = _dot(h, wu_ref[:, c * FF_TILE:(c + 1) * FF_TILE])
        u = jnp.maximum(u, 0.0)
        acc = acc + _dot((u * u).astype(BF16), wd_ref[c * FF_TILE:(c + 1) * FF_TILE, :])
    o_ref[...] = acc


def _out_mlp(a, g, wo, mixer_layer, x, nw, wu, wd, layer, cast_jobs=()):
    t = x.shape[0]
    steps = t // ROW_TILE
    cast_in_specs, cast_out_specs, cast_shapes = _cast_specs(cast_jobs, steps)
    ka = wo.shape[1]
    row = lambda i: (i, 0)
    const = lambda i: (0, 0)
    gated = g is not None
    if gated:
        mix_specs = [pl.BlockSpec((ROW_TILE, ka), row), pl.BlockSpec((ROW_TILE, ka), row)]
        mix_args = [a, g]
    else:
        mix_specs = [pl.BlockSpec((ka, ROW_TILE), lambda i: (0, i))]
        mix_args = [a]
    outs = pl.pallas_call(
        functools.partial(_out_mlp_kernel, gated, len(cast_jobs)),
        grid=(steps,),
        in_specs=mix_specs + [
            pl.BlockSpec((None, ka, D_MODEL), lambda i: (mixer_layer, 0, 0)),
            pl.BlockSpec((ROW_TILE, D_MODEL), row),
            pl.BlockSpec((1, D_MODEL), const),
            pl.BlockSpec((None, D_MODEL, D_FF), lambda i: (layer, 0, 0), pipeline_mode=pl.Buffered(1)),
            pl.BlockSpec((None, D_FF, D_MODEL), lambda i: (layer, 0, 0), pipeline_mode=pl.Buffered(1)),
        ] + cast_in_specs,
        out_specs=[pl.BlockSpec((ROW_TILE, D_MODEL), row)] + cast_out_specs,
        out_shape=[jax.ShapeDtypeStruct((t, D_MODEL), F32)] + cast_shapes,
        compiler_params=_params("parallel"),
        name="out_mlp_gated" if gated else "out_mlp",
    )(*mix_args, wo, x, nw, wu, wd, *[w for w, _ in cast_jobs])
    return outs[0], outs[1:]


def _bias_tiles_kernel(bucket_ranges, rb_ref, idx_ref, o_ref):
    head = pl.program_id(0)
    for d, (lo, hi) in enumerate(bucket_ranges):
        idx = idx_ref[d]
        acc = jnp.full(idx.shape, NEG, F32)
        for b in range(lo, hi + 1):
            acc = jnp.where(idx == b, rb_ref[b, head] * LOG2E, acc)
        o_ref[0, d] = acc


def _bucket_ranges():
    ranges = []
    for d in range(BIAS_TILES):
        lo_dist = max(d * ATT_TILE - (ATT_TILE - 1), 0)
        hi_dist = d * ATT_TILE + (ATT_TILE - 1)

        def bucket(n):
            if n < REL_MAX_EXACT:
                return n
            return min(REL_MAX_EXACT + int(math.log(n / REL_MAX_EXACT) / math.log(REL_MAX_DISTANCE / REL_MAX_EXACT)
                                           * (REL_BUCKETS - REL_MAX_EXACT)), REL_BUCKETS - 1)

        ranges.append((max(bucket(lo_dist) - 1, 0), min(bucket(hi_dist) + 1, REL_BUCKETS - 1)))
    return tuple(ranges)


def _bias_tiles(rel_bias, bucket_idx):
    nd, tq, tk = bucket_idx.shape
    return pl.pallas_call(
        functools.partial(_bias_tiles_kernel, _bucket_ranges()),
        grid=(ATTN_HEADS,),
        in_specs=[
            pl.BlockSpec(memory_space=pltpu.SMEM),
            pl.BlockSpec((nd, tq, tk), lambda h: (0, 0, 0)),
        ],
        out_specs=pl.BlockSpec((1, nd, tq, tk), lambda h: (h, 0, 0, 0)),
        out_shape=jax.ShapeDtypeStruct((ATTN_HEADS, nd, tq, tk), F32),
        compiler_params=_params("parallel"),
        name="bias_tiles",
    )(rel_bias, bucket_idx)


def _rel_bucket(dist):
    n = jnp.maximum(dist, 0)
    nf = jnp.maximum(n, 1).astype(F32)
    large = REL_MAX_EXACT + (jnp.log(nf / REL_MAX_EXACT) / math.log(REL_MAX_DISTANCE / REL_MAX_EXACT)
                             * (REL_BUCKETS - REL_MAX_EXACT)).astype(jnp.int32)
    large = jnp.minimum(large, REL_BUCKETS - 1)
    return jnp.where(n < REL_MAX_EXACT, n, large)


def _bucket_index_tiles():
    r = np.arange(ATT_TILE)
    dist = (np.arange(BIAS_TILES)[:, None, None] * ATT_TILE + r[None, None, :] - r[None, :, None])
    dist = jnp.asarray(dist, jnp.int32)
    return jnp.where(dist >= 0, _rel_bucket(dist), REL_BUCKETS).astype(jnp.int32)


TAB_Q, TAB_K, TAB_DELTA, TAB_FIRST, TAB_ACC = range(5)
PIPE_LAG = 2
PIPE_UNROLL = 6


def _tile_schedule(nq, own_first):
    rows = []
    for i in range(nq):
        keys = ([i] + list(range(i))) if own_first else list(range(i + 1))
        for n, j in enumerate(keys):
            rows.append((i, j, min(i - j, BIAS_TILES - 1), int(n == 0), i))
    n_iters = -(-(len(rows) + PIPE_LAG) // PIPE_UNROLL) * PIPE_UNROLL
    idle = (0, 0, 0, 1, nq)
    cols = [idle] * PIPE_LAG + rows
    cols += [idle] * (n_iters + PIPE_LAG - len(cols))
    return jnp.asarray(np.array(cols, np.int32).T), n_iters


def _tile_slice(idx):
    return pl.ds(pl.multiple_of(idx * ATT_TILE, ATT_TILE), ATT_TILE)


def _head_row_mask(h):
    row = lax.broadcasted_iota(jnp.int32, (LANES, 1), 0)
    return (row >= h * HEAD_DIM) & (row < (h + 1) * HEAD_DIM)


def _pair_rows(pair, n=LANES):
    return slice(pair * n, (pair + 1) * n)


def _attn_kernel(kind, lambda_init, n_iters, tab_ref, qt_ref, k_ref, vt_ref, bias_ref, *rest):
    if kind == "moba":
        blk_ref, ot_ref, qts_ref, vts_ref, s_ref, p_ref, acc_ref, neg_ref = rest
    else:
        lam_ref, sw_ref, ot_ref, qts_ref, vts_ref, s_ref, p_ref, acc_ref = rest
    heads = range(HEADS_PER_STEP)
    seq = qt_ref.shape[1]
    part_lanes = LANES if kind == "moba" else ATT_TILE
    query_parts = tuple(slice(c, c + part_lanes) for c in range(0, ATT_TILE, part_lanes))
    acc_rows = acc_ref.shape[2]

    n_tiles = seq // ATT_TILE
    q_heads = []
    for e in heads:
        pair, h = divmod(e, HEADS_PER_PAIR)
        qf = qt_ref[_pair_rows(pair), :].astype(F32)
        q_heads.append(jnp.where(_head_row_mask(h), qf, 0.0).astype(BF16))
        for i in range(n_tiles):
            qts_ref[i, e] = q_heads[e][:, i * ATT_TILE:(i + 1) * ATT_TILE]
    for j in range(n_tiles):
        vts_ref[j] = vt_ref[:, j * ATT_TILE:(j + 1) * ATT_TILE]
    s_ref[...] = jnp.zeros_like(s_ref)
    p_ref[...] = jnp.zeros_like(p_ref)
    acc_ref[...] = jnp.zeros_like(acc_ref)

    if kind == "moba":
        nb = blk_ref.shape[0]
        kmean = _dot(blk_ref[...], k_ref[...])
        km_hi = kmean.astype(BF16)
        km_lo = (kmean - km_hi.astype(F32)).astype(BF16)
        blk_id = lax.broadcasted_iota(jnp.int32, (nb, ATT_TILE), 0)
        for e in heads:
            pair = e // HEADS_PER_PAIR
            q_e = q_heads[e]
            gate_all = _dot(km_hi[:, _pair_rows(pair)], q_e) + _dot(km_lo[:, _pair_rows(pair)], q_e)
            for i in range(n_tiles):
                cols = slice(i * ATT_TILE, (i + 1) * ATT_TILE)
                past = blk_id < i
                gate = jnp.where(past, gate_all[:, cols], NEG)
                rank = jnp.zeros((nb, ATT_TILE), jnp.int32)
                for c in range(i):
                    gc = gate[c:c + 1, :]
                    tie = jnp.where(blk_id > c, 1, 0)
                    rank = rank + jnp.where(gc > gate, 1, jnp.where(gc == gate, tie, 0))
                chosen = jnp.where(rank < MOBA_TOPK, jnp.where(past, 1, 0), 0)
                keep = jnp.maximum(chosen, jnp.where(blk_id == i, 1, 0))
                neg_ref[e, :, cols] = jnp.where(keep == 1, 0.0, NEG)

    def step(t, cur, carry):
        nxt = 1 - cur
        m_prev, alpha_prev, mtile_prev = carry
        ic = tab_ref[TAB_ACC, t]
        jc = tab_ref[TAB_K, t]
        ib = tab_ref[TAB_Q, t + 1]
        jb = tab_ref[TAB_K, t + 1]
        first = tab_ref[TAB_FIRST, t + 1] != 0
        ia = tab_ref[TAB_Q, t + 2]
        ja = tab_ref[TAB_K, t + 2]
        da = tab_ref[TAB_DELTA, t + 2]
        m_new, alpha_new, mtile_new = [], [], []
        for e in heads:
            pair = e // HEADS_PER_PAIR

            def stage_a():
                kt = k_ref[_tile_slice(ja), _pair_rows(pair)]
                sb = _dot(kt, qts_ref[ia, e]) + bias_ref[e, da]
                s_ref[nxt, e] = sb
                mtile_new.append(tuple(jnp.max(sb[:, lanes], axis=0, keepdims=True) for lanes in query_parts))

            def stage_c():
                vt = vts_ref[jc, _pair_rows(e if kind == "moba" else pair, acc_rows), :]
                pv = _dot(vt, p_ref[cur, e])
                for part, lanes in enumerate(query_parts):
                    acc_ref[ic, e, :, lanes] = alpha_prev[e][part] * acc_ref[ic, e, :, lanes] + pv[:, lanes]

            for stage in ((stage_a, stage_c) if kind == "moba" else (stage_c, stage_a)):
                stage()

            m_parts, alpha_parts = [], []
            if kind == "moba":
                neg_row = neg_ref[e, pl.ds(jb, 1), _tile_slice(ib)]
            for part, lanes in enumerate(query_parts):
                s = s_ref[cur, e, :, lanes]
                m_in = jnp.where(first, -jnp.inf, m_prev[e][part])
                m_tile = mtile_prev[e][part]
                if kind == "moba":
                    neg = neg_row[:, lanes]
                    m_e = jnp.maximum(m_in, m_tile + neg)
                    shift = m_e - neg
                else:
                    m_e = jnp.maximum(m_in, m_tile)
                    shift = m_e
                p_ref[nxt, e, :, lanes] = jnp.exp2(s - shift).astype(BF16)
                alpha_parts.append(jnp.exp2(m_in - m_e))
                m_parts.append(m_e)
            alpha_new.append(tuple(alpha_parts))
            m_new.append(tuple(m_parts))

        return tuple(m_new), tuple(alpha_new), tuple(mtile_new)

    zeros = tuple(tuple(jnp.zeros((1, part_lanes), F32) for _ in query_parts) for _ in heads)

    def body(u, carry):
        for r in range(PIPE_UNROLL):
            carry = step(PIPE_UNROLL * u + r, r % 2, carry)
        return carry

    lax.fori_loop(0, n_iters // PIPE_UNROLL, body, (zeros, zeros, zeros))

    def emit(i, carry):
        cols = _tile_slice(i)
        v_group = acc_ref.shape[2] - ONES_ROWS
        for pair in range(PAIRS_PER_STEP):
            outs = []
            for h in range(HEADS_PER_PAIR):
                e = pair * HEADS_PER_PAIR + h
                outs.append(acc_ref[i, e, :v_group, :] * (1.0 / acc_ref[i, e, v_group:v_group + 1, :]))
            if kind == "moba":
                o = jnp.concatenate(outs, axis=0)
            else:
                lam = lam_ref[...]
                lam_full = (jnp.exp(jnp.sum(lam[0:1] * lam[1:2], axis=-1, keepdims=True))
                            - jnp.exp(jnp.sum(lam[2:3] * lam[3:4], axis=-1, keepdims=True)) + lambda_init)
                o = outs[0] - lam_full * outs[1]
                o = (o * lax.rsqrt(jnp.mean(o * o, axis=0, keepdims=True) + EPS)
                     * sw_ref[...] * (1.0 - lambda_init))
            ot_ref[_pair_rows(pair), cols] = o.astype(BF16)
        return carry

    lax.fori_loop(0, seq // ATT_TILE, emit, 0)


def _attention(kind, qt, k, vt_ext, bias, extras, lambda_init, batch, seq):
    v_group = _value_group(kind)
    acc_rows = v_group + ONES_ROWS
    vt_block_rows = acc_rows * (LANES * PAIRS_PER_STEP // v_group)
    t = k.shape[0]
    nq = seq // ATT_TILE
    tab, n_iters = _tile_schedule(nq, own_first=(kind == "moba"))
    groups = ATTN_HEADS // HEADS_PER_STEP
    rows = LANES * PAIRS_PER_STEP
    in_specs = [
        pl.BlockSpec(memory_space=pltpu.SMEM),
        pl.BlockSpec((rows, seq), lambda g, b: (g, b)),
        pl.BlockSpec((seq, rows), lambda g, b: (b, g)),
        pl.BlockSpec((vt_block_rows, seq), lambda g, b: (g, b)),
        pl.BlockSpec((HEADS_PER_STEP, BIAS_TILES, ATT_TILE, ATT_TILE), lambda g, b: (g, 0, 0, 0)),
    ]
    scratch = [
        pltpu.VMEM((nq, HEADS_PER_STEP, LANES, ATT_TILE), BF16),
        pltpu.VMEM((nq, vt_block_rows, ATT_TILE), BF16),
        pltpu.VMEM((2, HEADS_PER_STEP, ATT_TILE, ATT_TILE), F32),
        pltpu.VMEM((2, HEADS_PER_STEP, ATT_TILE, ATT_TILE), BF16),
        pltpu.VMEM((nq + 1, HEADS_PER_STEP, acc_rows, ATT_TILE), F32),
    ]
    if kind == "moba":
        (blk,) = extras
        nb = blk.shape[0]
        in_specs.append(pl.BlockSpec((nb, seq), lambda g, b: (0, 0)))
        scratch.append(pltpu.VMEM((HEADS_PER_STEP, nb, seq), F32))
    else:
        lam, sw = extras
        in_specs += [
            pl.BlockSpec((4, HEAD_DIM), lambda g, b: (0, 0)),
            pl.BlockSpec((LANES, 1), lambda g, b: (0, 0)),
        ]
    return pl.pallas_call(
        functools.partial(_attn_kernel, kind, lambda_init, n_iters),
        grid=(groups, batch),
        in_specs=in_specs,
        out_specs=pl.BlockSpec((rows, seq), lambda g, b: (g, b)),
        out_shape=jax.ShapeDtypeStruct((D_MODEL, t), BF16),
        scratch_shapes=scratch,
        compiler_params=_params("parallel", "parallel"),
        name=kind + "_attn",
    )(tab, qt, k, vt_ext, bias, *extras)


def _rotary_tables(seq):
    d = RET_DK
    inv_freq = ROPE_BASE ** (-np.arange(0, d, 2, dtype=np.float64) / d)
    ang = np.arange(seq, dtype=np.float64)[:, None] * inv_freq[None, :]
    return jnp.asarray(np.cos(ang), F32), jnp.asarray(np.sin(ang), F32)


def _retention_decay_tables():
    c_len = RET_CHUNK
    log_gamma = np.log(1.0 - 2.0 ** (-5.0 - np.arange(RET_HEADS, dtype=np.float64)))
    pos = np.arange(c_len, dtype=np.float64)
    rel = pos[:, None] - pos[None, :]
    dm = np.where(rel >= 0, np.exp(np.maximum(rel, 0.0)[None] * log_gamma[:, None, None]), 0.0)
    rs = np.exp((pos + 1.0)[None, :] * log_gamma[:, None])[:, :, None]
    ks = np.exp((c_len - 1.0 - pos)[None, :] * log_gamma[:, None])[:, :, None]
    cd = np.exp(c_len * log_gamma)[:, None, None]
    return tuple(jnp.asarray(a, F32) for a in (dm, rs, ks, cd))


def _block_mean_matrix(seq):
    nb = seq // MOBA_BLOCK
    m = (np.arange(seq)[None, :] // MOBA_BLOCK == np.arange(nb)[:, None]) / float(MOBA_BLOCK)
    return jnp.asarray(m, BF16)


def _head_group_matrix():
    g = np.arange(COL_TILE)[:, None] // HEAD_DIM == np.arange(COL_TILE)[None, :] // HEAD_DIM
    return jnp.asarray(g / float(HEAD_DIM), BF16)


def kernel(x, rel_bias, norm1, norm2, w_up, w_down, ret_w_in, ret_w_out,
           moba_w_in, moba_q_norm, moba_k_norm, moba_w_out,
           diff_w_in, diff_q_norm, diff_k_norm, diff_lambda, diff_subln, diff_w_out):
    batch, seq, d = x.shape
    depth = norm1.shape[0]
    assert d == D_MODEL and seq % ROW_TILE == 0 and seq % ATT_TILE == 0 and seq % (RET_UNROLL * RET_CHUNK) == 0
    assert seq % MOBA_BLOCK == 0 and MOBA_BLOCK == ATT_TILE
    t = batch * seq
    xf = x.reshape(t, d)

    bias = _bias_tiles(rel_bias.astype(F32), _bucket_index_tiles())
    grp = _head_group_matrix()
    q_scale = HEAD_DIM ** -0.5 * LOG2E
    mixer_weights = ((ret_w_in, ret_w_out), (moba_w_in, moba_w_out), (diff_w_in, diff_w_out))

    def layer_weights_f32(i):
        kind, j = i % N_MIXERS, i // N_MIXERS
        w_in, w_out = mixer_weights[kind]
        return [(w_in, j), (w_out, j), (w_up, i), (w_down, i)]

    first_jobs = layer_weights_f32(0)
    w_in = first_jobs[0][0][0:1].astype(BF16)
    w_out = wu = wd = None

    for i in range(depth):
        kind, j = i % N_MIXERS, i // N_MIXERS
        nw1 = norm1[i].reshape(1, d)
        nw2 = norm2[i].reshape(1, d)
        cast_jobs = layer_weights_f32(i + 1) if i + 1 < depth else ()
        if kind == 0:
            cos, sin = _rotary_tables(seq)
            dm, rs, ks, cd = _retention_decay_tables()
            (q, k, v, g), early = _ret_proj(xf, nw1, w_in, 0, cos, sin, seq, first_jobs[1:] if i == 0 else ())
            if early:
                w_out, wu, wd = early
            o = _ret_core(q, k, v, dm, rs, ks, cd, batch, seq)
            xf, nxt = _out_mlp(o, g, w_out, 0, xf, nw2, wu, wd, 0, cast_jobs)
        elif kind == 1:
            qn = (jnp.tile(moba_q_norm[j], ATTN_HEADS) * q_scale).reshape(1, d)
            kn = jnp.tile(moba_k_norm[j], ATTN_HEADS).reshape(1, d)
            qt, k, vt = _attn_proj(xf, nw1, w_in, 0, qn, kn, grp, _value_group("moba"))
            ot = _attention("moba", qt, k, vt, bias, (_block_mean_matrix(seq),), 0.0, batch, seq)
            xf, nxt = _out_mlp(ot, None, w_out, 0, xf, nw2, wu, wd, 0, cast_jobs)
        else:
            lambda_init = 0.8 - 0.6 * math.exp(-0.3 * i)
            qn = (jnp.tile(diff_q_norm[j], ATTN_HEADS) * q_scale).reshape(1, d)
            kn = jnp.tile(diff_k_norm[j], ATTN_HEADS).reshape(1, d)
            qt, k, vt = _attn_proj(xf, nw1, w_in, 0, qn, kn, grp, _value_group("diff"))
            extras = (diff_lambda[j].astype(F32), diff_subln[j].reshape(LANES, 1))
            ot = _attention("diff", qt, k, vt, bias, extras, lambda_init, batch, seq)
            xf, nxt = _out_mlp(ot, None, w_out, 0, xf, nw2, wu, wd, 0, cast_jobs)
        if nxt:
            w_in, w_out, wu, wd = nxt
    return xf.reshape(batch, seq, d)
```

```python
import functools
import math

import numpy as np
import jax
import jax.numpy as jnp
from jax import lax
from jax.experimental import pallas as pl
from jax.experimental.pallas import tpu as pltpu

F32 = jnp.float32
BF16 = jnp.bfloat16

D_MODEL = 1024
N_MIXERS = 3
RET_HEADS = 4
RET_DK = D_MODEL // RET_HEADS
RET_DV = 2 * RET_DK
ROPE_BASE = 10000.0
ATTN_HEADS = 16
HEAD_DIM = D_MODEL // ATTN_HEADS
MOBA_BLOCK = 256
MOBA_TOPK = 3
REL_BUCKETS = 32
REL_MAX_EXACT = REL_BUCKETS // 2
REL_MAX_DISTANCE = 1024
D_FF = 4 * D_MODEL
EPS = 1e-6
NEG = -1e30
LOG2E = math.log2(math.e)

LANES = 128
BF16_SUBLANES = 16
VMEM_LIMIT_BYTES = 56 * 1024 * 1024

ROW_TILE = 512
COL_TILE = 256
FF_TILE = 1024
GATE_TILE = 512
RET_CHUNK = 256
RET_UNROLL = 4
ATT_TILE = 256
BIAS_TILES = 6
HEADS_PER_PAIR = LANES // HEAD_DIM
PAIRS_PER_STEP = 2
HEADS_PER_STEP = HEADS_PER_PAIR * PAIRS_PER_STEP
ONES_ROWS = BF16_SUBLANES


def _value_group(kind):
    return HEAD_DIM if kind == "moba" else LANES


def _params(*sem):
    return pltpu.CompilerParams(dimension_semantics=sem, vmem_limit_bytes=VMEM_LIMIT_BYTES)


def _rms(xf, w):
    ms = jnp.mean(xf * xf, axis=-1, keepdims=True)
    return xf * lax.rsqrt(ms + EPS) * w


def _dot(a, b):
    return jnp.dot(a, b, preferred_element_type=F32)


def _dot_nt(a, b):
    return lax.dot_general(a, b, (((1,), (1,)), ((), ())), preferred_element_type=F32)


def _dot_tn(a, b):
    return lax.dot_general(a, b, (((0,), (0,)), ((), ())), preferred_element_type=F32)


def _cast_specs(cast_jobs, steps):
    in_specs, out_specs, shapes = [], [], []
    for w, w_layer in cast_jobs:
        _, rows, cols = w.shape
        slab = rows // steps
        assert slab * steps == rows and slab % BF16_SUBLANES == 0
        in_specs.append(pl.BlockSpec((None, slab, cols), lambda i, w_layer=w_layer: (w_layer, i, 0)))
        out_specs.append(pl.BlockSpec((None, slab, cols), lambda i: (0, i, 0)))
        shapes.append(jax.ShapeDtypeStruct((1, rows, cols), BF16))
    return in_specs, out_specs, shapes


def _split_cast_refs(refs, n_outputs, n_cast):
    n_in = len(refs) - n_outputs - 2 * n_cast
    sources = refs[n_in:n_in + n_cast]
    outputs = refs[n_in + n_cast:n_in + n_cast + n_outputs]
    return refs[:n_in] + outputs, sources, refs[n_in + n_cast + n_outputs:]


def _run_casts(sources, destinations):
    for src, dst in zip(sources, destinations):
        dst[...] = src[...].astype(BF16)


def _ret_proj_kernel(n_cast, *refs):
    refs, cast_in, cast_out = _split_cast_refs(refs, 4, n_cast)
    _run_casts(cast_in, cast_out)
    x_ref, nw_ref, w_ref, cos_ref, sin_ref, q_ref, k_ref, v_ref, g_ref = refs
    h = _rms(x_ref[...], nw_ref[...]).astype(BF16)
    cos = cos_ref[...]
    sin = sin_ref[...]
    half = RET_DK // 2
    chunks = []
    for out_ref, scale in ((q_ref, 1.0), (k_ref, RET_DK ** -0.5)):
        chunks += [(out_ref, hd * RET_DK, scale) for hd in range(RET_HEADS)]
    for out_ref in (v_ref, g_ref):
        chunks += [(out_ref, c * COL_TILE, None) for c in range(RET_HEADS * RET_DV // COL_TILE)]
    assert RET_DK == COL_TILE

    def project(n):
        return _dot(h, w_ref[:, n * COL_TILE:(n + 1) * COL_TILE])

    acc_next = project(0)
    for n, (out_ref, col0, scale) in enumerate(chunks):
        acc = acc_next
        if n + 1 < len(chunks):
            acc_next = project(n + 1)
        if scale is None:
            out_ref[:, col0:col0 + COL_TILE] = acc.astype(BF16)
        else:
            x1 = acc[:, :half]
            x2 = acc[:, half:]
            r1 = x1 * cos - x2 * sin
            r2 = x1 * sin + x2 * cos
            if scale != 1.0:
                r1 = r1 * scale
                r2 = r2 * scale
            out_ref[:, col0:col0 + half] = r1.astype(BF16)
            out_ref[:, col0 + half:col0 + RET_DK] = r2.astype(BF16)


def _ret_proj(x, nw, w, layer, cos, sin, seq, cast_jobs=()):
    t = x.shape[0]
    cast_in_specs, cast_out_specs, cast_shapes = _cast_specs(cast_jobs, t // ROW_TILE)
    n_in = w.shape[2]
    tiles_per_seq = seq // ROW_TILE
    row = lambda i: (i, 0)
    const = lambda i: (0, 0)
    pos = lambda i: (i % tiles_per_seq, 0)
    n_qk = RET_HEADS * RET_DK
    n_v = RET_HEADS * RET_DV
    outs = pl.pallas_call(
        functools.partial(_ret_proj_kernel, len(cast_jobs)),
        grid=(t // ROW_TILE,),
        in_specs=[
            pl.BlockSpec((ROW_TILE, D_MODEL), row),
            pl.BlockSpec((1, D_MODEL), const),
            pl.BlockSpec((None, D_MODEL, n_in), lambda i: (layer, 0, 0)),
            pl.BlockSpec((ROW_TILE, RET_DK // 2), pos),
            pl.BlockSpec((ROW_TILE, RET_DK // 2), pos),
        ] + cast_in_specs,
        out_specs=[
            pl.BlockSpec((ROW_TILE, n_qk), row),
            pl.BlockSpec((ROW_TILE, n_qk), row),
            pl.BlockSpec((ROW_TILE, n_v), row),
            pl.BlockSpec((ROW_TILE, n_v), row),
        ] + cast_out_specs,
        out_shape=[
            jax.ShapeDtypeStruct((t, n_qk), BF16),
            jax.ShapeDtypeStruct((t, n_qk), BF16),
            jax.ShapeDtypeStruct((t, n_v), BF16),
            jax.ShapeDtypeStruct((t, n_v), BF16),
        ] + cast_shapes,
        compiler_params=_params("parallel"),
        name="ret_proj",
    )(x, nw, w, cos, sin, *[cw for cw, _ in cast_jobs])
    return outs[:4], outs[4:]


def _attn_proj_kernel(v_group, x_ref, nw_ref, w_ref, qn_ref, kn_ref, grp_ref, qt_ref, k_ref, vt_ref):
    h = _rms(x_ref[...], nw_ref[...]).astype(BF16)
    grp = grp_ref[...]
    groups_per_chunk = COL_TILE // v_group
    chunks_per_part = D_MODEL // COL_TILE
    n_chunks = 3 * chunks_per_part

    def project(n):
        return _dot(h, w_ref[:, n * COL_TILE:(n + 1) * COL_TILE])

    acc_next = project(0)
    for n in range(n_chunks):
        acc = acc_next
        if n + 1 < n_chunks:
            acc_next = project(n + 1)
        part, c = divmod(n, chunks_per_part)
        cols = slice(c * COL_TILE, (c + 1) * COL_TILE)
        if part < 2:
            hw_ref = qn_ref if part == 0 else kn_ref
            ms = _dot((acc * acc).astype(BF16), grp)
            acc = acc * lax.rsqrt(ms + EPS) * hw_ref[:, cols]
        if part == 0:
            qt_ref[cols, :] = acc.T.astype(BF16)
        elif part == 1:
            k_ref[:, cols] = acc.astype(BF16)
        else:
            acc_t = acc.T.astype(BF16)
            for p in range(groups_per_chunk):
                row0 = (c * groups_per_chunk + p) * (v_group + ONES_ROWS)
                vt_ref[row0:row0 + v_group, :] = acc_t[p * v_group:(p + 1) * v_group, :]
                vt_ref[row0 + v_group:row0 + v_group + ONES_ROWS, :] = jnp.ones((ONES_ROWS, acc_t.shape[1]), BF16)


def _attn_proj(x, nw, w, layer, qn, kn, grp, v_group):
    t = x.shape[0]
    row = lambda i: (i, 0)
    col = lambda i: (0, i)
    const = lambda i: (0, 0)
    vt_rows = (D_MODEL // v_group) * (v_group + ONES_ROWS)
    return pl.pallas_call(
        functools.partial(_attn_proj_kernel, v_group),
        grid=(t // ROW_TILE,),
        in_specs=[
            pl.BlockSpec((ROW_TILE, D_MODEL), row),
            pl.BlockSpec((1, D_MODEL), const),
            pl.BlockSpec((None, D_MODEL, 3 * D_MODEL), lambda i: (layer, 0, 0)),
            pl.BlockSpec((1, D_MODEL), const),
            pl.BlockSpec((1, D_MODEL), const),
            pl.BlockSpec((COL_TILE, COL_TILE), const),
        ],
        out_specs=[
            pl.BlockSpec((D_MODEL, ROW_TILE), col),
            pl.BlockSpec((ROW_TILE, D_MODEL), row),
            pl.BlockSpec((vt_rows, ROW_TILE), col),
        ],
        out_shape=[
            jax.ShapeDtypeStruct((D_MODEL, t), BF16),
            jax.ShapeDtypeStruct((t, D_MODEL), BF16),
            jax.ShapeDtypeStruct((vt_rows, t), BF16),
        ],
        compiler_params=_params("parallel"),
        name="attn_proj",
    )(x, nw, w, qn, kn, grp)


def _ret_core_kernel(q_ref, k_ref, v_ref, dm_ref, rs_ref, ks_ref, cd_ref, o_ref, state_ref, raw_ref):
    c_len = RET_CHUNK
    n_chunks = q_ref.shape[0] // c_len
    state_ref[...] = jnp.zeros_like(state_ref)
    raw_ref[...] = jnp.zeros_like(raw_ref)

    def chunk(c):
        return pl.ds(pl.multiple_of(c * c_len, c_len), c_len)

    def masked_scores(c):
        return (_dot_nt(q_ref[chunk(c), :], k_ref[chunk(c), :]) * dm_ref[0]).astype(BF16)

    def normalise(c, slot):
        o = raw_ref[slot]
        o_ref[chunk(c), :] = (o * lax.rsqrt(jnp.mean(o * o, axis=-1, keepdims=True) + EPS)).astype(BF16)

    def step(c, slot, s_cur):
        sl = chunk(c)
        q = q_ref[sl, :]
        k = k_ref[sl, :]
        v = v_ref[sl, :]
        state = state_ref[...]
        kd = (k.astype(F32) * ks_ref[0]).astype(BF16)
        kv = _dot_tn(kd, v)
        inner = _dot(s_cur, v)
        cross = _dot(q, state.astype(BF16))
        s_next = masked_scores(jnp.minimum(c + 1, n_chunks - 1))
        normalise(jnp.maximum(c - 1, 0), 1 - slot)
        raw_ref[slot] = inner + cross * rs_ref[0]
        state_ref[...] = state * cd_ref[0] + kv
        return s_next

    def body(u, s_cur):
        for r in range(RET_UNROLL):
            s_cur = step(RET_UNROLL * u + r, r % 2, s_cur)
        return s_cur

    lax.fori_loop(0, n_chunks // RET_UNROLL, body, masked_scores(0))
    normalise(n_chunks - 1, 1)


def _ret_core(q, k, v, dm, rs, ks, cd, batch, seq):
    t = q.shape[0]
    c_len = RET_CHUNK
    tok = lambda b, h: (b, h)
    head3 = lambda b, h: (h, 0, 0)
    return pl.pallas_call(
        _ret_core_kernel,
        grid=(batch, RET_HEADS),
        in_specs=[
            pl.BlockSpec((seq, RET_DK), tok),
            pl.BlockSpec((seq, RET_DK), tok),
            pl.BlockSpec((seq, RET_DV), tok),
            pl.BlockSpec((1, c_len, c_len), head3),
            pl.BlockSpec((1, c_len, 1), head3),
            pl.BlockSpec((1, c_len, 1), head3),
            pl.BlockSpec((1, 1, 1), head3),
        ],
        out_specs=pl.BlockSpec((seq, RET_DV), tok),
        out_shape=jax.ShapeDtypeStruct((t, RET_HEADS * RET_DV), BF16),
        scratch_shapes=[pltpu.VMEM((RET_DK, RET_DV), F32), pltpu.VMEM((2, RET_CHUNK, RET_DV), F32)],
        compiler_params=_params("parallel", "parallel"),
        name="ret_core",
    )(q, k, v, dm, rs, ks, cd)


def _out_mlp_kernel(gated, n_cast, *refs):
    refs, cast_in, cast_out = _split_cast_refs(refs, 1, n_cast)
    _run_casts(cast_in, cast_out)
    if gated:
        a_ref, g_ref, wo_ref, x_ref, nw_ref, wu_ref, wd_ref, o_ref = refs
        n_chunks = a_ref.shape[1] // GATE_TILE

        def gate(c):
            cols = slice(c * GATE_TILE, (c + 1) * GATE_TILE)
            g = g_ref[:, cols].astype(F32)
            return (g * (1.0 / (1.0 + jnp.exp(-g))) * a_ref[:, cols].astype(F32)).astype(BF16)

        a_next = gate(0)
        mix = None
        for c in range(n_chunks):
            a = a_next
            if c + 1 < n_chunks:
                a_next = gate(c + 1)
            part = _dot(a, wo_ref[c * GATE_TILE:(c + 1) * GATE_TILE, :])
            mix = part if mix is None else mix + part
    else:
        a_ref, wo_ref, x_ref, nw_ref, wu_ref, wd_ref, o_ref = refs
        mix = _dot_tn(a_ref[...], wo_ref[...])
    x = x_ref[...] + mix
    h = _rms(x, nw_ref[...]).astype(BF16)
    acc = x
    for c in range(D_FF // FF_TILE):
        u = _dot(h, wu_ref[:, c * FF_TILE:(c + 1) * FF_TILE])
        u = jnp.maximum(u, 0.0)
        acc = acc + _dot((u * u).astype(BF16), wd_ref[c * FF_TILE:(c + 1) * FF_TILE, :])
    o_ref[...] = acc


def _out_mlp(a, g, wo, mixer_layer, x, nw, wu, wd, layer, cast_jobs=()):
    t = x.shape[0]
    steps = t // ROW_TILE
    cast_in_specs, cast_out_specs, cast_shapes = _cast_specs(cast_jobs, steps)
    ka = wo.shape[1]
    row = lambda i: (i, 0)
    const = lambda i: (0, 0)
    gated = g is not None
    if gated:
        mix_specs = [pl.BlockSpec((ROW_TILE, ka), row), pl.BlockSpec((ROW_TILE, ka), row)]
        mix_args = [a, g]
    else:
        mix_specs = [pl.BlockSpec((ka, ROW_TILE), lambda i: (0, i))]
        mix_args = [a]
    outs = pl.pallas_call(
        functools.partial(_out_mlp_kernel, gated, len(cast_jobs)),
        grid=(steps,),
        in_specs=mix_specs + [
            pl.BlockSpec((None, ka, D_MODEL), lambda i: (mixer_layer, 0, 0)),
            pl.BlockSpec((ROW_TILE, D_MODEL), row),
            pl.BlockSpec((1, D_MODEL), const),
            pl.BlockSpec((None, D_MODEL, D_FF), lambda i: (layer, 0, 0), pipeline_mode=pl.Buffered(1)),
            pl.BlockSpec((None, D_FF, D_MODEL), lambda i: (layer, 0, 0), pipeline_mode=pl.Buffered(1)),
        ] + cast_in_specs,
        out_specs=[pl.BlockSpec((ROW_TILE, D_MODEL), row)] + cast_out_specs,
        out_shape=[jax.ShapeDtypeStruct((t, D_MODEL), F32)] + cast_shapes,
        compiler_params=_params("parallel"),
        name="out_mlp_gated" if gated else "out_mlp",
    )(*mix_args, wo, x, nw, wu, wd, *[w for w, _ in cast_jobs])
    return outs[0], outs[1:]


def _bias_tiles_kernel(bucket_ranges, rb_ref, idx_ref, o_ref):
    head = pl.program_id(0)
    for d, (lo, hi) in enumerate(bucket_ranges):
        idx = idx_ref[d]
        acc = jnp.full(idx.shape, NEG, F32)
        for b in range(lo, hi + 1):
            acc = jnp.where(idx == b, rb_ref[b, head] * LOG2E, acc)
        o_ref[0, d] = acc


def _bucket_ranges():
    ranges = []
    for d in range(BIAS_TILES):
        lo_dist = max(d * ATT_TILE - (ATT_TILE - 1), 0)
        hi_dist = d * ATT_TILE + (ATT_TILE - 1)

        def bucket(n):
            if n < REL_MAX_EXACT:
                return n
            return min(REL_MAX_EXACT + int(math.log(n / REL_MAX_EXACT) / math.log(REL_MAX_DISTANCE / REL_MAX_EXACT)
                                           * (REL_BUCKETS - REL_MAX_EXACT)), REL_BUCKETS - 1)

        ranges.append((max(bucket(lo_dist) - 1, 0), min(bucket(hi_dist) + 1, REL_BUCKETS - 1)))
    return tuple(ranges)


def _bias_tiles(rel_bias, bucket_idx):
    nd, tq, tk = bucket_idx.shape
    return pl.pallas_call(
        functools.partial(_bias_tiles_kernel, _bucket_ranges()),
        grid=(ATTN_HEADS,),
        in_specs=[
            pl.BlockSpec(memory_space=pltpu.SMEM),
            pl.BlockSpec((nd, tq, tk), lambda h: (0, 0, 0)),
        ],
        out_specs=pl.BlockSpec((1, nd, tq, tk), lambda h: (h, 0, 0, 0)),
        out_shape=jax.ShapeDtypeStruct((ATTN_HEADS, nd, tq, tk), F32),
        compiler_params=_params("parallel"),
        name="bias_tiles",
    )(rel_bias, bucket_idx)


def _rel_bucket(dist):
    n = jnp.maximum(dist, 0)
    nf = jnp.maximum(n, 1).astype(F32)
    large = REL_MAX_EXACT + (jnp.log(nf / REL_MAX_EXACT) / math.log(REL_MAX_DISTANCE / REL_MAX_EXACT)
                             * (REL_BUCKETS - REL_MAX_EXACT)).astype(jnp.int32)
    large = jnp.minimum(large, REL_BUCKETS - 1)
    return jnp.where(n < REL_MAX_EXACT, n, large)


def _bucket_index_tiles():
    r = np.arange(ATT_TILE)
    dist = (np.arange(BIAS_TILES)[:, None, None] * ATT_TILE + r[None, None, :] - r[None, :, None])
    dist = jnp.asarray(dist, jnp.int32)
    return jnp.where(dist >= 0, _rel_bucket(dist), REL_BUCKETS).astype(jnp.int32)


TAB_Q, TAB_K, TAB_DELTA, TAB_FIRST, TAB_ACC = range(5)
PIPE_LAG = 2
PIPE_UNROLL = 6


def _tile_schedule(nq, own_first):
    rows = []
    for i in range(nq):
        keys = ([i] + list(range(i))) if own_first else list(range(i + 1))
        for n, j in enumerate(keys):
            rows.append((i, j, min(i - j, BIAS_TILES - 1), int(n == 0), i))
    n_iters = -(-(len(rows) + PIPE_LAG) // PIPE_UNROLL) * PIPE_UNROLL
    idle = (0, 0, 0, 1, nq)
    cols = [idle] * PIPE_LAG + rows
    cols += [idle] * (n_iters + PIPE_LAG - len(cols))
    return jnp.asarray(np.array(cols, np.int32).T), n_iters


def _tile_slice(idx):
    return pl.ds(pl.multiple_of(idx * ATT_TILE, ATT_TILE), ATT_TILE)


def _head_row_mask(h):
    row = lax.broadcasted_iota(jnp.int32, (LANES, 1), 0)
    return (row >= h * HEAD_DIM) & (row < (h + 1) * HEAD_DIM)


def _pair_rows(pair, n=LANES):
    return slice(pair * n, (pair + 1) * n)


def _attn_kernel(kind, lambda_init, n_iters, tab_ref, qt_ref, k_ref, vt_ref, bias_ref, *rest):
    if kind == "moba":
        blk_ref, ot_ref, qts_ref, vts_ref, s_ref, p_ref, acc_ref, neg_ref = rest
    else:
        lam_ref, sw_ref, ot_ref, qts_ref, vts_ref, s_ref, p_ref, acc_ref = rest
    heads = range(HEADS_PER_STEP)
    seq = qt_ref.shape[1]
    part_lanes = LANES if kind == "moba" else ATT_TILE
    query_parts = tuple(slice(c, c + part_lanes) for c in range(0, ATT_TILE, part_lanes))
    acc_rows = acc_ref.shape[2]

    n_tiles = seq // ATT_TILE
    q_heads = []
    for e in heads:
        pair, h = divmod(e, HEADS_PER_PAIR)
        qf = qt_ref[_pair_rows(pair), :].astype(F32)
        q_heads.append(jnp.where(_head_row_mask(h), qf, 0.0).astype(BF16))
        for i in range(n_tiles):
            qts_ref[i, e] = q_heads[e][:, i * ATT_TILE:(i + 1) * ATT_TILE]
    for j in range(n_tiles):
        vts_ref[j] = vt_ref[:, j * ATT_TILE:(j + 1) * ATT_TILE]
    s_ref[...] = jnp.zeros_like(s_ref)
    p_ref[...] = jnp.zeros_like(p_ref)
    acc_ref[...] = jnp.zeros_like(acc_ref)

    if kind == "moba":
        nb = blk_ref.shape[0]
        kmean = _dot(blk_ref[...], k_ref[...])
        km_hi = kmean.astype(BF16)
        km_lo = (kmean - km_hi.astype(F32)).astype(BF16)
        blk_id = lax.broadcasted_iota(jnp.int32, (nb, ATT_TILE), 0)
        for e in heads:
            pair = e // HEADS_PER_PAIR
            q_e = q_heads[e]
            gate_all = _dot(km_hi[:, _pair_rows(pair)], q_e) + _dot(km_lo[:, _pair_rows(pair)], q_e)
            for i in range(n_tiles):
                cols = slice(i * ATT_TILE, (i + 1) * ATT_TILE)
                past = blk_id < i
                gate = jnp.where(past, gate_all[:, cols], NEG)
                rank = jnp.zeros((nb, ATT_TILE), jnp.int32)
                for c in range(i):
                    gc = gate[c:c + 1, :]
                    tie = jnp.where(blk_id > c, 1, 0)
                    rank = rank + jnp.where(gc > gate, 1, jnp.where(gc == gate, tie, 0))
                chosen = jnp.where(rank < MOBA_TOPK, jnp.where(past, 1, 0), 0)
                keep = jnp.maximum(chosen, jnp.where(blk_id == i, 1, 0))
                neg_ref[e, :, cols] = jnp.where(keep == 1, 0.0, NEG)

    def step(t, cur, carry):
        nxt = 1 - cur
        m_prev, alpha_prev, mtile_prev = carry
        ic = tab_ref[TAB_ACC, t]
        jc = tab_ref[TAB_K, t]
        ib = tab_ref[TAB_Q, t + 1]
        jb = tab_ref[TAB_K, t + 1]
        first = tab_ref[TAB_FIRST, t + 1] != 0
        ia = tab_ref[TAB_Q, t + 2]
        ja = tab_ref[TAB_K, t + 2]
        da = tab_ref[TAB_DELTA, t + 2]
        m_new, alpha_new, mtile_new = [], [], []
        for e in heads:
            pair = e // HEADS_PER_PAIR

            def stage_a():
                kt = k_ref[_tile_slice(ja), _pair_rows(pair)]
                sb = _dot(kt, qts_ref[ia, e]) + bias_ref[e, da]
                s_ref[nxt, e] = sb
                mtile_new.append(tuple(jnp.max(sb[:, lanes], axis=0, keepdims=True) for lanes in query_parts))

            def stage_c():
                vt = vts_ref[jc, _pair_rows(e if kind == "moba" else pair, acc_rows), :]
                pv = _dot(vt, p_ref[cur, e])
                for part, lanes in enumerate(query_parts):
                    acc_ref[ic, e, :, lanes] = alpha_prev[e][part] * acc_ref[ic, e, :, lanes] + pv[:, lanes]

            for stage in (stage_a, stage_c):
                stage()

            m_parts, alpha_parts = [], []
            if kind == "moba":
                neg_row = neg_ref[e, pl.ds(jb, 1), _tile_slice(ib)]
            for part, lanes in enumerate(query_parts):
                s = s_ref[cur, e, :, lanes]
                m_in = jnp.where(first, -jnp.inf, m_prev[e][part])
                m_tile = mtile_prev[e][part]
                if kind == "moba":
                    neg = neg_row[:, lanes]
                    m_e = jnp.maximum(m_in, m_tile + neg)
                    shift = m_e - neg
                else:
                    m_e = jnp.maximum(m_in, m_tile)
                    shift = m_e
                p_ref[nxt, e, :, lanes] = jnp.exp2(s - shift).astype(BF16)
                alpha_parts.append(jnp.exp2(m_in - m_e))
                m_parts.append(m_e)
            alpha_new.append(tuple(alpha_parts))
            m_new.append(tuple(m_parts))

        return tuple(m_new), tuple(alpha_new), tuple(mtile_new)

    zeros = tuple(tuple(jnp.zeros((1, part_lanes), F32) for _ in query_parts) for _ in heads)

    def body(u, carry):
        for r in range(PIPE_UNROLL):
            carry = step(PIPE_UNROLL * u + r, r % 2, carry)
        return carry

    lax.fori_loop(0, n_iters // PIPE_UNROLL, body, (zeros, zeros, zeros))

    def emit(i, carry):
        cols = _tile_slice(i)
        v_group = acc_ref.shape[2] - ONES_ROWS
        for pair in range(PAIRS_PER_STEP):
            outs = []
            for h in range(HEADS_PER_PAIR):
                e = pair * HEADS_PER_PAIR + h
                outs.append(acc_ref[i, e, :v_group, :] * (1.0 / acc_ref[i, e, v_group:v_group + 1, :]))
            if kind == "moba":
                o = jnp.concatenate(outs, axis=0)
            else:
                lam = lam_ref[...]
                lam_full = (jnp.exp(jnp.sum(lam[0:1] * lam[1:2], axis=-1, keepdims=True))
                            - jnp.exp(jnp.sum(lam[2:3] * lam[3:4], axis=-1, keepdims=True)) + lambda_init)
                o = outs[0] - lam_full * outs[1]
                o = (o * lax.rsqrt(jnp.mean(o * o, axis=0, keepdims=True) + EPS)
                     * sw_ref[...] * (1.0 - lambda_init))
            ot_ref[_pair_rows(pair), cols] = o.astype(BF16)
        return carry

    lax.fori_loop(0, seq // ATT_TILE, emit, 0)


def _attention(kind, qt, k, vt_ext, bias, extras, lambda_init, batch, seq):
    v_group = _value_group(kind)
    acc_rows = v_group + ONES_ROWS
    vt_block_rows = acc_rows * (LANES * PAIRS_PER_STEP // v_group)
    t = k.shape[0]
    nq = seq // ATT_TILE
    tab, n_iters = _tile_schedule(nq, own_first=(kind == "moba"))
    groups = ATTN_HEADS // HEADS_PER_STEP
    rows = LANES * PAIRS_PER_STEP
    in_specs = [
        pl.BlockSpec(memory_space=pltpu.SMEM),
        pl.BlockSpec((rows, seq), lambda g, b: (g, b)),
        pl.BlockSpec((seq, rows), lambda g, b: (b, g)),
        pl.BlockSpec((vt_block_rows, seq), lambda g, b: (g, b)),
        pl.BlockSpec((HEADS_PER_STEP, BIAS_TILES, ATT_TILE, ATT_TILE), lambda g, b: (g, 0, 0, 0)),
    ]
    scratch = [
        pltpu.VMEM((nq, HEADS_PER_STEP, LANES, ATT_TILE), BF16),
        pltpu.VMEM((nq, vt_block_rows, ATT_TILE), BF16),
        pltpu.VMEM((2, HEADS_PER_STEP, ATT_TILE, ATT_TILE), F32),
        pltpu.VMEM((2, HEADS_PER_STEP, ATT_TILE, ATT_TILE), BF16),
        pltpu.VMEM((nq + 1, HEADS_PER_STEP, acc_rows, ATT_TILE), F32),
    ]
    if kind == "moba":
        (blk,) = extras
        nb = blk.shape[0]
        in_specs.append(pl.BlockSpec((nb, seq), lambda g, b: (0, 0)))
        scratch.append(pltpu.VMEM((HEADS_PER_STEP, nb, seq), F32))
    else:
        lam, sw = extras
        in_specs += [
            pl.BlockSpec((4, HEAD_DIM), lambda g, b: (0, 0)),
            pl.BlockSpec((LANES, 1), lambda g, b: (0, 0)),
        ]
    return pl.pallas_call(
        functools.partial(_attn_kernel, kind, lambda_init, n_iters),
        grid=(groups, batch),
        in_specs=in_specs,
        out_specs=pl.BlockSpec((rows, seq), lambda g, b: (g, b)),
        out_shape=jax.ShapeDtypeStruct((D_MODEL, t), BF16),
        scratch_shapes=scratch,
        compiler_params=_params("parallel", "parallel"),
        name=kind + "_attn",
    )(tab, qt, k, vt_ext, bias, *extras)


def _rotary_tables(seq):
    d = RET_DK
    inv_freq = ROPE_BASE ** (-np.arange(0, d, 2, dtype=np.float64) / d)
    ang = np.arange(seq, dtype=np.float64)[:, None] * inv_freq[None, :]
    return jnp.asarray(np.cos(ang), F32), jnp.asarray(np.sin(ang), F32)


def _retention_decay_tables():
    c_len = RET_CHUNK
    log_gamma = np.log(1.0 - 2.0 ** (-5.0 - np.arange(RET_HEADS, dtype=np.float64)))
    pos = np.arange(c_len, dtype=np.float64)
    rel = pos[:, None] - pos[None, :]
    dm = np.where(rel >= 0, np.exp(np.maximum(rel, 0.0)[None] * log_gamma[:, None, None]), 0.0)
    rs = np.exp((pos + 1.0)[None, :] * log_gamma[:, None])[:, :, None]
    ks = np.exp((c_len - 1.0 - pos)[None, :] * log_gamma[:, None])[:, :, None]
    cd = np.exp(c_len * log_gamma)[:, None, None]
    return tuple(jnp.asarray(a, F32) for a in (dm, rs, ks, cd))


def _block_mean_matrix(seq):
    nb = seq // MOBA_BLOCK
    m = (np.arange(seq)[None, :] // MOBA_BLOCK == np.arange(nb)[:, None]) / float(MOBA_BLOCK)
    return jnp.asarray(m, BF16)


def _head_group_matrix():
    g = np.arange(COL_TILE)[:, None] // HEAD_DIM == np.arange(COL_TILE)[None, :] // HEAD_DIM
    return jnp.asarray(g / float(HEAD_DIM), BF16)


def kernel(x, rel_bias, norm1, norm2, w_up, w_down, ret_w_in, ret_w_out,
           moba_w_in, moba_q_norm, moba_k_norm, moba_w_out,
           diff_w_in, diff_q_norm, diff_k_norm, diff_lambda, diff_subln, diff_w_out):
    batch, seq, d = x.shape
    depth = norm1.shape[0]
    assert d == D_MODEL and seq % ROW_TILE == 0 and seq % ATT_TILE == 0 and seq % (RET_UNROLL * RET_CHUNK) == 0
    assert seq % MOBA_BLOCK == 0 and MOBA_BLOCK == ATT_TILE
    t = batch * seq
    xf = x.reshape(t, d)

    bias = _bias_tiles(rel_bias.astype(F32), _bucket_index_tiles())
    grp = _head_group_matrix()
    q_scale = HEAD_DIM ** -0.5 * LOG2E
    mixer_weights = ((ret_w_in, ret_w_out), (moba_w_in, moba_w_out), (diff_w_in, diff_w_out))

    def layer_weights_f32(i):
        kind, j = i % N_MIXERS, i // N_MIXERS
        w_in, w_out = mixer_weights[kind]
        return [(w_in, j), (w_out, j), (w_up, i), (w_down, i)]

    first_jobs = layer_weights_f32(0)
    w_in = first_jobs[0][0][0:1].astype(BF16)
    w_out = wu = wd = None

    for i in range(depth):
        kind, j = i % N_MIXERS, i // N_MIXERS
        nw1 = norm1[i].reshape(1, d)
        nw2 = norm2[i].reshape(1, d)
        cast_jobs = layer_weights_f32(i + 1) if i + 1 < depth else ()
        if kind == 0:
            cos, sin = _rotary_tables(seq)
            dm, rs, ks, cd = _retention_decay_tables()
            (q, k, v, g), early = _ret_proj(xf, nw1, w_in, 0, cos, sin, seq, first_jobs[1:] if i == 0 else ())
            if early:
                w_out, wu, wd = early
            o = _ret_core(q, k, v, dm, rs, ks, cd, batch, seq)
            xf, nxt = _out_mlp(o, g, w_out, 0, xf, nw2, wu, wd, 0, cast_jobs)
        elif kind == 1:
            qn = (jnp.tile(moba_q_norm[j], ATTN_HEADS) * q_scale).reshape(1, d)
            kn = jnp.tile(moba_k_norm[j], ATTN_HEADS).reshape(1, d)
            qt, k, vt = _attn_proj(xf, nw1, w_in, 0, qn, kn, grp, _value_group("moba"))
            ot = _attention("moba", qt, k, vt, bias, (_block_mean_matrix(seq),), 0.0, batch, seq)
            xf, nxt = _out_mlp(ot, None, w_out, 0, xf, nw2, wu, wd, 0, cast_jobs)
        else:
            lambda_init = 0.8 - 0.6 * math.exp(-0.3 * i)
            qn = (jnp.tile(diff_q_norm[j], ATTN_HEADS) * q_scale).reshape(1, d)
            kn = jnp.tile(diff_k_norm[j], ATTN_HEADS).reshape(1, d)
            qt, k, vt = _attn_proj(xf, nw1, w_in, 0, qn, kn, grp, _value_group("diff"))
            extras = (diff_lambda[j].astype(F32), diff_subln[j].reshape(LANES, 1))
            ot = _attention("diff", qt, k, vt, bias, extras, lambda_init, batch, seq)
            xf, nxt = _out_mlp(ot, None, w_out, 0, xf, nw2, wu, wd, 0, cast_jobs)
        if nxt:
            w_in, w_out, wu, wd = nxt
    return xf.reshape(batch, seq, d)
```

```python
import functools
import math

import numpy as np
import jax
import jax.numpy as jnp
from jax import lax
from jax.experimental import pallas as pl
from jax.experimental.pallas import tpu as pltpu

F32 = jnp.float32
BF16 = jnp.bfloat16

D_MODEL = 1024
N_MIXERS = 3
RET_HEADS = 4
RET_DK = D_MODEL // RET_HEADS
RET_DV = 2 * RET_DK
ROPE_BASE = 10000.0
ATTN_HEADS = 16
HEAD_DIM = D_MODEL // ATTN_HEADS
MOBA_BLOCK = 256
MOBA_TOPK = 3
REL_BUCKETS = 32
REL_MAX_EXACT = REL_BUCKETS // 2
REL_MAX_DISTANCE = 1024
D_FF = 4 * D_MODEL
EPS = 1e-6
NEG = -1e30
LOG2E = math.log2(math.e)

LANES = 128
BF16_SUBLANES = 16
VMEM_LIMIT_BYTES = 56 * 1024 * 1024

ROW_TILE = 512
COL_TILE = 256
FF_TILE = 1024
GATE_TILE = 512
RET_CHUNK = 256
RET_UNROLL = 4
ATT_TILE = 256
BIAS_TILES = 6
HEADS_PER_PAIR = LANES // HEAD_DIM
PAIRS_PER_STEP = 2
HEADS_PER_STEP = HEADS_PER_PAIR * PAIRS_PER_STEP
HEAD_ORDER = tuple(p * HEADS_PER_PAIR + h for h in range(HEADS_PER_PAIR) for p in range(PAIRS_PER_STEP))
ONES_ROWS = BF16_SUBLANES


def _value_group(kind):
    return HEAD_DIM if kind == "moba" else LANES


def _params(*sem):
    return pltpu.CompilerParams(dimension_semantics=sem, vmem_limit_bytes=VMEM_LIMIT_BYTES)


def _rms(xf, w):
    ms = jnp.mean(xf * xf, axis=-1, keepdims=True)
    return xf * lax.rsqrt(ms + EPS) * w


def _dot(a, b):
    return jnp.dot(a, b, preferred_element_type=F32)


def _dot_nt(a, b):
    return lax.dot_general(a, b, (((1,), (1,)), ((), ())), preferred_element_type=F32)


def _dot_tn(a, b):
    return lax.dot_general(a, b, (((0,), (0,)), ((), ())), preferred_element_type=F32)


def _cast_specs(cast_jobs, steps):
    in_specs, out_specs, shapes = [], [], []
    for w, w_layer in cast_jobs:
        _, rows, cols = w.shape
        slab = rows // steps
        assert slab * steps == rows and slab % BF16_SUBLANES == 0
        in_specs.append(pl.BlockSpec((None, slab, cols), lambda i, w_layer=w_layer: (w_layer, i, 0)))
        out_specs.append(pl.BlockSpec((None, slab, cols), lambda i: (0, i, 0)))
        shapes.append(jax.ShapeDtypeStruct((1, rows, cols), BF16))
    return in_specs, out_specs, shapes


def _split_cast_refs(refs, n_outputs, n_cast):
    n_in = len(refs) - n_outputs - 2 * n_cast
    sources = refs[n_in:n_in + n_cast]
    outputs = refs[n_in + n_cast:n_in + n_cast + n_outputs]
    return refs[:n_in] + outputs, sources, refs[n_in + n_cast + n_outputs:]


def _run_casts(sources, destinations):
    for src, dst in zip(sources, destinations):
        dst[...] = src[...].astype(BF16)


def _ret_proj_kernel(n_cast, *refs):
    refs, cast_in, cast_out = _split_cast_refs(refs, 4, n_cast)
    _run_casts(cast_in, cast_out)
    x_ref, nw_ref, w_ref, cos_ref, sin_ref, q_ref, k_ref, v_ref, g_ref = refs
    h = _rms(x_ref[...], nw_ref[...]).astype(BF16)
    cos = cos_ref[...]
    sin = sin_ref[...]
    half = RET_DK // 2
    chunks = []
    for out_ref, scale in ((q_ref, 1.0), (k_ref, RET_DK ** -0.5)):
        chunks += [(out_ref, hd * RET_DK, scale) for hd in range(RET_HEADS)]
    for out_ref in (v_ref, g_ref):
        chunks += [(out_ref, c * COL_TILE, None) for c in range(RET_HEADS * RET_DV // COL_TILE)]
    assert RET_DK == COL_TILE

    def project(n):
        return _dot(h, w_ref[:, n * COL_TILE:(n + 1) * COL_TILE])

    acc_next = project(0)
    for n, (out_ref, col0, scale) in enumerate(chunks):
        acc = acc_next
        if n + 1 < len(chunks):
            acc_next = project(n + 1)
        if scale is None:
            out_ref[:, col0:col0 + COL_TILE] = acc.astype(BF16)
        else:
            x1 = acc[:, :half]
            x2 = acc[:, half:]
            r1 = x1 * cos - x2 * sin
            r2 = x1 * sin + x2 * cos
            if scale != 1.0:
                r1 = r1 * scale
                r2 = r2 * scale
            out_ref[:, col0:col0 + half] = r1.astype(BF16)
            out_ref[:, col0 + half:col0 + RET_DK] = r2.astype(BF16)


def _ret_proj(x, nw, w, layer, cos, sin, seq, cast_jobs=()):
    t = x.shape[0]
    cast_in_specs, cast_out_specs, cast_shapes = _cast_specs(cast_jobs, t // ROW_TILE)
    n_in = w.shape[2]
    tiles_per_seq = seq // ROW_TILE
    row = lambda i: (i, 0)
    const = lambda i: (0, 0)
    pos = lambda i: (i % tiles_per_seq, 0)
    n_qk = RET_HEADS * RET_DK
    n_v = RET_HEADS * RET_DV
    outs = pl.pallas_call(
        functools.partial(_ret_proj_kernel, len(cast_jobs)),
        grid=(t // ROW_TILE,),
        in_specs=[
            pl.BlockSpec((ROW_TILE, D_MODEL), row),
            pl.BlockSpec((1, D_MODEL), const),
            pl.BlockSpec((None, D_MODEL, n_in), lambda i: (layer, 0, 0)),
            pl.BlockSpec((ROW_TILE, RET_DK // 2), pos),
            pl.BlockSpec((ROW_TILE, RET_DK // 2), pos),
        ] + cast_in_specs,
        out_specs=[
            pl.BlockSpec((ROW_TILE, n_qk), row),
            pl.BlockSpec((ROW_TILE, n_qk), row),
            pl.BlockSpec((ROW_TILE, n_v), row),
            pl.BlockSpec((ROW_TILE, n_v), row),
        ] + cast_out_specs,
        out_shape=[
            jax.ShapeDtypeStruct((t, n_qk), BF16),
            jax.ShapeDtypeStruct((t, n_qk), BF16),
            jax.ShapeDtypeStruct((t, n_v), BF16),
            jax.ShapeDtypeStruct((t, n_v), BF16),
        ] + cast_shapes,
        compiler_params=_params("parallel"),
        name="ret_proj",
    )(x, nw, w, cos, sin, *[cw for cw, _ in cast_jobs])
    return outs[:4], outs[4:]


def _attn_proj_kernel(v_group, x_ref, nw_ref, w_ref, qn_ref, kn_ref, grp_ref, qt_ref, k_ref, vt_ref):
    h = _rms(x_ref[...], nw_ref[...]).astype(BF16)
    grp = grp_ref[...]
    groups_per_chunk = COL_TILE // v_group
    chunks_per_part = D_MODEL // COL_TILE
    n_chunks = 3 * chunks_per_part

    def project(n):
        return _dot(h, w_ref[:, n * COL_TILE:(n + 1) * COL_TILE])

    acc_next = project(0)
    for n in range(n_chunks):
        acc = acc_next
        if n + 1 < n_chunks:
            acc_next = project(n + 1)
        part, c = divmod(n, chunks_per_part)
        cols = slice(c * COL_TILE, (c + 1) * COL_TILE)
        if part < 2:
            hw_ref = qn_ref if part == 0 else kn_ref
            ms = _dot((acc * acc).astype(BF16), grp)
            acc = acc * lax.rsqrt(ms + EPS) * hw_ref[:, cols]
        if part == 0:
            qt_ref[cols, :] = acc.T.astype(BF16)
        elif part == 1:
            k_ref[:, cols] = acc.astype(BF16)
        else:
            acc_t = acc.T.astype(BF16)
            for p in range(groups_per_chunk):
                row0 = (c * groups_per_chunk + p) * (v_group + ONES_ROWS)
                vt_ref[row0:row0 + v_group, :] = acc_t[p * v_group:(p + 1) * v_group, :]
                vt_ref[row0 + v_group:row0 + v_group + ONES_ROWS, :] = jnp.ones((ONES_ROWS, acc_t.shape[1]), BF16)


def _attn_proj(x, nw, w, layer, qn, kn, grp, v_group):
    t = x.shape[0]
    row = lambda i: (i, 0)
    col = lambda i: (0, i)
    const = lambda i: (0, 0)
    vt_rows = (D_MODEL // v_group) * (v_group + ONES_ROWS)
    return pl.pallas_call(
        functools.partial(_attn_proj_kernel, v_group),
        grid=(t // ROW_TILE,),
        in_specs=[
            pl.BlockSpec((ROW_TILE, D_MODEL), row),
            pl.BlockSpec((1, D_MODEL), const),
            pl.BlockSpec((None, D_MODEL, 3 * D_MODEL), lambda i: (layer, 0, 0)),
            pl.BlockSpec((1, D_MODEL), const),
            pl.BlockSpec((1, D_MODEL), const),
            pl.BlockSpec((COL_TILE, COL_TILE), const),
        ],
        out_specs=[
            pl.BlockSpec((D_MODEL, ROW_TILE), col),
            pl.BlockSpec((ROW_TILE, D_MODEL), row),
            pl.BlockSpec((vt_rows, ROW_TILE), col),
        ],
        out_shape=[
            jax.ShapeDtypeStruct((D_MODEL, t), BF16),
            jax.ShapeDtypeStruct((t, D_MODEL), BF16),
            jax.ShapeDtypeStruct((vt_rows, t), BF16),
        ],
        compiler_params=_params("parallel"),
        name="attn_proj",
    )(x, nw, w, qn, kn, grp)


def _ret_core_kernel(q_ref, k_ref, v_ref, dm_ref, rs_ref, ks_ref, cd_ref, o_ref, state_ref, raw_ref):
    c_len = RET_CHUNK
    n_chunks = q_ref.shape[0] // c_len
    state_ref[...] = jnp.zeros_like(state_ref)
    raw_ref[...] = jnp.zeros_like(raw_ref)

    def chunk(c):
        return pl.ds(pl.multiple_of(c * c_len, c_len), c_len)

    def masked_scores(c):
        return (_dot_nt(q_ref[chunk(c), :], k_ref[chunk(c), :]) * dm_ref[0]).astype(BF16)

    def normalise(c, slot):
        o = raw_ref[slot]
        o_ref[chunk(c), :] = (o * lax.rsqrt(jnp.mean(o * o, axis=-1, keepdims=True) + EPS)).astype(BF16)

    def step(c, slot, s_cur):
        sl = chunk(c)
        q = q_ref[sl, :]
        k = k_ref[sl, :]
        v = v_ref[sl, :]
        state = state_ref[...]
        kd = (k.astype(F32) * ks_ref[0]).astype(BF16)
        kv = _dot_tn(kd, v)
        inner = _dot(s_cur, v)
        cross = _dot(q, state.astype(BF16))
        s_next = masked_scores(jnp.minimum(c + 1, n_chunks - 1))
        normalise(jnp.maximum(c - 1, 0), 1 - slot)
        raw_ref[slot] = inner + cross * rs_ref[0]
        state_ref[...] = state * cd_ref[0] + kv
        return s_next

    def body(u, s_cur):
        for r in range(RET_UNROLL):
            s_cur = step(RET_UNROLL * u + r, r % 2, s_cur)
        return s_cur

    lax.fori_loop(0, n_chunks // RET_UNROLL, body, masked_scores(0))
    normalise(n_chunks - 1, 1)


def _ret_core(q, k, v, dm, rs, ks, cd, batch, seq):
    t = q.shape[0]
    c_len = RET_CHUNK
    tok = lambda b, h: (b, h)
    head3 = lambda b, h: (h, 0, 0)
    return pl.pallas_call(
        _ret_core_kernel,
        grid=(batch, RET_HEADS),
        in_specs=[
            pl.BlockSpec((seq, RET_DK), tok),
            pl.BlockSpec((seq, RET_DK), tok),
            pl.BlockSpec((seq, RET_DV), tok),
            pl.BlockSpec((1, c_len, c_len), head3),
            pl.BlockSpec((1, c_len, 1), head3),
            pl.BlockSpec((1, c_len, 1), head3),
            pl.BlockSpec((1, 1, 1), head3),
        ],
        out_specs=pl.BlockSpec((seq, RET_DV), tok),
        out_shape=jax.ShapeDtypeStruct((t, RET_HEADS * RET_DV), BF16),
        scratch_shapes=[pltpu.VMEM((RET_DK, RET_DV), F32), pltpu.VMEM((2, RET_CHUNK, RET_DV), F32)],
        compiler_params=_params("parallel", "parallel"),
        name="ret_core",
    )(q, k, v, dm, rs, ks, cd)


def _out_mlp_kernel(gated, n_cast, *refs):
    refs, cast_in, cast_out = _split_cast_refs(refs, 1, n_cast)
    _run_casts(cast_in, cast_out)
    if gated:
        a_ref, g_ref, wo_ref, x_ref, nw_ref, wu_ref, wd_ref, o_ref = refs
        n_chunks = a_ref.shape[1] // GATE_TILE

        def gate(c):
            cols = slice(c * GATE_TILE, (c + 1) * GATE_TILE)
            g = g_ref[:, cols].astype(F32)
            return (g * (1.0 / (1.0 + jnp.exp(-g))) * a_ref[:, cols].astype(F32)).astype(BF16)

        a_next = gate(0)
        mix = None
        for c in range(n_chunks):
            a = a_next
            if c + 1 < n_chunks:
                a_next = gate(c + 1)
            part = _dot(a, wo_ref[c * GATE_TILE:(c + 1) * GATE_TILE, :])
            mix = part if mix is None else mix + part
    else:
        a_ref, wo_ref, x_ref, nw_ref, wu_ref, wd_ref, o_ref = refs
        mix = _dot_tn(a_ref[...], wo_ref[...])
    x = x_ref[...] + mix
    h = _rms(x, nw_ref[...]).astype(BF16)
    acc = x
    for c in range(D_FF // FF_TILE):
        u = _dot(h, wu_ref[:, c * FF_TILE:(c + 1) * FF_TILE])
        u = jnp.maximum(u, 0.0)
        acc = acc + _dot((u * u).astype(BF16), wd_ref[c * FF_TILE:(c + 1) * FF_TILE, :])
    o_ref[...] = acc


def _out_mlp(a, g, wo, mixer_layer, x, nw, wu, wd, layer, cast_jobs=()):
    t = x.shape[0]
    steps = t // ROW_TILE
    cast_in_specs, cast_out_specs, cast_shapes = _cast_specs(cast_jobs, steps)
    ka = wo.shape[1]
    row = lambda i: (i, 0)
    const = lambda i: (0, 0)
    gated = g is not None
    if gated:
        mix_specs = [pl.BlockSpec((ROW_TILE, ka), row), pl.BlockSpec((ROW_TILE, ka), row)]
        mix_args = [a, g]
    else:
        mix_specs = [pl.BlockSpec((ka, ROW_TILE), lambda i: (0, i))]
        mix_args = [a]
    outs = pl.pallas_call(
        functools.partial(_out_mlp_kernel, gated, len(cast_jobs)),
        grid=(steps,),
        in_specs=mix_specs + [
            pl.BlockSpec((None, ka, D_MODEL), lambda i: (mixer_layer, 0, 0)),
            pl.BlockSpec((ROW_TILE, D_MODEL), row),
            pl.BlockSpec((1, D_MODEL), const),
            pl.BlockSpec((None, D_MODEL, D_FF), lambda i: (layer, 0, 0), pipeline_mode=pl.Buffered(1)),
            pl.BlockSpec((None, D_FF, D_MODEL), lambda i: (layer, 0, 0), pipeline_mode=pl.Buffered(1)),
        ] + cast_in_specs,
        out_specs=[pl.BlockSpec((ROW_TILE, D_MODEL), row)] + cast_out_specs,
        out_shape=[jax.ShapeDtypeStruct((t, D_MODEL), F32)] + cast_shapes,
        compiler_params=_params("parallel"),
        name="out_mlp_gated" if gated else "out_mlp",
    )(*mix_args, wo, x, nw, wu, wd, *[w for w, _ in cast_jobs])
    return outs[0], outs[1:]


def _bias_tiles_kernel(bucket_ranges, rb_ref, idx_ref, o_ref):
    head = pl.program_id(0)
    for d, (lo, hi) in enumerate(bucket_ranges):
        idx = idx_ref[d]
        acc = jnp.full(idx.shape, NEG, F32)
        for b in range(lo, hi + 1):
            acc = jnp.where(idx == b, rb_ref[b, head] * LOG2E, acc)
        o_ref[0, d] = acc


def _bucket_ranges():
    ranges = []
    for d in range(BIAS_TILES):
        lo_dist = max(d * ATT_TILE - (ATT_TILE - 1), 0)
        hi_dist = d * ATT_TILE + (ATT_TILE - 1)

        def bucket(n):
            if n < REL_MAX_EXACT:
                return n
            return min(REL_MAX_EXACT + int(math.log(n / REL_MAX_EXACT) / math.log(REL_MAX_DISTANCE / REL_MAX_EXACT)
                                           * (REL_BUCKETS - REL_MAX_EXACT)), REL_BUCKETS - 1)

        ranges.append((max(bucket(lo_dist) - 1, 0), min(bucket(hi_dist) + 1, REL_BUCKETS - 1)))
    return tuple(ranges)


def _bias_tiles(rel_bias, bucket_idx):
    nd, tq, tk = bucket_idx.shape
    return pl.pallas_call(
        functools.partial(_bias_tiles_kernel, _bucket_ranges()),
        grid=(ATTN_HEADS,),
        in_specs=[
            pl.BlockSpec(memory_space=pltpu.SMEM),
            pl.BlockSpec((nd, tq, tk), lambda h: (0, 0, 0)),
        ],
        out_specs=pl.BlockSpec((1, nd, tq, tk), lambda h: (h, 0, 0, 0)),
        out_shape=jax.ShapeDtypeStruct((ATTN_HEADS, nd, tq, tk), F32),
        compiler_params=_params("parallel"),
        name="bias_tiles",
    )(rel_bias, bucket_idx)


def _rel_bucket(dist):
    n = jnp.maximum(dist, 0)
    nf = jnp.maximum(n, 1).astype(F32)
    large = REL_MAX_EXACT + (jnp.log(nf / REL_MAX_EXACT) / math.log(REL_MAX_DISTANCE / REL_MAX_EXACT)
                             * (REL_BUCKETS - REL_MAX_EXACT)).astype(jnp.int32)
    large = jnp.minimum(large, REL_BUCKETS - 1)
    return jnp.where(n < REL_MAX_EXACT, n, large)


def _bucket_index_tiles():
    r = np.arange(ATT_TILE)
    dist = (np.arange(BIAS_TILES)[:, None, None] * ATT_TILE + r[None, None, :] - r[None, :, None])
    dist = jnp.asarray(dist, jnp.int32)
    return jnp.where(dist >= 0, _rel_bucket(dist), REL_BUCKETS).astype(jnp.int32)


TAB_Q, TAB_K, TAB_DELTA, TAB_FIRST, TAB_ACC = range(5)
PIPE_LAG = 2
PIPE_UNROLL = 6


def _tile_schedule(nq, own_first):
    rows = []
    for i in range(nq):
        keys = ([i] + list(range(i))) if own_first else list(range(i + 1))
        for n, j in enumerate(keys):
            rows.append((i, j, min(i - j, BIAS_TILES - 1), int(n == 0), i))
    n_iters = -(-(len(rows) + PIPE_LAG) // PIPE_UNROLL) * PIPE_UNROLL
    idle = (0, 0, 0, 1, nq)
    cols = [idle] * PIPE_LAG + rows
    cols += [idle] * (n_iters + PIPE_LAG - len(cols))
    return jnp.asarray(np.array(cols, np.int32).T), n_iters


def _tile_slice(idx):
    return pl.ds(pl.multiple_of(idx * ATT_TILE, ATT_TILE), ATT_TILE)


def _head_row_mask(h):
    row = lax.broadcasted_iota(jnp.int32, (LANES, 1), 0)
    return (row >= h * HEAD_DIM) & (row < (h + 1) * HEAD_DIM)


def _pair_rows(pair, n=LANES):
    return slice(pair * n, (pair + 1) * n)


def _attn_kernel(kind, lambda_init, n_iters, tab_ref, qt_ref, k_ref, vt_ref, bias_ref, *rest):
    if kind == "moba":
        blk_ref, ot_ref, qts_ref, vts_ref, s_ref, p_ref, acc_ref, neg_ref = rest
    else:
        lam_ref, sw_ref, ot_ref, qts_ref, vts_ref, s_ref, p_ref, acc_ref = rest
    heads = range(HEADS_PER_STEP)
    seq = qt_ref.shape[1]
    part_lanes = LANES if kind == "moba" else ATT_TILE
    query_parts = tuple(slice(c, c + part_lanes) for c in range(0, ATT_TILE, part_lanes))
    acc_rows = acc_ref.shape[2]

    n_tiles = seq // ATT_TILE
    q_heads = []
    for e in heads:
        pair, h = divmod(e, HEADS_PER_PAIR)
        qf = qt_ref[_pair_rows(pair), :].astype(F32)
        q_heads.append(jnp.where(_head_row_mask(h), qf, 0.0).astype(BF16))
        for i in range(n_tiles):
            qts_ref[i, e] = q_heads[e][:, i * ATT_TILE:(i + 1) * ATT_TILE]
    for j in range(n_tiles):
        vts_ref[j] = vt_ref[:, j * ATT_TILE:(j + 1) * ATT_TILE]
    s_ref[...] = jnp.zeros_like(s_ref)
    p_ref[...] = jnp.zeros_like(p_ref)
    acc_ref[...] = jnp.zeros_like(acc_ref)

    if kind == "moba":
        nb = blk_ref.shape[0]
        kmean = _dot(blk_ref[...], k_ref[...])
        km_hi = kmean.astype(BF16)
        km_lo = (kmean - km_hi.astype(F32)).astype(BF16)
        blk_id = lax.broadcasted_iota(jnp.int32, (nb, ATT_TILE), 0)
        for e in heads:
            pair = e // HEADS_PER_PAIR
            q_e = q_heads[e]
            gate_all = _dot(km_hi[:, _pair_rows(pair)], q_e) + _dot(km_lo[:, _pair_rows(pair)], q_e)
            for i in range(n_tiles):
                cols = slice(i * ATT_TILE, (i + 1) * ATT_TILE)
                past = blk_id < i
                gate = jnp.where(past, gate_all[:, cols], NEG)
                rank = jnp.zeros((nb, ATT_TILE), jnp.int32)
                for c in range(i):
                    gc = gate[c:c + 1, :]
                    tie = jnp.where(blk_id > c, 1, 0)
                    rank = rank + jnp.where(gc > gate, 1, jnp.where(gc == gate, tie, 0))
                chosen = jnp.where(rank < MOBA_TOPK, jnp.where(past, 1, 0), 0)
                keep = jnp.maximum(chosen, jnp.where(blk_id == i, 1, 0))
                neg_ref[e, :, cols] = jnp.where(keep == 1, 0.0, NEG)

    def step(t, cur, carry):
        nxt = 1 - cur
        m_prev, alpha_prev, mtile_prev = carry
        ic = tab_ref[TAB_ACC, t]
        jc = tab_ref[TAB_K, t]
        ib = tab_ref[TAB_Q, t + 1]
        jb = tab_ref[TAB_K, t + 1]
        first = tab_ref[TAB_FIRST, t + 1] != 0
        ia = tab_ref[TAB_Q, t + 2]
        ja = tab_ref[TAB_K, t + 2]
        da = tab_ref[TAB_DELTA, t + 2]
        m_new, alpha_new, mtile_new = ([None] * HEADS_PER_STEP for _ in range(3))
        for e in HEAD_ORDER:
            pair = e // HEADS_PER_PAIR

            def stage_a():
                kt = k_ref[_tile_slice(ja), _pair_rows(pair)]
                sb = _dot(kt, qts_ref[ia, e]) + bias_ref[e, da]
                s_ref[nxt, e] = sb
                mtile_new[e] = tuple(jnp.max(sb[:, lanes], axis=0, keepdims=True) for lanes in query_parts)

            def stage_c():
                vt = vts_ref[jc, _pair_rows(e if kind == "moba" else pair, acc_rows), :]
                pv = _dot(vt, p_ref[cur, e])
                for part, lanes in enumerate(query_parts):
                    acc_ref[ic, e, :, lanes] = alpha_prev[e][part] * acc_ref[ic, e, :, lanes] + pv[:, lanes]

            for stage in ((stage_a, stage_c) if kind == "moba" else (stage_c, stage_a)):
                stage()

            m_parts, alpha_parts = [], []
            if kind == "moba":
                neg_row = neg_ref[e, pl.ds(jb, 1), _tile_slice(ib)]
            for part, lanes in enumerate(query_parts):
                s = s_ref[cur, e, :, lanes]
                m_in = jnp.where(first, -jnp.inf, m_prev[e][part])
                m_tile = mtile_prev[e][part]
                if kind == "moba":
                    neg = neg_row[:, lanes]
                    m_e = jnp.maximum(m_in, m_tile + neg)
                    shift = m_e - neg
                else:
                    m_e = jnp.maximum(m_in, m_tile)
                    shift = m_e
                p_ref[nxt, e, :, lanes] = jnp.exp2(s - shift).astype(BF16)
                alpha_parts.append(jnp.exp2(m_in - m_e))
                m_parts.append(m_e)
            alpha_new[e] = tuple(alpha_parts)
            m_new[e] = tuple(m_parts)

        return tuple(m_new), tuple(alpha_new), tuple(mtile_new)

    zeros = tuple(tuple(jnp.zeros((1, part_lanes), F32) for _ in query_parts) for _ in heads)

    def body(u, carry):
        for r in range(PIPE_UNROLL):
            carry = step(PIPE_UNROLL * u + r, r % 2, carry)
        return carry

    lax.fori_loop(0, n_iters // PIPE_UNROLL, body, (zeros, zeros, zeros))

    def emit(i, carry):
        cols = _tile_slice(i)
        v_group = acc_ref.shape[2] - ONES_ROWS
        for pair in range(PAIRS_PER_STEP):
            outs = []
            for h in range(HEADS_PER_PAIR):
                e = pair * HEADS_PER_PAIR + h
                outs.append(acc_ref[i, e, :v_group, :] * (1.0 / acc_ref[i, e, v_group:v_group + 1, :]))
            if kind == "moba":
                o = jnp.concatenate(outs, axis=0)
            else:
                lam = lam_ref[...]
                lam_full = (jnp.exp(jnp.sum(lam[0:1] * lam[1:2], axis=-1, keepdims=True))
                            - jnp.exp(jnp.sum(lam[2:3] * lam[3:4], axis=-1, keepdims=True)) + lambda_init)
                o = outs[0] - lam_full * outs[1]
                o = (o * lax.rsqrt(jnp.mean(o * o, axis=0, keepdims=True) + EPS)
                     * sw_ref[...] * (1.0 - lambda_init))
            ot_ref[_pair_rows(pair), cols] = o.astype(BF16)
        return carry

    lax.fori_loop(0, seq // ATT_TILE, emit, 0)


def _attention(kind, qt, k, vt_ext, bias, extras, lambda_init, batch, seq):
    v_group = _value_group(kind)
    acc_rows = v_group + ONES_ROWS
    vt_block_rows = acc_rows * (LANES * PAIRS_PER_STEP // v_group)
    t = k.shape[0]
    nq = seq // ATT_TILE
    tab, n_iters = _tile_schedule(nq, own_first=(kind == "moba"))
    groups = ATTN_HEADS // HEADS_PER_STEP
    rows = LANES * PAIRS_PER_STEP
    in_specs = [
        pl.BlockSpec(memory_space=pltpu.SMEM),
        pl.BlockSpec((rows, seq), lambda g, b: (g, b)),
        pl.BlockSpec((seq, rows), lambda g, b: (b, g)),
        pl.BlockSpec((vt_block_rows, seq), lambda g, b: (g, b)),
        pl.BlockSpec((HEADS_PER_STEP, BIAS_TILES, ATT_TILE, ATT_TILE), lambda g, b: (g, 0, 0, 0)),
    ]
    scratch = [
        pltpu.VMEM((nq, HEADS_PER_STEP, LANES, ATT_TILE), BF16),
        pltpu.VMEM((nq, vt_block_rows, ATT_TILE), BF16),
        pltpu.VMEM((2, HEADS_PER_STEP, ATT_TILE, ATT_TILE), F32),
        pltpu.VMEM((2, HEADS_PER_STEP, ATT_TILE, ATT_TILE), BF16),
        pltpu.VMEM((nq + 1, HEADS_PER_STEP, acc_rows, ATT_TILE), F32),
    ]
    if kind == "moba":
        (blk,) = extras
        nb = blk.shape[0]
        in_specs.append(pl.BlockSpec((nb, seq), lambda g, b: (0, 0)))
        scratch.append(pltpu.VMEM((HEADS_PER_STEP, nb, seq), F32))
    else:
        lam, sw = extras
        in_specs += [
            pl.BlockSpec((4, HEAD_DIM), lambda g, b: (0, 0)),
            pl.BlockSpec((LANES, 1), lambda g, b: (0, 0)),
        ]
    return pl.pallas_call(
        functools.partial(_attn_kernel, kind, lambda_init, n_iters),
        grid=(groups, batch),
        in_specs=in_specs,
        out_specs=pl.BlockSpec((rows, seq), lambda g, b: (g, b)),
        out_shape=jax.ShapeDtypeStruct((D_MODEL, t), BF16),
        scratch_shapes=scratch,
        compiler_params=_params("parallel", "parallel"),
        name=kind + "_attn",
    )(tab, qt, k, vt_ext, bias, *extras)


def _rotary_tables(seq):
    d = RET_DK
    inv_freq = ROPE_BASE ** (-np.arange(0, d, 2, dtype=np.float64) / d)
    ang = np.arange(seq, dtype=np.float64)[:, None] * inv_freq[None, :]
    return jnp.asarray(np.cos(ang), F32), jnp.asarray(np.sin(ang), F32)


def _retention_decay_tables():
    c_len = RET_CHUNK
    log_gamma = np.log(1.0 - 2.0 ** (-5.0 - np.arange(RET_HEADS, dtype=np.float64)))
    pos = np.arange(c_len, dtype=np.float64)
    rel = pos[:, None] - pos[None, :]
    dm = np.where(rel >= 0, np.exp(np.maximum(rel, 0.0)[None] * log_gamma[:, None, None]), 0.0)
    rs = np.exp((pos + 1.0)[None, :] * log_gamma[:, None])[:, :, None]
    ks = np.exp((c_len - 1.0 - pos)[None, :] * log_gamma[:, None])[:, :, None]
    cd = np.exp(c_len * log_gamma)[:, None, None]
    return tuple(jnp.asarray(a, F32) for a in (dm, rs, ks, cd))


def _block_mean_matrix(seq):
    nb = seq // MOBA_BLOCK
    m = (np.arange(seq)[None, :] // MOBA_BLOCK == np.arange(nb)[:, None]) / float(MOBA_BLOCK)
    return jnp.asarray(m, BF16)


def _head_group_matrix():
    g = np.arange(COL_TILE)[:, None] // HEAD_DIM == np.arange(COL_TILE)[None, :] // HEAD_DIM
    return jnp.asarray(g / float(HEAD_DIM), BF16)


def kernel(x, rel_bias, norm1, norm2, w_up, w_down, ret_w_in, ret_w_out,
           moba_w_in, moba_q_norm, moba_k_norm, moba_w_out,
           diff_w_in, diff_q_norm, diff_k_norm, diff_lambda, diff_subln, diff_w_out):
    batch, seq, d = x.shape
    depth = norm1.shape[0]
    assert d == D_MODEL and seq % ROW_TILE == 0 and seq % ATT_TILE == 0 and seq % (RET_UNROLL * RET_CHUNK) == 0
    assert seq % MOBA_BLOCK == 0 and MOBA_BLOCK == ATT_TILE
    t = batch * seq
    xf = x.reshape(t, d)

    bias = _bias_tiles(rel_bias.astype(F32), _bucket_index_tiles())
    grp = _head_group_matrix()
    q_scale = HEAD_DIM ** -0.5 * LOG2E
    mixer_weights = ((ret_w_in, ret_w_out), (moba_w_in, moba_w_out), (diff_w_in, diff_w_out))

    def layer_weights_f32(i):
        kind, j = i % N_MIXERS, i // N_MIXERS
        w_in, w_out = mixer_weights[kind]
        return [(w_in, j), (w_out, j), (w_up, i), (w_down, i)]

    first_jobs = layer_weights_f32(0)
    w_in = first_jobs[0][0][0:1].astype(BF16)
    w_out = wu = wd = None

    for i in range(depth):
        kind, j = i % N_MIXERS, i // N_MIXERS
        nw1 = norm1[i].reshape(1, d)
        nw2 = norm2[i].reshape(1, d)
        cast_jobs = layer_weights_f32(i + 1) if i + 1 < depth else ()
        if kind == 0:
            cos, sin = _rotary_tables(seq)
            dm, rs, ks, cd = _retention_decay_tables()
            (q, k, v, g), early = _ret_proj(xf, nw1, w_in, 0, cos, sin, seq, first_jobs[1:] if i == 0 else ())
            if early:
                w_out, wu, wd = early
            o = _ret_core(q, k, v, dm, rs, ks, cd, batch, seq)
            xf, nxt = _out_mlp(o, g, w_out, 0, xf, nw2, wu, wd, 0, cast_jobs)
        elif kind == 1:
            qn = (jnp.tile(moba_q_norm[j], ATTN_HEADS) * q_scale).reshape(1, d)
            kn = jnp.tile(moba_k_norm[j], ATTN_HEADS).reshape(1, d)
            qt, k, vt = _attn_proj(xf, nw1, w_in, 0, qn, kn, grp, _value_group("moba"))
            ot = _attention("moba", qt, k, vt, bias, (_block_mean_matrix(seq),), 0.0, batch, seq)
            xf, nxt = _out_mlp(ot, None, w_out, 0, xf, nw2, wu, wd, 0, cast_jobs)
        else:
            lambda_init = 0.8 - 0.6 * math.exp(-0.3 * i)
            qn = (jnp.tile(diff_q_norm[j], ATTN_HEADS) * q_scale).reshape(1, d)
            kn = jnp.tile(diff_k_norm[j], ATTN_HEADS).reshape(1, d)
            qt, k, vt = _attn_proj(xf, nw1, w_in, 0, qn, kn, grp, _value_group("diff"))
            extras = (diff_lambda[j].astype(F32), diff_subln[j].reshape(LANES, 1))
            ot = _attention("diff", qt, k, vt, bias, extras, lambda_init, batch, seq)
            xf, nxt = _out_mlp(ot, None, w_out, 0, xf, nw2, wu, wd, 0, cast_jobs)
        if nxt:
            w_in, w_out, wu, wd = nxt
    return xf.reshape(batch, seq, d)
```

```python
import functools
import math

import numpy as np
import jax
import jax.numpy as jnp
from jax import lax
from jax.experimental import pallas as pl
from jax.experimental.pallas import tpu as pltpu

F32 = jnp.float32
BF16 = jnp.bfloat16

D_MODEL = 1024
N_MIXERS = 3
RET_HEADS = 4
RET_DK = D_MODEL // RET_HEADS
RET_DV = 2 * RET_DK
ROPE_BASE = 10000.0
ATTN_HEADS = 16
HEAD_DIM = D_MODEL // ATTN_HEADS
MOBA_BLOCK = 256
MOBA_TOPK = 3
REL_BUCKETS = 32
REL_MAX_EXACT = REL_BUCKETS // 2
REL_MAX_DISTANCE = 1024
D_FF = 4 * D_MODEL
EPS = 1e-6
NEG = -1e30
LOG2E = math.log2(math.e)

LANES = 128
BF16_SUBLANES = 16
VMEM_LIMIT_BYTES = 56 * 1024 * 1024

ROW_TILE = 512
COL_TILE = 256
FF_TILE = 1024
GATE_TILE = 512
RET_CHUNK = 256
RET_UNROLL = 4
ATT_TILE = 256
BIAS_TILES = 6
HEADS_PER_PAIR = LANES // HEAD_DIM
PAIRS_PER_STEP = 2
HEADS_PER_STEP = HEADS_PER_PAIR * PAIRS_PER_STEP
HEAD_ORDER = tuple(p * HEADS_PER_PAIR + h for h in range(HEADS_PER_PAIR) for p in range(PAIRS_PER_STEP))
ONES_ROWS = BF16_SUBLANES


def _value_group(kind):
    return HEAD_DIM if kind == "moba" else LANES


def _params(*sem):
    return pltpu.CompilerParams(dimension_semantics=sem, vmem_limit_bytes=VMEM_LIMIT_BYTES)


def _rms(xf, w):
    ms = jnp.mean(xf * xf, axis=-1, keepdims=True)
    return xf * lax.rsqrt(ms + EPS) * w


def _dot(a, b):
    return jnp.dot(a, b, preferred_element_type=F32)


def _dot_nt(a, b):
    return lax.dot_general(a, b, (((1,), (1,)), ((), ())), preferred_element_type=F32)


def _dot_tn(a, b):
    return lax.dot_general(a, b, (((0,), (0,)), ((), ())), preferred_element_type=F32)


def _cast_specs(cast_jobs, steps):
    in_specs, out_specs, shapes = [], [], []
    for w, w_layer in cast_jobs:
        _, rows, cols = w.shape
        slab = rows // steps
        assert slab * steps == rows and slab % BF16_SUBLANES == 0
        in_specs.append(pl.BlockSpec((None, slab, cols), lambda i, w_layer=w_layer: (w_layer, i, 0)))
        out_specs.append(pl.BlockSpec((None, slab, cols), lambda i: (0, i, 0)))
        shapes.append(jax.ShapeDtypeStruct((1, rows, cols), BF16))
    return in_specs, out_specs, shapes


def _split_cast_refs(refs, n_outputs, n_cast):
    n_in = len(refs) - n_outputs - 2 * n_cast
    sources = refs[n_in:n_in + n_cast]
    outputs = refs[n_in + n_cast:n_in + n_cast + n_outputs]
    return refs[:n_in] + outputs, sources, refs[n_in + n_cast + n_outputs:]


def _run_casts(sources, destinations):
    for src, dst in zip(sources, destinations):
        dst[...] = src[...].astype(BF16)


def _ret_proj_kernel(n_cast, *refs):
    refs, cast_in, cast_out = _split_cast_refs(refs, 4, n_cast)
    _run_casts(cast_in, cast_out)
    x_ref, nw_ref, w_ref, cos_ref, sin_ref, q_ref, k_ref, v_ref, g_ref = refs
    h = _rms(x_ref[...], nw_ref[...]).astype(BF16)
    cos = cos_ref[...]
    sin = sin_ref[...]
    half = RET_DK // 2
    chunks = []
    for out_ref, scale in ((q_ref, 1.0), (k_ref, RET_DK ** -0.5)):
        chunks += [(out_ref, hd * RET_DK, scale) for hd in range(RET_HEADS)]
    for out_ref in (v_ref, g_ref):
        chunks += [(out_ref, c * COL_TILE, None) for c in range(RET_HEADS * RET_DV // COL_TILE)]
    assert RET_DK == COL_TILE

    def project(n):
        return _dot(h, w_ref[:, n * COL_TILE:(n + 1) * COL_TILE])

    acc_next = project(0)
    for n, (out_ref, col0, scale) in enumerate(chunks):
        acc = acc_next
        if n + 1 < len(chunks):
            acc_next = project(n + 1)
        if scale is None:
            out_ref[:, col0:col0 + COL_TILE] = acc.astype(BF16)
        else:
            x1 = acc[:, :half]
            x2 = acc[:, half:]
            r1 = x1 * cos - x2 * sin
            r2 = x1 * sin + x2 * cos
            if scale != 1.0:
                r1 = r1 * scale
                r2 = r2 * scale
            out_ref[:, col0:col0 + half] = r1.astype(BF16)
            out_ref[:, col0 + half:col0 + RET_DK] = r2.astype(BF16)


def _ret_proj(x, nw, w, layer, cos, sin, seq, cast_jobs=()):
    t = x.shape[0]
    cast_in_specs, cast_out_specs, cast_shapes = _cast_specs(cast_jobs, t // ROW_TILE)
    n_in = w.shape[2]
    tiles_per_seq = seq // ROW_TILE
    row = lambda i: (i, 0)
    const = lambda i: (0, 0)
    pos = lambda i: (i % tiles_per_seq, 0)
    n_qk = RET_HEADS * RET_DK
    n_v = RET_HEADS * RET_DV
    outs = pl.pallas_call(
        functools.partial(_ret_proj_kernel, len(cast_jobs)),
        grid=(t // ROW_TILE,),
        in_specs=[
            pl.BlockSpec((ROW_TILE, D_MODEL), row),
            pl.BlockSpec((1, D_MODEL), const),
            pl.BlockSpec((None, D_MODEL, n_in), lambda i: (layer, 0, 0)),
            pl.BlockSpec((ROW_TILE, RET_DK // 2), pos),
            pl.BlockSpec((ROW_TILE, RET_DK // 2), pos),
        ] + cast_in_specs,
        out_specs=[
            pl.BlockSpec((ROW_TILE, n_qk), row),
            pl.BlockSpec((ROW_TILE, n_qk), row),
            pl.BlockSpec((ROW_TILE, n_v), row),
            pl.BlockSpec((ROW_TILE, n_v), row),
        ] + cast_out_specs,
        out_shape=[
            jax.ShapeDtypeStruct((t, n_qk), BF16),
            jax.ShapeDtypeStruct((t, n_qk), BF16),
            jax.ShapeDtypeStruct((t, n_v), BF16),
            jax.ShapeDtypeStruct((t, n_v), BF16),
        ] + cast_shapes,
        compiler_params=_params("parallel"),
        name="ret_proj",
    )(x, nw, w, cos, sin, *[cw for cw, _ in cast_jobs])
    return outs[:4], outs[4:]


def _attn_proj_kernel(v_group, x_ref, nw_ref, w_ref, qn_ref, kn_ref, grp_ref, qt_ref, k_ref, vt_ref):
    h = _rms(x_ref[...], nw_ref[...]).astype(BF16)
    grp = grp_ref[...]
    groups_per_chunk = COL_TILE // v_group
    chunks_per_part = D_MODEL // COL_TILE
    n_chunks = 3 * chunks_per_part

    def project(n):
        return _dot(h, w_ref[:, n * COL_TILE:(n + 1) * COL_TILE])

    acc_next = project(0)
    for n in range(n_chunks):
        acc = acc_next
        if n + 1 < n_chunks:
            acc_next = project(n + 1)
        part, c = divmod(n, chunks_per_part)
        cols = slice(c * COL_TILE, (c + 1) * COL_TILE)
        if part < 2:
            hw_ref = qn_ref if part == 0 else kn_ref
            ms = _dot((acc * acc).astype(BF16), grp)
            acc = acc * lax.rsqrt(ms + EPS) * hw_ref[:, cols]
        if part == 0:
            qt_ref[cols, :] = acc.T.astype(BF16)
        elif part == 1:
            k_ref[:, cols] = acc.astype(BF16)
        else:
            acc_t = acc.T.astype(BF16)
            for p in range(groups_per_chunk):
                row0 = (c * groups_per_chunk + p) * (v_group + ONES_ROWS)
                vt_ref[row0:row0 + v_group, :] = acc_t[p * v_group:(p + 1) * v_group, :]
                vt_ref[row0 + v_group:row0 + v_group + ONES_ROWS, :] = jnp.ones((ONES_ROWS, acc_t.shape[1]), BF16)


def _attn_proj(x, nw, w, layer, qn, kn, grp, v_group):
    t = x.shape[0]
    row = lambda i: (i, 0)
    col = lambda i: (0, i)
    const = lambda i: (0, 0)
    vt_rows = (D_MODEL // v_group) * (v_group + ONES_ROWS)
    return pl.pallas_call(
        functools.partial(_attn_proj_kernel, v_group),
        grid=(t // ROW_TILE,),
        in_specs=[
            pl.BlockSpec((ROW_TILE, D_MODEL), row),
            pl.BlockSpec((1, D_MODEL), const),
            pl.BlockSpec((None, D_MODEL, 3 * D_MODEL), lambda i: (layer, 0, 0)),
            pl.BlockSpec((1, D_MODEL), const),
            pl.BlockSpec((1, D_MODEL), const),
            pl.BlockSpec((COL_TILE, COL_TILE), const),
        ],
        out_specs=[
            pl.BlockSpec((D_MODEL, ROW_TILE), col),
            pl.BlockSpec((ROW_TILE, D_MODEL), row),
            pl.BlockSpec((vt_rows, ROW_TILE), col),
        ],
        out_shape=[
            jax.ShapeDtypeStruct((D_MODEL, t), BF16),
            jax.ShapeDtypeStruct((t, D_MODEL), BF16),
            jax.ShapeDtypeStruct((vt_rows, t), BF16),
        ],
        compiler_params=_params("parallel"),
        name="attn_proj",
    )(x, nw, w, qn, kn, grp)


def _ret_core_kernel(q_ref, k_ref, v_ref, dm_ref, rs_ref, ks_ref, cd_ref, o_ref, state_ref, raw_ref):
    c_len = RET_CHUNK
    n_chunks = q_ref.shape[0] // c_len
    state_ref[...] = jnp.zeros_like(state_ref)
    raw_ref[...] = jnp.zeros_like(raw_ref)

    def chunk(c):
        return pl.ds(pl.multiple_of(c * c_len, c_len), c_len)

    def masked_scores(c):
        return (_dot_nt(q_ref[chunk(c), :], k_ref[chunk(c), :]) * dm_ref[0]).astype(BF16)

    def normalise(c, slot):
        o = raw_ref[slot]
        o_ref[chunk(c), :] = (o * lax.rsqrt(jnp.mean(o * o, axis=-1, keepdims=True) + EPS)).astype(BF16)

    def step(c, slot, s_cur):
        sl = chunk(c)
        q = q_ref[sl, :]
        k = k_ref[sl, :]
        v = v_ref[sl, :]
        state = state_ref[...]
        kd = (k.astype(F32) * ks_ref[0]).astype(BF16)
        kv = _dot_tn(kd, v)
        inner = _dot(s_cur, v)
        cross = _dot(q, state.astype(BF16))
        s_next = masked_scores(jnp.minimum(c + 1, n_chunks - 1))
        normalise(jnp.maximum(c - 1, 0), 1 - slot)
        raw_ref[slot] = inner + cross * rs_ref[0]
        state_ref[...] = state * cd_ref[0] + kv
        return s_next

    def body(u, s_cur):
        for r in range(RET_UNROLL):
            s_cur = step(RET_UNROLL * u + r, r % 2, s_cur)
        return s_cur

    lax.fori_loop(0, n_chunks // RET_UNROLL, body, masked_scores(0))
    normalise(n_chunks - 1, 1)


def _ret_core(q, k, v, dm, rs, ks, cd, batch, seq):
    t = q.shape[0]
    c_len = RET_CHUNK
    tok = lambda b, h: (b, h)
    head3 = lambda b, h: (h, 0, 0)
    return pl.pallas_call(
        _ret_core_kernel,
        grid=(batch, RET_HEADS),
        in_specs=[
            pl.BlockSpec((seq, RET_DK), tok),
            pl.BlockSpec((seq, RET_DK), tok),
            pl.BlockSpec((seq, RET_DV), tok),
            pl.BlockSpec((1, c_len, c_len), head3),
            pl.BlockSpec((1, c_len, 1), head3),
            pl.BlockSpec((1, c_len, 1), head3),
            pl.BlockSpec((1, 1, 1), head3),
        ],
        out_specs=pl.BlockSpec((seq, RET_DV), tok),
        out_shape=jax.ShapeDtypeStruct((t, RET_HEADS * RET_DV), BF16),
        scratch_shapes=[pltpu.VMEM((RET_DK, RET_DV), F32), pltpu.VMEM((2, RET_CHUNK, RET_DV), F32)],
        compiler_params=_params("parallel", "parallel"),
        name="ret_core",
    )(q, k, v, dm, rs, ks, cd)


def _out_mlp_kernel(gated, n_cast, *refs):
    refs, cast_in, cast_out = _split_cast_refs(refs, 1, n_cast)
    _run_casts(cast_in, cast_out)
    if gated:
        a_ref, g_ref, wo_ref, x_ref, nw_ref, wu_ref, wd_ref, o_ref = refs
        n_chunks = a_ref.shape[1] // GATE_TILE

        def gate(c):
            cols = slice(c * GATE_TILE, (c + 1) * GATE_TILE)
            g = g_ref[:, cols].astype(F32)
            return (g * (1.0 / (1.0 + jnp.exp(-g))) * a_ref[:, cols].astype(F32)).astype(BF16)

        a_next = gate(0)
        mix = None
        for c in range(n_chunks):
            a = a_next
            if c + 1 < n_chunks:
                a_next = gate(c + 1)
            part = _dot(a, wo_ref[c * GATE_TILE:(c + 1) * GATE_TILE, :])
            mix = part if mix is None else mix + part
    else:
        a_ref, wo_ref, x_ref, nw_ref, wu_ref, wd_ref, o_ref = refs
        mix = _dot_tn(a_ref[...], wo_ref[...])
    x = x_ref[...] + mix
    h = _rms(x, nw_ref[...]).astype(BF16)
    acc = x
    for c in range(D_FF // FF_TILE):
        u = _dot(h, wu_ref[:, c * FF_TILE:(c + 1) * FF_TILE])
        u = jnp.maximum(u, 0.0)
        acc = acc + _dot((u * u).astype(BF16), wd_ref[c * FF_TILE:(c + 1) * FF_TILE, :])
    o_ref[...] = acc


def _out_mlp(a, g, wo, mixer_layer, x, nw, wu, wd, layer, cast_jobs=()):
    t = x.shape[0]
    steps = t // ROW_TILE
    cast_in_specs, cast_out_specs, cast_shapes = _cast_specs(cast_jobs, steps)
    ka = wo.shape[1]
    row = lambda i: (i, 0)
    const = lambda i: (0, 0)
    gated = g is not None
    if gated:
        mix_specs = [pl.BlockSpec((ROW_TILE, ka), row), pl.BlockSpec((ROW_TILE, ka), row)]
        mix_args = [a, g]
    else:
        mix_specs = [pl.BlockSpec((ka, ROW_TILE), lambda i: (0, i))]
        mix_args = [a]
    outs = pl.pallas_call(
        functools.partial(_out_mlp_kernel, gated, len(cast_jobs)),
        grid=(steps,),
        in_specs=mix_specs + [
            pl.BlockSpec((None, ka, D_MODEL), lambda i: (mixer_layer, 0, 0)),
            pl.BlockSpec((ROW_TILE, D_MODEL), row),
            pl.BlockSpec((1, D_MODEL), const),
            pl.BlockSpec((None, D_MODEL, D_FF), lambda i: (layer, 0, 0), pipeline_mode=pl.Buffered(1)),
            pl.BlockSpec((None, D_FF, D_MODEL), lambda i: (layer, 0, 0), pipeline_mode=pl.Buffered(1)),
        ] + cast_in_specs,
        out_specs=[pl.BlockSpec((ROW_TILE, D_MODEL), row)] + cast_out_specs,
        out_shape=[jax.ShapeDtypeStruct((t, D_MODEL), F32)] + cast_shapes,
        compiler_params=_params("parallel"),
        name="out_mlp_gated" if gated else "out_mlp",
    )(*mix_args, wo, x, nw, wu, wd, *[w for w, _ in cast_jobs])
    return outs[0], outs[1:]


def _bias_tiles_kernel(bucket_ranges, rb_ref, idx_ref, o_ref):
    head = pl.program_id(0)
    for d, (lo, hi) in enumerate(bucket_ranges):
        idx = idx_ref[d]
        acc = jnp.full(idx.shape, NEG, F32)
        for b in range(lo, hi + 1):
            acc = jnp.where(idx == b, rb_ref[b, head] * LOG2E, acc)
        o_ref[0, d] = acc


def _bucket_ranges():
    ranges = []
    for d in range(BIAS_TILES):
        lo_dist = max(d * ATT_TILE - (ATT_TILE - 1), 0)
        hi_dist = d * ATT_TILE + (ATT_TILE - 1)

        def bucket(n):
            if n < REL_MAX_EXACT:
                return n
            return min(REL_MAX_EXACT + int(math.log(n / REL_MAX_EXACT) / math.log(REL_MAX_DISTANCE / REL_MAX_EXACT)
                                           * (REL_BUCKETS - REL_MAX_EXACT)), REL_BUCKETS - 1)

        ranges.append((max(bucket(lo_dist) - 1, 0), min(bucket(hi_dist) + 1, REL_BUCKETS - 1)))
    return tuple(ranges)


def _bias_tiles(rel_bias, bucket_idx):
    nd, tq, tk = bucket_idx.shape
    return pl.pallas_call(
        functools.partial(_bias_tiles_kernel, _bucket_ranges()),
        grid=(ATTN_HEADS,),
        in_specs=[
            pl.BlockSpec(memory_space=pltpu.SMEM),
            pl.BlockSpec((nd, tq, tk), lambda h: (0, 0, 0)),
        ],
        out_specs=pl.BlockSpec((1, nd, tq, tk), lambda h: (h, 0, 0, 0)),
        out_shape=jax.ShapeDtypeStruct((ATTN_HEADS, nd, tq, tk), F32),
        compiler_params=_params("parallel"),
        name="bias_tiles",
    )(rel_bias, bucket_idx)


def _rel_bucket(dist):
    n = jnp.maximum(dist, 0)
    nf = jnp.maximum(n, 1).astype(F32)
    large = REL_MAX_EXACT + (jnp.log(nf / REL_MAX_EXACT) / math.log(REL_MAX_DISTANCE / REL_MAX_EXACT)
                             * (REL_BUCKETS - REL_MAX_EXACT)).astype(jnp.int32)
    large = jnp.minimum(large, REL_BUCKETS - 1)
    return jnp.where(n < REL_MAX_EXACT, n, large)


def _bucket_index_tiles():
    r = np.arange(ATT_TILE)
    dist = (np.arange(BIAS_TILES)[:, None, None] * ATT_TILE + r[None, None, :] - r[None, :, None])
    dist = jnp.asarray(dist, jnp.int32)
    return jnp.where(dist >= 0, _rel_bucket(dist), REL_BUCKETS).astype(jnp.int32)


TAB_Q, TAB_K, TAB_DELTA, TAB_FIRST, TAB_ACC = range(5)
PIPE_LAG = 2
PIPE_UNROLL = 6


def _tile_schedule(nq, own_first):
    rows = []
    for i in range(nq):
        keys = ([i] + list(range(i))) if own_first else list(range(i + 1))
        for n, j in enumerate(keys):
            rows.append((i, j, min(i - j, BIAS_TILES - 1), int(n == 0), i))
    n_iters = -(-(len(rows) + PIPE_LAG) // PIPE_UNROLL) * PIPE_UNROLL
    idle = (0, 0, 0, 1, nq)
    cols = [idle] * PIPE_LAG + rows
    cols += [idle] * (n_iters + PIPE_LAG - len(cols))
    return jnp.asarray(np.array(cols, np.int32).T), n_iters


def _tile_slice(idx):
    return pl.ds(pl.multiple_of(idx * ATT_TILE, ATT_TILE), ATT_TILE)


def _head_row_mask(h):
    row = lax.broadcasted_iota(jnp.int32, (LANES, 1), 0)
    return (row >= h * HEAD_DIM) & (row < (h + 1) * HEAD_DIM)


def _pair_rows(pair, n=LANES):
    return slice(pair * n, (pair + 1) * n)


def _attn_kernel(kind, lambda_init, n_iters, tab_ref, qt_ref, k_ref, vt_ref, bias_ref, *rest):
    if kind == "moba":
        blk_ref, ot_ref, qts_ref, vts_ref, s_ref, p_ref, acc_ref, neg_ref = rest
    else:
        lam_ref, sw_ref, ot_ref, qts_ref, vts_ref, s_ref, p_ref, acc_ref = rest
    heads = range(HEADS_PER_STEP)
    seq = qt_ref.shape[1]
    part_lanes = LANES if kind == "moba" else ATT_TILE
    query_parts = tuple(slice(c, c + part_lanes) for c in range(0, ATT_TILE, part_lanes))
    acc_rows = acc_ref.shape[2]

    n_tiles = seq // ATT_TILE
    q_heads = []
    for e in heads:
        pair, h = divmod(e, HEADS_PER_PAIR)
        qf = qt_ref[_pair_rows(pair), :].astype(F32)
        q_heads.append(jnp.where(_head_row_mask(h), qf, 0.0).astype(BF16))
        for i in range(n_tiles):
            qts_ref[i, e] = q_heads[e][:, i * ATT_TILE:(i + 1) * ATT_TILE]
    for j in range(n_tiles):
        vts_ref[j] = vt_ref[:, j * ATT_TILE:(j + 1) * ATT_TILE]
    s_ref[...] = jnp.zeros_like(s_ref)
    p_ref[...] = jnp.zeros_like(p_ref)
    acc_ref[...] = jnp.zeros_like(acc_ref)

    if kind == "moba":
        nb = blk_ref.shape[0]
        kmean = _dot(blk_ref[...], k_ref[...])
        km_hi = kmean.astype(BF16)
        km_lo = (kmean - km_hi.astype(F32)).astype(BF16)
        blk_id = lax.broadcasted_iota(jnp.int32, (nb, ATT_TILE), 0)
        for e in heads:
            pair = e // HEADS_PER_PAIR
            q_e = q_heads[e]
            gate_all = _dot(km_hi[:, _pair_rows(pair)], q_e) + _dot(km_lo[:, _pair_rows(pair)], q_e)
            for i in range(n_tiles):
                cols = slice(i * ATT_TILE, (i + 1) * ATT_TILE)
                past = blk_id < i
                gate = jnp.where(past, gate_all[:, cols], NEG)
                rank = jnp.zeros((nb, ATT_TILE), jnp.int32)
                for c in range(i):
                    gc = gate[c:c + 1, :]
                    tie = jnp.where(blk_id > c, 1, 0)
                    rank = rank + jnp.where(gc > gate, 1, jnp.where(gc == gate, tie, 0))
                chosen = jnp.where(rank < MOBA_TOPK, jnp.where(past, 1, 0), 0)
                keep = jnp.maximum(chosen, jnp.where(blk_id == i, 1, 0))
                neg_ref[e, :, cols] = jnp.where(keep == 1, 0.0, NEG)

    def step(t, cur, carry):
        nxt = 1 - cur
        m_prev, alpha_prev, mtile_prev = carry
        ic = tab_ref[TAB_ACC, t]
        jc = tab_ref[TAB_K, t]
        ib = tab_ref[TAB_Q, t + 1]
        jb = tab_ref[TAB_K, t + 1]
        first = tab_ref[TAB_FIRST, t + 1] != 0
        ia = tab_ref[TAB_Q, t + 2]
        ja = tab_ref[TAB_K, t + 2]
        da = tab_ref[TAB_DELTA, t + 2]
        m_new, alpha_new, mtile_new = ([None] * HEADS_PER_STEP for _ in range(3))
        for e in HEAD_ORDER:
            pair = e // HEADS_PER_PAIR

            def stage_a():
                kt = k_ref[_tile_slice(ja), _pair_rows(pair)]
                sb = _dot(kt, qts_ref[ia, e]) + bias_ref[e, da]
                s_ref[nxt, e] = sb
                mtile_new[e] = tuple(jnp.max(sb[:, lanes], axis=0, keepdims=True) for lanes in query_parts)

            def stage_c():
                vt = vts_ref[jc, _pair_rows(e if kind == "moba" else pair, acc_rows), :]
                pv = _dot(vt, p_ref[cur, e])
                alpha = alpha_prev[e][0] if len(query_parts) == 1 else jnp.concatenate(alpha_prev[e], axis=1)
                acc_ref[ic, e] = alpha * acc_ref[ic, e] + pv

            for stage in ((stage_a, stage_c) if kind == "moba" else (stage_c, stage_a)):
                stage()

            m_parts, alpha_parts = [], []
            if kind == "moba":
                neg_row = neg_ref[e, pl.ds(jb, 1), _tile_slice(ib)]
            for part, lanes in enumerate(query_parts):
                s = s_ref[cur, e, :, lanes]
                m_in = jnp.where(first, -jnp.inf, m_prev[e][part])
                m_tile = mtile_prev[e][part]
                if kind == "moba":
                    neg = neg_row[:, lanes]
                    m_e = jnp.maximum(m_in, m_tile + neg)
                    shift = m_e - neg
                else:
                    m_e = jnp.maximum(m_in, m_tile)
                    shift = m_e
                p_ref[nxt, e, :, lanes] = jnp.exp2(s - shift).astype(BF16)
                alpha_parts.append(jnp.exp2(m_in - m_e))
                m_parts.append(m_e)
            alpha_new[e] = tuple(alpha_parts)
            m_new[e] = tuple(m_parts)

        return tuple(m_new), tuple(alpha_new), tuple(mtile_new)

    zeros = tuple(tuple(jnp.zeros((1, part_lanes), F32) for _ in query_parts) for _ in heads)

    def body(u, carry):
        for r in range(PIPE_UNROLL):
            carry = step(PIPE_UNROLL * u + r, r % 2, carry)
        return carry

    lax.fori_loop(0, n_iters // PIPE_UNROLL, body, (zeros, zeros, zeros))

    def emit(i, carry):
        cols = _tile_slice(i)
        v_group = acc_ref.shape[2] - ONES_ROWS
        for pair in range(PAIRS_PER_STEP):
            outs = []
            for h in range(HEADS_PER_PAIR):
                e = pair * HEADS_PER_PAIR + h
                outs.append(acc_ref[i, e, :v_group, :] * (1.0 / acc_ref[i, e, v_group:v_group + 1, :]))
            if kind == "moba":
                o = jnp.concatenate(outs, axis=0)
            else:
                lam = lam_ref[...]
                lam_full = (jnp.exp(jnp.sum(lam[0:1] * lam[1:2], axis=-1, keepdims=True))
                            - jnp.exp(jnp.sum(lam[2:3] * lam[3:4], axis=-1, keepdims=True)) + lambda_init)
                o = outs[0] - lam_full * outs[1]
                o = (o * lax.rsqrt(jnp.mean(o * o, axis=0, keepdims=True) + EPS)
                     * sw_ref[...] * (1.0 - lambda_init))
            ot_ref[_pair_rows(pair), cols] = o.astype(BF16)
        return carry

    lax.fori_loop(0, seq // ATT_TILE, emit, 0)


def _attention(kind, qt, k, vt_ext, bias, extras, lambda_init, batch, seq):
    v_group = _value_group(kind)
    acc_rows = v_group + ONES_ROWS
    vt_block_rows = acc_rows * (LANES * PAIRS_PER_STEP // v_group)
    t = k.shape[0]
    nq = seq // ATT_TILE
    tab, n_iters = _tile_schedule(nq, own_first=(kind == "moba"))
    groups = ATTN_HEADS // HEADS_PER_STEP
    rows = LANES * PAIRS_PER_STEP
    in_specs = [
        pl.BlockSpec(memory_space=pltpu.SMEM),
        pl.BlockSpec((rows, seq), lambda g, b: (g, b)),
        pl.BlockSpec((seq, rows), lambda g, b: (b, g)),
        pl.BlockSpec((vt_block_rows, seq), lambda g, b: (g, b)),
        pl.BlockSpec((HEADS_PER_STEP, BIAS_TILES, ATT_TILE, ATT_TILE), lambda g, b: (g, 0, 0, 0)),
    ]
    scratch = [
        pltpu.VMEM((nq, HEADS_PER_STEP, LANES, ATT_TILE), BF16),
        pltpu.VMEM((nq, vt_block_rows, ATT_TILE), BF16),
        pltpu.VMEM((2, HEADS_PER_STEP, ATT_TILE, ATT_TILE), F32),
        pltpu.VMEM((2, HEADS_PER_STEP, ATT_TILE, ATT_TILE), BF16),
        pltpu.VMEM((nq + 1, HEADS_PER_STEP, acc_rows, ATT_TILE), F32),
    ]
    if kind == "moba":
        (blk,) = extras
        nb = blk.shape[0]
        in_specs.append(pl.BlockSpec((nb, seq), lambda g, b: (0, 0)))
        scratch.append(pltpu.VMEM((HEADS_PER_STEP, nb, seq), F32))
    else:
        lam, sw = extras
        in_specs += [
            pl.BlockSpec((4, HEAD_DIM), lambda g, b: (0, 0)),
            pl.BlockSpec((LANES, 1), lambda g, b: (0, 0)),
        ]
    return pl.pallas_call(
        functools.partial(_attn_kernel, kind, lambda_init, n_iters),
        grid=(groups, batch),
        in_specs=in_specs,
        out_specs=pl.BlockSpec((rows, seq), lambda g, b: (g, b)),
        out_shape=jax.ShapeDtypeStruct((D_MODEL, t), BF16),
        scratch_shapes=scratch,
        compiler_params=_params("parallel", "parallel"),
        name=kind + "_attn",
    )(tab, qt, k, vt_ext, bias, *extras)


def _rotary_tables(seq):
    d = RET_DK
    inv_freq = ROPE_BASE ** (-np.arange(0, d, 2, dtype=np.float64) / d)
    ang = np.arange(seq, dtype=np.float64)[:, None] * inv_freq[None, :]
    return jnp.asarray(np.cos(ang), F32), jnp.asarray(np.sin(ang), F32)


def _retention_decay_tables():
    c_len = RET_CHUNK
    log_gamma = np.log(1.0 - 2.0 ** (-5.0 - np.arange(RET_HEADS, dtype=np.float64)))
    pos = np.arange(c_len, dtype=np.float64)
    rel = pos[:, None] - pos[None, :]
    dm = np.where(rel >= 0, np.exp(np.maximum(rel, 0.0)[None] * log_gamma[:, None, None]), 0.0)
    rs = np.exp((pos + 1.0)[None, :] * log_gamma[:, None])[:, :, None]
    ks = np.exp((c_len - 1.0 - pos)[None, :] * log_gamma[:, None])[:, :, None]
    cd = np.exp(c_len * log_gamma)[:, None, None]
    return tuple(jnp.asarray(a, F32) for a in (dm, rs, ks, cd))


def _block_mean_matrix(seq):
    nb = seq // MOBA_BLOCK
    m = (np.arange(seq)[None, :] // MOBA_BLOCK == np.arange(nb)[:, None]) / float(MOBA_BLOCK)
    return jnp.asarray(m, BF16)


def _head_group_matrix():
    g = np.arange(COL_TILE)[:, None] // HEAD_DIM == np.arange(COL_TILE)[None, :] // HEAD_DIM
    return jnp.asarray(g / float(HEAD_DIM), BF16)


def kernel(x, rel_bias, norm1, norm2, w_up, w_down, ret_w_in, ret_w_out,
           moba_w_in, moba_q_norm, moba_k_norm, moba_w_out,
           diff_w_in, diff_q_norm, diff_k_norm, diff_lambda, diff_subln, diff_w_out):
    batch, seq, d = x.shape
    depth = norm1.shape[0]
    assert d == D_MODEL and seq % ROW_TILE == 0 and seq % ATT_TILE == 0 and seq % (RET_UNROLL * RET_CHUNK) == 0
    assert seq % MOBA_BLOCK == 0 and MOBA_BLOCK == ATT_TILE
    t = batch * seq
    xf = x.reshape(t, d)

    bias = _bias_tiles(rel_bias.astype(F32), _bucket_index_tiles())
    grp = _head_group_matrix()
    q_scale = HEAD_DIM ** -0.5 * LOG2E
    mixer_weights = ((ret_w_in, ret_w_out), (moba_w_in, moba_w_out), (diff_w_in, diff_w_out))

    def layer_weights_f32(i):
        kind, j = i % N_MIXERS, i // N_MIXERS
        w_in, w_out = mixer_weights[kind]
        return [(w_in, j), (w_out, j), (w_up, i), (w_down, i)]

    first_jobs = layer_weights_f32(0)
    w_in = first_jobs[0][0][0:1].astype(BF16)
    w_out = wu = wd = None

    for i in range(depth):
        kind, j = i % N_MIXERS, i // N_MIXERS
        nw1 = norm1[i].reshape(1, d)
        nw2 = norm2[i].reshape(1, d)
        cast_jobs = layer_weights_f32(i + 1) if i + 1 < depth else ()
        if kind == 0:
            cos, sin = _rotary_tables(seq)
            dm, rs, ks, cd = _retention_decay_tables()
            (q, k, v, g), early = _ret_proj(xf, nw1, w_in, 0, cos, sin, seq, first_jobs[1:] if i == 0 else ())
            if early:
                w_out, wu, wd = early
            o = _ret_core(q, k, v, dm, rs, ks, cd, batch, seq)
            xf, nxt = _out_mlp(o, g, w_out, 0, xf, nw2, wu, wd, 0, cast_jobs)
        elif kind == 1:
            qn = (jnp.tile(moba_q_norm[j], ATTN_HEADS) * q_scale).reshape(1, d)
            kn = jnp.tile(moba_k_norm[j], ATTN_HEADS).reshape(1, d)
            qt, k, vt = _attn_proj(xf, nw1, w_in, 0, qn, kn, grp, _value_group("moba"))
            ot = _attention("moba", qt, k, vt, bias, (_block_mean_matrix(seq),), 0.0, batch, seq)
            xf, nxt = _out_mlp(ot, None, w_out, 0, xf, nw2, wu, wd, 0, cast_jobs)
        else:
            lambda_init = 0.8 - 0.6 * math.exp(-0.3 * i)
            qn = (jnp.tile(diff_q_norm[j], ATTN_HEADS) * q_scale).reshape(1, d)
            kn = jnp.tile(diff_k_norm[j], ATTN_HEADS).reshape(1, d)
            qt, k, vt = _attn_proj(xf, nw1, w_in, 0, qn, kn, grp, _value_group("diff"))
            extras = (diff_lambda[j].astype(F32), diff_subln[j].reshape(LANES, 1))
            ot = _attention("diff", qt, k, vt, bias, extras, lambda_init, batch, seq)
            xf, nxt = _out_mlp(ot, None, w_out, 0, xf, nw2, wu, wd, 0, cast_jobs)
        if nxt:
            w_in, w_out, wu, wd = nxt
    return xf.reshape(batch, seq, d)
```

```python
import functools
import math

import numpy as np
import jax
import jax.numpy as jnp
from jax import lax
from jax.experimental import pallas as pl
from jax.experimental.pallas import tpu as pltpu

F32 = jnp.float32
BF16 = jnp.bfloat16

D_MODEL = 1024
N_MIXERS = 3
RET_HEADS = 4
RET_DK = D_MODEL // RET_HEADS
RET_DV = 2 * RET_DK
ROPE_BASE = 10000.0
ATTN_HEADS = 16
HEAD_DIM = D_MODEL // ATTN_HEADS
MOBA_BLOCK = 256
MOBA_TOPK = 3
REL_BUCKETS = 32
REL_MAX_EXACT = REL_BUCKETS // 2
REL_MAX_DISTANCE = 1024
D_FF = 4 * D_MODEL
EPS = 1e-6
NEG = -1e30
LOG2E = math.log2(math.e)

LANES = 128
BF16_SUBLANES = 16
VMEM_LIMIT_BYTES = 56 * 1024 * 1024

ROW_TILE = 512
COL_TILE = 256
FF_TILE = 1024
GATE_TILE = 512
RET_CHUNK = 256
RET_UNROLL = 4
ATT_TILE = 256
BIAS_TILES = 6
HEADS_PER_PAIR = LANES // HEAD_DIM
PAIRS_PER_STEP = 2
HEADS_PER_STEP = HEADS_PER_PAIR * PAIRS_PER_STEP
HEAD_ORDER = tuple(p * HEADS_PER_PAIR + h for h in range(HEADS_PER_PAIR) for p in range(PAIRS_PER_STEP))
ONES_ROWS = BF16_SUBLANES


def _value_group(kind):
    return HEAD_DIM if kind == "moba" else LANES


def _params(*sem):
    return pltpu.CompilerParams(dimension_semantics=sem, vmem_limit_bytes=VMEM_LIMIT_BYTES)


def _rms(xf, w):
    ms = jnp.mean(xf * xf, axis=-1, keepdims=True)
    return xf * lax.rsqrt(ms + EPS) * w


def _dot(a, b):
    return jnp.dot(a, b, preferred_element_type=F32)


def _dot_nt(a, b):
    return lax.dot_general(a, b, (((1,), (1,)), ((), ())), preferred_element_type=F32)


def _dot_tn(a, b):
    return lax.dot_general(a, b, (((0,), (0,)), ((), ())), preferred_element_type=F32)


def _cast_specs(cast_jobs, steps):
    in_specs, out_specs, shapes = [], [], []
    for w, w_layer in cast_jobs:
        _, rows, cols = w.shape
        slab = rows // steps
        assert slab * steps == rows and slab % BF16_SUBLANES == 0
        in_specs.append(pl.BlockSpec((None, slab, cols), lambda i, w_layer=w_layer: (w_layer, i, 0)))
        out_specs.append(pl.BlockSpec((None, slab, cols), lambda i: (0, i, 0)))
        shapes.append(jax.ShapeDtypeStruct((1, rows, cols), BF16))
    return in_specs, out_specs, shapes


def _split_cast_refs(refs, n_outputs, n_cast):
    n_in = len(refs) - n_outputs - 2 * n_cast
    sources = refs[n_in:n_in + n_cast]
    outputs = refs[n_in + n_cast:n_in + n_cast + n_outputs]
    return refs[:n_in] + outputs, sources, refs[n_in + n_cast + n_outputs:]


def _run_casts(sources, destinations):
    for src, dst in zip(sources, destinations):
        dst[...] = src[...].astype(BF16)


def _ret_proj_kernel(n_cast, *refs):
    refs, cast_in, cast_out = _split_cast_refs(refs, 4, n_cast)
    _run_casts(cast_in, cast_out)
    x_ref, nw_ref, w_ref, cos_ref, sin_ref, q_ref, k_ref, v_ref, g_ref = refs
    h = _rms(x_ref[...], nw_ref[...]).astype(BF16)
    cos = cos_ref[...]
    sin = sin_ref[...]
    half = RET_DK // 2
    chunks = []
    for out_ref, scale in ((q_ref, 1.0), (k_ref, RET_DK ** -0.5)):
        chunks += [(out_ref, hd * RET_DK, scale) for hd in range(RET_HEADS)]
    for out_ref in (v_ref, g_ref):
        chunks += [(out_ref, c * COL_TILE, None) for c in range(RET_HEADS * RET_DV // COL_TILE)]
    assert RET_DK == COL_TILE

    def project(n):
        return _dot(h, w_ref[:, n * COL_TILE:(n + 1) * COL_TILE])

    acc_next = project(0)
    for n, (out_ref, col0, scale) in enumerate(chunks):
        acc = acc_next
        if n + 1 < len(chunks):
            acc_next = project(n + 1)
        if scale is None:
            out_ref[:, col0:col0 + COL_TILE] = acc.astype(BF16)
        else:
            x1 = acc[:, :half]
            x2 = acc[:, half:]
            r1 = x1 * cos - x2 * sin
            r2 = x1 * sin + x2 * cos
            if scale != 1.0:
                r1 = r1 * scale
                r2 = r2 * scale
            out_ref[:, col0:col0 + half] = r1.astype(BF16)
            out_ref[:, col0 + half:col0 + RET_DK] = r2.astype(BF16)


def _ret_proj(x, nw, w, layer, cos, sin, seq, cast_jobs=()):
    t = x.shape[0]
    cast_in_specs, cast_out_specs, cast_shapes = _cast_specs(cast_jobs, t // ROW_TILE)
    n_in = w.shape[2]
    tiles_per_seq = seq // ROW_TILE
    row = lambda i: (i, 0)
    const = lambda i: (0, 0)
    pos = lambda i: (i % tiles_per_seq, 0)
    n_qk = RET_HEADS * RET_DK
    n_v = RET_HEADS * RET_DV
    outs = pl.pallas_call(
        functools.partial(_ret_proj_kernel, len(cast_jobs)),
        grid=(t // ROW_TILE,),
        in_specs=[
            pl.BlockSpec((ROW_TILE, D_MODEL), row),
            pl.BlockSpec((1, D_MODEL), const),
            pl.BlockSpec((None, D_MODEL, n_in), lambda i: (layer, 0, 0)),
            pl.BlockSpec((ROW_TILE, RET_DK // 2), pos),
            pl.BlockSpec((ROW_TILE, RET_DK // 2), pos),
        ] + cast_in_specs,
        out_specs=[
            pl.BlockSpec((ROW_TILE, n_qk), row),
            pl.BlockSpec((ROW_TILE, n_qk), row),
            pl.BlockSpec((ROW_TILE, n_v), row),
            pl.BlockSpec((ROW_TILE, n_v), row),
        ] + cast_out_specs,
        out_shape=[
            jax.ShapeDtypeStruct((t, n_qk), BF16),
            jax.ShapeDtypeStruct((t, n_qk), BF16),
            jax.ShapeDtypeStruct((t, n_v), BF16),
            jax.ShapeDtypeStruct((t, n_v), BF16),
        ] + cast_shapes,
        compiler_params=_params("parallel"),
        name="ret_proj",
    )(x, nw, w, cos, sin, *[cw for cw, _ in cast_jobs])
    return outs[:4], outs[4:]


def _attn_proj_kernel(v_group, x_ref, nw_ref, w_ref, qn_ref, kn_ref, grp_ref, qt_ref, k_ref, vt_ref):
    h = _rms(x_ref[...], nw_ref[...]).astype(BF16)
    grp = grp_ref[...]
    groups_per_chunk = COL_TILE // v_group
    chunks_per_part = D_MODEL // COL_TILE
    n_chunks = 3 * chunks_per_part

    def project(n):
        return _dot(h, w_ref[:, n * COL_TILE:(n + 1) * COL_TILE])

    acc_next = project(0)
    for n in range(n_chunks):
        acc = acc_next
        if n + 1 < n_chunks:
            acc_next = project(n + 1)
        part, c = divmod(n, chunks_per_part)
        cols = slice(c * COL_TILE, (c + 1) * COL_TILE)
        if part < 2:
            hw_ref = qn_ref if part == 0 else kn_ref
            ms = _dot((acc * acc).astype(BF16), grp)
            acc = acc * lax.rsqrt(ms + EPS) * hw_ref[:, cols]
        if part == 0:
            qt_ref[cols, :] = acc.T.astype(BF16)
        elif part == 1:
            k_ref[:, cols] = acc.astype(BF16)
        else:
            acc_t = acc.T.astype(BF16)
            for p in range(groups_per_chunk):
                row0 = (c * groups_per_chunk + p) * (v_group + ONES_ROWS)
                vt_ref[row0:row0 + v_group, :] = acc_t[p * v_group:(p + 1) * v_group, :]
                vt_ref[row0 + v_group:row0 + v_group + ONES_ROWS, :] = jnp.ones((ONES_ROWS, acc_t.shape[1]), BF16)


def _attn_proj(x, nw, w, layer, qn, kn, grp, v_group):
    t = x.shape[0]
    row = lambda i: (i, 0)
    col = lambda i: (0, i)
    const = lambda i: (0, 0)
    vt_rows = (D_MODEL // v_group) * (v_group + ONES_ROWS)
    return pl.pallas_call(
        functools.partial(_attn_proj_kernel, v_group),
        grid=(t // ROW_TILE,),
        in_specs=[
            pl.BlockSpec((ROW_TILE, D_MODEL), row),
            pl.BlockSpec((1, D_MODEL), const),
            pl.BlockSpec((None, D_MODEL, 3 * D_MODEL), lambda i: (layer, 0, 0)),
            pl.BlockSpec((1, D_MODEL), const),
            pl.BlockSpec((1, D_MODEL), const),
            pl.BlockSpec((COL_TILE, COL_TILE), const),
        ],
        out_specs=[
            pl.BlockSpec((D_MODEL, ROW_TILE), col),
            pl.BlockSpec((ROW_TILE, D_MODEL), row),
            pl.BlockSpec((vt_rows, ROW_TILE), col),
        ],
        out_shape=[
            jax.ShapeDtypeStruct((D_MODEL, t), BF16),
            jax.ShapeDtypeStruct((t, D_MODEL), BF16),
            jax.ShapeDtypeStruct((vt_rows, t), BF16),
        ],
        compiler_params=_params("parallel"),
        name="attn_proj",
    )(x, nw, w, qn, kn, grp)


def _ret_core_kernel(q_ref, k_ref, v_ref, dm_ref, rs_ref, ks_ref, cd_ref, o_ref, state_ref, raw_ref):
    c_len = RET_CHUNK
    n_chunks = q_ref.shape[0] // c_len
    state_ref[...] = jnp.zeros_like(state_ref)
    raw_ref[...] = jnp.zeros_like(raw_ref)

    def chunk(c):
        return pl.ds(pl.multiple_of(c * c_len, c_len), c_len)

    def masked_scores(c):
        return (_dot_nt(q_ref[chunk(c), :], k_ref[chunk(c), :]) * dm_ref[0]).astype(BF16)

    def normalise(c, slot):
        o = raw_ref[slot]
        o_ref[chunk(c), :] = (o * lax.rsqrt(jnp.mean(o * o, axis=-1, keepdims=True) + EPS)).astype(BF16)

    def step(c, slot, s_cur):
        sl = chunk(c)
        q = q_ref[sl, :]
        k = k_ref[sl, :]
        v = v_ref[sl, :]
        state = state_ref[...]
        kd = (k.astype(F32) * ks_ref[0]).astype(BF16)
        kv = _dot_tn(kd, v)
        inner = _dot(s_cur, v)
        cross = _dot(q, state.astype(BF16))
        s_next = masked_scores(jnp.minimum(c + 1, n_chunks - 1))
        normalise(jnp.maximum(c - 1, 0), 1 - slot)
        raw_ref[slot] = inner + cross * rs_ref[0]
        state_ref[...] = state * cd_ref[0] + kv
        return s_next

    def body(u, s_cur):
        for r in range(RET_UNROLL):
            s_cur = step(RET_UNROLL * u + r, r % 2, s_cur)
        return s_cur

    lax.fori_loop(0, n_chunks // RET_UNROLL, body, masked_scores(0))
    normalise(n_chunks - 1, 1)


def _ret_core(q, k, v, dm, rs, ks, cd, batch, seq):
    t = q.shape[0]
    c_len = RET_CHUNK
    tok = lambda b, h: (b, h)
    head3 = lambda b, h: (h, 0, 0)
    return pl.pallas_call(
        _ret_core_kernel,
        grid=(batch, RET_HEADS),
        in_specs=[
            pl.BlockSpec((seq, RET_DK), tok),
            pl.BlockSpec((seq, RET_DK), tok),
            pl.BlockSpec((seq, RET_DV), tok),
            pl.BlockSpec((1, c_len, c_len), head3),
            pl.BlockSpec((1, c_len, 1), head3),
            pl.BlockSpec((1, c_len, 1), head3),
            pl.BlockSpec((1, 1, 1), head3),
        ],
        out_specs=pl.BlockSpec((seq, RET_DV), tok),
        out_shape=jax.ShapeDtypeStruct((t, RET_HEADS * RET_DV), BF16),
        scratch_shapes=[pltpu.VMEM((RET_DK, RET_DV), F32), pltpu.VMEM((2, RET_CHUNK, RET_DV), F32)],
        compiler_params=_params("parallel", "parallel"),
        name="ret_core",
    )(q, k, v, dm, rs, ks, cd)


def _out_mlp_kernel(gated, n_cast, *refs):
    refs, cast_in, cast_out = _split_cast_refs(refs, 1, n_cast)
    _run_casts(cast_in, cast_out)
    if gated:
        a_ref, g_ref, wo_ref, x_ref, nw_ref, wu_ref, wd_ref, o_ref = refs
        n_chunks = a_ref.shape[1] // GATE_TILE

        def gate(c):
            cols = slice(c * GATE_TILE, (c + 1) * GATE_TILE)
            g = g_ref[:, cols].astype(F32)
            return (g * (1.0 / (1.0 + jnp.exp(-g))) * a_ref[:, cols].astype(F32)).astype(BF16)

        a_next = gate(0)
        mix = None
        for c in range(n_chunks):
            a = a_next
            if c + 1 < n_chunks:
                a_next = gate(c + 1)
            part = _dot(a, wo_ref[c * GATE_TILE:(c + 1) * GATE_TILE, :])
            mix = part if mix is None else mix + part
    else:
        a_ref, wo_ref, x_ref, nw_ref, wu_ref, wd_ref, o_ref = refs
        mix = _dot_tn(a_ref[...], wo_ref[...])
    x = x_ref[...] + mix
    h = _rms(x, nw_ref[...]).astype(BF16)
    acc = x
    for c in range(D_FF // FF_TILE):
        u = _dot(h, wu_ref[:, c * FF_TILE:(c + 1) * FF_TILE])
        u = jnp.maximum(u, 0.0)
        acc = acc + _dot((u * u).astype(BF16), wd_ref[c * FF_TILE:(c + 1) * FF_TILE, :])
    o_ref[...] = acc


def _out_mlp(a, g, wo, mixer_layer, x, nw, wu, wd, layer, cast_jobs=()):
    t = x.shape[0]
    steps = t // ROW_TILE
    cast_in_specs, cast_out_specs, cast_shapes = _cast_specs(cast_jobs, steps)
    ka = wo.shape[1]
    row = lambda i: (i, 0)
    const = lambda i: (0, 0)
    gated = g is not None
    if gated:
        mix_specs = [pl.BlockSpec((ROW_TILE, ka), row), pl.BlockSpec((ROW_TILE, ka), row)]
        mix_args = [a, g]
    else:
        mix_specs = [pl.BlockSpec((ka, ROW_TILE), lambda i: (0, i))]
        mix_args = [a]
    outs = pl.pallas_call(
        functools.partial(_out_mlp_kernel, gated, len(cast_jobs)),
        grid=(steps,),
        in_specs=mix_specs + [
            pl.BlockSpec((None, ka, D_MODEL), lambda i: (mixer_layer, 0, 0)),
            pl.BlockSpec((ROW_TILE, D_MODEL), row),
            pl.BlockSpec((1, D_MODEL), const),
            pl.BlockSpec((None, D_MODEL, D_FF), lambda i: (layer, 0, 0), pipeline_mode=pl.Buffered(1)),
            pl.BlockSpec((None, D_FF, D_MODEL), lambda i: (layer, 0, 0), pipeline_mode=pl.Buffered(1)),
        ] + cast_in_specs,
        out_specs=[pl.BlockSpec((ROW_TILE, D_MODEL), row)] + cast_out_specs,
        out_shape=[jax.ShapeDtypeStruct((t, D_MODEL), F32)] + cast_shapes,
        compiler_params=_params("parallel"),
        name="out_mlp_gated" if gated else "out_mlp",
    )(*mix_args, wo, x, nw, wu, wd, *[w for w, _ in cast_jobs])
    return outs[0], outs[1:]


def _bias_tiles_kernel(bucket_ranges, rb_ref, idx_ref, o_ref):
    head = pl.program_id(0)
    for d, (lo, hi) in enumerate(bucket_ranges):
        idx = idx_ref[d]
        acc = jnp.full(idx.shape, NEG, F32)
        for b in range(lo, hi + 1):
            acc = jnp.where(idx == b, rb_ref[b, head] * LOG2E, acc)
        o_ref[0, d] = acc


def _bucket_ranges():
    ranges = []
    for d in range(BIAS_TILES):
        lo_dist = max(d * ATT_TILE - (ATT_TILE - 1), 0)
        hi_dist = d * ATT_TILE + (ATT_TILE - 1)

        def bucket(n):
            if n < REL_MAX_EXACT:
                return n
            return min(REL_MAX_EXACT + int(math.log(n / REL_MAX_EXACT) / math.log(REL_MAX_DISTANCE / REL_MAX_EXACT)
                                           * (REL_BUCKETS - REL_MAX_EXACT)), REL_BUCKETS - 1)

        ranges.append((max(bucket(lo_dist) - 1, 0), min(bucket(hi_dist) + 1, REL_BUCKETS - 1)))
    return tuple(ranges)


def _bias_tiles(rel_bias, bucket_idx):
    nd, tq, tk = bucket_idx.shape
    return pl.pallas_call(
        functools.partial(_bias_tiles_kernel, _bucket_ranges()),
        grid=(ATTN_HEADS,),
        in_specs=[
            pl.BlockSpec(memory_space=pltpu.SMEM),
            pl.BlockSpec((nd, tq, tk), lambda h: (0, 0, 0)),
        ],
        out_specs=pl.BlockSpec((1, nd, tq, tk), lambda h: (h, 0, 0, 0)),
        out_shape=jax.ShapeDtypeStruct((ATTN_HEADS, nd, tq, tk), F32),
        compiler_params=_params("parallel"),
        name="bias_tiles",
    )(rel_bias, bucket_idx)


def _rel_bucket(dist):
    n = jnp.maximum(dist, 0)
    nf = jnp.maximum(n, 1).astype(F32)
    large = REL_MAX_EXACT + (jnp.log(nf / REL_MAX_EXACT) / math.log(REL_MAX_DISTANCE / REL_MAX_EXACT)
                             * (REL_BUCKETS - REL_MAX_EXACT)).astype(jnp.int32)
    large = jnp.minimum(large, REL_BUCKETS - 1)
    return jnp.where(n < REL_MAX_EXACT, n, large)


def _bucket_index_tiles():
    r = np.arange(ATT_TILE)
    dist = (np.arange(BIAS_TILES)[:, None, None] * ATT_TILE + r[None, None, :] - r[None, :, None])
    dist = jnp.asarray(dist, jnp.int32)
    return jnp.where(dist >= 0, _rel_bucket(dist), REL_BUCKETS).astype(jnp.int32)


TAB_Q, TAB_K, TAB_DELTA, TAB_FIRST, TAB_ACC = range(5)
PIPE_LAG = 2
EMIT_UNROLL = 4
PIPE_UNROLL = 6


def _tile_schedule(nq, own_first):
    rows = []
    for i in range(nq):
        keys = ([i] + list(range(i))) if own_first else list(range(i + 1))
        for n, j in enumerate(keys):
            rows.append((i, j, min(i - j, BIAS_TILES - 1), int(n == 0), i))
    n_iters = -(-(len(rows) + PIPE_LAG) // PIPE_UNROLL) * PIPE_UNROLL
    idle = (0, 0, 0, 1, nq)
    cols = [idle] * PIPE_LAG + rows
    cols += [idle] * (n_iters + PIPE_LAG - len(cols))
    return jnp.asarray(np.array(cols, np.int32).T), n_iters


def _tile_slice(idx):
    return pl.ds(pl.multiple_of(idx * ATT_TILE, ATT_TILE), ATT_TILE)


def _head_row_mask(h):
    row = lax.broadcasted_iota(jnp.int32, (LANES, 1), 0)
    return (row >= h * HEAD_DIM) & (row < (h + 1) * HEAD_DIM)


def _pair_rows(pair, n=LANES):
    return slice(pair * n, (pair + 1) * n)


def _attn_kernel(kind, lambda_init, n_iters, tab_ref, qt_ref, k_ref, vt_ref, bias_ref, *rest):
    if kind == "moba":
        blk_ref, ot_ref, qts_ref, vts_ref, s_ref, p_ref, acc_ref, neg_ref = rest
    else:
        lam_ref, sw_ref, ot_ref, qts_ref, vts_ref, s_ref, p_ref, acc_ref = rest
    heads = range(HEADS_PER_STEP)
    seq = qt_ref.shape[1]
    part_lanes = LANES if kind == "moba" else ATT_TILE
    query_parts = tuple(slice(c, c + part_lanes) for c in range(0, ATT_TILE, part_lanes))
    acc_rows = acc_ref.shape[2]

    n_tiles = seq // ATT_TILE
    q_heads = []
    for e in heads:
        pair, h = divmod(e, HEADS_PER_PAIR)
        qf = qt_ref[_pair_rows(pair), :].astype(F32)
        q_heads.append(jnp.where(_head_row_mask(h), qf, 0.0).astype(BF16))
        for i in range(n_tiles):
            qts_ref[i, e] = q_heads[e][:, i * ATT_TILE:(i + 1) * ATT_TILE]
    for j in range(n_tiles):
        vts_ref[j] = vt_ref[:, j * ATT_TILE:(j + 1) * ATT_TILE]
    s_ref[...] = jnp.zeros_like(s_ref)
    p_ref[...] = jnp.zeros_like(p_ref)
    acc_ref[...] = jnp.zeros_like(acc_ref)

    if kind == "moba":
        nb = blk_ref.shape[0]
        kmean = _dot(blk_ref[...], k_ref[...])
        km_hi = kmean.astype(BF16)
        km_lo = (kmean - km_hi.astype(F32)).astype(BF16)
        blk_id = lax.broadcasted_iota(jnp.int32, (nb, ATT_TILE), 0)
        for e in heads:
            pair = e // HEADS_PER_PAIR
            q_e = q_heads[e]
            gate_all = _dot(km_hi[:, _pair_rows(pair)], q_e) + _dot(km_lo[:, _pair_rows(pair)], q_e)
            for i in range(n_tiles):
                cols = slice(i * ATT_TILE, (i + 1) * ATT_TILE)
                past = blk_id < i
                gate = jnp.where(past, gate_all[:, cols], NEG)
                rank = jnp.zeros((nb, ATT_TILE), jnp.int32)
                for c in range(i):
                    gc = gate[c:c + 1, :]
                    tie = jnp.where(blk_id > c, 1, 0)
                    rank = rank + jnp.where(gc > gate, 1, jnp.where(gc == gate, tie, 0))
                chosen = jnp.where(rank < MOBA_TOPK, jnp.where(past, 1, 0), 0)
                keep = jnp.maximum(chosen, jnp.where(blk_id == i, 1, 0))
                neg_ref[e, :, cols] = jnp.where(keep == 1, 0.0, NEG)

    def step(t, cur, carry):
        nxt = 1 - cur
        m_prev, alpha_prev, mtile_prev = carry
        ic = tab_ref[TAB_ACC, t]
        jc = tab_ref[TAB_K, t]
        ib = tab_ref[TAB_Q, t + 1]
        jb = tab_ref[TAB_K, t + 1]
        first = tab_ref[TAB_FIRST, t + 1] != 0
        ia = tab_ref[TAB_Q, t + 2]
        ja = tab_ref[TAB_K, t + 2]
        da = tab_ref[TAB_DELTA, t + 2]
        m_new, alpha_new, mtile_new = ([None] * HEADS_PER_STEP for _ in range(3))
        for e in HEAD_ORDER:
            pair = e // HEADS_PER_PAIR

            def stage_a():
                kt = k_ref[_tile_slice(ja), _pair_rows(pair)]
                sb = _dot(kt, qts_ref[ia, e]) + bias_ref[e, da]
                s_ref[nxt, e] = sb
                mtile_new[e] = tuple(jnp.max(sb[:, lanes], axis=0, keepdims=True) for lanes in query_parts)

            def stage_c():
                vt = vts_ref[jc, _pair_rows(e if kind == "moba" else pair, acc_rows), :]
                pv = _dot(vt, p_ref[cur, e])
                alpha = alpha_prev[e][0] if len(query_parts) == 1 else jnp.concatenate(alpha_prev[e], axis=1)
                acc_ref[ic, e] = alpha * acc_ref[ic, e] + pv

            for stage in ((stage_a, stage_c) if kind == "moba" else (stage_c, stage_a)):
                stage()

            m_parts, alpha_parts = [], []
            if kind == "moba":
                neg_row = neg_ref[e, pl.ds(jb, 1), _tile_slice(ib)]
            for part, lanes in enumerate(query_parts):
                s = s_ref[cur, e, :, lanes]
                m_in = jnp.where(first, -jnp.inf, m_prev[e][part])
                m_tile = mtile_prev[e][part]
                if kind == "moba":
                    neg = neg_row[:, lanes]
                    m_e = jnp.maximum(m_in, m_tile + neg)
                    shift = m_e - neg
                else:
                    m_e = jnp.maximum(m_in, m_tile)
                    shift = m_e
                p_ref[nxt, e, :, lanes] = jnp.exp2(s - shift).astype(BF16)
                alpha_parts.append(jnp.exp2(m_in - m_e))
                m_parts.append(m_e)
            alpha_new[e] = tuple(alpha_parts)
            m_new[e] = tuple(m_parts)

        return tuple(m_new), tuple(alpha_new), tuple(mtile_new)

    zeros = tuple(tuple(jnp.zeros((1, part_lanes), F32) for _ in query_parts) for _ in heads)

    def body(u, carry):
        for r in range(PIPE_UNROLL):
            carry = step(PIPE_UNROLL * u + r, r % 2, carry)
        return carry

    lax.fori_loop(0, n_iters // PIPE_UNROLL, body, (zeros, zeros, zeros))

    v_group = acc_rows - ONES_ROWS
    if kind != "moba":
        lam = lam_ref[...]
        lam_full = (jnp.exp(jnp.sum(lam[0:1] * lam[1:2], axis=-1, keepdims=True))
                    - jnp.exp(jnp.sum(lam[2:3] * lam[3:4], axis=-1, keepdims=True)) + lambda_init)

    def emit_row(i):
        cols = _tile_slice(i)
        for pair in range(PAIRS_PER_STEP):
            outs = []
            for h in range(HEADS_PER_PAIR):
                e = pair * HEADS_PER_PAIR + h
                outs.append(acc_ref[i, e, :v_group, :] * (1.0 / acc_ref[i, e, v_group:v_group + 1, :]))
            if kind == "moba":
                o = jnp.concatenate(outs, axis=0)
            else:
                o = outs[0] - lam_full * outs[1]
                o = (o * lax.rsqrt(jnp.mean(o * o, axis=0, keepdims=True) + EPS)
                     * sw_ref[...] * (1.0 - lambda_init))
            ot_ref[_pair_rows(pair), cols] = o.astype(BF16)

    def emit(u, carry):
        for r in range(EMIT_UNROLL):
            emit_row(EMIT_UNROLL * u + r)
        return carry

    assert n_tiles % EMIT_UNROLL == 0
    lax.fori_loop(0, n_tiles // EMIT_UNROLL, emit, 0)


def _attention(kind, qt, k, vt_ext, bias, extras, lambda_init, batch, seq):
    v_group = _value_group(kind)
    acc_rows = v_group + ONES_ROWS
    vt_block_rows = acc_rows * (LANES * PAIRS_PER_STEP // v_group)
    t = k.shape[0]
    nq = seq // ATT_TILE
    tab, n_iters = _tile_schedule(nq, own_first=(kind == "moba"))
    groups = ATTN_HEADS // HEADS_PER_STEP
    rows = LANES * PAIRS_PER_STEP
    in_specs = [
        pl.BlockSpec(memory_space=pltpu.SMEM),
        pl.BlockSpec((rows, seq), lambda g, b: (g, b)),
        pl.BlockSpec((seq, rows), lambda g, b: (b, g)),
        pl.BlockSpec((vt_block_rows, seq), lambda g, b: (g, b)),
        pl.BlockSpec((HEADS_PER_STEP, BIAS_TILES, ATT_TILE, ATT_TILE), lambda g, b: (g, 0, 0, 0)),
    ]
    scratch = [
        pltpu.VMEM((nq, HEADS_PER_STEP, LANES, ATT_TILE), BF16),
        pltpu.VMEM((nq, vt_block_rows, ATT_TILE), BF16),
        pltpu.VMEM((2, HEADS_PER_STEP, ATT_TILE, ATT_TILE), F32),
        pltpu.VMEM((2, HEADS_PER_STEP, ATT_TILE, ATT_TILE), BF16),
        pltpu.VMEM((nq + 1, HEADS_PER_STEP, acc_rows, ATT_TILE), F32),
    ]
    if kind == "moba":
        (blk,) = extras
        nb = blk.shape[0]
        in_specs.append(pl.BlockSpec((nb, seq), lambda g, b: (0, 0)))
        scratch.append(pltpu.VMEM((HEADS_PER_STEP, nb, seq), F32))
    else:
        lam, sw = extras
        in_specs += [
            pl.BlockSpec((4, HEAD_DIM), lambda g, b: (0, 0)),
            pl.BlockSpec((LANES, 1), lambda g, b: (0, 0)),
        ]
    return pl.pallas_call(
        functools.partial(_attn_kernel, kind, lambda_init, n_iters),
        grid=(groups, batch),
        in_specs=in_specs,
        out_specs=pl.BlockSpec((rows, seq), lambda g, b: (g, b)),
        out_shape=jax.ShapeDtypeStruct((D_MODEL, t), BF16),
        scratch_shapes=scratch,
        compiler_params=_params("parallel", "parallel"),
        name=kind + "_attn",
    )(tab, qt, k, vt_ext, bias, *extras)


def _rotary_tables(seq):
    d = RET_DK
    inv_freq = ROPE_BASE ** (-np.arange(0, d, 2, dtype=np.float64) / d)
    ang = np.arange(seq, dtype=np.float64)[:, None] * inv_freq[None, :]
    return jnp.asarray(np.cos(ang), F32), jnp.asarray(np.sin(ang), F32)


def _retention_decay_tables():
    c_len = RET_CHUNK
    log_gamma = np.log(1.0 - 2.0 ** (-5.0 - np.arange(RET_HEADS, dtype=np.float64)))
    pos = np.arange(c_len, dtype=np.float64)
    rel = pos[:, None] - pos[None, :]
    dm = np.where(rel >= 0, np.exp(np.maximum(rel, 0.0)[None] * log_gamma[:, None, None]), 0.0)
    rs = np.exp((pos + 1.0)[None, :] * log_gamma[:, None])[:, :, None]
    ks = np.exp((c_len - 1.0 - pos)[None, :] * log_gamma[:, None])[:, :, None]
    cd = np.exp(c_len * log_gamma)[:, None, None]
    return tuple(jnp.asarray(a, F32) for a in (dm, rs, ks, cd))


def _block_mean_matrix(seq):
    nb = seq // MOBA_BLOCK
    m = (np.arange(seq)[None, :] // MOBA_BLOCK == np.arange(nb)[:, None]) / float(MOBA_BLOCK)
    return jnp.asarray(m, BF16)


def _head_group_matrix():
    g = np.arange(COL_TILE)[:, None] // HEAD_DIM == np.arange(COL_TILE)[None, :] // HEAD_DIM
    return jnp.asarray(g / float(HEAD_DIM), BF16)


def kernel(x, rel_bias, norm1, norm2, w_up, w_down, ret_w_in, ret_w_out,
           moba_w_in, moba_q_norm, moba_k_norm, moba_w_out,
           diff_w_in, diff_q_norm, diff_k_norm, diff_lambda, diff_subln, diff_w_out):
    batch, seq, d = x.shape
    depth = norm1.shape[0]
    assert d == D_MODEL and seq % ROW_TILE == 0 and seq % ATT_TILE == 0 and seq % (RET_UNROLL * RET_CHUNK) == 0
    assert seq % MOBA_BLOCK == 0 and MOBA_BLOCK == ATT_TILE
    t = batch * seq
    xf = x.reshape(t, d)

    bias = _bias_tiles(rel_bias.astype(F32), _bucket_index_tiles())
    grp = _head_group_matrix()
    q_scale = HEAD_DIM ** -0.5 * LOG2E
    mixer_weights = ((ret_w_in, ret_w_out), (moba_w_in, moba_w_out), (diff_w_in, diff_w_out))

    def layer_weights_f32(i):
        kind, j = i % N_MIXERS, i // N_MIXERS
        w_in, w_out = mixer_weights[kind]
        return [(w_in, j), (w_out, j), (w_up, i), (w_down, i)]

    first_jobs = layer_weights_f32(0)
    w_in = first_jobs[0][0][0:1].astype(BF16)
    w_out = wu = wd = None

    for i in range(depth):
        kind, j = i % N_MIXERS, i // N_MIXERS
        nw1 = norm1[i].reshape(1, d)
        nw2 = norm2[i].reshape(1, d)
        cast_jobs = layer_weights_f32(i + 1) if i + 1 < depth else ()
        if kind == 0:
            cos, sin = _rotary_tables(seq)
            dm, rs, ks, cd = _retention_decay_tables()
            (q, k, v, g), early = _ret_proj(xf, nw1, w_in, 0, cos, sin, seq, first_jobs[1:] if i == 0 else ())
            if early:
                w_out, wu, wd = early
            o = _ret_core(q, k, v, dm, rs, ks, cd, batch, seq)
            xf, nxt = _out_mlp(o, g, w_out, 0, xf, nw2, wu, wd, 0, cast_jobs)
        elif kind == 1:
            qn = (jnp.tile(moba_q_norm[j], ATTN_HEADS) * q_scale).reshape(1, d)
            kn = jnp.tile(moba_k_norm[j], ATTN_HEADS).reshape(1, d)
            qt, k, vt = _attn_proj(xf, nw1, w_in, 0, qn, kn, grp, _value_group("moba"))
            ot = _attention("moba", qt, k, vt, bias, (_block_mean_matrix(seq),), 0.0, batch, seq)
            xf, nxt = _out_mlp(ot, None, w_out, 0, xf, nw2, wu, wd, 0, cast_jobs)
        else:
            lambda_init = 0.8 - 0.6 * math.exp(-0.3 * i)
            qn = (jnp.tile(diff_q_norm[j], ATTN_HEADS) * q_scale).reshape(1, d)
            kn = jnp.tile(diff_k_norm[j], ATTN_HEADS).reshape(1, d)
            qt, k, vt = _attn_proj(xf, nw1, w_in, 0, qn, kn, grp, _value_group("diff"))
            extras = (diff_lambda[j].astype(F32), diff_subln[j].reshape(LANES, 1))
            ot = _attention("diff", qt, k, vt, bias, extras, lambda_init, batch, seq)
            xf, nxt = _out_mlp(ot, None, w_out, 0, xf, nw2, wu, wd, 0, cast_jobs)
        if nxt:
            w_in, w_out, wu, wd = nxt
    return xf.reshape(batch, seq, d)
```

```python
import functools
import math

import numpy as np
import jax
import jax.numpy as jnp
from jax import lax
from jax.experimental import pallas as pl
from jax.experimental.pallas import tpu as pltpu

F32 = jnp.float32
BF16 = jnp.bfloat16

D_MODEL = 1024
N_MIXERS = 3
RET_HEADS = 4
RET_DK = D_MODEL // RET_HEADS
RET_DV = 2 * RET_DK
ROPE_BASE = 10000.0
ATTN_HEADS = 16
HEAD_DIM = D_MODEL // ATTN_HEADS
MOBA_BLOCK = 256
MOBA_TOPK = 3
REL_BUCKETS = 32
REL_MAX_EXACT = REL_BUCKETS // 2
REL_MAX_DISTANCE = 1024
D_FF = 4 * D_MODEL
EPS = 1e-6
NEG = -1e30
LOG2E = math.log2(math.e)

LANES = 128
BF16_SUBLANES = 16
VMEM_LIMIT_BYTES = 56 * 1024 * 1024

ROW_TILE = 512
COL_TILE = 256
FF_TILE = 1024
GATE_TILE = 512
RET_CHUNK = 256
RET_UNROLL = 4
ATT_TILE = 256
BIAS_TILES = 6
HEADS_PER_PAIR = LANES // HEAD_DIM
PAIRS_PER_STEP = 2
HEADS_PER_STEP = HEADS_PER_PAIR * PAIRS_PER_STEP
HEAD_ORDER = tuple(p * HEADS_PER_PAIR + h for h in range(HEADS_PER_PAIR) for p in range(PAIRS_PER_STEP))
ONES_ROWS = BF16_SUBLANES


def _value_group(kind):
    return HEAD_DIM if kind == "moba" else LANES


def _params(*sem):
    return pltpu.CompilerParams(dimension_semantics=sem, vmem_limit_bytes=VMEM_LIMIT_BYTES)


def _rms(xf, w):
    ms = jnp.mean(xf * xf, axis=-1, keepdims=True)
    return xf * lax.rsqrt(ms + EPS) * w


def _dot(a, b):
    return jnp.dot(a, b, preferred_element_type=F32)


def _dot_nt(a, b):
    return lax.dot_general(a, b, (((1,), (1,)), ((), ())), preferred_element_type=F32)


def _dot_tn(a, b):
    return lax.dot_general(a, b, (((0,), (0,)), ((), ())), preferred_element_type=F32)


def _cast_specs(cast_jobs, steps):
    in_specs, out_specs, shapes = [], [], []
    for w, w_layer in cast_jobs:
        _, rows, cols = w.shape
        slab = rows // steps
        assert slab * steps == rows and slab % BF16_SUBLANES == 0
        in_specs.append(pl.BlockSpec((None, slab, cols), lambda i, w_layer=w_layer: (w_layer, i, 0)))
        out_specs.append(pl.BlockSpec((None, slab, cols), lambda i: (0, i, 0)))
        shapes.append(jax.ShapeDtypeStruct((1, rows, cols), BF16))
    return in_specs, out_specs, shapes


def _split_cast_refs(refs, n_outputs, n_cast):
    n_in = len(refs) - n_outputs - 2 * n_cast
    sources = refs[n_in:n_in + n_cast]
    outputs = refs[n_in + n_cast:n_in + n_cast + n_outputs]
    return refs[:n_in] + outputs, sources, refs[n_in + n_cast + n_outputs:]


def _run_casts(sources, destinations):
    for src, dst in zip(sources, destinations):
        dst[...] = src[...].astype(BF16)


def _ret_proj_kernel(n_cast, *refs):
    refs, cast_in, cast_out = _split_cast_refs(refs, 4, n_cast)
    _run_casts(cast_in, cast_out)
    x_ref, nw_ref, w_ref, cos_ref, sin_ref, q_ref, k_ref, v_ref, g_ref = refs
    h = _rms(x_ref[...], nw_ref[...]).astype(BF16)
    cos = cos_ref[...]
    sin = sin_ref[...]
    half = RET_DK // 2
    chunks = []
    for out_ref, scale in ((q_ref, 1.0), (k_ref, RET_DK ** -0.5)):
        chunks += [(out_ref, hd * RET_DK, scale) for hd in range(RET_HEADS)]
    for out_ref in (v_ref, g_ref):
        chunks += [(out_ref, c * COL_TILE, None) for c in range(RET_HEADS * RET_DV // COL_TILE)]
    assert RET_DK == COL_TILE

    def project(n):
        return _dot(h, w_ref[:, n * COL_TILE:(n + 1) * COL_TILE])

    acc_next = project(0)
    for n, (out_ref, col0, scale) in enumerate(chunks):
        acc = acc_next
        if n + 1 < len(chunks):
            acc_next = project(n + 1)
        if scale is None:
            out_ref[:, col0:col0 + COL_TILE] = acc.astype(BF16)
        else:
            x1 = acc[:, :half]
            x2 = acc[:, half:]
            r1 = x1 * cos - x2 * sin
            r2 = x1 * sin + x2 * cos
            if scale != 1.0:
                r1 = r1 * scale
                r2 = r2 * scale
            out_ref[:, col0:col0 + half] = r1.astype(BF16)
            out_ref[:, col0 + half:col0 + RET_DK] = r2.astype(BF16)


def _ret_proj(x, nw, w, layer, cos, sin, seq, cast_jobs=()):
    t = x.shape[0]
    cast_in_specs, cast_out_specs, cast_shapes = _cast_specs(cast_jobs, t // ROW_TILE)
    n_in = w.shape[2]
    tiles_per_seq = seq // ROW_TILE
    row = lambda i: (i, 0)
    const = lambda i: (0, 0)
    pos = lambda i: (i % tiles_per_seq, 0)
    n_qk = RET_HEADS * RET_DK
    n_v = RET_HEADS * RET_DV
    outs = pl.pallas_call(
        functools.partial(_ret_proj_kernel, len(cast_jobs)),
        grid=(t // ROW_TILE,),
        in_specs=[
            pl.BlockSpec((ROW_TILE, D_MODEL), row),
            pl.BlockSpec((1, D_MODEL), const),
            pl.BlockSpec((None, D_MODEL, n_in), lambda i: (layer, 0, 0)),
            pl.BlockSpec((ROW_TILE, RET_DK // 2), pos),
            pl.BlockSpec((ROW_TILE, RET_DK // 2), pos),
        ] + cast_in_specs,
        out_specs=[
            pl.BlockSpec((ROW_TILE, n_qk), row),
            pl.BlockSpec((ROW_TILE, n_qk), row),
            pl.BlockSpec((ROW_TILE, n_v), row),
            pl.BlockSpec((ROW_TILE, n_v), row),
        ] + cast_out_specs,
        out_shape=[
            jax.ShapeDtypeStruct((t, n_qk), BF16),
            jax.ShapeDtypeStruct((t, n_qk), BF16),
            jax.ShapeDtypeStruct((t, n_v), BF16),
            jax.ShapeDtypeStruct((t, n_v), BF16),
        ] + cast_shapes,
        compiler_params=_params("parallel"),
        name="ret_proj",
    )(x, nw, w, cos, sin, *[cw for cw, _ in cast_jobs])
    return outs[:4], outs[4:]


def _attn_proj_kernel(v_group, x_ref, nw_ref, w_ref, qn_ref, kn_ref, grp_ref, qt_ref, k_ref, vt_ref):
    h = _rms(x_ref[...], nw_ref[...]).astype(BF16)
    grp = grp_ref[...]
    groups_per_chunk = COL_TILE // v_group
    chunks_per_part = D_MODEL // COL_TILE
    n_chunks = 3 * chunks_per_part

    def project(n):
        return _dot(h, w_ref[:, n * COL_TILE:(n + 1) * COL_TILE])

    acc_next = project(0)
    for n in range(n_chunks):
        acc = acc_next
        if n + 1 < n_chunks:
            acc_next = project(n + 1)
        part, c = divmod(n, chunks_per_part)
        cols = slice(c * COL_TILE, (c + 1) * COL_TILE)
        if part < 2:
            hw_ref = qn_ref if part == 0 else kn_ref
            ms = _dot((acc * acc).astype(BF16), grp)
            acc = acc * lax.rsqrt(ms + EPS) * hw_ref[:, cols]
        if part == 0:
            qt_ref[cols, :] = acc.T.astype(BF16)
        elif part == 1:
            k_ref[:, cols] = acc.astype(BF16)
        else:
            acc_t = acc.T.astype(BF16)
            for p in range(groups_per_chunk):
                row0 = (c * groups_per_chunk + p) * (v_group + ONES_ROWS)
                vt_ref[row0:row0 + v_group, :] = acc_t[p * v_group:(p + 1) * v_group, :]
                vt_ref[row0 + v_group:row0 + v_group + ONES_ROWS, :] = jnp.ones((ONES_ROWS, acc_t.shape[1]), BF16)


def _attn_proj(x, nw, w, layer, qn, kn, grp, v_group):
    t = x.shape[0]
    row = lambda i: (i, 0)
    col = lambda i: (0, i)
    const = lambda i: (0, 0)
    vt_rows = (D_MODEL // v_group) * (v_group + ONES_ROWS)
    return pl.pallas_call(
        functools.partial(_attn_proj_kernel, v_group),
        grid=(t // ROW_TILE,),
        in_specs=[
            pl.BlockSpec((ROW_TILE, D_MODEL), row),
            pl.BlockSpec((1, D_MODEL), const),
            pl.BlockSpec((None, D_MODEL, 3 * D_MODEL), lambda i: (layer, 0, 0)),
            pl.BlockSpec((1, D_MODEL), const),
            pl.BlockSpec((1, D_MODEL), const),
            pl.BlockSpec((COL_TILE, COL_TILE), const),
        ],
        out_specs=[
            pl.BlockSpec((D_MODEL, ROW_TILE), col),
            pl.BlockSpec((ROW_TILE, D_MODEL), row),
            pl.BlockSpec((vt_rows, ROW_TILE), col),
        ],
        out_shape=[
            jax.ShapeDtypeStruct((D_MODEL, t), BF16),
            jax.ShapeDtypeStruct((t, D_MODEL), BF16),
            jax.ShapeDtypeStruct((vt_rows, t), BF16),
        ],
        compiler_params=_params("parallel"),
        name="attn_proj",
    )(x, nw, w, qn, kn, grp)


def _ret_core_kernel(q_ref, k_ref, v_ref, dm_ref, rs_ref, ks_ref, cd_ref, o_ref, state_ref, raw_ref):
    c_len = RET_CHUNK
    n_chunks = q_ref.shape[0] // c_len
    state_ref[...] = jnp.zeros_like(state_ref)
    raw_ref[...] = jnp.zeros_like(raw_ref)

    def chunk(c):
        return pl.ds(pl.multiple_of(c * c_len, c_len), c_len)

    def masked_scores(c):
        return (_dot_nt(q_ref[chunk(c), :], k_ref[chunk(c), :]) * dm_ref[0]).astype(BF16)

    def normalise(c, slot):
        o = raw_ref[slot]
        o_ref[chunk(c), :] = (o * lax.rsqrt(jnp.mean(o * o, axis=-1, keepdims=True) + EPS)).astype(BF16)

    def step(c, slot, s_cur):
        sl = chunk(c)
        q = q_ref[sl, :]
        k = k_ref[sl, :]
        v = v_ref[sl, :]
        state = state_ref[...]
        kd = (k.astype(F32) * ks_ref[0]).astype(BF16)
        kv = _dot_tn(kd, v)
        inner = _dot(s_cur, v)
        cross = _dot(q, state.astype(BF16))
        s_next = masked_scores(jnp.minimum(c + 1, n_chunks - 1))
        normalise(jnp.maximum(c - 1, 0), 1 - slot)
        raw_ref[slot] = inner + cross * rs_ref[0]
        state_ref[...] = state * cd_ref[0] + kv
        return s_next

    def body(u, s_cur):
        for r in range(RET_UNROLL):
            s_cur = step(RET_UNROLL * u + r, r % 2, s_cur)
        return s_cur

    lax.fori_loop(0, n_chunks // RET_UNROLL, body, masked_scores(0))
    normalise(n_chunks - 1, 1)


def _ret_core(q, k, v, dm, rs, ks, cd, batch, seq):
    t = q.shape[0]
    c_len = RET_CHUNK
    tok = lambda b, h: (b, h)
    head3 = lambda b, h: (h, 0, 0)
    return pl.pallas_call(
        _ret_core_kernel,
        grid=(batch, RET_HEADS),
        in_specs=[
            pl.BlockSpec((seq, RET_DK), tok),
            pl.BlockSpec((seq, RET_DK), tok),
            pl.BlockSpec((seq, RET_DV), tok),
            pl.BlockSpec((1, c_len, c_len), head3),
            pl.BlockSpec((1, c_len, 1), head3),
            pl.BlockSpec((1, c_len, 1), head3),
            pl.BlockSpec((1, 1, 1), head3),
        ],
        out_specs=pl.BlockSpec((seq, RET_DV), tok),
        out_shape=jax.ShapeDtypeStruct((t, RET_HEADS * RET_DV), BF16),
        scratch_shapes=[pltpu.VMEM((RET_DK, RET_DV), F32), pltpu.VMEM((2, RET_CHUNK, RET_DV), F32)],
        compiler_params=_params("parallel", "parallel"),
        name="ret_core",
    )(q, k, v, dm, rs, ks, cd)


def _out_mlp_kernel(gated, n_cast, *refs):
    refs, cast_in, cast_out = _split_cast_refs(refs, 1, n_cast)
    _run_casts(cast_in, cast_out)
    if gated:
        a_ref, g_ref, wo_ref, x_ref, nw_ref, wu_ref, wd_ref, o_ref = refs
        n_chunks = a_ref.shape[1] // GATE_TILE

        def gate(c):
            cols = slice(c * GATE_TILE, (c + 1) * GATE_TILE)
            g = g_ref[:, cols].astype(F32)
            return (g * (1.0 / (1.0 + jnp.exp(-g))) * a_ref[:, cols].astype(F32)).astype(BF16)

        a_next = gate(0)
        mix = None
        for c in range(n_chunks):
            a = a_next
            if c + 1 < n_chunks:
                a_next = gate(c + 1)
            part = _dot(a, wo_ref[c * GATE_TILE:(c + 1) * GATE_TILE, :])
            mix = part if mix is None else mix + part
    else:
        a_ref, wo_ref, x_ref, nw_ref, wu_ref, wd_ref, o_ref = refs
        mix = _dot_tn(a_ref[...], wo_ref[...])
    x = x_ref[...] + mix
    h = _rms(x, nw_ref[...]).astype(BF16)
    acc = x
    for c in range(D_FF // FF_TILE):
        u = _dot(h, wu_ref[:, c * FF_TILE:(c + 1) * FF_TILE])
        u = jnp.maximum(u, 0.0)
        acc = acc + _dot((u * u).astype(BF16), wd_ref[c * FF_TILE:(c + 1) * FF_TILE, :])
    o_ref[...] = acc


def _out_mlp(a, g, wo, mixer_layer, x, nw, wu, wd, layer, cast_jobs=()):
    t = x.shape[0]
    steps = t // ROW_TILE
    cast_in_specs, cast_out_specs, cast_shapes = _cast_specs(cast_jobs, steps)
    ka = wo.shape[1]
    row = lambda i: (i, 0)
    const = lambda i: (0, 0)
    gated = g is not None
    if gated:
        mix_specs = [pl.BlockSpec((ROW_TILE, ka), row), pl.BlockSpec((ROW_TILE, ka), row)]
        mix_args = [a, g]
    else:
        mix_specs = [pl.BlockSpec((ka, ROW_TILE), lambda i: (0, i))]
        mix_args = [a]
    outs = pl.pallas_call(
        functools.partial(_out_mlp_kernel, gated, len(cast_jobs)),
        grid=(steps,),
        in_specs=mix_specs + [
            pl.BlockSpec((None, ka, D_MODEL), lambda i: (mixer_layer, 0, 0)),
            pl.BlockSpec((ROW_TILE, D_MODEL), row),
            pl.BlockSpec((1, D_MODEL), const),
            pl.BlockSpec((None, D_MODEL, D_FF), lambda i: (layer, 0, 0), pipeline_mode=pl.Buffered(1)),
            pl.BlockSpec((None, D_FF, D_MODEL), lambda i: (layer, 0, 0), pipeline_mode=pl.Buffered(1)),
        ] + cast_in_specs,
        out_specs=[pl.BlockSpec((ROW_TILE, D_MODEL), row)] + cast_out_specs,
        out_shape=[jax.ShapeDtypeStruct((t, D_MODEL), F32)] + cast_shapes,
        compiler_params=_params("parallel"),
        name="out_mlp_gated" if gated else "out_mlp",
    )(*mix_args, wo, x, nw, wu, wd, *[w for w, _ in cast_jobs])
    return outs[0], outs[1:]


def _bias_tiles_kernel(bucket_ranges, rb_ref, idx_ref, o_ref):
    head = pl.program_id(0)
    for d, (lo, hi) in enumerate(bucket_ranges):
        idx = idx_ref[d]
        acc = jnp.full(idx.shape, NEG, F32)
        for b in range(lo, hi + 1):
            acc = jnp.where(idx == b, rb_ref[b, head] * LOG2E, acc)
        o_ref[0, d] = acc


def _bucket_ranges():
    ranges = []
    for d in range(BIAS_TILES):
        lo_dist = max(d * ATT_TILE - (ATT_TILE - 1), 0)
        hi_dist = d * ATT_TILE + (ATT_TILE - 1)

        def bucket(n):
            if n < REL_MAX_EXACT:
                return n
            return min(REL_MAX_EXACT + int(math.log(n / REL_MAX_EXACT) / math.log(REL_MAX_DISTANCE / REL_MAX_EXACT)
                                           * (REL_BUCKETS - REL_MAX_EXACT)), REL_BUCKETS - 1)

        ranges.append((max(bucket(lo_dist) - 1, 0), min(bucket(hi_dist) + 1, REL_BUCKETS - 1)))
    return tuple(ranges)


def _bias_tiles(rel_bias, bucket_idx):
    nd, tq, tk = bucket_idx.shape
    return pl.pallas_call(
        functools.partial(_bias_tiles_kernel, _bucket_ranges()),
        grid=(ATTN_HEADS,),
        in_specs=[
            pl.BlockSpec(memory_space=pltpu.SMEM),
            pl.BlockSpec((nd, tq, tk), lambda h: (0, 0, 0)),
        ],
        out_specs=pl.BlockSpec((1, nd, tq, tk), lambda h: (h, 0, 0, 0)),
        out_shape=jax.ShapeDtypeStruct((ATTN_HEADS, nd, tq, tk), F32),
        compiler_params=_params("parallel"),
        name="bias_tiles",
    )(rel_bias, bucket_idx)


def _rel_bucket(dist):
    n = jnp.maximum(dist, 0)
    nf = jnp.maximum(n, 1).astype(F32)
    large = REL_MAX_EXACT + (jnp.log(nf / REL_MAX_EXACT) / math.log(REL_MAX_DISTANCE / REL_MAX_EXACT)
                             * (REL_BUCKETS - REL_MAX_EXACT)).astype(jnp.int32)
    large = jnp.minimum(large, REL_BUCKETS - 1)
    return jnp.where(n < REL_MAX_EXACT, n, large)


def _bucket_index_tiles():
    r = np.arange(ATT_TILE)
    dist = (np.arange(BIAS_TILES)[:, None, None] * ATT_TILE + r[None, None, :] - r[None, :, None])
    dist = jnp.asarray(dist, jnp.int32)
    return jnp.where(dist >= 0, _rel_bucket(dist), REL_BUCKETS).astype(jnp.int32)


TAB_Q, TAB_K, TAB_DELTA, TAB_FIRST, TAB_ACC = range(5)
PIPE_LAG = 2
EMIT_UNROLL = 4
PIPE_UNROLL = 6


def _tile_schedule(nq, own_first):
    rows = []
    for i in range(nq):
        keys = ([i] + list(range(i))) if own_first else list(range(i + 1))
        for n, j in enumerate(keys):
            rows.append((i, j, min(i - j, BIAS_TILES - 1), int(n == 0), i))
    n_iters = -(-(len(rows) + PIPE_LAG) // PIPE_UNROLL) * PIPE_UNROLL
    idle = (0, 0, 0, 1, nq)
    cols = [idle] * PIPE_LAG + rows
    cols += [idle] * (n_iters + PIPE_LAG - len(cols))
    return jnp.asarray(np.array(cols, np.int32).T), n_iters


def _tile_slice(idx):
    return pl.ds(pl.multiple_of(idx * ATT_TILE, ATT_TILE), ATT_TILE)


def _head_row_mask(h):
    row = lax.broadcasted_iota(jnp.int32, (LANES, 1), 0)
    return (row >= h * HEAD_DIM) & (row < (h + 1) * HEAD_DIM)


def _pair_rows(pair, n=LANES):
    return slice(pair * n, (pair + 1) * n)


def _attn_kernel(kind, lambda_init, n_iters, tab_ref, qt_ref, k_ref, vt_ref, bias_ref, *rest):
    if kind == "moba":
        blk_ref, ot_ref, qts_ref, vts_ref, s_ref, p_ref, acc_ref, neg_ref = rest
    else:
        lam_ref, sw_ref, ot_ref, qts_ref, vts_ref, s_ref, p_ref, acc_ref = rest
    heads = range(HEADS_PER_STEP)
    seq = qt_ref.shape[1]
    part_lanes = LANES if kind == "moba" else ATT_TILE
    query_parts = tuple(slice(c, c + part_lanes) for c in range(0, ATT_TILE, part_lanes))
    acc_rows = acc_ref.shape[2]

    n_tiles = seq // ATT_TILE
    q_heads = []
    for e in heads:
        pair, h = divmod(e, HEADS_PER_PAIR)
        qf = qt_ref[_pair_rows(pair), :].astype(F32)
        q_heads.append(jnp.where(_head_row_mask(h), qf, 0.0).astype(BF16))
        for i in range(n_tiles):
            qts_ref[i, pair, :, h * ATT_TILE:(h + 1) * ATT_TILE] = q_heads[e][:, i * ATT_TILE:(i + 1) * ATT_TILE]
    for j in range(n_tiles):
        vts_ref[j] = vt_ref[:, j * ATT_TILE:(j + 1) * ATT_TILE]
    s_ref[...] = jnp.zeros_like(s_ref)
    p_ref[...] = jnp.zeros_like(p_ref)
    acc_ref[...] = jnp.zeros_like(acc_ref)

    if kind == "moba":
        nb = blk_ref.shape[0]
        kmean = _dot(blk_ref[...], k_ref[...])
        km_hi = kmean.astype(BF16)
        km_lo = (kmean - km_hi.astype(F32)).astype(BF16)
        blk_id = lax.broadcasted_iota(jnp.int32, (nb, ATT_TILE), 0)
        for e in heads:
            pair = e // HEADS_PER_PAIR
            q_e = q_heads[e]
            gate_all = _dot(km_hi[:, _pair_rows(pair)], q_e) + _dot(km_lo[:, _pair_rows(pair)], q_e)
            for i in range(n_tiles):
                cols = slice(i * ATT_TILE, (i + 1) * ATT_TILE)
                past = blk_id < i
                gate = jnp.where(past, gate_all[:, cols], NEG)
                rank = jnp.zeros((nb, ATT_TILE), jnp.int32)
                for c in range(i):
                    gc = gate[c:c + 1, :]
                    tie = jnp.where(blk_id > c, 1, 0)
                    rank = rank + jnp.where(gc > gate, 1, jnp.where(gc == gate, tie, 0))
                chosen = jnp.where(rank < MOBA_TOPK, jnp.where(past, 1, 0), 0)
                keep = jnp.maximum(chosen, jnp.where(blk_id == i, 1, 0))
                neg_ref[e, :, cols] = jnp.where(keep == 1, 0.0, NEG)

    def step(t, cur, carry):
        nxt = 1 - cur
        m_prev, alpha_prev, mtile_prev = carry
        ic = tab_ref[TAB_ACC, t]
        jc = tab_ref[TAB_K, t]
        ib = tab_ref[TAB_Q, t + 1]
        jb = tab_ref[TAB_K, t + 1]
        first = tab_ref[TAB_FIRST, t + 1] != 0
        ia = tab_ref[TAB_Q, t + 2]
        ja = tab_ref[TAB_K, t + 2]
        da = tab_ref[TAB_DELTA, t + 2]
        m_new, alpha_new, mtile_new = ([None] * HEADS_PER_STEP for _ in range(3))
        pair_scores = {}
        for e in HEAD_ORDER:
            pair = e // HEADS_PER_PAIR

            def stage_a():
                if pair not in pair_scores:
                    pair_scores[pair] = _dot(k_ref[_tile_slice(ja), _pair_rows(pair)], qts_ref[ia, pair])
                h = e % HEADS_PER_PAIR
                sb = pair_scores[pair][:, h * ATT_TILE:(h + 1) * ATT_TILE] + bias_ref[e, da]
                s_ref[nxt, e] = sb
                mtile_new[e] = tuple(jnp.max(sb[:, lanes], axis=0, keepdims=True) for lanes in query_parts)

            def stage_c():
                vt = vts_ref[jc, _pair_rows(e if kind == "moba" else pair, acc_rows), :]
                pv = _dot(vt, p_ref[cur, e])
                alpha = alpha_prev[e][0] if len(query_parts) == 1 else jnp.concatenate(alpha_prev[e], axis=1)
                acc_ref[ic, e] = alpha * acc_ref[ic, e] + pv

            for stage in ((stage_a, stage_c) if kind == "moba" else (stage_c, stage_a)):
                stage()

            m_parts, alpha_parts = [], []
            if kind == "moba":
                neg_row = neg_ref[e, pl.ds(jb, 1), _tile_slice(ib)]
            for part, lanes in enumerate(query_parts):
                s = s_ref[cur, e, :, lanes]
                m_in = jnp.where(first, -jnp.inf, m_prev[e][part])
                m_tile = mtile_prev[e][part]
                if kind == "moba":
                    neg = neg_row[:, lanes]
                    m_e = jnp.maximum(m_in, m_tile + neg)
                    shift = m_e - neg
                else:
                    m_e = jnp.maximum(m_in, m_tile)
                    shift = m_e
                p_ref[nxt, e, :, lanes] = jnp.exp2(s - shift).astype(BF16)
                alpha_parts.append(jnp.exp2(m_in - m_e))
                m_parts.append(m_e)
            alpha_new[e] = tuple(alpha_parts)
            m_new[e] = tuple(m_parts)

        return tuple(m_new), tuple(alpha_new), tuple(mtile_new)

    zeros = tuple(tuple(jnp.zeros((1, part_lanes), F32) for _ in query_parts) for _ in heads)

    def body(u, carry):
        for r in range(PIPE_UNROLL):
            carry = step(PIPE_UNROLL * u + r, r % 2, carry)
        return carry

    lax.fori_loop(0, n_iters // PIPE_UNROLL, body, (zeros, zeros, zeros))

    v_group = acc_rows - ONES_ROWS
    if kind != "moba":
        lam = lam_ref[...]
        lam_full = (jnp.exp(jnp.sum(lam[0:1] * lam[1:2], axis=-1, keepdims=True))
                    - jnp.exp(jnp.sum(lam[2:3] * lam[3:4], axis=-1, keepdims=True)) + lambda_init)

    def emit_row(i):
        cols = _tile_slice(i)
        for pair in range(PAIRS_PER_STEP):
            outs = []
            for h in range(HEADS_PER_PAIR):
                e = pair * HEADS_PER_PAIR + h
                outs.append(acc_ref[i, e, :v_group, :] * (1.0 / acc_ref[i, e, v_group:v_group + 1, :]))
            if kind == "moba":
                o = jnp.concatenate(outs, axis=0)
            else:
                o = outs[0] - lam_full * outs[1]
                o = (o * lax.rsqrt(jnp.mean(o * o, axis=0, keepdims=True) + EPS)
                     * sw_ref[...] * (1.0 - lambda_init))
            ot_ref[_pair_rows(pair), cols] = o.astype(BF16)

    def emit(u, carry):
        for r in range(EMIT_UNROLL):
            emit_row(EMIT_UNROLL * u + r)
        return carry

    assert n_tiles % EMIT_UNROLL == 0
    lax.fori_loop(0, n_tiles // EMIT_UNROLL, emit, 0)


def _attention(kind, qt, k, vt_ext, bias, extras, lambda_init, batch, seq):
    v_group = _value_group(kind)
    acc_rows = v_group + ONES_ROWS
    vt_block_rows = acc_rows * (LANES * PAIRS_PER_STEP // v_group)
    t = k.shape[0]
    nq = seq // ATT_TILE
    tab, n_iters = _tile_schedule(nq, own_first=(kind == "moba"))
    groups = ATTN_HEADS // HEADS_PER_STEP
    rows = LANES * PAIRS_PER_STEP
    in_specs = [
        pl.BlockSpec(memory_space=pltpu.SMEM),
        pl.BlockSpec((rows, seq), lambda g, b: (g, b)),
        pl.BlockSpec((seq, rows), lambda g, b: (b, g)),
        pl.BlockSpec((vt_block_rows, seq), lambda g, b: (g, b)),
        pl.BlockSpec((HEADS_PER_STEP, BIAS_TILES, ATT_TILE, ATT_TILE), lambda g, b: (g, 0, 0, 0)),
    ]
    scratch = [
        pltpu.VMEM((nq, PAIRS_PER_STEP, LANES, HEADS_PER_PAIR * ATT_TILE), BF16),
        pltpu.VMEM((nq, vt_block_rows, ATT_TILE), BF16),
        pltpu.VMEM((2, HEADS_PER_STEP, ATT_TILE, ATT_TILE), F32),
        pltpu.VMEM((2, HEADS_PER_STEP, ATT_TILE, ATT_TILE), BF16),
        pltpu.VMEM((nq + 1, HEADS_PER_STEP, acc_rows, ATT_TILE), F32),
    ]
    if kind == "moba":
        (blk,) = extras
        nb = blk.shape[0]
        in_specs.append(pl.BlockSpec((nb, seq), lambda g, b: (0, 0)))
        scratch.append(pltpu.VMEM((HEADS_PER_STEP, nb, seq), F32))
    else:
        lam, sw = extras
        in_specs += [
            pl.BlockSpec((4, HEAD_DIM), lambda g, b: (0, 0)),
            pl.BlockSpec((LANES, 1), lambda g, b: (0, 0)),
        ]
    return pl.pallas_call(
        functools.partial(_attn_kernel, kind, lambda_init, n_iters),
        grid=(groups, batch),
        in_specs=in_specs,
        out_specs=pl.BlockSpec((rows, seq), lambda g, b: (g, b)),
        out_shape=jax.ShapeDtypeStruct((D_MODEL, t), BF16),
        scratch_shapes=scratch,
        compiler_params=_params("parallel", "parallel"),
        name=kind + "_attn",
    )(tab, qt, k, vt_ext, bias, *extras)


def _rotary_tables(seq):
    d = RET_DK
    inv_freq = ROPE_BASE ** (-np.arange(0, d, 2, dtype=np.float64) / d)
    ang = np.arange(seq, dtype=np.float64)[:, None] * inv_freq[None, :]
    return jnp.asarray(np.cos(ang), F32), jnp.asarray(np.sin(ang), F32)


def _retention_decay_tables():
    c_len = RET_CHUNK
    log_gamma = np.log(1.0 - 2.0 ** (-5.0 - np.arange(RET_HEADS, dtype=np.float64)))
    pos = np.arange(c_len, dtype=np.float64)
    rel = pos[:, None] - pos[None, :]
    dm = np.where(rel >= 0, np.exp(np.maximum(rel, 0.0)[None] * log_gamma[:, None, None]), 0.0)
    rs = np.exp((pos + 1.0)[None, :] * log_gamma[:, None])[:, :, None]
    ks = np.exp((c_len - 1.0 - pos)[None, :] * log_gamma[:, None])[:, :, None]
    cd = np.exp(c_len * log_gamma)[:, None, None]
    return tuple(jnp.asarray(a, F32) for a in (dm, rs, ks, cd))


def _block_mean_matrix(seq):
    nb = seq // MOBA_BLOCK
    m = (np.arange(seq)[None, :] // MOBA_BLOCK == np.arange(nb)[:, None]) / float(MOBA_BLOCK)
    return jnp.asarray(m, BF16)


def _head_group_matrix():
    g = np.arange(COL_TILE)[:, None] // HEAD_DIM == np.arange(COL_TILE)[None, :] // HEAD_DIM
    return jnp.asarray(g / float(HEAD_DIM), BF16)


def kernel(x, rel_bias, norm1, norm2, w_up, w_down, ret_w_in, ret_w_out,
           moba_w_in, moba_q_norm, moba_k_norm, moba_w_out,
           diff_w_in, diff_q_norm, diff_k_norm, diff_lambda, diff_subln, diff_w_out):
    batch, seq, d = x.shape
    depth = norm1.shape[0]
    assert d == D_MODEL and seq % ROW_TILE == 0 and seq % ATT_TILE == 0 and seq % (RET_UNROLL * RET_CHUNK) == 0
    assert seq % MOBA_BLOCK == 0 and MOBA_BLOCK == ATT_TILE
    t = batch * seq
    xf = x.reshape(t, d)

    bias = _bias_tiles(rel_bias.astype(F32), _bucket_index_tiles())
    grp = _head_group_matrix()
    q_scale = HEAD_DIM ** -0.5 * LOG2E
    mixer_weights = ((ret_w_in, ret_w_out), (moba_w_in, moba_w_out), (diff_w_in, diff_w_out))

    def layer_weights_f32(i):
        kind, j = i % N_MIXERS, i // N_MIXERS
        w_in, w_out = mixer_weights[kind]
        return [(w_in, j), (w_out, j), (w_up, i), (w_down, i)]

    first_jobs = layer_weights_f32(0)
    w_in = first_jobs[0][0][0:1].astype(BF16)
    w_out = wu = wd = None

    for i in range(depth):
        kind, j = i % N_MIXERS, i // N_MIXERS
        nw1 = norm1[i].reshape(1, d)
        nw2 = norm2[i].reshape(1, d)
        cast_jobs = layer_weights_f32(i + 1) if i + 1 < depth else ()
        if kind == 0:
            cos, sin = _rotary_tables(seq)
            dm, rs, ks, cd = _retention_decay_tables()
            (q, k, v, g), early = _ret_proj(xf, nw1, w_in, 0, cos, sin, seq, first_jobs[1:] if i == 0 else ())
            if early:
                w_out, wu, wd = early
            o = _ret_core(q, k, v, dm, rs, ks, cd, batch, seq)
            xf, nxt = _out_mlp(o, g, w_out, 0, xf, nw2, wu, wd, 0, cast_jobs)
        elif kind == 1:
            qn = (jnp.tile(moba_q_norm[j], ATTN_HEADS) * q_scale).reshape(1, d)
            kn = jnp.tile(moba_k_norm[j], ATTN_HEADS).reshape(1, d)
            qt, k, vt = _attn_proj(xf, nw1, w_in, 0, qn, kn, grp, _value_group("moba"))
            ot = _attention("moba", qt, k, vt, bias, (_block_mean_matrix(seq),), 0.0, batch, seq)
            xf, nxt = _out_mlp(ot, None, w_out, 0, xf, nw2, wu, wd, 0, cast_jobs)
        else:
            lambda_init = 0.8 - 0.6 * math.exp(-0.3 * i)
            qn = (jnp.tile(diff_q_norm[j], ATTN_HEADS) * q_scale).reshape(1, d)
            kn = jnp.tile(diff_k_norm[j], ATTN_HEADS).reshape(1, d)
            qt, k, vt = _attn_proj(xf, nw1, w_in, 0, qn, kn, grp, _value_group("diff"))
            extras = (diff_lambda[j].astype(F32), diff_subln[j].reshape(LANES, 1))
            ot = _attention("diff", qt, k, vt, bias, extras, lambda_init, batch, seq)
            xf, nxt = _out_mlp(ot, None, w_out, 0, xf, nw2, wu, wd, 0, cast_jobs)
        if nxt:
            w_in, w_out, wu, wd = nxt
    return xf.reshape(batch, seq, d)
```

```python
import functools
import math

import numpy as np
import jax
import jax.numpy as jnp
from jax import lax
from jax.experimental import pallas as pl
from jax.experimental.pallas import tpu as pltpu

F32 = jnp.float32
BF16 = jnp.bfloat16

D_MODEL = 1024
N_MIXERS = 3
RET_HEADS = 4
RET_DK = D_MODEL // RET_HEADS
RET_DV = 2 * RET_DK
ROPE_BASE = 10000.0
ATTN_HEADS = 16
HEAD_DIM = D_MODEL // ATTN_HEADS
MOBA_BLOCK = 256
MOBA_TOPK = 3
REL_BUCKETS = 32
REL_MAX_EXACT = REL_BUCKETS // 2
REL_MAX_DISTANCE = 1024
D_FF = 4 * D_MODEL
EPS = 1e-6
NEG = -1e30
LOG2E = math.log2(math.e)

LANES = 128
BF16_SUBLANES = 16
VMEM_LIMIT_BYTES = 56 * 1024 * 1024

ROW_TILE = 512
COL_TILE = 256
FF_TILE = 1024
GATE_TILE = 512
RET_CHUNK = 256
RET_UNROLL = 4
ATT_TILE = 256
BIAS_TILES = 6
HEADS_PER_PAIR = LANES // HEAD_DIM
PAIRS_PER_STEP = 2
HEADS_PER_STEP = HEADS_PER_PAIR * PAIRS_PER_STEP
HEAD_ORDER = tuple(p * HEADS_PER_PAIR + h for h in range(HEADS_PER_PAIR) for p in range(PAIRS_PER_STEP))
ONES_ROWS = BF16_SUBLANES


def _value_group(kind):
    return HEAD_DIM if kind == "moba" else LANES


def _params(*sem):
    return pltpu.CompilerParams(dimension_semantics=sem, vmem_limit_bytes=VMEM_LIMIT_BYTES)


def _rms(xf, w):
    ms = jnp.mean(xf * xf, axis=-1, keepdims=True)
    return xf * lax.rsqrt(ms + EPS) * w


def _dot(a, b):
    return jnp.dot(a, b, preferred_element_type=F32)


def _dot_nt(a, b):
    return lax.dot_general(a, b, (((1,), (1,)), ((), ())), preferred_element_type=F32)


def _dot_tn(a, b):
    return lax.dot_general(a, b, (((0,), (0,)), ((), ())), preferred_element_type=F32)


def _cast_specs(cast_jobs, steps):
    in_specs, out_specs, shapes = [], [], []
    for w, w_layer in cast_jobs:
        _, rows, cols = w.shape
        slab = rows // steps
        assert slab * steps == rows and slab % BF16_SUBLANES == 0
        in_specs.append(pl.BlockSpec((None, slab, cols), lambda i, w_layer=w_layer: (w_layer, i, 0)))
        out_specs.append(pl.BlockSpec((None, slab, cols), lambda i: (0, i, 0)))
        shapes.append(jax.ShapeDtypeStruct((1, rows, cols), BF16))
    return in_specs, out_specs, shapes


def _split_cast_refs(refs, n_outputs, n_cast):
    n_in = len(refs) - n_outputs - 2 * n_cast
    sources = refs[n_in:n_in + n_cast]
    outputs = refs[n_in + n_cast:n_in + n_cast + n_outputs]
    return refs[:n_in] + outputs, sources, refs[n_in + n_cast + n_outputs:]


def _run_casts(sources, destinations):
    for src, dst in zip(sources, destinations):
        dst[...] = src[...].astype(BF16)


def _ret_proj_kernel(n_cast, *refs):
    refs, cast_in, cast_out = _split_cast_refs(refs, 4, n_cast)
    _run_casts(cast_in, cast_out)
    x_ref, nw_ref, w_ref, cos_ref, sin_ref, q_ref, k_ref, v_ref, g_ref = refs
    h = _rms(x_ref[...], nw_ref[...]).astype(BF16)
    cos = cos_ref[...]
    sin = sin_ref[...]
    half = RET_DK // 2
    chunks = []
    for out_ref, scale in ((q_ref, 1.0), (k_ref, RET_DK ** -0.5)):
        chunks += [(out_ref, hd * RET_DK, scale) for hd in range(RET_HEADS)]
    for out_ref in (v_ref, g_ref):
        chunks += [(out_ref, c * COL_TILE, None) for c in range(RET_HEADS * RET_DV // COL_TILE)]
    assert RET_DK == COL_TILE

    def project(n):
        return _dot(h, w_ref[:, n * COL_TILE:(n + 1) * COL_TILE])

    acc_next = project(0)
    for n, (out_ref, col0, scale) in enumerate(chunks):
        acc = acc_next
        if n + 1 < len(chunks):
            acc_next = project(n + 1)
        if scale is None:
            out_ref[:, col0:col0 + COL_TILE] = acc.astype(BF16)
        else:
            x1 = acc[:, :half]
            x2 = acc[:, half:]
            r1 = x1 * cos - x2 * sin
            r2 = x1 * sin + x2 * cos
            if scale != 1.0:
                r1 = r1 * scale
                r2 = r2 * scale
            out_ref[:, col0:col0 + half] = r1.astype(BF16)
            out_ref[:, col0 + half:col0 + RET_DK] = r2.astype(BF16)


def _ret_proj(x, nw, w, layer, cos, sin, seq, cast_jobs=()):
    t = x.shape[0]
    cast_in_specs, cast_out_specs, cast_shapes = _cast_specs(cast_jobs, t // ROW_TILE)
    n_in = w.shape[2]
    tiles_per_seq = seq // ROW_TILE
    row = lambda i: (i, 0)
    const = lambda i: (0, 0)
    pos = lambda i: (i % tiles_per_seq, 0)
    n_qk = RET_HEADS * RET_DK
    n_v = RET_HEADS * RET_DV
    outs = pl.pallas_call(
        functools.partial(_ret_proj_kernel, len(cast_jobs)),
        grid=(t // ROW_TILE,),
        in_specs=[
            pl.BlockSpec((ROW_TILE, D_MODEL), row),
            pl.BlockSpec((1, D_MODEL), const),
            pl.BlockSpec((None, D_MODEL, n_in), lambda i: (layer, 0, 0)),
            pl.BlockSpec((ROW_TILE, RET_DK // 2), pos),
            pl.BlockSpec((ROW_TILE, RET_DK // 2), pos),
        ] + cast_in_specs,
        out_specs=[
            pl.BlockSpec((ROW_TILE, n_qk), row),
            pl.BlockSpec((ROW_TILE, n_qk), row),
            pl.BlockSpec((ROW_TILE, n_v), row),
            pl.BlockSpec((ROW_TILE, n_v), row),
        ] + cast_out_specs,
        out_shape=[
            jax.ShapeDtypeStruct((t, n_qk), BF16),
            jax.ShapeDtypeStruct((t, n_qk), BF16),
            jax.ShapeDtypeStruct((t, n_v), BF16),
            jax.ShapeDtypeStruct((t, n_v), BF16),
        ] + cast_shapes,
        compiler_params=_params("parallel"),
        name="ret_proj",
    )(x, nw, w, cos, sin, *[cw for cw, _ in cast_jobs])
    return outs[:4], outs[4:]


def _attn_proj_kernel(v_group, x_ref, nw_ref, w_ref, qn_ref, kn_ref, grp_ref, qt_ref, k_ref, vt_ref):
    h = _rms(x_ref[...], nw_ref[...]).astype(BF16)
    grp = grp_ref[...]
    groups_per_chunk = COL_TILE // v_group
    chunks_per_part = D_MODEL // COL_TILE
    n_chunks = 3 * chunks_per_part
    tiles_per_step = ROW_TILE // ATT_TILE

    def project(n):
        return _dot(h, w_ref[:, n * COL_TILE:(n + 1) * COL_TILE])

    acc_next = project(0)
    for n in range(n_chunks):
        acc = acc_next
        if n + 1 < n_chunks:
            acc_next = project(n + 1)
        part, c = divmod(n, chunks_per_part)
        cols = slice(c * COL_TILE, (c + 1) * COL_TILE)
        if part < 2:
            hw_ref = qn_ref if part == 0 else kn_ref
            ms = _dot((acc * acc).astype(BF16), grp)
            acc = acc * lax.rsqrt(ms + EPS) * hw_ref[:, cols]
        if part == 0:
            acc_t = acc.T.astype(BF16)
            for tile in range(tiles_per_step):
                qt_ref[tile, cols, :] = acc_t[:, tile * ATT_TILE:(tile + 1) * ATT_TILE]
        elif part == 1:
            k_ref[:, cols] = acc.astype(BF16)
        else:
            acc_t = acc.T.astype(BF16)
            for p in range(groups_per_chunk):
                row0 = (c * groups_per_chunk + p) * (v_group + ONES_ROWS)
                for tile in range(tiles_per_step):
                    vt_ref[tile, row0:row0 + v_group, :] = acc_t[p * v_group:(p + 1) * v_group,
                                                                 tile * ATT_TILE:(tile + 1) * ATT_TILE]
                    vt_ref[tile, row0 + v_group:row0 + v_group + ONES_ROWS, :] = jnp.ones((ONES_ROWS, ATT_TILE), BF16)


def _attn_proj(x, nw, w, layer, qn, kn, grp, v_group):
    t = x.shape[0]
    row = lambda i: (i, 0)
    tile3 = lambda i: (i, 0, 0)
    const = lambda i: (0, 0)
    vt_rows = (D_MODEL // v_group) * (v_group + ONES_ROWS)
    tiles_per_step = ROW_TILE // ATT_TILE
    return pl.pallas_call(
        functools.partial(_attn_proj_kernel, v_group),
        grid=(t // ROW_TILE,),
        in_specs=[
            pl.BlockSpec((ROW_TILE, D_MODEL), row),
            pl.BlockSpec((1, D_MODEL), const),
            pl.BlockSpec((None, D_MODEL, 3 * D_MODEL), lambda i: (layer, 0, 0)),
            pl.BlockSpec((1, D_MODEL), const),
            pl.BlockSpec((1, D_MODEL), const),
            pl.BlockSpec((COL_TILE, COL_TILE), const),
        ],
        out_specs=[
            pl.BlockSpec((tiles_per_step, D_MODEL, ATT_TILE), tile3),
            pl.BlockSpec((ROW_TILE, D_MODEL), row),
            pl.BlockSpec((tiles_per_step, vt_rows, ATT_TILE), tile3),
        ],
        out_shape=[
            jax.ShapeDtypeStruct((t // ATT_TILE, D_MODEL, ATT_TILE), BF16),
            jax.ShapeDtypeStruct((t, D_MODEL), BF16),
            jax.ShapeDtypeStruct((t // ATT_TILE, vt_rows, ATT_TILE), BF16),
        ],
        compiler_params=_params("parallel"),
        name="attn_proj",
    )(x, nw, w, qn, kn, grp)


def _ret_core_kernel(q_ref, k_ref, v_ref, dm_ref, rs_ref, ks_ref, cd_ref, o_ref, state_ref, raw_ref):
    c_len = RET_CHUNK
    n_chunks = q_ref.shape[0] // c_len
    state_ref[...] = jnp.zeros_like(state_ref)
    raw_ref[...] = jnp.zeros_like(raw_ref)

    def chunk(c):
        return pl.ds(pl.multiple_of(c * c_len, c_len), c_len)

    def masked_scores(c):
        return (_dot_nt(q_ref[chunk(c), :], k_ref[chunk(c), :]) * dm_ref[0]).astype(BF16)

    def normalise(c, slot):
        o = raw_ref[slot]
        o_ref[chunk(c), :] = (o * lax.rsqrt(jnp.mean(o * o, axis=-1, keepdims=True) + EPS)).astype(BF16)

    def step(c, slot, s_cur):
        sl = chunk(c)
        q = q_ref[sl, :]
        k = k_ref[sl, :]
        v = v_ref[sl, :]
        state = state_ref[...]
        kd = (k.astype(F32) * ks_ref[0]).astype(BF16)
        kv = _dot_tn(kd, v)
        inner = _dot(s_cur, v)
        cross = _dot(q, state.astype(BF16))
        s_next = masked_scores(jnp.minimum(c + 1, n_chunks - 1))
        normalise(jnp.maximum(c - 1, 0), 1 - slot)
        raw_ref[slot] = inner + cross * rs_ref[0]
        state_ref[...] = state * cd_ref[0] + kv
        return s_next

    def body(u, s_cur):
        for r in range(RET_UNROLL):
            s_cur = step(RET_UNROLL * u + r, r % 2, s_cur)
        return s_cur

    lax.fori_loop(0, n_chunks // RET_UNROLL, body, masked_scores(0))
    normalise(n_chunks - 1, 1)


def _ret_core(q, k, v, dm, rs, ks, cd, batch, seq):
    t = q.shape[0]
    c_len = RET_CHUNK
    tok = lambda b, h: (b, h)
    head3 = lambda b, h: (h, 0, 0)
    return pl.pallas_call(
        _ret_core_kernel,
        grid=(batch, RET_HEADS),
        in_specs=[
            pl.BlockSpec((seq, RET_DK), tok),
            pl.BlockSpec((seq, RET_DK), tok),
            pl.BlockSpec((seq, RET_DV), tok),
            pl.BlockSpec((1, c_len, c_len), head3),
            pl.BlockSpec((1, c_len, 1), head3),
            pl.BlockSpec((1, c_len, 1), head3),
            pl.BlockSpec((1, 1, 1), head3),
        ],
        out_specs=pl.BlockSpec((seq, RET_DV), tok),
        out_shape=jax.ShapeDtypeStruct((t, RET_HEADS * RET_DV), BF16),
        scratch_shapes=[pltpu.VMEM((RET_DK, RET_DV), F32), pltpu.VMEM((2, RET_CHUNK, RET_DV), F32)],
        compiler_params=_params("parallel", "parallel"),
        name="ret_core",
    )(q, k, v, dm, rs, ks, cd)


def _out_mlp_kernel(gated, n_cast, *refs):
    refs, cast_in, cast_out = _split_cast_refs(refs, 1, n_cast)
    _run_casts(cast_in, cast_out)
    if gated:
        a_ref, g_ref, wo_ref, x_ref, nw_ref, wu_ref, wd_ref, o_ref = refs
        n_chunks = a_ref.shape[1] // GATE_TILE

        def gate(c):
            cols = slice(c * GATE_TILE, (c + 1) * GATE_TILE)
            g = g_ref[:, cols].astype(F32)
            return (g * (1.0 / (1.0 + jnp.exp(-g))) * a_ref[:, cols].astype(F32)).astype(BF16)

        a_next = gate(0)
        mix = None
        for c in range(n_chunks):
            a = a_next
            if c + 1 < n_chunks:
                a_next = gate(c + 1)
            part = _dot(a, wo_ref[c * GATE_TILE:(c + 1) * GATE_TILE, :])
            mix = part if mix is None else mix + part
    else:
        a_ref, wo_ref, x_ref, nw_ref, wu_ref, wd_ref, o_ref = refs
        mix = _dot_tn(a_ref[...], wo_ref[...])
    x = x_ref[...] + mix
    h = _rms(x, nw_ref[...]).astype(BF16)
    acc = x
    for c in range(D_FF // FF_TILE):
        u = _dot(h, wu_ref[:, c * FF_TILE:(c + 1) * FF_TILE])
        u = jnp.maximum(u, 0.0)
        acc = acc + _dot((u * u).astype(BF16), wd_ref[c * FF_TILE:(c + 1) * FF_TILE, :])
    o_ref[...] = acc


def _out_mlp(a, g, wo, mixer_layer, x, nw, wu, wd, layer, cast_jobs=()):
    t = x.shape[0]
    steps = t // ROW_TILE
    cast_in_specs, cast_out_specs, cast_shapes = _cast_specs(cast_jobs, steps)
    ka = wo.shape[1]
    row = lambda i: (i, 0)
    const = lambda i: (0, 0)
    gated = g is not None
    if gated:
        mix_specs = [pl.BlockSpec((ROW_TILE, ka), row), pl.BlockSpec((ROW_TILE, ka), row)]
        mix_args = [a, g]
    else:
        mix_specs = [pl.BlockSpec((ka, ROW_TILE), lambda i: (0, i))]
        mix_args = [a]
    outs = pl.pallas_call(
        functools.partial(_out_mlp_kernel, gated, len(cast_jobs)),
        grid=(steps,),
        in_specs=mix_specs + [
            pl.BlockSpec((None, ka, D_MODEL), lambda i: (mixer_layer, 0, 0)),
            pl.BlockSpec((ROW_TILE, D_MODEL), row),
            pl.BlockSpec((1, D_MODEL), const),
            pl.BlockSpec((None, D_MODEL, D_FF), lambda i: (layer, 0, 0), pipeline_mode=pl.Buffered(1)),
            pl.BlockSpec((None, D_FF, D_MODEL), lambda i: (layer, 0, 0), pipeline_mode=pl.Buffered(1)),
        ] + cast_in_specs,
        out_specs=[pl.BlockSpec((ROW_TILE, D_MODEL), row)] + cast_out_specs,
        out_shape=[jax.ShapeDtypeStruct((t, D_MODEL), F32)] + cast_shapes,
        compiler_params=_params("parallel"),
        name="out_mlp_gated" if gated else "out_mlp",
    )(*mix_args, wo, x, nw, wu, wd, *[w for w, _ in cast_jobs])
    return outs[0], outs[1:]


def _bias_tiles_kernel(bucket_ranges, rb_ref, idx_ref, o_ref):
    head = pl.program_id(0)
    for d, (lo, hi) in enumerate(bucket_ranges):
        idx = idx_ref[d]
        acc = jnp.full(idx.shape, NEG, F32)
        for b in range(lo, hi + 1):
            acc = jnp.where(idx == b, rb_ref[b, head] * LOG2E, acc)
        o_ref[0, d] = acc


def _bucket_ranges():
    ranges = []
    for d in range(BIAS_TILES):
        lo_dist = max(d * ATT_TILE - (ATT_TILE - 1), 0)
        hi_dist = d * ATT_TILE + (ATT_TILE - 1)

        def bucket(n):
            if n < REL_MAX_EXACT:
                return n
            return min(REL_MAX_EXACT + int(math.log(n / REL_MAX_EXACT) / math.log(REL_MAX_DISTANCE / REL_MAX_EXACT)
                                           * (REL_BUCKETS - REL_MAX_EXACT)), REL_BUCKETS - 1)

        ranges.append((max(bucket(lo_dist) - 1, 0), min(bucket(hi_dist) + 1, REL_BUCKETS - 1)))
    return tuple(ranges)


def _bias_tiles(rel_bias, bucket_idx):
    nd, tq, tk = bucket_idx.shape
    return pl.pallas_call(
        functools.partial(_bias_tiles_kernel, _bucket_ranges()),
        grid=(ATTN_HEADS,),
        in_specs=[
            pl.BlockSpec(memory_space=pltpu.SMEM),
            pl.BlockSpec((nd, tq, tk), lambda h: (0, 0, 0)),
        ],
        out_specs=pl.BlockSpec((1, nd, tq, tk), lambda h: (h, 0, 0, 0)),
        out_shape=jax.ShapeDtypeStruct((ATTN_HEADS, nd, tq, tk), F32),
        compiler_params=_params("parallel"),
        name="bias_tiles",
    )(rel_bias, bucket_idx)


def _rel_bucket(dist):
    n = jnp.maximum(dist, 0)
    nf = jnp.maximum(n, 1).astype(F32)
    large = REL_MAX_EXACT + (jnp.log(nf / REL_MAX_EXACT) / math.log(REL_MAX_DISTANCE / REL_MAX_EXACT)
                             * (REL_BUCKETS - REL_MAX_EXACT)).astype(jnp.int32)
    large = jnp.minimum(large, REL_BUCKETS - 1)
    return jnp.where(n < REL_MAX_EXACT, n, large)


def _bucket_index_tiles():
    r = np.arange(ATT_TILE)
    dist = (np.arange(BIAS_TILES)[:, None, None] * ATT_TILE + r[None, None, :] - r[None, :, None])
    dist = jnp.asarray(dist, jnp.int32)
    return jnp.where(dist >= 0, _rel_bucket(dist), REL_BUCKETS).astype(jnp.int32)


TAB_Q, TAB_K, TAB_DELTA, TAB_FIRST, TAB_ACC = range(5)
PIPE_LAG = 2
EMIT_UNROLL = 4
PIPE_UNROLL = 6


def _tile_schedule(nq, own_first):
    rows = []
    for i in range(nq):
        keys = ([i] + list(range(i))) if own_first else list(range(i + 1))
        for n, j in enumerate(keys):
            rows.append((i, j, min(i - j, BIAS_TILES - 1), int(n == 0), i))
    n_iters = -(-(len(rows) + PIPE_LAG) // PIPE_UNROLL) * PIPE_UNROLL
    idle = (0, 0, 0, 1, nq)
    cols = [idle] * PIPE_LAG + rows
    cols += [idle] * (n_iters + PIPE_LAG - len(cols))
    return jnp.asarray(np.array(cols, np.int32).T), n_iters


def _tile_slice(idx):
    return pl.ds(pl.multiple_of(idx * ATT_TILE, ATT_TILE), ATT_TILE)


def _head_row_mask(h):
    row = lax.broadcasted_iota(jnp.int32, (LANES, 1), 0)
    return (row >= h * HEAD_DIM) & (row < (h + 1) * HEAD_DIM)


def _pair_rows(pair, n=LANES):
    return slice(pair * n, (pair + 1) * n)


def _attn_kernel(kind, lambda_init, n_iters, tab_ref, qt_ref, k_ref, vt_ref, bias_ref, *rest):
    if kind == "moba":
        blk_ref, ot_ref, qts_ref, s_ref, p_ref, acc_ref, neg_ref = rest
    else:
        lam_ref, sw_ref, ot_ref, qts_ref, s_ref, p_ref, acc_ref = rest
    heads = range(HEADS_PER_STEP)
    part_lanes = LANES if kind == "moba" else ATT_TILE
    query_parts = tuple(slice(c, c + part_lanes) for c in range(0, ATT_TILE, part_lanes))
    acc_rows = acc_ref.shape[2]

    n_tiles = qt_ref.shape[0]
    seq = n_tiles * ATT_TILE
    vts_ref = vt_ref
    for e in heads:
        pair, h = divmod(e, HEADS_PER_PAIR)
        for i in range(n_tiles):
            qf = qt_ref[i, _pair_rows(pair), :].astype(F32)
            qts_ref[i, e] = jnp.where(_head_row_mask(h), qf, 0.0).astype(BF16)
    s_ref[...] = jnp.zeros_like(s_ref)
    p_ref[...] = jnp.zeros_like(p_ref)
    acc_ref[...] = jnp.zeros_like(acc_ref)

    if kind == "moba":
        nb = blk_ref.shape[0]
        kmean = _dot(blk_ref[...], k_ref[...])
        km_hi = kmean.astype(BF16)
        km_lo = (kmean - km_hi.astype(F32)).astype(BF16)
        blk_id = lax.broadcasted_iota(jnp.int32, (nb, ATT_TILE), 0)
        for e in heads:
            pair = e // HEADS_PER_PAIR
            for i in range(n_tiles):
                cols = slice(i * ATT_TILE, (i + 1) * ATT_TILE)
                q_tile = qts_ref[i, e]
                gate = _dot(km_hi[:, _pair_rows(pair)], q_tile) + _dot(km_lo[:, _pair_rows(pair)], q_tile)
                past = blk_id < i
                gate = jnp.where(past, gate, NEG)
                rank = jnp.zeros((nb, ATT_TILE), jnp.int32)
                for c in range(i):
                    gc = gate[c:c + 1, :]
                    tie = jnp.where(blk_id > c, 1, 0)
                    rank = rank + jnp.where(gc > gate, 1, jnp.where(gc == gate, tie, 0))
                chosen = jnp.where(rank < MOBA_TOPK, jnp.where(past, 1, 0), 0)
                keep = jnp.maximum(chosen, jnp.where(blk_id == i, 1, 0))
                neg_ref[e, :, cols] = jnp.where(keep == 1, 0.0, NEG)

    def step(t, cur, carry):
        nxt = 1 - cur
        m_prev, alpha_prev, mtile_prev = carry
        ic = tab_ref[TAB_ACC, t]
        jc = tab_ref[TAB_K, t]
        ib = tab_ref[TAB_Q, t + 1]
        jb = tab_ref[TAB_K, t + 1]
        first = tab_ref[TAB_FIRST, t + 1] != 0
        ia = tab_ref[TAB_Q, t + 2]
        ja = tab_ref[TAB_K, t + 2]
        da = tab_ref[TAB_DELTA, t + 2]
        m_new, alpha_new, mtile_new = ([None] * HEADS_PER_STEP for _ in range(3))
        for e in HEAD_ORDER:
            pair = e // HEADS_PER_PAIR

            def stage_a():
                kt = k_ref[_tile_slice(ja), _pair_rows(pair)]
                sb = _dot(kt, qts_ref[ia, e]) + bias_ref[e, da]
                s_ref[nxt, e] = sb
                mtile_new[e] = tuple(jnp.max(sb[:, lanes], axis=0, keepdims=True) for lanes in query_parts)

            def stage_c():
                vt = vts_ref[jc, _pair_rows(e if kind == "moba" else pair, acc_rows), :]
                pv = _dot(vt, p_ref[cur, e])
                alpha = alpha_prev[e][0] if len(query_parts) == 1 else jnp.concatenate(alpha_prev[e], axis=1)
                acc_ref[ic, e] = alpha * acc_ref[ic, e] + pv

            for stage in ((stage_a, stage_c) if kind == "moba" else (stage_c, stage_a)):
                stage()

            m_parts, alpha_parts = [], []
            if kind == "moba":
                neg_row = neg_ref[e, pl.ds(jb, 1), _tile_slice(ib)]
            for part, lanes in enumerate(query_parts):
                s = s_ref[cur, e, :, lanes]
                m_in = jnp.where(first, -jnp.inf, m_prev[e][part])
                m_tile = mtile_prev[e][part]
                if kind == "moba":
                    neg = neg_row[:, lanes]
                    m_e = jnp.maximum(m_in, m_tile + neg)
                    shift = m_e - neg
                else:
                    m_e = jnp.maximum(m_in, m_tile)
                    shift = m_e
                p_ref[nxt, e, :, lanes] = jnp.exp2(s - shift).astype(BF16)
                alpha_parts.append(jnp.exp2(m_in - m_e))
                m_parts.append(m_e)
            alpha_new[e] = tuple(alpha_parts)
            m_new[e] = tuple(m_parts)

        return tuple(m_new), tuple(alpha_new), tuple(mtile_new)

    zeros = tuple(tuple(jnp.zeros((1, part_lanes), F32) for _ in query_parts) for _ in heads)

    def body(u, carry):
        for r in range(PIPE_UNROLL):
            carry = step(PIPE_UNROLL * u + r, r % 2, carry)
        return carry

    lax.fori_loop(0, n_iters // PIPE_UNROLL, body, (zeros, zeros, zeros))

    v_group = acc_rows - ONES_ROWS
    if kind != "moba":
        lam = lam_ref[...]
        lam_full = (jnp.exp(jnp.sum(lam[0:1] * lam[1:2], axis=-1, keepdims=True))
                    - jnp.exp(jnp.sum(lam[2:3] * lam[3:4], axis=-1, keepdims=True)) + lambda_init)

    def emit_row(i):
        cols = _tile_slice(i)
        for pair in range(PAIRS_PER_STEP):
            outs = []
            for h in range(HEADS_PER_PAIR):
                e = pair * HEADS_PER_PAIR + h
                outs.append(acc_ref[i, e, :v_group, :] * (1.0 / acc_ref[i, e, v_group:v_group + 1, :]))
            if kind == "moba":
                o = jnp.concatenate(outs, axis=0)
            else:
                o = outs[0] - lam_full * outs[1]
                o = (o * lax.rsqrt(jnp.mean(o * o, axis=0, keepdims=True) + EPS)
                     * sw_ref[...] * (1.0 - lambda_init))
            ot_ref[_pair_rows(pair), cols] = o.astype(BF16)

    def emit(u, carry):
        for r in range(EMIT_UNROLL):
            emit_row(EMIT_UNROLL * u + r)
        return carry

    assert n_tiles % EMIT_UNROLL == 0
    lax.fori_loop(0, n_tiles // EMIT_UNROLL, emit, 0)


def _attention(kind, qt, k, vt_ext, bias, extras, lambda_init, batch, seq):
    v_group = _value_group(kind)
    acc_rows = v_group + ONES_ROWS
    vt_block_rows = acc_rows * (LANES * PAIRS_PER_STEP // v_group)
    t = k.shape[0]
    nq = seq // ATT_TILE
    tab, n_iters = _tile_schedule(nq, own_first=(kind == "moba"))
    groups = ATTN_HEADS // HEADS_PER_STEP
    rows = LANES * PAIRS_PER_STEP
    in_specs = [
        pl.BlockSpec(memory_space=pltpu.SMEM),
        pl.BlockSpec((nq, rows, ATT_TILE), lambda g, b: (b, g, 0)),
        pl.BlockSpec((seq, rows), lambda g, b: (b, g)),
        pl.BlockSpec((nq, vt_block_rows, ATT_TILE), lambda g, b: (b, g, 0)),
        pl.BlockSpec((HEADS_PER_STEP, BIAS_TILES, ATT_TILE, ATT_TILE), lambda g, b: (g, 0, 0, 0)),
    ]
    scratch = [
        pltpu.VMEM((nq, HEADS_PER_STEP, LANES, ATT_TILE), BF16),
        pltpu.VMEM((2, HEADS_PER_STEP, ATT_TILE, ATT_TILE), F32),
        pltpu.VMEM((2, HEADS_PER_STEP, ATT_TILE, ATT_TILE), BF16),
        pltpu.VMEM((nq + 1, HEADS_PER_STEP, acc_rows, ATT_TILE), F32),
    ]
    if kind == "moba":
        (blk,) = extras
        nb = blk.shape[0]
        in_specs.append(pl.BlockSpec((nb, seq), lambda g, b: (0, 0)))
        scratch.append(pltpu.VMEM((HEADS_PER_STEP, nb, seq), F32))
    else:
        lam, sw = extras
        in_specs += [
            pl.BlockSpec((4, HEAD_DIM), lambda g, b: (0, 0)),
            pl.BlockSpec((LANES, 1), lambda g, b: (0, 0)),
        ]
    return pl.pallas_call(
        functools.partial(_attn_kernel, kind, lambda_init, n_iters),
        grid=(groups, batch),
        in_specs=in_specs,
        out_specs=pl.BlockSpec((rows, seq), lambda g, b: (g, b)),
        out_shape=jax.ShapeDtypeStruct((D_MODEL, t), BF16),
        scratch_shapes=scratch,
        compiler_params=_params("parallel", "parallel"),
        name=kind + "_attn",
    )(tab, qt, k, vt_ext, bias, *extras)


def _rotary_tables(seq):
    d = RET_DK
    inv_freq = ROPE_BASE ** (-np.arange(0, d, 2, dtype=np.float64) / d)
    ang = np.arange(seq, dtype=np.float64)[:, None] * inv_freq[None, :]
    return jnp.asarray(np.cos(ang), F32), jnp.asarray(np.sin(ang), F32)


def _retention_decay_tables():
    c_len = RET_CHUNK
    log_gamma = np.log(1.0 - 2.0 ** (-5.0 - np.arange(RET_HEADS, dtype=np.float64)))
    pos = np.arange(c_len, dtype=np.float64)
    rel = pos[:, None] - pos[None, :]
    dm = np.where(rel >= 0, np.exp(np.maximum(rel, 0.0)[None] * log_gamma[:, None, None]), 0.0)
    rs = np.exp((pos + 1.0)[None, :] * log_gamma[:, None])[:, :, None]
    ks = np.exp((c_len - 1.0 - pos)[None, :] * log_gamma[:, None])[:, :, None]
    cd = np.exp(c_len * log_gamma)[:, None, None]
    return tuple(jnp.asarray(a, F32) for a in (dm, rs, ks, cd))


def _block_mean_matrix(seq):
    nb = seq // MOBA_BLOCK
    m = (np.arange(seq)[None, :] // MOBA_BLOCK == np.arange(nb)[:, None]) / float(MOBA_BLOCK)
    return jnp.asarray(m, BF16)


def _head_group_matrix():
    g = np.arange(COL_TILE)[:, None] // HEAD_DIM == np.arange(COL_TILE)[None, :] // HEAD_DIM
    return jnp.asarray(g / float(HEAD_DIM), BF16)


def kernel(x, rel_bias, norm1, norm2, w_up, w_down, ret_w_in, ret_w_out,
           moba_w_in, moba_q_norm, moba_k_norm, moba_w_out,
           diff_w_in, diff_q_norm, diff_k_norm, diff_lambda, diff_subln, diff_w_out):
    batch, seq, d = x.shape
    depth = norm1.shape[0]
    assert d == D_MODEL and seq % ROW_TILE == 0 and seq % ATT_TILE == 0 and seq % (RET_UNROLL * RET_CHUNK) == 0
    assert seq % MOBA_BLOCK == 0 and MOBA_BLOCK == ATT_TILE
    t = batch * seq
    xf = x.reshape(t, d)

    bias = _bias_tiles(rel_bias.astype(F32), _bucket_index_tiles())
    grp = _head_group_matrix()
    q_scale = HEAD_DIM ** -0.5 * LOG2E
    mixer_weights = ((ret_w_in, ret_w_out), (moba_w_in, moba_w_out), (diff_w_in, diff_w_out))

    def layer_weights_f32(i):
        kind, j = i % N_MIXERS, i // N_MIXERS
        w_in, w_out = mixer_weights[kind]
        return [(w_in, j), (w_out, j), (w_up, i), (w_down, i)]

    first_jobs = layer_weights_f32(0)
    w_in = first_jobs[0][0][0:1].astype(BF16)
    w_out = wu = wd = None

    for i in range(depth):
        kind, j = i % N_MIXERS, i // N_MIXERS
        nw1 = norm1[i].reshape(1, d)
        nw2 = norm2[i].reshape(1, d)
        cast_jobs = layer_weights_f32(i + 1) if i + 1 < depth else ()
        if kind == 0:
            cos, sin = _rotary_tables(seq)
            dm, rs, ks, cd = _retention_decay_tables()
            (q, k, v, g), early = _ret_proj(xf, nw1, w_in, 0, cos, sin, seq, first_jobs[1:] if i == 0 else ())
            if early:
                w_out, wu, wd = early
            o = _ret_core(q, k, v, dm, rs, ks, cd, batch, seq)
            xf, nxt = _out_mlp(o, g, w_out, 0, xf, nw2, wu, wd, 0, cast_jobs)
        elif kind == 1:
            qn = (jnp.tile(moba_q_norm[j], ATTN_HEADS) * q_scale).reshape(1, d)
            kn = jnp.tile(moba_k_norm[j], ATTN_HEADS).reshape(1, d)
            qt, k, vt = _attn_proj(xf, nw1, w_in, 0, qn, kn, grp, _value_group("moba"))
            ot = _attention("moba", qt, k, vt, bias, (_block_mean_matrix(seq),), 0.0, batch, seq)
            xf, nxt = _out_mlp(ot, None, w_out, 0, xf, nw2, wu, wd, 0, cast_jobs)
        else:
            lambda_init = 0.8 - 0.6 * math.exp(-0.3 * i)
            qn = (jnp.tile(diff_q_norm[j], ATTN_HEADS) * q_scale).reshape(1, d)
            kn = jnp.tile(diff_k_norm[j], ATTN_HEADS).reshape(1, d)
            qt, k, vt = _attn_proj(xf, nw1, w_in, 0, qn, kn, grp, _value_group("diff"))
            extras = (diff_lambda[j].astype(F32), diff_subln[j].reshape(LANES, 1))
            ot = _attention("diff", qt, k, vt, bias, extras, lambda_init, batch, seq)
            xf, nxt = _out_mlp(ot, None, w_out, 0, xf, nw2, wu, wd, 0, cast_jobs)
        if nxt:
            w_in, w_out, wu, wd = nxt
    return xf.reshape(batch, seq, d)
```

```python
import functools
import math

import numpy as np
import jax
import jax.numpy as jnp
from jax import lax
from jax.experimental import pallas as pl
from jax.experimental.pallas import tpu as pltpu

F32 = jnp.float32
BF16 = jnp.bfloat16

D_MODEL = 1024
N_MIXERS = 3
RET_HEADS = 4
RET_DK = D_MODEL // RET_HEADS
RET_DV = 2 * RET_DK
ROPE_BASE = 10000.0
ATTN_HEADS = 16
HEAD_DIM = D_MODEL // ATTN_HEADS
MOBA_BLOCK = 256
MOBA_TOPK = 3
REL_BUCKETS = 32
REL_MAX_EXACT = REL_BUCKETS // 2
REL_MAX_DISTANCE = 1024
D_FF = 4 * D_MODEL
EPS = 1e-6
NEG = -1e30
LOG2E = math.log2(math.e)

LANES = 128
BF16_SUBLANES = 16
VMEM_LIMIT_BYTES = 56 * 1024 * 1024

ROW_TILE = 512
COL_TILE = 256
FF_TILE = 1024
GATE_TILE = 256
RET_CHUNK = 256
RET_UNROLL = 4
ATT_TILE = 256
BIAS_TILES = 6
HEADS_PER_PAIR = LANES // HEAD_DIM
PAIRS_PER_STEP = 2
HEADS_PER_STEP = HEADS_PER_PAIR * PAIRS_PER_STEP
HEAD_ORDER = tuple(p * HEADS_PER_PAIR + h for h in range(HEADS_PER_PAIR) for p in range(PAIRS_PER_STEP))
ONES_ROWS = BF16_SUBLANES


def _value_group(kind):
    return HEAD_DIM if kind == "moba" else LANES


def _params(*sem):
    return pltpu.CompilerParams(dimension_semantics=sem, vmem_limit_bytes=VMEM_LIMIT_BYTES)


def _rms(xf, w):
    ms = jnp.mean(xf * xf, axis=-1, keepdims=True)
    return xf * lax.rsqrt(ms + EPS) * w


def _dot(a, b):
    return jnp.dot(a, b, preferred_element_type=F32)


def _dot_nt(a, b):
    return lax.dot_general(a, b, (((1,), (1,)), ((), ())), preferred_element_type=F32)


def _dot_tn(a, b):
    return lax.dot_general(a, b, (((0,), (0,)), ((), ())), preferred_element_type=F32)


def _cast_specs(cast_jobs, steps):
    in_specs, out_specs, shapes = [], [], []
    for w, w_layer in cast_jobs:
        _, rows, cols = w.shape
        slab = rows // steps
        assert slab * steps == rows and slab % BF16_SUBLANES == 0
        in_specs.append(pl.BlockSpec((None, slab, cols), lambda i, w_layer=w_layer: (w_layer, i, 0)))
        out_specs.append(pl.BlockSpec((None, slab, cols), lambda i: (0, i, 0)))
        shapes.append(jax.ShapeDtypeStruct((1, rows, cols), BF16))
    return in_specs, out_specs, shapes


def _split_cast_refs(refs, n_outputs, n_cast):
    n_in = len(refs) - n_outputs - 2 * n_cast
    sources = refs[n_in:n_in + n_cast]
    outputs = refs[n_in + n_cast:n_in + n_cast + n_outputs]
    return refs[:n_in] + outputs, sources, refs[n_in + n_cast + n_outputs:]


def _run_casts(sources, destinations):
    for src, dst in zip(sources, destinations):
        dst[...] = src[...].astype(BF16)


def _ret_proj_kernel(n_cast, *refs):
    refs, cast_in, cast_out = _split_cast_refs(refs, 4, n_cast)
    _run_casts(cast_in, cast_out)
    x_ref, nw_ref, w_ref, cos_ref, sin_ref, q_ref, k_ref, v_ref, g_ref = refs
    h = _rms(x_ref[...], nw_ref[...]).astype(BF16)
    cos = cos_ref[...]
    sin = sin_ref[...]
    half = RET_DK // 2
    chunks = []
    for out_ref, scale in ((q_ref, 1.0), (k_ref, RET_DK ** -0.5)):
        chunks += [(out_ref, hd * RET_DK, scale) for hd in range(RET_HEADS)]
    for out_ref in (v_ref, g_ref):
        chunks += [(out_ref, c * COL_TILE, None) for c in range(RET_HEADS * RET_DV // COL_TILE)]
    assert RET_DK == COL_TILE

    def project(n):
        return _dot(h, w_ref[:, n * COL_TILE:(n + 1) * COL_TILE])

    acc_next = project(0)
    for n, (out_ref, col0, scale) in enumerate(chunks):
        acc = acc_next
        if n + 1 < len(chunks):
            acc_next = project(n + 1)
        if scale is None:
            out_ref[:, col0:col0 + COL_TILE] = acc.astype(BF16)
        else:
            x1 = acc[:, :half]
            x2 = acc[:, half:]
            r1 = x1 * cos - x2 * sin
            r2 = x1 * sin + x2 * cos
            if scale != 1.0:
                r1 = r1 * scale
                r2 = r2 * scale
            out_ref[:, col0:col0 + half] = r1.astype(BF16)
            out_ref[:, col0 + half:col0 + RET_DK] = r2.astype(BF16)


def _ret_proj(x, nw, w, layer, cos, sin, seq, cast_jobs=()):
    t = x.shape[0]
    cast_in_specs, cast_out_specs, cast_shapes = _cast_specs(cast_jobs, t // ROW_TILE)
    n_in = w.shape[2]
    tiles_per_seq = seq // ROW_TILE
    row = lambda i: (i, 0)
    const = lambda i: (0, 0)
    pos = lambda i: (i % tiles_per_seq, 0)
    n_qk = RET_HEADS * RET_DK
    n_v = RET_HEADS * RET_DV
    outs = pl.pallas_call(
        functools.partial(_ret_proj_kernel, len(cast_jobs)),
        grid=(t // ROW_TILE,),
        in_specs=[
            pl.BlockSpec((ROW_TILE, D_MODEL), row),
            pl.BlockSpec((1, D_MODEL), const),
            pl.BlockSpec((None, D_MODEL, n_in), lambda i: (layer, 0, 0)),
            pl.BlockSpec((ROW_TILE, RET_DK // 2), pos),
            pl.BlockSpec((ROW_TILE, RET_DK // 2), pos),
        ] + cast_in_specs,
        out_specs=[
            pl.BlockSpec((ROW_TILE, n_qk), row),
            pl.BlockSpec((ROW_TILE, n_qk), row),
            pl.BlockSpec((ROW_TILE, n_v), row),
            pl.BlockSpec((ROW_TILE, n_v), row),
        ] + cast_out_specs,
        out_shape=[
            jax.ShapeDtypeStruct((t, n_qk), BF16),
            jax.ShapeDtypeStruct((t, n_qk), BF16),
            jax.ShapeDtypeStruct((t, n_v), BF16),
            jax.ShapeDtypeStruct((t, n_v), BF16),
        ] + cast_shapes,
        compiler_params=_params("parallel"),
        name="ret_proj",
    )(x, nw, w, cos, sin, *[cw for cw, _ in cast_jobs])
    return outs[:4], outs[4:]


def _attn_proj_kernel(v_group, x_ref, nw_ref, w_ref, qn_ref, kn_ref, grp_ref, qt_ref, k_ref, vt_ref):
    h = _rms(x_ref[...], nw_ref[...]).astype(BF16)
    grp = grp_ref[...]
    groups_per_chunk = COL_TILE // v_group
    chunks_per_part = D_MODEL // COL_TILE
    n_chunks = 3 * chunks_per_part
    tiles_per_step = ROW_TILE // ATT_TILE

    def project(n):
        return _dot(h, w_ref[:, n * COL_TILE:(n + 1) * COL_TILE])

    acc_next = project(0)
    for n in range(n_chunks):
        acc = acc_next
        if n + 1 < n_chunks:
            acc_next = project(n + 1)
        part, c = divmod(n, chunks_per_part)
        cols = slice(c * COL_TILE, (c + 1) * COL_TILE)
        if part < 2:
            hw_ref = qn_ref if part == 0 else kn_ref
            ms = _dot((acc * acc).astype(BF16), grp)
            acc = acc * lax.rsqrt(ms + EPS) * hw_ref[:, cols]
        if part == 0:
            acc_t = acc.T.astype(BF16)
            for tile in range(tiles_per_step):
                qt_ref[tile, cols, :] = acc_t[:, tile * ATT_TILE:(tile + 1) * ATT_TILE]
        elif part == 1:
            k_ref[:, cols] = acc.astype(BF16)
        else:
            acc_t = acc.T.astype(BF16)
            for p in range(groups_per_chunk):
                row0 = (c * groups_per_chunk + p) * (v_group + ONES_ROWS)
                for tile in range(tiles_per_step):
                    vt_ref[tile, row0:row0 + v_group, :] = acc_t[p * v_group:(p + 1) * v_group,
                                                                 tile * ATT_TILE:(tile + 1) * ATT_TILE]
                    vt_ref[tile, row0 + v_group:row0 + v_group + ONES_ROWS, :] = jnp.ones((ONES_ROWS, ATT_TILE), BF16)


def _attn_proj(x, nw, w, layer, qn, kn, grp, v_group):
    t = x.shape[0]
    row = lambda i: (i, 0)
    tile3 = lambda i: (i, 0, 0)
    const = lambda i: (0, 0)
    vt_rows = (D_MODEL // v_group) * (v_group + ONES_ROWS)
    tiles_per_step = ROW_TILE // ATT_TILE
    return pl.pallas_call(
        functools.partial(_attn_proj_kernel, v_group),
        grid=(t // ROW_TILE,),
        in_specs=[
            pl.BlockSpec((ROW_TILE, D_MODEL), row),
            pl.BlockSpec((1, D_MODEL), const),
            pl.BlockSpec((None, D_MODEL, 3 * D_MODEL), lambda i: (layer, 0, 0)),
            pl.BlockSpec((1, D_MODEL), const),
            pl.BlockSpec((1, D_MODEL), const),
            pl.BlockSpec((COL_TILE, COL_TILE), const),
        ],
        out_specs=[
            pl.BlockSpec((tiles_per_step, D_MODEL, ATT_TILE), tile3),
            pl.BlockSpec((ROW_TILE, D_MODEL), row),
            pl.BlockSpec((tiles_per_step, vt_rows, ATT_TILE), tile3),
        ],
        out_shape=[
            jax.ShapeDtypeStruct((t // ATT_TILE, D_MODEL, ATT_TILE), BF16),
            jax.ShapeDtypeStruct((t, D_MODEL), BF16),
            jax.ShapeDtypeStruct((t // ATT_TILE, vt_rows, ATT_TILE), BF16),
        ],
        compiler_params=_params("parallel"),
        name="attn_proj",
    )(x, nw, w, qn, kn, grp)


def _ret_core_kernel(q_ref, k_ref, v_ref, dm_ref, rs_ref, ks_ref, cd_ref, o_ref, state_ref, raw_ref):
    c_len = RET_CHUNK
    n_chunks = q_ref.shape[0] // c_len
    state_ref[...] = jnp.zeros_like(state_ref)
    raw_ref[...] = jnp.zeros_like(raw_ref)

    def chunk(c):
        return pl.ds(pl.multiple_of(c * c_len, c_len), c_len)

    def masked_scores(c):
        return (_dot_nt(q_ref[chunk(c), :], k_ref[chunk(c), :]) * dm_ref[0]).astype(BF16)

    def normalise(c, slot):
        o = raw_ref[slot]
        o_ref[chunk(c), :] = (o * lax.rsqrt(jnp.mean(o * o, axis=-1, keepdims=True) + EPS)).astype(BF16)

    def step(c, slot, s_cur):
        sl = chunk(c)
        q = q_ref[sl, :]
        k = k_ref[sl, :]
        v = v_ref[sl, :]
        state = state_ref[...]
        kd = (k.astype(F32) * ks_ref[0]).astype(BF16)
        kv = _dot_tn(kd, v)
        inner = _dot(s_cur, v)
        cross = _dot(q, state.astype(BF16))
        s_next = masked_scores(jnp.minimum(c + 1, n_chunks - 1))
        normalise(jnp.maximum(c - 1, 0), 1 - slot)
        raw_ref[slot] = inner + cross * rs_ref[0]
        state_ref[...] = state * cd_ref[0] + kv
        return s_next

    def body(u, s_cur):
        for r in range(RET_UNROLL):
            s_cur = step(RET_UNROLL * u + r, r % 2, s_cur)
        return s_cur

    lax.fori_loop(0, n_chunks // RET_UNROLL, body, masked_scores(0))
    normalise(n_chunks - 1, 1)


def _ret_core(q, k, v, dm, rs, ks, cd, batch, seq):
    t = q.shape[0]
    c_len = RET_CHUNK
    tok = lambda b, h: (b, h)
    head3 = lambda b, h: (h, 0, 0)
    return pl.pallas_call(
        _ret_core_kernel,
        grid=(batch, RET_HEADS),
        in_specs=[
            pl.BlockSpec((seq, RET_DK), tok),
            pl.BlockSpec((seq, RET_DK), tok),
            pl.BlockSpec((seq, RET_DV), tok),
            pl.BlockSpec((1, c_len, c_len), head3),
            pl.BlockSpec((1, c_len, 1), head3),
            pl.BlockSpec((1, c_len, 1), head3),
            pl.BlockSpec((1, 1, 1), head3),
        ],
        out_specs=pl.BlockSpec((seq, RET_DV), tok),
        out_shape=jax.ShapeDtypeStruct((t, RET_HEADS * RET_DV), BF16),
        scratch_shapes=[pltpu.VMEM((RET_DK, RET_DV), F32), pltpu.VMEM((2, RET_CHUNK, RET_DV), F32)],
        compiler_params=_params("parallel", "parallel"),
        name="ret_core",
    )(q, k, v, dm, rs, ks, cd)


def _out_mlp_kernel(gated, n_cast, *refs):
    refs, cast_in, cast_out = _split_cast_refs(refs, 1, n_cast)
    _run_casts(cast_in, cast_out)
    if gated:
        a_ref, g_ref, wo_ref, x_ref, nw_ref, wu_ref, wd_ref, o_ref = refs
        n_chunks = a_ref.shape[1] // GATE_TILE

        def gate(c):
            cols = slice(c * GATE_TILE, (c + 1) * GATE_TILE)
            g = g_ref[:, cols].astype(F32)
            return (g * (1.0 / (1.0 + jnp.exp(-g))) * a_ref[:, cols].astype(F32)).astype(BF16)

        a_next = gate(0)
        mix = None
        for c in range(n_chunks):
            a = a_next
            if c + 1 < n_chunks:
                a_next = gate(c + 1)
            part = _dot(a, wo_ref[c * GATE_TILE:(c + 1) * GATE_TILE, :])
            mix = part if mix is None else mix + part
    else:
        a_ref, wo_ref, x_ref, nw_ref, wu_ref, wd_ref, o_ref = refs
        mix = _dot_tn(a_ref[...], wo_ref[...])
    x = x_ref[...] + mix
    h = _rms(x, nw_ref[...]).astype(BF16)
    acc = x
    for c in range(D_FF // FF_TILE):
        u = _dot(h, wu_ref[:, c * FF_TILE:(c + 1) * FF_TILE])
        u = jnp.maximum(u, 0.0)
        acc = acc + _dot((u * u).astype(BF16), wd_ref[c * FF_TILE:(c + 1) * FF_TILE, :])
    o_ref[...] = acc


def _out_mlp(a, g, wo, mixer_layer, x, nw, wu, wd, layer, cast_jobs=()):
    t = x.shape[0]
    steps = t // ROW_TILE
    cast_in_specs, cast_out_specs, cast_shapes = _cast_specs(cast_jobs, steps)
    ka = wo.shape[1]
    row = lambda i: (i, 0)
    const = lambda i: (0, 0)
    gated = g is not None
    if gated:
        mix_specs = [pl.BlockSpec((ROW_TILE, ka), row), pl.BlockSpec((ROW_TILE, ka), row)]
        mix_args = [a, g]
    else:
        mix_specs = [pl.BlockSpec((ka, ROW_TILE), lambda i: (0, i))]
        mix_args = [a]
    outs = pl.pallas_call(
        functools.partial(_out_mlp_kernel, gated, len(cast_jobs)),
        grid=(steps,),
        in_specs=mix_specs + [
            pl.BlockSpec((None, ka, D_MODEL), lambda i: (mixer_layer, 0, 0)),
            pl.BlockSpec((ROW_TILE, D_MODEL), row),
            pl.BlockSpec((1, D_MODEL), const),
            pl.BlockSpec((None, D_MODEL, D_FF), lambda i: (layer, 0, 0), pipeline_mode=pl.Buffered(1)),
            pl.BlockSpec((None, D_FF, D_MODEL), lambda i: (layer, 0, 0), pipeline_mode=pl.Buffered(1)),
        ] + cast_in_specs,
        out_specs=[pl.BlockSpec((ROW_TILE, D_MODEL), row)] + cast_out_specs,
        out_shape=[jax.ShapeDtypeStruct((t, D_MODEL), F32)] + cast_shapes,
        compiler_params=_params("parallel"),
        name="out_mlp_gated" if gated else "out_mlp",
    )(*mix_args, wo, x, nw, wu, wd, *[w for w, _ in cast_jobs])
    return outs[0], outs[1:]


def _bias_tiles_kernel(bucket_ranges, rb_ref, idx_ref, o_ref):
    head = pl.program_id(0)
    for d, (lo, hi) in enumerate(bucket_ranges):
        idx = idx_ref[d]
        acc = jnp.full(idx.shape, NEG, F32)
        for b in range(lo, hi + 1):
            acc = jnp.where(idx == b, rb_ref[b, head] * LOG2E, acc)
        o_ref[0, d] = acc


def _bucket_ranges():
    ranges = []
    for d in range(BIAS_TILES):
        lo_dist = max(d * ATT_TILE - (ATT_TILE - 1), 0)
        hi_dist = d * ATT_TILE + (ATT_TILE - 1)

        def bucket(n):
            if n < REL_MAX_EXACT:
                return n
            return min(REL_MAX_EXACT + int(math.log(n / REL_MAX_EXACT) / math.log(REL_MAX_DISTANCE / REL_MAX_EXACT)
                                           * (REL_BUCKETS - REL_MAX_EXACT)), REL_BUCKETS - 1)

        ranges.append((max(bucket(lo_dist) - 1, 0), min(bucket(hi_dist) + 1, REL_BUCKETS - 1)))
    return tuple(ranges)


def _bias_tiles(rel_bias, bucket_idx):
    nd, tq, tk = bucket_idx.shape
    return pl.pallas_call(
        functools.partial(_bias_tiles_kernel, _bucket_ranges()),
        grid=(ATTN_HEADS,),
        in_specs=[
            pl.BlockSpec(memory_space=pltpu.SMEM),
            pl.BlockSpec((nd, tq, tk), lambda h: (0, 0, 0)),
        ],
        out_specs=pl.BlockSpec((1, nd, tq, tk), lambda h: (h, 0, 0, 0)),
        out_shape=jax.ShapeDtypeStruct((ATTN_HEADS, nd, tq, tk), F32),
        compiler_params=_params("parallel"),
        name="bias_tiles",
    )(rel_bias, bucket_idx)


def _rel_bucket(dist):
    n = jnp.maximum(dist, 0)
    nf = jnp.maximum(n, 1).astype(F32)
    large = REL_MAX_EXACT + (jnp.log(nf / REL_MAX_EXACT) / math.log(REL_MAX_DISTANCE / REL_MAX_EXACT)
                             * (REL_BUCKETS - REL_MAX_EXACT)).astype(jnp.int32)
    large = jnp.minimum(large, REL_BUCKETS - 1)
    return jnp.where(n < REL_MAX_EXACT, n, large)


def _bucket_index_tiles():
    r = np.arange(ATT_TILE)
    dist = (np.arange(BIAS_TILES)[:, None, None] * ATT_TILE + r[None, None, :] - r[None, :, None])
    dist = jnp.asarray(dist, jnp.int32)
    return jnp.where(dist >= 0, _rel_bucket(dist), REL_BUCKETS).astype(jnp.int32)


TAB_Q, TAB_K, TAB_DELTA, TAB_FIRST, TAB_ACC = range(5)
PIPE_LAG = 2
EMIT_UNROLL = 4
PIPE_UNROLL = 6


def _tile_schedule(nq, own_first):
    rows = []
    for i in range(nq):
        keys = ([i] + list(range(i))) if own_first else list(range(i + 1))
        for n, j in enumerate(keys):
            rows.append((i, j, min(i - j, BIAS_TILES - 1), int(n == 0), i))
    n_iters = -(-(len(rows) + PIPE_LAG) // PIPE_UNROLL) * PIPE_UNROLL
    idle = (0, 0, 0, 1, nq)
    cols = [idle] * PIPE_LAG + rows
    cols += [idle] * (n_iters + PIPE_LAG - len(cols))
    return jnp.asarray(np.array(cols, np.int32).T), n_iters


def _tile_slice(idx):
    return pl.ds(pl.multiple_of(idx * ATT_TILE, ATT_TILE), ATT_TILE)


def _head_row_mask(h):
    row = lax.broadcasted_iota(jnp.int32, (LANES, 1), 0)
    return (row >= h * HEAD_DIM) & (row < (h + 1) * HEAD_DIM)


def _pair_rows(pair, n=LANES):
    return slice(pair * n, (pair + 1) * n)


def _attn_kernel(kind, lambda_init, n_iters, tab_ref, qt_ref, k_ref, vt_ref, bias_ref, *rest):
    if kind == "moba":
        blk_ref, ot_ref, qts_ref, s_ref, p_ref, acc_ref, neg_ref = rest
    else:
        lam_ref, sw_ref, ot_ref, qts_ref, s_ref, p_ref, acc_ref = rest
    heads = range(HEADS_PER_STEP)
    part_lanes = LANES if kind == "moba" else ATT_TILE
    query_parts = tuple(slice(c, c + part_lanes) for c in range(0, ATT_TILE, part_lanes))
    acc_rows = acc_ref.shape[2]

    n_tiles = qt_ref.shape[0]
    seq = n_tiles * ATT_TILE
    vts_ref = vt_ref
    for e in heads:
        pair, h = divmod(e, HEADS_PER_PAIR)
        for i in range(n_tiles):
            qf = qt_ref[i, _pair_rows(pair), :].astype(F32)
            qts_ref[i, e] = jnp.where(_head_row_mask(h), qf, 0.0).astype(BF16)
    s_ref[...] = jnp.zeros_like(s_ref)
    p_ref[...] = jnp.zeros_like(p_ref)
    acc_ref[...] = jnp.zeros_like(acc_ref)

    if kind == "moba":
        nb = blk_ref.shape[0]
        kmean = _dot(blk_ref[...], k_ref[...])
        km_hi = kmean.astype(BF16)
        km_lo = (kmean - km_hi.astype(F32)).astype(BF16)
        blk_id = lax.broadcasted_iota(jnp.int32, (nb, ATT_TILE), 0)
        for e in heads:
            pair = e // HEADS_PER_PAIR
            for i in range(n_tiles):
                cols = slice(i * ATT_TILE, (i + 1) * ATT_TILE)
                q_tile = qts_ref[i, e]
                gate = _dot(km_hi[:, _pair_rows(pair)], q_tile) + _dot(km_lo[:, _pair_rows(pair)], q_tile)
                past = blk_id < i
                gate = jnp.where(past, gate, NEG)
                rank = jnp.zeros((nb, ATT_TILE), jnp.int32)
                for c in range(i if i > MOBA_TOPK else 0):
                    gc = gate[c:c + 1, :]
                    tie = jnp.where(blk_id > c, 1, 0)
                    rank = rank + jnp.where(gc > gate, 1, jnp.where(gc == gate, tie, 0))
                chosen = jnp.where(rank < MOBA_TOPK, jnp.where(past, 1, 0), 0)
                keep = jnp.maximum(chosen, jnp.where(blk_id == i, 1, 0))
                neg_ref[e, :, cols] = jnp.where(keep == 1, 0.0, NEG)

    def step(t, cur, carry):
        nxt = 1 - cur
        m_prev, alpha_prev, mtile_prev = carry
        ic = tab_ref[TAB_ACC, t]
        jc = tab_ref[TAB_K, t]
        ib = tab_ref[TAB_Q, t + 1]
        jb = tab_ref[TAB_K, t + 1]
        first = tab_ref[TAB_FIRST, t + 1] != 0
        ia = tab_ref[TAB_Q, t + 2]
        ja = tab_ref[TAB_K, t + 2]
        da = tab_ref[TAB_DELTA, t + 2]
        m_new, alpha_new, mtile_new = ([None] * HEADS_PER_STEP for _ in range(3))
        for e in HEAD_ORDER:
            pair = e // HEADS_PER_PAIR

            def stage_a():
                kt = k_ref[_tile_slice(ja), _pair_rows(pair)]
                sb = _dot(kt, qts_ref[ia, e]) + bias_ref[e, da]
                s_ref[nxt, e] = sb
                mtile_new[e] = tuple(jnp.max(sb[:, lanes], axis=0, keepdims=True) for lanes in query_parts)

            def stage_c():
                vt = vts_ref[jc, _pair_rows(e if kind == "moba" else pair, acc_rows), :]
                pv = _dot(vt, p_ref[cur, e])
                alpha = alpha_prev[e][0] if len(query_parts) == 1 else jnp.concatenate(alpha_prev[e], axis=1)
                acc_ref[ic, e] = alpha * acc_ref[ic, e] + pv

            for stage in ((stage_a, stage_c) if kind == "moba" else (stage_c, stage_a)):
                stage()

            m_parts, alpha_parts = [], []
            if kind == "moba":
                neg_row = neg_ref[e, pl.ds(jb, 1), _tile_slice(ib)]
            for part, lanes in enumerate(query_parts):
                s = s_ref[cur, e, :, lanes]
                m_in = jnp.where(first, -jnp.inf, m_prev[e][part])
                m_tile = mtile_prev[e][part]
                if kind == "moba":
                    neg = neg_row[:, lanes]
                    m_e = jnp.maximum(m_in, m_tile + neg)
                    shift = m_e - neg
                else:
                    m_e = jnp.maximum(m_in, m_tile)
                    shift = m_e
                p_ref[nxt, e, :, lanes] = jnp.exp2(s - shift).astype(BF16)
                alpha_parts.append(jnp.exp2(m_in - m_e))
                m_parts.append(m_e)
            alpha_new[e] = tuple(alpha_parts)
            m_new[e] = tuple(m_parts)

        return tuple(m_new), tuple(alpha_new), tuple(mtile_new)

    zeros = tuple(tuple(jnp.zeros((1, part_lanes), F32) for _ in query_parts) for _ in heads)

    def body(u, carry):
        for r in range(PIPE_UNROLL):
            carry = step(PIPE_UNROLL * u + r, r % 2, carry)
        return carry

    lax.fori_loop(0, n_iters // PIPE_UNROLL, body, (zeros, zeros, zeros))

    v_group = acc_rows - ONES_ROWS
    if kind != "moba":
        lam = lam_ref[...]
        lam_full = (jnp.exp(jnp.sum(lam[0:1] * lam[1:2], axis=-1, keepdims=True))
                    - jnp.exp(jnp.sum(lam[2:3] * lam[3:4], axis=-1, keepdims=True)) + lambda_init)

    def emit_row(i):
        cols = _tile_slice(i)
        for pair in range(PAIRS_PER_STEP):
            outs = []
            for h in range(HEADS_PER_PAIR):
                e = pair * HEADS_PER_PAIR + h
                outs.append(acc_ref[i, e, :v_group, :] * (1.0 / acc_ref[i, e, v_group:v_group + 1, :]))
            if kind == "moba":
                o = jnp.concatenate(outs, axis=0)
            else:
                o = outs[0] - lam_full * outs[1]
                o = (o * lax.rsqrt(jnp.mean(o * o, axis=0, keepdims=True) + EPS)
                     * sw_ref[...] * (1.0 - lambda_init))
            ot_ref[_pair_rows(pair), cols] = o.astype(BF16)

    def emit(u, carry):
        for r in range(EMIT_UNROLL):
            emit_row(EMIT_UNROLL * u + r)
        return carry

    assert n_tiles % EMIT_UNROLL == 0
    lax.fori_loop(0, n_tiles // EMIT_UNROLL, emit, 0)


def _attention(kind, qt, k, vt_ext, bias, extras, lambda_init, batch, seq):
    v_group = _value_group(kind)
    acc_rows = v_group + ONES_ROWS
    vt_block_rows = acc_rows * (LANES * PAIRS_PER_STEP // v_group)
    t = k.shape[0]
    nq = seq // ATT_TILE
    tab, n_iters = _tile_schedule(nq, own_first=(kind == "moba"))
    groups = ATTN_HEADS // HEADS_PER_STEP
    rows = LANES * PAIRS_PER_STEP
    in_specs = [
        pl.BlockSpec(memory_space=pltpu.SMEM),
        pl.BlockSpec((nq, rows, ATT_TILE), lambda g, b: (b, g, 0)),
        pl.BlockSpec((seq, rows), lambda g, b: (b, g)),
        pl.BlockSpec((nq, vt_block_rows, ATT_TILE), lambda g, b: (b, g, 0)),
        pl.BlockSpec((HEADS_PER_STEP, BIAS_TILES, ATT_TILE, ATT_TILE), lambda g, b: (g, 0, 0, 0)),
    ]
    scratch = [
        pltpu.VMEM((nq, HEADS_PER_STEP, LANES, ATT_TILE), BF16),
        pltpu.VMEM((2, HEADS_PER_STEP, ATT_TILE, ATT_TILE), F32),
        pltpu.VMEM((2, HEADS_PER_STEP, ATT_TILE, ATT_TILE), BF16),
        pltpu.VMEM((nq + 1, HEADS_PER_STEP, acc_rows, ATT_TILE), F32),
    ]
    if kind == "moba":
        (blk,) = extras
        nb = blk.shape[0]
        in_specs.append(pl.BlockSpec((nb, seq), lambda g, b: (0, 0)))
        scratch.append(pltpu.VMEM((HEADS_PER_STEP, nb, seq), F32))
    else:
        lam, sw = extras
        in_specs += [
            pl.BlockSpec((4, HEAD_DIM), lambda g, b: (0, 0)),
            pl.BlockSpec((LANES, 1), lambda g, b: (0, 0)),
        ]
    return pl.pallas_call(
        functools.partial(_attn_kernel, kind, lambda_init, n_iters),
        grid=(groups, batch),
        in_specs=in_specs,
        out_specs=pl.BlockSpec((rows, seq), lambda g, b: (g, b)),
        out_shape=jax.ShapeDtypeStruct((D_MODEL, t), BF16),
        scratch_shapes=scratch,
        compiler_params=_params("parallel", "parallel"),
        name=kind + "_attn",
    )(tab, qt, k, vt_ext, bias, *extras)


def _rotary_tables(seq):
    d = RET_DK
    inv_freq = ROPE_BASE ** (-np.arange(0, d, 2, dtype=np.float64) / d)
    ang = np.arange(seq, dtype=np.float64)[:, None] * inv_freq[None, :]
    return jnp.asarray(np.cos(ang), F32), jnp.asarray(np.sin(ang), F32)


def _retention_decay_tables():
    c_len = RET_CHUNK
    log_gamma = np.log(1.0 - 2.0 ** (-5.0 - np.arange(RET_HEADS, dtype=np.float64)))
    pos = np.arange(c_len, dtype=np.float64)
    rel = pos[:, None] - pos[None, :]
    dm = np.where(rel >= 0, np.exp(np.maximum(rel, 0.0)[None] * log_gamma[:, None, None]), 0.0)
    rs = np.exp((pos + 1.0)[None, :] * log_gamma[:, None])[:, :, None]
    ks = np.exp((c_len - 1.0 - pos)[None, :] * log_gamma[:, None])[:, :, None]
    cd = np.exp(c_len * log_gamma)[:, None, None]
    return tuple(jnp.asarray(a, F32) for a in (dm, rs, ks, cd))


def _block_mean_matrix(seq):
    nb = seq // MOBA_BLOCK
    m = (np.arange(seq)[None, :] // MOBA_BLOCK == np.arange(nb)[:, None]) / float(MOBA_BLOCK)
    return jnp.asarray(m, BF16)


def _head_group_matrix():
    g = np.arange(COL_TILE)[:, None] // HEAD_DIM == np.arange(COL_TILE)[None, :] // HEAD_DIM
    return jnp.asarray(g / float(HEAD_DIM), BF16)


def kernel(x, rel_bias, norm1, norm2, w_up, w_down, ret_w_in, ret_w_out,
           moba_w_in, moba_q_norm, moba_k_norm, moba_w_out,
           diff_w_in, diff_q_norm, diff_k_norm, diff_lambda, diff_subln, diff_w_out):
    batch, seq, d = x.shape
    depth = norm1.shape[0]
    assert d == D_MODEL and seq % ROW_TILE == 0 and seq % ATT_TILE == 0 and seq % (RET_UNROLL * RET_CHUNK) == 0
    assert seq % MOBA_BLOCK == 0 and MOBA_BLOCK == ATT_TILE
    t = batch * seq
    xf = x.reshape(t, d)

    bias = _bias_tiles(rel_bias.astype(F32), _bucket_index_tiles())
    grp = _head_group_matrix()
    q_scale = HEAD_DIM ** -0.5 * LOG2E
    mixer_weights = ((ret_w_in, ret_w_out), (moba_w_in, moba_w_out), (diff_w_in, diff_w_out))

    def layer_weights_f32(i):
        kind, j = i % N_MIXERS, i // N_MIXERS
        w_in, w_out = mixer_weights[kind]
        return [(w_in, j), (w_out, j), (w_up, i), (w_down, i)]

    first_jobs = layer_weights_f32(0)
    w_in = first_jobs[0][0][0:1].astype(BF16)
    w_out = wu = wd = None

    for i in range(depth):
        kind, j = i % N_MIXERS, i // N_MIXERS
        nw1 = norm1[i].reshape(1, d)
        nw2 = norm2[i].reshape(1, d)
        cast_jobs = layer_weights_f32(i + 1) if i + 1 < depth else ()
        if kind == 0:
            cos, sin = _rotary_tables(seq)
            dm, rs, ks, cd = _retention_decay_tables()
            (q, k, v, g), early = _ret_proj(xf, nw1, w_in, 0, cos, sin, seq, first_jobs[1:] if i == 0 else ())
            if early:
                w_out, wu, wd = early
            o = _ret_core(q, k, v, dm, rs, ks, cd, batch, seq)
            xf, nxt = _out_mlp(o, g, w_out, 0, xf, nw2, wu, wd, 0, cast_jobs)
        elif kind == 1:
            qn = (jnp.tile(moba_q_norm[j], ATTN_HEADS) * q_scale).reshape(1, d)
            kn = jnp.tile(moba_k_norm[j], ATTN_HEADS).reshape(1, d)
            qt, k, vt = _attn_proj(xf, nw1, w_in, 0, qn, kn, grp, _value_group("moba"))
            ot = _attention("moba", qt, k, vt, bias, (_block_mean_matrix(seq),), 0.0, batch, seq)
            xf, nxt = _out_mlp(ot, None, w_out, 0, xf, nw2, wu, wd, 0, cast_jobs)
        else:
            lambda_init = 0.8 - 0.6 * math.exp(-0.3 * i)
            qn = (jnp.tile(diff_q_norm[j], ATTN_HEADS) * q_scale).reshape(1, d)
            kn = jnp.tile(diff_k_norm[j], ATTN_HEADS).reshape(1, d)
            qt, k, vt = _attn_proj(xf, nw1, w_in, 0, qn, kn, grp, _value_group("diff"))
            extras = (diff_lambda[j].astype(F32), diff_subln[j].reshape(LANES, 1))
            ot = _attention("diff", qt, k, vt, bias, extras, lambda_init, batch, seq)
            xf, nxt = _out_mlp(ot, None, w_out, 0, xf, nw2, wu, wd, 0, cast_jobs)
        if nxt:
            w_in, w_out, wu, wd = nxt
    return xf.reshape(batch, seq, d)
```
